```python
import math
import jax, jax.numpy as jnp
from jax import lax
import numpy as np

D_MODEL = 1024
BATCH = 8
SEQ = 2048
DEPTH = 2
DEC_BATCH = 128
DEC_SEQ = 8
PAST_LEN = 16384
PAGE_SIZE = 128

N_EVEN = (DEPTH + 1) // 2
N_ODD = DEPTH // 2

H_RET = 4
DK_RET = 128
DV_RET = 128
ROPE_BASE = 10000.0
H_HG = 4
DK_HG = 128
DV_HG = 128
IN_EVEN = 2 * H_RET * DK_RET + 2 * H_RET * DV_RET + 2 * H_HG * DK_HG + 2 * H_HG * DV_HG
MIX_EVEN = H_RET * DV_RET + H_HG * DV_HG
H_SSD = 16
P_SSD = 64
N_SSD = 128
G_SSD = 2
D_INNER = H_SSD * P_SSD
CONV_W = 4
CONV_DIM = D_INNER + 2 * G_SSD * N_SSD
S5_GS = 16
S5_WIDTH = D_MODEL // 2
S5_G = S5_WIDTH // S5_GS
S5_P = 64
IN_ODD = D_INNER + CONV_DIM + H_SSD + S5_WIDTH
MIX_ODD = D_INNER + S5_WIDTH
D_FF = 4 * D_MODEL
CHUNK = 128
CHUNK_VEC = 64
EPS = 1e-6

kernel_name = 'hybrid_retention_hgrn2_ssd_s5_decode_step'


def _rmsnorm(x, w):
    xf = x.astype(jnp.float32)
    y = xf * lax.rsqrt(jnp.mean(xf * xf, axis=-1, keepdims=True) + EPS)
    return (y * w.astype(jnp.float32)).astype(x.dtype)


def _split(x, sizes):
    out, start = [], 0
    for s in sizes:
        out.append(x[..., start:start + s])
        start += s
    return out


def _rope(x, pos):
    half = x.shape[-1] // 2
    inv_freq = ROPE_BASE ** (-jnp.arange(half, dtype=jnp.float32) / half)
    ang = pos[:, None] * inv_freq[None, :]
    cos = jnp.cos(ang)[None, :, None, :]
    sin = jnp.sin(ang)[None, :, None, :]
    x1, x2 = x[..., :half], x[..., half:]
    return jnp.concatenate([x1 * cos - x2 * sin, x1 * sin + x2 * cos], axis=-1)


def _pick_chunk(t, c):
    return c if t % c == 0 else t


def _to_chunks(a, c):
    b, t = a.shape[:2]
    return jnp.moveaxis(a.reshape((b, t // c, c) + a.shape[2:]), 1, 0)


def _from_chunks(a):
    n, b, c = a.shape[:3]
    return jnp.moveaxis(a, 0, 1).reshape((b, n * c) + a.shape[3:])


def _scalar_decay_recurrence(q, k, v, log_a, s0, chunk):
    c = _pick_chunk(q.shape[1], chunk)
    causal = jnp.tril(jnp.ones((c, c), dtype=bool))

    def step(s, blk):
        qc, kc, vc, lc = blk
        cum = jnp.cumsum(lc, axis=1)
        diff = cum[:, :, None, :] - cum[:, None, :, :]
        decay = jnp.exp(jnp.where(causal[None, :, :, None], diff, -jnp.inf))
        scores = jnp.einsum('bthn,bshn->btsh', qc, kc) * decay
        o = (jnp.einsum('btsh,bshp->bthp', scores, vc)
             + jnp.einsum('bthn,bhnp->bthp', qc, s) * jnp.exp(cum)[..., None])
        to_end = jnp.exp(cum[:, -1:, :] - cum)
        s_new = (s * jnp.exp(cum[:, -1, :])[:, :, None, None]
                 + jnp.einsum('bshn,bshp->bhnp', kc * to_end[..., None], vc))
        return s_new, o

    blocks = (_to_chunks(q, c), _to_chunks(k, c), _to_chunks(v, c), _to_chunks(log_a, c))
    s_t, o = lax.scan(step, s0, blocks)
    return _from_chunks(o), s_t


def _vector_decay_recurrence(q, k, v, log_f, s0, chunk):
    c = _pick_chunk(q.shape[1], chunk)
    causal = jnp.tril(jnp.ones((c, c), dtype=bool))

    def step(s, blk):
        qc, kc, vc, lc = blk
        cum = jnp.cumsum(lc, axis=1)
        diff = cum[:, :, None] - cum[:, None]
        decay = jnp.exp(jnp.where(causal[None, :, :, None, None], diff, -jnp.inf))
        scores = jnp.einsum('btshk,bshk->btsh', qc[:, :, None] * decay, kc)
        o = (jnp.einsum('btsh,bshv->bthv', scores, vc)
             + jnp.einsum('bthk,bhkv->bthv', qc * jnp.exp(cum), s))
        s_new = (s * jnp.exp(cum[:, -1])[..., None]
                 + jnp.einsum('bshk,bshv->bhkv', kc * jnp.exp(cum[:, -1:] - cum), vc))
        return s_new, o

    blocks = (_to_chunks(q, c), _to_chunks(k, c), _to_chunks(v, c), _to_chunks(log_f, c))
    s_t, o = lax.scan(step, s0, blocks)
    return _from_chunks(o), s_t


def _linear_combine(e1, e2):
    a1, b1 = e1
    a2, b2 = e2
    return a1 * a2, a2 * b1 + b2


def _s5(u, h0_re, h0_im, lam_re, lam_im, log_step, b_re, b_im, c_re, c_im, d):
    f32 = jnp.float32
    lam = lax.complex(lam_re.astype(f32), lam_im.astype(f32))
    dt = jnp.exp(log_step.astype(f32))[:, None]
    lam_bar = jnp.exp(lam * dt)
    b_bar = ((lam_bar - 1.0) / lam)[..., None] * lax.complex(b_re.astype(f32), b_im.astype(f32))
    bu = jnp.einsum('btgc,gpc->btgp', u.astype(jnp.complex64), b_bar)
    h0 = lax.complex(h0_re.astype(f32), h0_im.astype(f32))
    bu = bu.at[:, 0].add(lam_bar * h0)
    a = jnp.broadcast_to(lam_bar, bu.shape)
    _, hs = lax.associative_scan(_linear_combine, (a, bu), axis=1)
    c = lax.complex(c_re.astype(f32), c_im.astype(f32))
    y = jnp.einsum('btgp,gcp->btgc', hs, c).real + d.astype(f32).reshape(S5_G, S5_GS) * u
    h_last = hs[:, -1]
    return y, h_last.real, h_last.imag


def _even_mixer(h, pos, s_ret, s_hg, w_in, w_out, ret_norm_w, hg_norm_w, lb):
    f32 = jnp.float32
    bsz, t, _ = h.shape
    proj = jnp.einsum('btd,de->bte', h, w_in)
    q_r, k_r, v_r, g_r, q_h, f_h, i_h, g_h = _split(
        proj, [H_RET * DK_RET] * 2 + [H_RET * DV_RET] * 2 + [H_HG * DK_HG] * 2 + [H_HG * DV_HG] * 2)
    q_r = _rope(q_r.astype(f32).reshape(bsz, t, H_RET, DK_RET), pos)
    k_r = _rope(k_r.astype(f32).reshape(bsz, t, H_RET, DK_RET), pos) * (DK_RET ** -0.5)
    v_r = v_r.astype(f32).reshape(bsz, t, H_RET, DV_RET)
    log_gamma = jnp.log1p(-jnp.exp2(-5.0 - jnp.arange(H_RET, dtype=f32)))
    o_r, s_ret_new = _scalar_decay_recurrence(
        q_r, k_r, v_r, jnp.broadcast_to(log_gamma, (bsz, t, H_RET)), s_ret.astype(f32), CHUNK)
    mu = jnp.mean(o_r, axis=-1, keepdims=True)
    var = jnp.mean(jnp.square(o_r - mu), axis=-1, keepdims=True)
    o_r = (o_r - mu) * lax.rsqrt(var + EPS) * ret_norm_w.astype(f32)
    o_r = o_r.reshape(bsz, t, H_RET * DV_RET) * jax.nn.silu(g_r.astype(f32))
    f = lb + (1.0 - lb) * jax.nn.sigmoid(f_h.astype(f32))
    f = f.reshape(bsz, t, H_HG, DK_HG)
    q_h = q_h.astype(f32).reshape(bsz, t, H_HG, DK_HG)
    i_h = i_h.astype(f32).reshape(bsz, t, H_HG, DV_HG)
    o_h, s_hg_new = _vector_decay_recurrence(q_h, 1.0 - f, i_h, jnp.log(f), s_hg.astype(f32), CHUNK_VEC)
    o_h = o_h * lax.rsqrt(jnp.mean(o_h * o_h, axis=-1, keepdims=True) + EPS) * hg_norm_w.astype(f32)
    o_h = o_h.reshape(bsz, t, H_HG * DV_HG) * jax.nn.silu(g_h.astype(f32))
    mix = jnp.concatenate([o_r, o_h], axis=-1).astype(h.dtype)
    out = jnp.einsum('bte,ed->btd', mix, w_out)
    return out, s_ret_new.astype(s_ret.dtype), s_hg_new.astype(s_hg.dtype)


def _odd_mixer(h, s_ssm, s_conv, s_re, s_im, w_in, w_out, conv_w, conv_b, dt_bias, a_log, d_ssm,
               ssm_norm_w, lam_re, lam_im, log_step, b_re, b_im, c_re, c_im, s5_d, w_glu, b_glu):
    f32 = jnp.float32
    bsz, t, _ = h.shape
    proj = jnp.einsum('btd,de->bte', h, w_in)
    z, xbc, dt, u = _split(proj, [D_INNER, CONV_DIM, H_SSD, S5_WIDTH])
    xbc_in = jnp.concatenate([s_conv.astype(f32), xbc.astype(f32)], axis=1)
    conv = lax.conv_general_dilated(
        xbc_in, conv_w.astype(f32)[:, None, :], window_strides=(1,), padding='VALID',
        dimension_numbers=('NWC', 'WIO', 'NWC'), feature_group_count=CONV_DIM)
    conv_new = xbc_in[:, -(CONV_W - 1):]
    xbc = jax.nn.silu(conv + conv_b.astype(f32))
    xs, bm, cm = _split(xbc, [D_INNER, G_SSD * N_SSD, G_SSD * N_SSD])
    xs = xs.reshape(bsz, t, H_SSD, P_SSD)
    bm = jnp.repeat(bm.reshape(bsz, t, G_SSD, N_SSD), H_SSD // G_SSD, axis=2)
    cm = jnp.repeat(cm.reshape(bsz, t, G_SSD, N_SSD), H_SSD // G_SSD, axis=2)
    dt = jax.nn.softplus(dt.astype(f32) + dt_bias.astype(f32))
    a = -jnp.exp(a_log.astype(f32))
    y, s_ssm_new = _scalar_decay_recurrence(cm, bm, xs * dt[..., None], dt * a, s_ssm.astype(f32), CHUNK)
    y = y + d_ssm.astype(f32)[:, None] * xs
    y = y.reshape(bsz, t, D_INNER) * jax.nn.silu(z.astype(f32))
    yg = y.reshape(bsz, t, G_SSD, D_INNER // G_SSD)
    yg = yg * lax.rsqrt(jnp.mean(yg * yg, axis=-1, keepdims=True) + EPS)
    y = yg.reshape(bsz, t, D_INNER) * ssm_norm_w.astype(f32)
    yd, re_new, im_new = _s5(u.astype(f32).reshape(bsz, t, S5_G, S5_GS), s_re, s_im, lam_re, lam_im,
                             log_step, b_re, b_im, c_re, c_im, s5_d)
    g = jax.nn.gelu(yd.reshape(bsz, t, S5_WIDTH))
    yd = g * jax.nn.sigmoid(jnp.einsum('bte,ef->btf', g, w_glu.astype(f32)) + b_glu.astype(f32))
    mix = jnp.concatenate([y, yd], axis=-1).astype(h.dtype)
    out = jnp.einsum('bte,ed->btd', mix, w_out)
    return (out, s_ssm_new.astype(s_ssm.dtype), conv_new.astype(s_conv.dtype),
            re_new.astype(s_re.dtype), im_new.astype(s_im.dtype))


def _ffn(h, w_up, w_down):
    return jnp.einsum('btf,fd->btd', jnp.square(jax.nn.relu(jnp.einsum('btd,df->btf', h, w_up))), w_down)


def _forward(x, pos_offset, s_ret, s_hg, s_ssm, s_conv, s_re, s_im, p):
    t = x.shape[1]
    pos = jnp.arange(pos_offset, pos_offset + t, dtype=jnp.float32)
    lb_all = jnp.cumsum(jax.nn.softmax(p['hgrn_lower_bounds'].astype(jnp.float32), axis=0), axis=0)
    ret_l, hg_l, ssm_l, conv_l, re_l, im_l = [], [], [], [], [], []
    h = x
    for layer in range(DEPTH):
        i = layer // 2
        hn = _rmsnorm(h, p['norm_mix_pre'][layer])
        if layer % 2 == 0:
            mix, sr, sh = _even_mixer(hn, pos, s_ret[i], s_hg[i], p['w_in_even'][i], p['w_out_even'][i],
                                      p['ret_norm_w'][i], p['hgrn_norm_w'][i], lb_all[i])
            ret_l.append(sr)
            hg_l.append(sh)
        else:
            mix, ss, sc, sre, sim = _odd_mixer(
                hn, s_ssm[i], s_conv[i], s_re[i], s_im[i], p['w_in_odd'][i], p['w_out_odd'][i],
                p['conv_w'][i], p['conv_b'][i], p['dt_bias'][i], p['a_log'][i], p['d_ssm'][i],
                p['ssm_norm_w'][i], p['s5_lam_re'][i], p['s5_lam_im'][i], p['s5_log_step'][i],
                p['s5_b_re'][i], p['s5_b_im'][i], p['s5_c_re'][i], p['s5_c_im'][i], p['s5_d'][i],
                p['w_glu'][i], p['b_glu'][i])
            ssm_l.append(ss)
            conv_l.append(sc)
            re_l.append(sre)
            im_l.append(sim)
        h = h + _rmsnorm(mix, p['norm_mix_post'][layer])
        ff = _ffn(_rmsnorm(h, p['norm_ffn_pre'][layer]), p['w_ffn_up'][layer], p['w_ffn_down'][layer])
        h = h + _rmsnorm(ff, p['norm_ffn_post'][layer])
    return (h, jnp.stack(ret_l), jnp.stack(hg_l), jnp.stack(ssm_l), jnp.stack(conv_l),
            jnp.stack(re_l), jnp.stack(im_l))


def setup_inputs(seed: int = 0) -> dict:
    key = jax.random.key(seed)
    ks = iter(jax.random.split(key, 48))
    f32 = jnp.float32

    def nrm(shape, scale):
        return jax.random.normal(next(ks), shape, f32) * scale

    def unif(shape, lo, hi):
        return jax.random.uniform(next(ks), shape, f32, lo, hi)

    dt0 = jnp.exp(unif((N_ODD, H_SSD), math.log(1e-3), math.log(1e-1)))
    return {
        'x_prompt': nrm((BATCH, SEQ, D_MODEL), 1.0),
        'x_sample': nrm((DEC_BATCH, DEC_SEQ, D_MODEL), 1.0),
        'state_ret': nrm((N_EVEN, DEC_BATCH, H_RET, DK_RET, DV_RET), 0.5),
        'state_hgrn': nrm((N_EVEN, DEC_BATCH, H_HG, DK_HG, DV_HG), 0.5),
        'state_ssm': nrm((N_ODD, DEC_BATCH, H_SSD, N_SSD, P_SSD), 0.5),
        'state_conv': nrm((N_ODD, DEC_BATCH, CONV_W - 1, CONV_DIM), 1.0),
        'state_s5_re': nrm((N_ODD, DEC_BATCH, S5_G, S5_P), 0.5),
        'state_s5_im': nrm((N_ODD, DEC_BATCH, S5_G, S5_P), 0.5),
        'norm_mix_pre': 1.0 + nrm((DEPTH, D_MODEL), 0.05),
        'norm_mix_post': 1.0 + nrm((DEPTH, D_MODEL), 0.05),
        'norm_ffn_pre': 1.0 + nrm((DEPTH, D_MODEL), 0.05),
        'norm_ffn_post': 1.0 + nrm((DEPTH, D_MODEL), 0.05),
        'w_in_even': nrm((N_EVEN, D_MODEL, IN_EVEN), D_MODEL ** -0.5),
        'w_out_even': nrm((N_EVEN, MIX_EVEN, D_MODEL), MIX_EVEN ** -0.5),
        'ret_norm_w': 1.0 + nrm((N_EVEN, H_RET, DV_RET), 0.05),
        'hgrn_lower_bounds': nrm((N_EVEN + 1, H_HG * DK_HG), 0.1),
        'hgrn_norm_w': 1.0 + nrm((N_EVEN, H_HG, DV_HG), 0.05),
        'w_in_odd': nrm((N_ODD, D_MODEL, IN_ODD), D_MODEL ** -0.5),
        'conv_w': nrm((N_ODD, CONV_W, CONV_DIM), CONV_W ** -0.5),
        'conv_b': nrm((N_ODD, CONV_DIM), 0.02),
        'dt_bias': dt0 + jnp.log(-jnp.expm1(-dt0)),
        'a_log': jnp.log(unif((N_ODD, H_SSD), 1.0, 16.0)),
        'd_ssm': 1.0 + nrm((N_ODD, H_SSD), 0.05),
        'ssm_norm_w': 1.0 + nrm((N_ODD, D_INNER), 0.05),
        's5_lam_re': -0.5 + nrm((N_ODD, S5_G, S5_P), 0.01),
        's5_lam_im': math.pi * jnp.arange(S5_P, dtype=f32) + nrm((N_ODD, S5_G, S5_P), 0.01),
        's5_log_step': unif((N_ODD, S5_G), math.log(1e-3), math.log(1e-1)),
        's5_b_re': nrm((N_ODD, S5_G, S5_P, S5_GS), (2 * S5_GS) ** -0.5),
        's5_b_im': nrm((N_ODD, S5_G, S5_P, S5_GS), (2 * S5_GS) ** -0.5),
        's5_c_re': nrm((N_ODD, S5_G, S5_GS, S5_P), (2 * S5_P) ** -0.5),
        's5_c_im': nrm((N_ODD, S5_G, S5_GS, S5_P), (2 * S5_P) ** -0.5),
        's5_d': nrm((N_ODD, S5_WIDTH), 0.5),
        'w_glu': nrm((N_ODD, S5_WIDTH, S5_WIDTH), S5_WIDTH ** -0.5),
        'b_glu': nrm((N_ODD, S5_WIDTH), 0.02),
        'w_out_odd': nrm((N_ODD, MIX_ODD, D_MODEL), MIX_ODD ** -0.5),
        'w_ffn_up': nrm((DEPTH, D_MODEL, D_FF), D_MODEL ** -0.5),
        'w_ffn_down': nrm((DEPTH, D_FF, D_MODEL), D_FF ** -0.5),
    }


def reference(x_prompt, x_sample, state_ret, state_hgrn, state_ssm, state_conv, state_s5_re, state_s5_im,
              norm_mix_pre, norm_mix_post, norm_ffn_pre, norm_ffn_post, w_in_even, w_out_even, ret_norm_w,
              hgrn_lower_bounds, hgrn_norm_w, w_in_odd, conv_w, conv_b, dt_bias, a_log, d_ssm, ssm_norm_w,
              s5_lam_re, s5_lam_im, s5_log_step, s5_b_re, s5_b_im, s5_c_re, s5_c_im, s5_d, w_glu, b_glu,
              w_out_odd, w_ffn_up, w_ffn_down):
    p = dict(norm_mix_pre=norm_mix_pre, norm_mix_post=norm_mix_post, norm_ffn_pre=norm_ffn_pre,
             norm_ffn_post=norm_ffn_post, w_in_even=w_in_even, w_out_even=w_out_even, ret_norm_w=ret_norm_w,
             hgrn_lower_bounds=hgrn_lower_bounds, hgrn_norm_w=hgrn_norm_w, w_in_odd=w_in_odd, conv_w=conv_w,
             conv_b=conv_b, dt_bias=dt_bias, a_log=a_log, d_ssm=d_ssm, ssm_norm_w=ssm_norm_w,
             s5_lam_re=s5_lam_re, s5_lam_im=s5_lam_im, s5_log_step=s5_log_step, s5_b_re=s5_b_re,
             s5_b_im=s5_b_im, s5_c_re=s5_c_re, s5_c_im=s5_c_im, s5_d=s5_d, w_glu=w_glu, b_glu=b_glu,
             w_out_odd=w_out_odd, w_ffn_up=w_ffn_up, w_ffn_down=w_ffn_down)
    bp = x_prompt.shape[0]
    z_ret = jnp.zeros((N_EVEN, bp, H_RET, DK_RET, DV_RET), state_ret.dtype)
    z_hg = jnp.zeros((N_EVEN, bp, H_HG, DK_HG, DV_HG), state_hgrn.dtype)
    z_ssm = jnp.zeros((N_ODD, bp, H_SSD, N_SSD, P_SSD), state_ssm.dtype)
    z_conv = jnp.zeros((N_ODD, bp, CONV_W - 1, CONV_DIM), state_conv.dtype)
    z_re = jnp.zeros((N_ODD, bp, S5_G, S5_P), state_s5_re.dtype)
    z_im = jnp.zeros((N_ODD, bp, S5_G, S5_P), state_s5_im.dtype)
    y_prompt, ret_p, hgrn_p, ssm_p, conv_p, s5re_p, s5im_p = _forward(
        x_prompt, 0, z_ret, z_hg, z_ssm, z_conv, z_re, z_im, p)
    y_sample, ret_s, hgrn_s, ssm_s, conv_s, s5re_s, s5im_s = _forward(
        x_sample, PAST_LEN, state_ret, state_hgrn, state_ssm, state_conv, state_s5_re, state_s5_im, p)
    return (y_prompt, y_sample, ret_p, ret_s, hgrn_p, hgrn_s, ssm_p, ssm_s, conv_p, conv_s,
            s5re_p, s5re_s, s5im_p, s5im_s)
```

```python
import functools
import math

import jax
import jax.numpy as jnp
from jax import lax
from jax.experimental import pallas as pl
from jax.experimental.pallas import tpu as pltpu

F32 = jnp.float32
BF16 = jnp.bfloat16
EPS = 1e-6
ROPE_BASE = 10000.0
LANES = 128
VMEM_LIMIT = 56 * 1024 * 1024

H_RET = 4
H_HG = 4
HD = 128
H_SSD = 16
P_SSD = 64
N_SSD = 128
G_SSD = 2
S5_G = 32
S5_GS = 16
S5_P = 64
HG_BLK = 16


def _rms(x, w):
    return x * lax.rsqrt(jnp.mean(x * x, axis=-1, keepdims=True) + EPS) * w


def _dot(a, b):
    return jnp.dot(a.astype(BF16), b.astype(BF16), preferred_element_type=F32)


def _dot_nt(a, b):
    return lax.dot_general(a.astype(BF16), b.astype(BF16), (((1,), (1,)), ((), ())),
                           preferred_element_type=F32)


def _dot_tn(a, b):
    return lax.dot_general(a.astype(BF16), b.astype(BF16), (((0,), (0,)), ((), ())),
                           preferred_element_type=F32)


def _split3(x):
    hi = x.astype(BF16)
    r1 = x - hi.astype(F32)
    mid = r1.astype(BF16)
    lo = (r1 - mid.astype(F32)).astype(BF16)
    return hi, mid, lo


def _dot3(a_exact, parts):
    acc = None
    for p in parts:
        d = jnp.dot(a_exact, p, preferred_element_type=F32)
        acc = d if acc is None else acc + d
    return acc


def _dot3_tn(parts, b_exact):
    acc = None
    for p in parts:
        d = lax.dot_general(p, b_exact, (((0,), (0,)), ((), ())), preferred_element_type=F32)
        acc = d if acc is None else acc + d
    return acc


def _tri(c, upper=False):
    r = lax.broadcasted_iota(jnp.int32, (c, c), 0)
    s = lax.broadcasted_iota(jnp.int32, (c, c), 1)
    return jnp.where((r <= s) if upper else (r >= s), 1.0, 0.0).astype(BF16)


def _const_spec(shape):
    nd = len(shape)
    return pl.BlockSpec(shape, lambda *_: (0,) * nd, pipeline_mode=pl.Buffered(1))


def _post_kernel(n_mix, ff_chunk, h_ref, *refs):
    mix_refs = refs[:n_mix]
    wout_refs = refs[n_mix:2 * n_mix]
    npost_ref, nfpre_ref, nfpost_ref, wup_ref, wdn_ref, o_ref = refs[2 * n_mix:]
    acc = None
    for m_ref, w_ref in zip(mix_refs, wout_refs):
        d = jnp.dot(m_ref[...], w_ref[...], preferred_element_type=F32)
        acc = d if acc is None else acc + d
    h1 = h_ref[...] + _rms(acc, npost_ref[...])
    hn = _rms(h1, nfpre_ref[...]).astype(BF16)
    ff = None
    for j in range(wup_ref.shape[1] // ff_chunk):
        sl = slice(j * ff_chunk, (j + 1) * ff_chunk)
        up = jnp.dot(hn, wup_ref[:, sl], preferred_element_type=F32)
        act = jnp.square(jnp.maximum(up, 0.0)).astype(BF16)
        d = jnp.dot(act, wdn_ref[sl, :], preferred_element_type=F32)
        ff = d if ff is None else ff + d
    o_ref[...] = h1 + _rms(ff, nfpost_ref[...])


def _post_call(h, mixes, wouts, npost, nfpre, nfpost, wup, wdn, rows):
    n, d = h.shape
    n_mix = len(mixes)
    row_spec = lambda w: pl.BlockSpec((rows, w), lambda i: (i, 0))
    in_specs = ([row_spec(d)] + [row_spec(m.shape[1]) for m in mixes]
                + [_const_spec(w.shape) for w in wouts]
                + [_const_spec((1, d))] * 3 + [_const_spec(wup.shape), _const_spec(wdn.shape)])
    return pl.pallas_call(
        functools.partial(_post_kernel, n_mix, 1024),
        grid=(n // rows,),
        in_specs=in_specs,
        out_specs=row_spec(d),
        out_shape=jax.ShapeDtypeStruct((n, d), F32),
        compiler_params=pltpu.CompilerParams(dimension_semantics=("arbitrary",),
                                             vmem_limit_bytes=VMEM_LIMIT),
        name="post_ffn",
    )(h, *mixes, *wouts, npost, nfpre, nfpost, wup, wdn)


def _retention_head(q, k, v, s, lg, c):
    ti = lax.broadcasted_iota(jnp.int32, (c, c), 0)
    si = lax.broadcasted_iota(jnp.int32, (c, c), 1)
    dm = jnp.where(ti >= si, jnp.exp((ti - si).astype(F32) * lg), 0.0)
    tl = lax.broadcasted_iota(jnp.int32, (c, HD), 0).astype(F32)
    qdec = jnp.exp((tl + 1.0) * lg)
    kdec = jnp.exp((float(c - 1) - tl) * lg)
    a = _dot_nt(q, k) * dm
    o = _dot(a, v) + _dot(q, s) * qdec
    s_new = s * math.exp(c * lg) + _dot_tn(k * kdec, v)
    return o, s_new


def _hgrn_head(q, k, v, cum, lf_parts, s, c):
    blk = min(HG_BLK, c)
    cl = cum[c - 1:c, :]
    o_state = _dot(q * jnp.exp(cum), s)
    dcol = _dot3_tn(lf_parts, jnp.ones((c, HD), BF16))
    s_new = s * jnp.exp(dcol) + _dot_tn(k * jnp.exp(cl - cum), v)
    outs = []
    for i in range(c // blk):
        lo, hi = i * blk, (i + 1) * blk
        ref = cum[lo + blk // 2 - 1:lo + blk // 2, :]
        qh = q[lo:hi] * jnp.exp(cum[lo:hi] - ref)
        kh = k[:hi] * jnp.exp(ref - cum[:hi])
        a = _dot_nt(qh, kh)
        ti = lax.broadcasted_iota(jnp.int32, (blk, hi), 0) + lo
        si = lax.broadcasted_iota(jnp.int32, (blk, hi), 1)
        a = jnp.where(ti >= si, a, 0.0)
        outs.append(_dot(a, v[:hi]))
    o = outs[0] if len(outs) == 1 else jnp.concatenate(outs, axis=0)
    return o + o_state, s_new


def _even_kernel(has_state, x_ref, cos_ref, sin_ref, npre_ref, win_ref, retw_ref, hgw_ref,
                 lb_ref, *refs):
    if has_state:
        sret_ref, shg_ref, mix_ref, oret_ref, ohg_ref, proj_s = refs
    else:
        mix_ref, oret_ref, ohg_ref, proj_s = refs
    bb, c, d = x_ref.shape
    n_in = win_ref.shape[1]

    @pl.when(pl.program_id(1) == 0)
    def _():
        if has_state:
            oret_ref[...] = sret_ref[...]
            ohg_ref[...] = shg_ref[...]
        else:
            oret_ref[...] = jnp.zeros(oret_ref.shape, F32)
            ohg_ref[...] = jnp.zeros(ohg_ref.shape, F32)

    hn = _rms(x_ref[...].reshape(bb * c, d), npre_ref[...]).astype(BF16)
    for j in range(n_in // 512):
        sl = slice(j * 512, (j + 1) * 512)
        proj_s[:, :, sl] = jnp.dot(hn, win_ref[:, sl], preferred_element_type=F32).reshape(bb, c, 512)

    cos = cos_ref[...]
    sin = sin_ref[...]
    tri = _tri(c)
    lb = lb_ref[...]
    hw = H_RET * HD

    def seq_body(b, carry):
        for h in range(H_RET):
            col = lambda base: slice(base + h * HD, base + (h + 1) * HD)
            q = proj_s[b, :, col(0)]
            k = proj_s[b, :, col(hw)]
            v = proj_s[b, :, col(2 * hw)]
            g = proj_s[b, :, col(3 * hw)]
            q = q * cos + pltpu.roll(q, HD // 2, axis=1) * sin
            k = (k * cos + pltpu.roll(k, HD // 2, axis=1) * sin) * (HD ** -0.5)
            lg = math.log1p(-(2.0 ** (-5.0 - h)))
            o, s_new = _retention_head(q, k, v, oret_ref[b, h], lg, c)
            oret_ref[b, h] = s_new
            mu = jnp.mean(o, axis=-1, keepdims=True)
            oc = o - mu
            var = jnp.mean(oc * oc, axis=-1, keepdims=True)
            o = oc * lax.rsqrt(var + EPS) * retw_ref[:, col(0)]
            mix_ref[b, :, col(0)] = (o * (g * jax.nn.sigmoid(g))).astype(mix_ref.dtype)
        base = 4 * hw
        f = lb + (1.0 - lb) * jax.nn.sigmoid(proj_s[b, :, base + hw:base + 2 * hw])
        lf = jnp.log(f)
        lf_parts = _split3(lf)
        cum = _dot3(tri, lf_parts)
        for h in range(H_HG):
            hs = slice(h * HD, (h + 1) * HD)
            col = lambda off: slice(base + off + h * HD, base + off + (h + 1) * HD)
            q = proj_s[b, :, col(0)]
            v = proj_s[b, :, col(2 * hw)]
            g = proj_s[b, :, col(3 * hw)]
            o, s_new = _hgrn_head(q, 1.0 - f[:, hs], v, cum[:, hs],
                                  tuple(p[:, hs] for p in lf_parts), ohg_ref[b, h], c)
            ohg_ref[b, h] = s_new
            o = _rms(o, hgw_ref[:, hs])
            mix_ref[b, :, hw + h * HD:hw + (h + 1) * HD] = (
                o * (g * jax.nn.sigmoid(g))).astype(mix_ref.dtype)
        return carry

    if bb == 1:
        seq_body(0, 0)
    else:
        lax.fori_loop(0, bb, seq_body, 0)


def _even_call(x, cos, sin, npre, win, retw, hgw, lb, states, bb, c):
    nb_total, t, d = x.shape
    has_state = states is not None
    n_in = win.shape[1]
    grid = (nb_total // bb, t // c)
    st_spec = pl.BlockSpec((bb, H_RET, HD, HD), lambda b, i: (b, 0, 0, 0))
    in_specs = [pl.BlockSpec((bb, c, d), lambda b, i: (b, i, 0)),
                pl.BlockSpec((c, HD), lambda b, i: (i, 0)),
                pl.BlockSpec((c, HD), lambda b, i: (i, 0)),
                _const_spec((1, d)), _const_spec(win.shape),
                _const_spec((1, H_RET * HD)), _const_spec((1, H_HG * HD)), _const_spec((1, H_HG * HD))]
    args = [x, cos, sin, npre, win, retw, hgw, lb]
    if has_state:
        in_specs += [st_spec, st_spec]
        args += list(states)
    mix_w = (H_RET + H_HG) * HD
    return pl.pallas_call(
        functools.partial(_even_kernel, has_state),
        grid=grid,
        in_specs=in_specs,
        out_specs=[pl.BlockSpec((bb, c, mix_w), lambda b, i: (b, i, 0)), st_spec, st_spec],
        out_shape=[jax.ShapeDtypeStruct((nb_total, t, mix_w), BF16),
                   jax.ShapeDtypeStruct((nb_total, H_RET, HD, HD), F32),
                   jax.ShapeDtypeStruct((nb_total, H_HG, HD, HD), F32)],
        scratch_shapes=[pltpu.VMEM((bb, c, n_in), F32)],
        compiler_params=pltpu.CompilerParams(dimension_semantics=("arbitrary", "arbitrary"),
                                             vmem_limit_bytes=VMEM_LIMIT),
        name="even_mixer",
    )(*args)


CONV_PAD = 8


def _odd_kernel(has_state, x_ref, npre_ref, win_ref, convw_ref, convb_ref, dtb_ref, alog_ref,
                dssm_ref, ssmw_ref, *refs):
    if has_state:
        sconv_ref, sssm_ref, y_ref, u_ref, oconv_ref, ossm_ref, proj_s, xin_s, y_s = refs
    else:
        y_ref, u_ref, oconv_ref, ossm_ref, proj_s, xin_s, y_s = refs
    bb, c, d = x_ref.shape
    n_in = win_ref.shape[1]
    d_inner = H_SSD * P_SSD
    conv_dim = d_inner + 2 * G_SSD * N_SSD
    conv_w = convw_ref.shape[0]
    halo = conv_w - 1
    u_w = u_ref.shape[2]

    @pl.when(pl.program_id(1) == 0)
    def _():
        if has_state:
            oconv_ref[...] = sconv_ref[...]
            ossm_ref[...] = sssm_ref[...]
        else:
            oconv_ref[...] = jnp.zeros(oconv_ref.shape, F32)
            ossm_ref[...] = jnp.zeros(ossm_ref.shape, F32)

    hn = _rms(x_ref[...].reshape(bb * c, d), npre_ref[...]).astype(BF16)
    col = 0
    while col < n_in:
        w = min(512, n_in - col)
        proj_s[:, :, col:col + w] = jnp.dot(hn, win_ref[:, col:col + w],
                                            preferred_element_type=F32).reshape(bb, c, w)
        col += w

    tri = _tri(c)
    triu = _tri(c, upper=True)
    ti = lax.broadcasted_iota(jnp.int32, (c, c), 0)
    si = lax.broadcasted_iota(jnp.int32, (c, c), 1)
    causal = ti >= si
    a_row = -jnp.exp(alog_ref[...])
    heads_per_group = H_SSD // G_SSD

    def seq_body(b, carry):
        z = proj_s[b, :, 0:d_inner]
        xbc = proj_s[b, :, d_inner:d_inner + conv_dim]
        u_ref[b] = proj_s[b, :, d_inner + conv_dim:d_inner + conv_dim + u_w]
        dt_raw = proj_s[b, :, d_inner + conv_dim + u_w:d_inner + conv_dim + u_w + LANES]
        xin_s[b, CONV_PAD - halo:CONV_PAD, :] = oconv_ref[b]
        xin_s[b, CONV_PAD:CONV_PAD + c, :] = xbc
        conv = convb_ref[...]
        for j in range(conv_w):
            conv = conv + xin_s[b, CONV_PAD - halo + j:CONV_PAD - halo + j + c, :] * convw_ref[j:j + 1, :]
        oconv_ref[b] = xin_s[b, CONV_PAD + c - halo:CONV_PAD + c, :]
        act = conv * jax.nn.sigmoid(conv)
        xs = act[:, :d_inner]
        dt = jax.nn.softplus(dt_raw + dtb_ref[...])
        la = dt * a_row
        la_parts = _split3(la)
        cum = _dot3(tri, la_parts)
        cum_t = _dot3_tn(la_parts, triu)
        for g in range(G_SSD):
            bm = act[:, d_inner + g * N_SSD:d_inner + (g + 1) * N_SSD]
            cm = act[:, d_inner + (G_SSD + g) * N_SSD:d_inner + (G_SSD + g + 1) * N_SSD]
            scores = _dot_nt(cm, bm)
            for hh in range(heads_per_group):
                h = g * heads_per_group + hh
                ps = slice(h * P_SSD, (h + 1) * P_SSD)
                cum_c = cum[:, h:h + 1]
                cum_r = cum_t[h:h + 1, :]
                cl = cum[c - 1:c, h:h + 1]
                decay = jnp.where(causal, jnp.exp(jnp.where(causal, cum_c - cum_r, 0.0)), 0.0)
                xh = xs[:, ps]
                xdt = xh * dt[:, h:h + 1]
                s = ossm_ref[b, h]
                yh = (_dot(scores * decay, xdt) + _dot(cm, s) * jnp.exp(cum_c)
                      + dssm_ref[:, h:h + 1] * xh)
                ossm_ref[b, h] = s * jnp.exp(cl) + _dot_tn(bm * jnp.exp(cl - cum_c), xdt)
                y_s[:, ps] = yh
        y = y_s[...] * (z * jax.nn.sigmoid(z))
        gw = d_inner // G_SSD
        for g in range(G_SSD):
            gs = slice(g * gw, (g + 1) * gw)
            y_ref[b, :, gs] = _rms(y[:, gs], ssmw_ref[:, gs]).astype(y_ref.dtype)
        return carry

    if bb == 1:
        seq_body(0, 0)
    else:
        lax.fori_loop(0, bb, seq_body, 0)


def _odd_call(x, npre, win, convw, convb, dtb, alog, dssm, ssmw, states, bb, c):
    nb_total, t, d = x.shape
    has_state = states is not None
    n_in = win.shape[1]
    d_inner = H_SSD * P_SSD
    conv_dim = d_inner + 2 * G_SSD * N_SSD
    u_w = S5_G * S5_GS
    halo = convw.shape[0] - 1
    grid = (nb_total // bb, t // c)
    conv_spec = pl.BlockSpec((bb, halo, conv_dim), lambda b, i: (b, 0, 0))
    ssm_spec = pl.BlockSpec((bb, H_SSD, N_SSD, P_SSD), lambda b, i: (b, 0, 0, 0))
    in_specs = [pl.BlockSpec((bb, c, d), lambda b, i: (b, i, 0)),
                _const_spec((1, d)), _const_spec(win.shape), _const_spec(convw.shape),
                _const_spec((1, conv_dim)), _const_spec((1, LANES)), _const_spec((1, LANES)),
                _const_spec((1, LANES)), _const_spec((1, d_inner))]
    args = [x, npre, win, convw, convb, dtb, alog, dssm, ssmw]
    if has_state:
        in_specs += [conv_spec, ssm_spec]
        args += list(states)
    return pl.pallas_call(
        functools.partial(_odd_kernel, has_state),
        grid=grid,
        in_specs=in_specs,
        out_specs=[pl.BlockSpec((bb, c, d_inner), lambda b, i: (b, i, 0)),
                   pl.BlockSpec((bb, c, u_w), lambda b, i: (b, i, 0)),
                   conv_spec, ssm_spec],
        out_shape=[jax.ShapeDtypeStruct((nb_total, t, d_inner), BF16),
                   jax.ShapeDtypeStruct((nb_total, t, u_w), F32),
                   jax.ShapeDtypeStruct((nb_total, halo, conv_dim), F32),
                   jax.ShapeDtypeStruct((nb_total, H_SSD, N_SSD, P_SSD), F32)],
        scratch_shapes=[pltpu.VMEM((bb, c, n_in), F32),
                        pltpu.VMEM((bb, CONV_PAD + c, conv_dim), F32),
                        pltpu.VMEM((c, d_inner), F32)],
        compiler_params=pltpu.CompilerParams(dimension_semantics=("arbitrary", "arbitrary"),
                                             vmem_limit_bytes=VMEM_LIMIT),
        name="odd_mixer",
    )(*args)


S5_HALF_IN = S5_G * S5_GS // 2
S5_HALF_ST = S5_G * S5_P // 2
S5_SCAN_VREGS = 4


def _s5_kernel(u_ref, bmat_ref, cmat_ref, lre_ref, lim_ref, d_ref, wglu_ref, bglu_ref,
               h0re_ref, h0im_ref, yd_ref, ore_ref, oim_ref, buf_s):
    steps, r, uw = u_ref.shape
    hs = S5_HALF_ST
    scan_w = max(LANES, S5_SCAN_VREGS * 8 * LANES // r)

    @pl.when(pl.program_id(1) == 0)
    def _():
        ore_ref[...] = h0re_ref[...]
        oim_ref[...] = h0im_ref[...]

    u = u_ref[...].reshape(steps * r, uw)
    ub = u.astype(BF16)
    for half in range(2):
        buf_s[:, half * 2 * hs:(half + 1) * 2 * hs] = jnp.dot(
            ub[:, half * S5_HALF_IN:(half + 1) * S5_HALF_IN], bmat_ref[half],
            preferred_element_type=F32)

    for half in range(2):
        for j in range(hs // scan_w):
            st = slice(half * hs + j * scan_w, half * hs + (j + 1) * scan_w)
            cre = half * 2 * hs + j * scan_w
            cim = cre + hs
            lre = jnp.broadcast_to(lre_ref[:, st], (r, scan_w))
            lim = jnp.broadcast_to(lim_ref[:, st], (r, scan_w))

            def step(t, carry):
                hre, him = carry
                rows = pl.ds(pl.multiple_of(t * r, r), r)
                nre = lre * hre - lim * him + buf_s[rows, cre:cre + scan_w]
                nim = lre * him + lim * hre + buf_s[rows, cim:cim + scan_w]
                buf_s[rows, cre:cre + scan_w] = nre
                buf_s[rows, cim:cim + scan_w] = nim
                return nre, nim

            hre, him = lax.fori_loop(0, steps, step, (ore_ref[:, st], oim_ref[:, st]))
            ore_ref[:, st] = hre
            oim_ref[:, st] = him

    ys = []
    for half in range(2):
        ys.append(jnp.dot(buf_s[:, half * 2 * hs:(half + 1) * 2 * hs].astype(BF16), cmat_ref[half],
                          preferred_element_type=F32))
    y = jnp.concatenate(ys, axis=1) + d_ref[...] * u
    g = jax.nn.gelu(y)
    gate = jax.nn.sigmoid(jnp.dot(g.astype(BF16), wglu_ref[...], preferred_element_type=F32)
                          + bglu_ref[...])
    yd_ref[...] = (g * gate).reshape(steps, r, uw).astype(yd_ref.dtype)


def _s5_call(u_t, bmat, cmat, lre, lim, dvec, wglu, bglu, h0re, h0im, steps, r):
    t, nb_total, uw = u_t.shape
    ns = S5_G * S5_P
    grid = (nb_total // r, t // steps)
    st_spec = pl.BlockSpec((r, ns), lambda b, i: (b, 0))
    return pl.pallas_call(
        _s5_kernel,
        grid=grid,
        in_specs=[pl.BlockSpec((steps, r, uw), lambda b, i: (i, b, 0)),
                  _const_spec(bmat.shape), _const_spec(cmat.shape),
                  _const_spec((1, ns)), _const_spec((1, ns)), _const_spec((1, uw)),
                  _const_spec(wglu.shape), _const_spec((1, uw)), st_spec, st_spec],
        out_specs=[pl.BlockSpec((steps, r, uw), lambda b, i: (i, b, 0)), st_spec, st_spec],
        out_shape=[jax.ShapeDtypeStruct((t, nb_total, uw), BF16),
                   jax.ShapeDtypeStruct((nb_total, ns), F32),
                   jax.ShapeDtypeStruct((nb_total, ns), F32)],
        scratch_shapes=[pltpu.VMEM((steps * r, 2 * ns), F32)],
        compiler_params=pltpu.CompilerParams(dimension_semantics=("arbitrary", "arbitrary"),
                                             vmem_limit_bytes=VMEM_LIMIT),
        name="s5_mixer",
    )(u_t, bmat, cmat, lre, lim, dvec, wglu, bglu, h0re, h0im)


def _rope_tables(pos):
    half = HD // 2
    inv_freq = ROPE_BASE ** (-jnp.arange(half, dtype=F32) / half)
    ang = pos[:, None] * inv_freq[None, :]
    cos, sin = jnp.cos(ang), jnp.sin(ang)
    return jnp.concatenate([cos, cos], axis=-1), jnp.concatenate([-sin, sin], axis=-1)


def _pad_lanes(v):
    return jnp.pad(v.astype(F32), (0, LANES - v.shape[0]))[None, :]


def _s5_tables(lam_re, lam_im, log_step, b_re, b_im, c_re, c_im):
    lam = lax.complex(lam_re.astype(F32), lam_im.astype(F32))
    dt = jnp.exp(log_step.astype(F32))[:, None]
    lam_bar = jnp.exp(lam * dt)
    b_bar = ((lam_bar - 1.0) / lam)[..., None] * lax.complex(b_re.astype(F32), b_im.astype(F32))
    gh = S5_G // 2
    eye = jnp.eye(gh, dtype=F32)

    def in_mat(b):
        return jnp.einsum('gpc,gk->gckp', b, eye).reshape(gh * S5_GS, gh * S5_P)

    def out_mat(cc):
        return jnp.einsum('gcp,gk->gpkc', cc, eye).reshape(gh * S5_P, gh * S5_GS)

    bmats, cmats = [], []
    for half in range(2):
        sl = slice(half * gh, (half + 1) * gh)
        bmats.append(jnp.concatenate([in_mat(b_bar.real[sl]), in_mat(b_bar.imag[sl])], axis=1))
        cmats.append(jnp.concatenate([out_mat(c_re.astype(F32)[sl]), out_mat(-c_im.astype(F32)[sl])],
                                     axis=0))
    return (jnp.stack(bmats).astype(BF16), jnp.stack(cmats).astype(BF16),
            lam_bar.real.reshape(1, -1), lam_bar.imag.reshape(1, -1))


def _row(v):
    return v.astype(F32).reshape(1, -1)


def _forward(x, pos_offset, states, p, cfg):
    nb, t, d = x.shape
    n = nb * t
    has_state = states is not None
    pos = jnp.arange(pos_offset, pos_offset + t, dtype=F32)
    cos, sin = _rope_tables(pos)
    lb = jnp.cumsum(jax.nn.softmax(p['hgrn_lower_bounds'].astype(F32), axis=0), axis=0)[0]
    mix, ret_o, hg_o = _even_call(
        x, cos, sin, _row(p['norm_mix_pre'][0]), p['w_in_even'][0].astype(BF16),
        _row(p['ret_norm_w'][0]), _row(p['hgrn_norm_w'][0]), _row(lb),
        (states['ret'][0], states['hgrn'][0]) if has_state else None, cfg['bb'], cfg['c'])
    h = _post_call(x.reshape(n, d), [mix.reshape(n, -1)], [p['w_out_even'][0].astype(BF16)],
                   _row(p['norm_mix_post'][0]), _row(p['norm_ffn_pre'][0]),
                   _row(p['norm_ffn_post'][0]), p['w_ffn_up'][0].astype(BF16),
                   p['w_ffn_down'][0].astype(BF16), cfg['rows'])
    d_inner = H_SSD * P_SSD
    conv_dim = d_inner + 2 * G_SSD * N_SSD
    u_w = S5_G * S5_GS
    w = p['w_in_odd'][0]
    c0 = d_inner + conv_dim
    w1 = jnp.concatenate([w[:, :c0], w[:, c0 + H_SSD:], w[:, c0:c0 + H_SSD],
                          jnp.zeros((d, LANES - H_SSD), w.dtype)], axis=1).astype(BF16)
    y, u, conv_o, ssm_o = _odd_call(
        h.reshape(nb, t, d), _row(p['norm_mix_pre'][1]), w1, p['conv_w'][0].astype(F32),
        _row(p['conv_b'][0]), _pad_lanes(p['dt_bias'][0]), _pad_lanes(p['a_log'][0]),
        _pad_lanes(p['d_ssm'][0]), _row(p['ssm_norm_w'][0]),
        (states['conv'][0], states['ssm'][0]) if has_state else None, cfg['bb'], cfg['c'])
    bmat, cmat, lre, lim = _s5_tables(p['s5_lam_re'][0], p['s5_lam_im'][0], p['s5_log_step'][0],
                                      p['s5_b_re'][0], p['s5_b_im'][0], p['s5_c_re'][0],
                                      p['s5_c_im'][0])
    ns = S5_G * S5_P
    if has_state:
        h0re = states['s5_re'][0].reshape(nb, ns)
        h0im = states['s5_im'][0].reshape(nb, ns)
    else:
        h0re = jnp.zeros((nb, ns), F32)
        h0im = jnp.zeros((nb, ns), F32)
    yd_t, re_o, im_o = _s5_call(jnp.swapaxes(u, 0, 1), bmat, cmat, lre, lim, _row(p['s5_d'][0]),
                                p['w_glu'][0].astype(BF16), _row(p['b_glu'][0]), h0re, h0im,
                                cfg['s5_steps'], cfg['s5_rows'])
    yd = jnp.swapaxes(yd_t, 0, 1).reshape(n, u_w)
    wo = p['w_out_odd'][0].astype(BF16)
    h = _post_call(h, [y.reshape(n, d_inner), yd], [wo[:d_inner], wo[d_inner:]],
                   _row(p['norm_mix_post'][1]), _row(p['norm_ffn_pre'][1]),
                   _row(p['norm_ffn_post'][1]), p['w_ffn_up'][1].astype(BF16),
                   p['w_ffn_down'][1].astype(BF16), cfg['rows'])
    return (h.reshape(nb, t, d), ret_o[None], hg_o[None], ssm_o[None], conv_o[None],
            re_o.reshape(1, nb, S5_G, S5_P), im_o.reshape(1, nb, S5_G, S5_P))


PROMPT_CFG = dict(bb=1, c=128, rows=512, s5_steps=32, s5_rows=8)
SAMPLE_CFG = dict(bb=8, c=8, rows=512, s5_steps=8, s5_rows=32)


def kernel(x_prompt, x_sample, state_ret, state_hgrn, state_ssm, state_conv, state_s5_re, state_s5_im,
           norm_mix_pre, norm_mix_post, norm_ffn_pre, norm_ffn_post, w_in_even, w_out_even, ret_norm_w,
           hgrn_lower_bounds, hgrn_norm_w, w_in_odd, conv_w, conv_b, dt_bias, a_log, d_ssm, ssm_norm_w,
           s5_lam_re, s5_lam_im, s5_log_step, s5_b_re, s5_b_im, s5_c_re, s5_c_im, s5_d, w_glu, b_glu,
           w_out_odd, w_ffn_up, w_ffn_down):
    p = dict(norm_mix_pre=norm_mix_pre, norm_mix_post=norm_mix_post, norm_ffn_pre=norm_ffn_pre,
             norm_ffn_post=norm_ffn_post, w_in_even=w_in_even, w_out_even=w_out_even, ret_norm_w=ret_norm_w,
             hgrn_lower_bounds=hgrn_lower_bounds, hgrn_norm_w=hgrn_norm_w, w_in_odd=w_in_odd, conv_w=conv_w,
             conv_b=conv_b, dt_bias=dt_bias, a_log=a_log, d_ssm=d_ssm, ssm_norm_w=ssm_norm_w,
             s5_lam_re=s5_lam_re, s5_lam_im=s5_lam_im, s5_log_step=s5_log_step, s5_b_re=s5_b_re,
             s5_b_im=s5_b_im, s5_c_re=s5_c_re, s5_c_im=s5_c_im, s5_d=s5_d, w_glu=w_glu, b_glu=b_glu,
             w_out_odd=w_out_odd, w_ffn_up=w_ffn_up, w_ffn_down=w_ffn_down)
    past_len = 16384
    states = dict(ret=state_ret, hgrn=state_hgrn, ssm=state_ssm, conv=state_conv,
                  s5_re=state_s5_re, s5_im=state_s5_im)
    y_p, ret_p, hg_p, ssm_p, conv_p, re_p, im_p = _forward(x_prompt, 0, None, p, PROMPT_CFG)
    y_s, ret_s, hg_s, ssm_s, conv_s, re_s, im_s = _forward(x_sample, past_len, states, p, SAMPLE_CFG)
    return (y_p, y_s, ret_p, ret_s, hg_p, hg_s, ssm_p, ssm_s, conv_p, conv_s,
            re_p, re_s, im_p, im_s)
```

```python
import functools
import math

import jax
import jax.numpy as jnp
from jax import lax
from jax.experimental import pallas as pl
from jax.experimental.pallas import tpu as pltpu

F32 = jnp.float32
BF16 = jnp.bfloat16
EPS = 1e-6
ROPE_BASE = 10000.0
LANES = 128
VMEM_LIMIT = 56 * 1024 * 1024

H_RET = 4
H_HG = 4
HD = 128
H_SSD = 16
P_SSD = 64
N_SSD = 128
G_SSD = 2
S5_G = 32
S5_GS = 16
S5_P = 64
HG_BLK = 16


def _rms(x, w):
    return x * lax.rsqrt(jnp.mean(x * x, axis=-1, keepdims=True) + EPS) * w


def _dot(a, b):
    return jnp.dot(a.astype(BF16), b.astype(BF16), preferred_element_type=F32)


def _dot_nt(a, b):
    return lax.dot_general(a.astype(BF16), b.astype(BF16), (((1,), (1,)), ((), ())),
                           preferred_element_type=F32)


def _dot_tn(a, b):
    return lax.dot_general(a.astype(BF16), b.astype(BF16), (((0,), (0,)), ((), ())),
                           preferred_element_type=F32)


def _split3(x):
    hi = x.astype(BF16)
    r1 = x - hi.astype(F32)
    mid = r1.astype(BF16)
    lo = (r1 - mid.astype(F32)).astype(BF16)
    return hi, mid, lo


def _dot3(a_exact, parts):
    acc = None
    for p in parts:
        d = jnp.dot(a_exact, p, preferred_element_type=F32)
        acc = d if acc is None else acc + d
    return acc


def _dot3_tn(parts, b_exact):
    acc = None
    for p in parts:
        d = lax.dot_general(p, b_exact, (((0,), (0,)), ((), ())), preferred_element_type=F32)
        acc = d if acc is None else acc + d
    return acc


def _tri(c, upper=False):
    r = lax.broadcasted_iota(jnp.int32, (c, c), 0)
    s = lax.broadcasted_iota(jnp.int32, (c, c), 1)
    return jnp.where((r <= s) if upper else (r >= s), 1.0, 0.0).astype(BF16)


def _const_spec(shape):
    nd = len(shape)
    return pl.BlockSpec(shape, lambda *_: (0,) * nd, pipeline_mode=pl.Buffered(1))


def _post_kernel(n_mix, ff_chunk, h_ref, *refs):
    mix_refs = refs[:n_mix]
    wout_refs = refs[n_mix:2 * n_mix]
    npost_ref, nfpre_ref, nfpost_ref, wup_ref, wdn_ref, o_ref = refs[2 * n_mix:]
    acc = None
    for m_ref, w_ref in zip(mix_refs, wout_refs):
        d = jnp.dot(m_ref[...], w_ref[...], preferred_element_type=F32)
        acc = d if acc is None else acc + d
    h1 = h_ref[...] + _rms(acc, npost_ref[...])
    hn = _rms(h1, nfpre_ref[...]).astype(BF16)
    ff = None
    for j in range(wup_ref.shape[1] // ff_chunk):
        sl = slice(j * ff_chunk, (j + 1) * ff_chunk)
        up = jnp.dot(hn, wup_ref[:, sl], preferred_element_type=F32)
        act = jnp.square(jnp.maximum(up, 0.0)).astype(BF16)
        d = jnp.dot(act, wdn_ref[sl, :], preferred_element_type=F32)
        ff = d if ff is None else ff + d
    o_ref[...] = h1 + _rms(ff, nfpost_ref[...])


def _post_call(h, mixes, wouts, npost, nfpre, nfpost, wup, wdn, rows):
    n, d = h.shape
    n_mix = len(mixes)
    row_spec = lambda w: pl.BlockSpec((rows, w), lambda i: (i, 0))
    in_specs = ([row_spec(d)] + [row_spec(m.shape[1]) for m in mixes]
                + [_const_spec(w.shape) for w in wouts]
                + [_const_spec((1, d))] * 3 + [_const_spec(wup.shape), _const_spec(wdn.shape)])
    return pl.pallas_call(
        functools.partial(_post_kernel, n_mix, 1024),
        grid=(n // rows,),
        in_specs=in_specs,
        out_specs=row_spec(d),
        out_shape=jax.ShapeDtypeStruct((n, d), F32),
        compiler_params=pltpu.CompilerParams(dimension_semantics=("arbitrary",),
                                             vmem_limit_bytes=VMEM_LIMIT),
        name="post_ffn",
    )(h, *mixes, *wouts, npost, nfpre, nfpost, wup, wdn)


def _even_seq(b, c, proj_s, cos, sin, lb, retw_ref, hgw_ref, oret_ref, ohg_ref, mix_ref):
    hw = H_RET * HD
    blk = min(HG_BLK, c)
    nblk = c // blk
    tri = _tri(c)
    ti = lax.broadcasted_iota(jnp.int32, (c, c), 0)
    si = lax.broadcasted_iota(jnp.int32, (c, c), 1)
    tl = lax.broadcasted_iota(jnp.int32, (c, HD), 0).astype(F32)

    base = 4 * hw
    f = lb + (1.0 - lb) * jax.nn.sigmoid(proj_s[b, :, base + hw:base + 2 * hw])
    lf_parts = _split3(jnp.log(f))
    cum = _dot3(tri, lf_parts)
    dcol = _dot3_tn(lf_parts, jnp.ones((c, HD), BF16))

    r_a, r_qs, r_kv, r_v = [], [], [], []
    for h in range(H_RET):
        col = lambda off: slice(off + h * HD, off + (h + 1) * HD)
        lg = math.log1p(-(2.0 ** (-5.0 - h)))
        q = proj_s[b, :, col(0)]
        k = proj_s[b, :, col(hw)]
        v = proj_s[b, :, col(2 * hw)].astype(BF16)
        q = (q * cos + pltpu.roll(q, HD // 2, axis=1) * sin).astype(BF16)
        k = (k * cos + pltpu.roll(k, HD // 2, axis=1) * sin) * (HD ** -0.5)
        r_a.append(_dot_nt(q, k))
        r_qs.append(_dot(q, oret_ref[b, h]))
        r_kv.append(_dot_tn(k * jnp.exp((float(c - 1) - tl) * lg), v))
        r_v.append(v)

    g_a, g_qs, g_kv, g_v = [], [], [], []
    for h in range(H_HG):
        hs = slice(h * HD, (h + 1) * HD)
        col = lambda off: slice(base + off + h * HD, base + off + (h + 1) * HD)
        q = proj_s[b, :, col(0)]
        v = proj_s[b, :, col(2 * hw)]
        k = 1.0 - f[:, hs]
        cum_h = cum[:, hs]
        cl = cum_h[c - 1:c, :]
        g_qs.append(_dot(q * jnp.exp(cum_h), ohg_ref[b, h]))
        g_kv.append(_dot_tn(k * jnp.exp(cl - cum_h), v))
        rows = []
        for i in range(nblk):
            lo, hi = i * blk, (i + 1) * blk
            ref = cum_h[lo + blk // 2 - 1:lo + blk // 2, :]
            qh = q[lo:hi] * jnp.exp(cum_h[lo:hi] - ref)
            kh = k[:hi] * jnp.exp(ref - cum_h[:hi])
            rows.append(_dot_nt(qh, kh))
        g_a.append(rows)
        g_v.append(v)

    r_o = []
    for h in range(H_RET):
        lg = math.log1p(-(2.0 ** (-5.0 - h)))
        dm = jnp.where(ti >= si, jnp.exp((ti - si).astype(F32) * lg), 0.0)
        r_o.append(_dot(r_a[h] * dm, r_v[h]) + r_qs[h] * jnp.exp((tl + 1.0) * lg))
        oret_ref[b, h] = oret_ref[b, h] * math.exp(c * lg) + r_kv[h]

    g_o = []
    for h in range(H_HG):
        outs = []
        for i in range(nblk):
            lo, hi = i * blk, (i + 1) * blk
            tb = lax.broadcasted_iota(jnp.int32, (blk, hi), 0) + lo
            sb = lax.broadcasted_iota(jnp.int32, (blk, hi), 1)
            outs.append(_dot(jnp.where(tb >= sb, g_a[h][i], 0.0), g_v[h][:hi]))
        o = outs[0] if nblk == 1 else jnp.concatenate(outs, axis=0)
        g_o.append(o + g_qs[h])
        ohg_ref[b, h] = ohg_ref[b, h] * jnp.exp(dcol[h * HD:(h + 1) * HD, :]) + g_kv[h]

    for h in range(H_RET):
        col = lambda off: slice(off + h * HD, off + (h + 1) * HD)
        g = proj_s[b, :, col(3 * hw)]
        mu = jnp.mean(r_o[h], axis=-1, keepdims=True)
        oc = r_o[h] - mu
        var = jnp.mean(oc * oc, axis=-1, keepdims=True)
        o = oc * lax.rsqrt(var + EPS) * retw_ref[:, col(0)]
        mix_ref[b, :, col(0)] = (o * (g * jax.nn.sigmoid(g))).astype(mix_ref.dtype)
    for h in range(H_HG):
        hs = slice(h * HD, (h + 1) * HD)
        g = proj_s[b, :, base + 3 * hw + h * HD:base + 3 * hw + (h + 1) * HD]
        o = _rms(g_o[h], hgw_ref[:, hs])
        mix_ref[b, :, hw + h * HD:hw + (h + 1) * HD] = (o * (g * jax.nn.sigmoid(g))).astype(mix_ref.dtype)


def _even_kernel(has_state, x_ref, cos_ref, sin_ref, npre_ref, win_ref, retw_ref, hgw_ref,
                 lb_ref, *refs):
    if has_state:
        sret_ref, shg_ref, mix_ref, oret_ref, ohg_ref, proj_s = refs
    else:
        mix_ref, oret_ref, ohg_ref, proj_s = refs
    bb, c, d = x_ref.shape
    n_in = win_ref.shape[1]

    @pl.when(pl.program_id(1) == 0)
    def _():
        if has_state:
            oret_ref[...] = sret_ref[...]
            ohg_ref[...] = shg_ref[...]
        else:
            oret_ref[...] = jnp.zeros(oret_ref.shape, F32)
            ohg_ref[...] = jnp.zeros(ohg_ref.shape, F32)

    hn = _rms(x_ref[...].reshape(bb * c, d), npre_ref[...]).astype(BF16)
    for j in range(n_in // 512):
        sl = slice(j * 512, (j + 1) * 512)
        proj_s[:, :, sl] = jnp.dot(hn, win_ref[:, sl], preferred_element_type=F32).reshape(bb, c, 512)

    cos = cos_ref[...]
    sin = sin_ref[...]
    lb = lb_ref[...]

    def seq_body(b, carry):
        _even_seq(b, c, proj_s, cos, sin, lb, retw_ref, hgw_ref, oret_ref, ohg_ref, mix_ref)
        return carry

    if bb == 1:
        seq_body(0, 0)
    else:
        lax.fori_loop(0, bb, seq_body, 0)


def _even_call(x, cos, sin, npre, win, retw, hgw, lb, states, bb, c):
    nb_total, t, d = x.shape
    has_state = states is not None
    n_in = win.shape[1]
    grid = (nb_total // bb, t // c)
    st_spec = pl.BlockSpec((bb, H_RET, HD, HD), lambda b, i: (b, 0, 0, 0))
    in_specs = [pl.BlockSpec((bb, c, d), lambda b, i: (b, i, 0)),
                pl.BlockSpec((c, HD), lambda b, i: (i, 0)),
                pl.BlockSpec((c, HD), lambda b, i: (i, 0)),
                _const_spec((1, d)), _const_spec(win.shape),
                _const_spec((1, H_RET * HD)), _const_spec((1, H_HG * HD)), _const_spec((1, H_HG * HD))]
    args = [x, cos, sin, npre, win, retw, hgw, lb]
    if has_state:
        in_specs += [st_spec, st_spec]
        args += list(states)
    mix_w = (H_RET + H_HG) * HD
    return pl.pallas_call(
        functools.partial(_even_kernel, has_state),
        grid=grid,
        in_specs=in_specs,
        out_specs=[pl.BlockSpec((bb, c, mix_w), lambda b, i: (b, i, 0)), st_spec, st_spec],
        out_shape=[jax.ShapeDtypeStruct((nb_total, t, mix_w), BF16),
                   jax.ShapeDtypeStruct((nb_total, H_RET, HD, HD), F32),
                   jax.ShapeDtypeStruct((nb_total, H_HG, HD, HD), F32)],
        scratch_shapes=[pltpu.VMEM((bb, c, n_in), F32)],
        compiler_params=pltpu.CompilerParams(dimension_semantics=("arbitrary", "arbitrary"),
                                             vmem_limit_bytes=VMEM_LIMIT),
        name="even_mixer",
    )(*args)


def _odd_kernel(has_state, x_ref, npre_ref, win_ref, convw_ref, convb_ref, dtb_ref, alog_ref,
                dssm_ref, ssmw_ref, *refs):
    if has_state:
        sconv_ref, sssm_ref, y_ref, u_ref, oconv_ref, ossm_ref, proj_s, tail_s, sst_s = refs
    else:
        y_ref, u_ref, oconv_ref, ossm_ref, proj_s, tail_s, sst_s = refs
    bb, c, d = x_ref.shape
    n_in = win_ref.shape[1]
    d_inner = H_SSD * P_SSD
    conv_dim = d_inner + 2 * G_SSD * N_SSD
    conv_w = convw_ref.shape[0]
    halo = conv_w - 1
    u_w = u_ref.shape[2]

    @pl.when(pl.program_id(1) == 0)
    def _():
        tail_s[...] = jnp.zeros(tail_s.shape, F32)
        if has_state:
            def in_body(b, carry):
                tail_s[b, 8 - halo:8, :] = sconv_ref[b]
                for h in range(H_SSD):
                    g, hh = divmod(h, H_SSD // G_SSD)
                    sst_s[b, g, :, hh * P_SSD:(hh + 1) * P_SSD] = sssm_ref[b, h]
                return carry
            lax.fori_loop(0, bb, in_body, 0)
        else:
            sst_s[...] = jnp.zeros(sst_s.shape, F32)

    hn = _rms(x_ref[...].reshape(bb * c, d), npre_ref[...]).astype(BF16)
    col = 0
    while col < n_in:
        w = min(512, n_in - col)
        proj_s[:, :, col:col + w] = jnp.dot(hn, win_ref[:, col:col + w],
                                            preferred_element_type=F32).reshape(bb, c, w)
        col += w

    tri = _tri(c)
    triu = _tri(c, upper=True)
    causal = (lax.broadcasted_iota(jnp.int32, (c, c), 0) >= lax.broadcasted_iota(jnp.int32, (c, c), 1))
    lo_half = lax.broadcasted_iota(jnp.int32, (c, LANES), 1) < P_SSD
    row8 = lax.broadcasted_iota(jnp.int32, (8, conv_dim), 0)
    a_row = -jnp.exp(alog_ref[...])
    pairs_per_group = H_SSD // G_SSD // 2
    gw = d_inner // G_SSD

    def seq_body(b, carry):
        z = proj_s[b, :, 0:d_inner]
        xbc = proj_s[b, :, d_inner:d_inner + conv_dim]
        u_ref[b] = proj_s[b, :, d_inner + conv_dim:d_inner + conv_dim + u_w]
        dt_raw = proj_s[b, :, d_inner + conv_dim + u_w:d_inner + conv_dim + u_w + LANES]
        tail = tail_s[b]
        conv = convb_ref[...] + xbc * convw_ref[halo:halo + 1, :]
        for s in range(1, conv_w):
            rolled = pltpu.roll(xbc, s, axis=0)
            head = jnp.where(row8 < s, pltpu.roll(tail, s, axis=0), rolled[0:8])
            shifted = head if c == 8 else jnp.concatenate([head, rolled[8:]], axis=0)
            conv = conv + shifted * convw_ref[halo - s:halo - s + 1, :]
        tail_s[b] = xbc[c - 8:c]
        act = conv * jax.nn.sigmoid(conv)
        xs = act[:, :d_inner]

        dt = jax.nn.softplus(dt_raw + dtb_ref[...])
        la_parts = _split3(dt * a_row)
        cum = _dot3(tri, la_parts)
        cum_t = _dot3_tn(la_parts, triu)
        bms = [act[:, d_inner + g * N_SSD:d_inner + (g + 1) * N_SSD] for g in range(G_SSD)]
        cms = [act[:, d_inner + (G_SSD + g) * N_SSD:d_inner + (G_SSD + g + 1) * N_SSD]
               for g in range(G_SSD)]
        scores = [_dot_nt(cms[g], bms[g]) for g in range(G_SSD)]

        bc_cum = [jnp.broadcast_to(cum[:, h:h + 1], (c, LANES)) for h in range(H_SSD)]
        bc_dt = [jnp.broadcast_to(dt[:, h:h + 1], (c, LANES)) for h in range(H_SSD)]
        for g in range(G_SSD):
            xdts, xws, cums, cls = [], [], [], []
            for pp in range(pairs_per_group):
                p = g * pairs_per_group + pp
                cum_p = jnp.where(lo_half, bc_cum[2 * p], bc_cum[2 * p + 1])
                xdt = xs[:, p * LANES:(p + 1) * LANES] * jnp.where(lo_half, bc_dt[2 * p], bc_dt[2 * p + 1])
                cl = cum_p[c - 1:c, :]
                xdts.append(xdt)
                xws.append(xdt * jnp.exp(cl - cum_p))
                cums.append(cum_p)
                cls.append(cl)
            y_state = _dot(cms[g], sst_s[b, g])
            kv = _dot_tn(bms[g], jnp.concatenate(xws, axis=1))
            y_intra = []
            for pp in range(pairs_per_group):
                p = g * pairs_per_group + pp
                a_pair = []
                for h in (2 * p, 2 * p + 1):
                    diff = bc_cum[h][:, :c] - cum_t[h:h + 1, :]
                    a_pair.append(scores[g] * jnp.where(causal, jnp.exp(jnp.where(causal, diff, 0.0)), 0.0))
                both = _dot(jnp.concatenate(a_pair, axis=0), xdts[pp])
                y_intra.append(jnp.where(lo_half, both[:c], both[c:]))
            sst_s[b, g] = sst_s[b, g] * jnp.exp(jnp.concatenate(cls, axis=1)) + kv
            y = (jnp.concatenate(y_intra, axis=1) + y_state * jnp.exp(jnp.concatenate(cums, axis=1))
                 + dssm_ref[:, g * gw:(g + 1) * gw] * xs[:, g * gw:(g + 1) * gw])
            zg = z[:, g * gw:(g + 1) * gw]
            y = y * (zg * jax.nn.sigmoid(zg))
            y_ref[b, :, g * gw:(g + 1) * gw] = _rms(y, ssmw_ref[:, g * gw:(g + 1) * gw]).astype(y_ref.dtype)
        return carry

    if bb == 1:
        seq_body(0, 0)
    else:
        lax.fori_loop(0, bb, seq_body, 0)

    @pl.when(pl.program_id(1) == pl.num_programs(1) - 1)
    def _():
        def out_body(b, carry):
            oconv_ref[b] = tail_s[b, 8 - halo:8, :]
            for h in range(H_SSD):
                g, hh = divmod(h, H_SSD // G_SSD)
                ossm_ref[b, h] = sst_s[b, g, :, hh * P_SSD:(hh + 1) * P_SSD]
            return carry
        lax.fori_loop(0, bb, out_body, 0)


def _odd_call(x, npre, win, convw, convb, dtb, alog, dssm, ssmw, states, bb, c):
    nb_total, t, d = x.shape
    has_state = states is not None
    n_in = win.shape[1]
    d_inner = H_SSD * P_SSD
    conv_dim = d_inner + 2 * G_SSD * N_SSD
    u_w = S5_G * S5_GS
    halo = convw.shape[0] - 1
    grid = (nb_total // bb, t // c)
    conv_spec = pl.BlockSpec((bb, halo, conv_dim), lambda b, i: (b, 0, 0))
    ssm_spec = pl.BlockSpec((bb, H_SSD, N_SSD, P_SSD), lambda b, i: (b, 0, 0, 0))
    in_specs = [pl.BlockSpec((bb, c, d), lambda b, i: (b, i, 0)),
                _const_spec((1, d)), _const_spec(win.shape), _const_spec(convw.shape),
                _const_spec((1, conv_dim)), _const_spec((1, LANES)), _const_spec((1, LANES)),
                _const_spec((1, d_inner)), _const_spec((1, d_inner))]
    args = [x, npre, win, convw, convb, dtb, alog, dssm, ssmw]
    if has_state:
        in_specs += [conv_spec, ssm_spec]
        args += list(states)
    return pl.pallas_call(
        functools.partial(_odd_kernel, has_state),
        grid=grid,
        in_specs=in_specs,
        out_specs=[pl.BlockSpec((bb, c, d_inner), lambda b, i: (b, i, 0)),
                   pl.BlockSpec((bb, c, u_w), lambda b, i: (b, i, 0)),
                   conv_spec, ssm_spec],
        out_shape=[jax.ShapeDtypeStruct((nb_total, t, d_inner), BF16),
                   jax.ShapeDtypeStruct((nb_total, t, u_w), F32),
                   jax.ShapeDtypeStruct((nb_total, halo, conv_dim), F32),
                   jax.ShapeDtypeStruct((nb_total, H_SSD, N_SSD, P_SSD), F32)],
        scratch_shapes=[pltpu.VMEM((bb, c, n_in), F32),
                        pltpu.VMEM((bb, 8, conv_dim), F32),
                        pltpu.VMEM((bb, G_SSD, N_SSD, d_inner // G_SSD), F32)],
        compiler_params=pltpu.CompilerParams(dimension_semantics=("arbitrary", "arbitrary"),
                                             vmem_limit_bytes=VMEM_LIMIT),
        name="odd_mixer",
    )(*args)


S5_HALF_IN = S5_G * S5_GS // 2
S5_HALF_ST = S5_G * S5_P // 2
S5_SCAN_VREGS = 4


def _s5_kernel(u_ref, bmat_ref, cmat_ref, lre_ref, lim_ref, d_ref, wglu_ref, bglu_ref,
               h0re_ref, h0im_ref, yd_ref, ore_ref, oim_ref, buf_s):
    steps, r, uw = u_ref.shape
    hs = S5_HALF_ST
    scan_w = max(LANES, S5_SCAN_VREGS * 8 * LANES // r)

    @pl.when(pl.program_id(1) == 0)
    def _():
        ore_ref[...] = h0re_ref[...]
        oim_ref[...] = h0im_ref[...]

    u = u_ref[...].reshape(steps * r, uw)
    ub = u.astype(BF16)
    for half in range(2):
        buf_s[:, half * 2 * hs:(half + 1) * 2 * hs] = jnp.dot(
            ub[:, half * S5_HALF_IN:(half + 1) * S5_HALF_IN], bmat_ref[half],
            preferred_element_type=F32)

    for half in range(2):
        for j in range(hs // scan_w):
            st = slice(half * hs + j * scan_w, half * hs + (j + 1) * scan_w)
            cre = half * 2 * hs + j * scan_w
            cim = cre + hs
            lre = jnp.broadcast_to(lre_ref[:, st], (r, scan_w))
            lim = jnp.broadcast_to(lim_ref[:, st], (r, scan_w))

            def step(t, carry):
                hre, him = carry
                rows = pl.ds(pl.multiple_of(t * r, r), r)
                nre = lre * hre - lim * him + buf_s[rows, cre:cre + scan_w]
                nim = lre * him + lim * hre + buf_s[rows, cim:cim + scan_w]
                buf_s[rows, cre:cre + scan_w] = nre
                buf_s[rows, cim:cim + scan_w] = nim
                return nre, nim

            hre, him = lax.fori_loop(0, steps, step, (ore_ref[:, st], oim_ref[:, st]))
            ore_ref[:, st] = hre
            oim_ref[:, st] = him

    ys = []
    for half in range(2):
        ys.append(jnp.dot(buf_s[:, half * 2 * hs:(half + 1) * 2 * hs].astype(BF16), cmat_ref[half],
                          preferred_element_type=F32))
    y = jnp.concatenate(ys, axis=1) + d_ref[...] * u
    g = jax.nn.gelu(y)
    gate = jax.nn.sigmoid(jnp.dot(g.astype(BF16), wglu_ref[...], preferred_element_type=F32)
                          + bglu_ref[...])
    yd_ref[...] = (g * gate).reshape(steps, r, uw).astype(yd_ref.dtype)


def _s5_call(u_t, bmat, cmat, lre, lim, dvec, wglu, bglu, h0re, h0im, steps, r):
    t, nb_total, uw = u_t.shape
    ns = S5_G * S5_P
    grid = (nb_total // r, t // steps)
    st_spec = pl.BlockSpec((r, ns), lambda b, i: (b, 0))
    return pl.pallas_call(
        _s5_kernel,
        grid=grid,
        in_specs=[pl.BlockSpec((steps, r, uw), lambda b, i: (i, b, 0)),
                  _const_spec(bmat.shape), _const_spec(cmat.shape),
                  _const_spec((1, ns)), _const_spec((1, ns)), _const_spec((1, uw)),
                  _const_spec(wglu.shape), _const_spec((1, uw)), st_spec, st_spec],
        out_specs=[pl.BlockSpec((steps, r, uw), lambda b, i: (i, b, 0)), st_spec, st_spec],
        out_shape=[jax.ShapeDtypeStruct((t, nb_total, uw), BF16),
                   jax.ShapeDtypeStruct((nb_total, ns), F32),
                   jax.ShapeDtypeStruct((nb_total, ns), F32)],
        scratch_shapes=[pltpu.VMEM((steps * r, 2 * ns), F32)],
        compiler_params=pltpu.CompilerParams(dimension_semantics=("arbitrary", "arbitrary"),
                                             vmem_limit_bytes=VMEM_LIMIT),
        name="s5_mixer",
    )(u_t, bmat, cmat, lre, lim, dvec, wglu, bglu, h0re, h0im)


def _rope_tables(pos):
    half = HD // 2
    inv_freq = ROPE_BASE ** (-jnp.arange(half, dtype=F32) / half)
    ang = pos[:, None] * inv_freq[None, :]
    cos, sin = jnp.cos(ang), jnp.sin(ang)
    return jnp.concatenate([cos, cos], axis=-1), jnp.concatenate([-sin, sin], axis=-1)


def _pad_lanes(v):
    return jnp.pad(v.astype(F32), (0, LANES - v.shape[0]))[None, :]


def _s5_tables(lam_re, lam_im, log_step, b_re, b_im, c_re, c_im):
    lr, li = lam_re.astype(F32), lam_im.astype(F32)
    dt = jnp.exp(log_step.astype(F32))[:, None]
    mag = jnp.exp(lr * dt)
    bar_re, bar_im = mag * jnp.cos(li * dt), mag * jnp.sin(li * dt)
    den = lr * lr + li * li
    cf_re = ((bar_re - 1.0) * lr + bar_im * li) / den
    cf_im = (bar_im * lr - (bar_re - 1.0) * li) / den
    bb_re = cf_re[..., None] * b_re.astype(F32) - cf_im[..., None] * b_im.astype(F32)
    bb_im = cf_re[..., None] * b_im.astype(F32) + cf_im[..., None] * b_re.astype(F32)
    gh = S5_G // 2
    eye = jnp.eye(gh, dtype=F32)

    def in_mat(b):
        return jnp.einsum('gpc,gk->gckp', b, eye).reshape(gh * S5_GS, gh * S5_P)

    def out_mat(cc):
        return jnp.einsum('gcp,gk->gpkc', cc, eye).reshape(gh * S5_P, gh * S5_GS)

    bmats, cmats = [], []
    for half in range(2):
        sl = slice(half * gh, (half + 1) * gh)
        bmats.append(jnp.concatenate([in_mat(bb_re[sl]), in_mat(bb_im[sl])], axis=1))
        cmats.append(jnp.concatenate([out_mat(c_re.astype(F32)[sl]), out_mat(-c_im.astype(F32)[sl])],
                                     axis=0))
    return (jnp.stack(bmats).astype(BF16), jnp.stack(cmats).astype(BF16),
            bar_re.reshape(1, -1), bar_im.reshape(1, -1))


def _row(v):
    return v.astype(F32).reshape(1, -1)


def _forward(x, pos_offset, states, p, cfg):
    nb, t, d = x.shape
    n = nb * t
    has_state = states is not None
    pos = jnp.arange(pos_offset, pos_offset + t, dtype=F32)
    cos, sin = _rope_tables(pos)
    lb = jnp.cumsum(jax.nn.softmax(p['hgrn_lower_bounds'].astype(F32), axis=0), axis=0)[0]
    mix, ret_o, hg_o = _even_call(
        x, cos, sin, _row(p['norm_mix_pre'][0]), p['w_in_even'][0].astype(BF16),
        _row(p['ret_norm_w'][0]), _row(p['hgrn_norm_w'][0]), _row(lb),
        (states['ret'][0], states['hgrn'][0]) if has_state else None, cfg['bb'], cfg['c'])
    h = _post_call(x.reshape(n, d), [mix.reshape(n, -1)], [p['w_out_even'][0].astype(BF16)],
                   _row(p['norm_mix_post'][0]), _row(p['norm_ffn_pre'][0]),
                   _row(p['norm_ffn_post'][0]), p['w_ffn_up'][0].astype(BF16),
                   p['w_ffn_down'][0].astype(BF16), cfg['rows'])
    d_inner = H_SSD * P_SSD
    conv_dim = d_inner + 2 * G_SSD * N_SSD
    u_w = S5_G * S5_GS
    w = p['w_in_odd'][0]
    c0 = d_inner + conv_dim
    w1 = jnp.concatenate([w[:, :c0], w[:, c0 + H_SSD:], w[:, c0:c0 + H_SSD],
                          jnp.zeros((d, LANES - H_SSD), w.dtype)], axis=1).astype(BF16)
    y, u, conv_o, ssm_o = _odd_call(
        h.reshape(nb, t, d), _row(p['norm_mix_pre'][1]), w1, p['conv_w'][0].astype(F32),
        _row(p['conv_b'][0]), _pad_lanes(p['dt_bias'][0]), _pad_lanes(p['a_log'][0]),
        _row(jnp.repeat(p['d_ssm'][0], P_SSD)), _row(p['ssm_norm_w'][0]),
        (states['conv'][0], states['ssm'][0]) if has_state else None, cfg['bb'], cfg['c'])
    bmat, cmat, lre, lim = _s5_tables(p['s5_lam_re'][0], p['s5_lam_im'][0], p['s5_log_step'][0],
                                      p['s5_b_re'][0], p['s5_b_im'][0], p['s5_c_re'][0],
                                      p['s5_c_im'][0])
    ns = S5_G * S5_P
    if has_state:
        h0re = states['s5_re'][0].reshape(nb, ns)
        h0im = states['s5_im'][0].reshape(nb, ns)
    else:
        h0re = jnp.zeros((nb, ns), F32)
        h0im = jnp.zeros((nb, ns), F32)
    yd_t, re_o, im_o = _s5_call(jnp.swapaxes(u, 0, 1), bmat, cmat, lre, lim, _row(p['s5_d'][0]),
                                p['w_glu'][0].astype(BF16), _row(p['b_glu'][0]), h0re, h0im,
                                cfg['s5_steps'], cfg['s5_rows'])
    yd = jnp.swapaxes(yd_t, 0, 1).reshape(n, u_w)
    wo = p['w_out_odd'][0].astype(BF16)
    h = _post_call(h, [y.reshape(n, d_inner), yd], [wo[:d_inner], wo[d_inner:]],
                   _row(p['norm_mix_post'][1]), _row(p['norm_ffn_pre'][1]),
                   _row(p['norm_ffn_post'][1]), p['w_ffn_up'][1].astype(BF16),
                   p['w_ffn_down'][1].astype(BF16), cfg['rows'])
    return (h.reshape(nb, t, d), ret_o[None], hg_o[None], ssm_o[None], conv_o[None],
            re_o.reshape(1, nb, S5_G, S5_P), im_o.reshape(1, nb, S5_G, S5_P))


PROMPT_CFG = dict(bb=1, c=128, rows=512, s5_steps=32, s5_rows=8)
SAMPLE_CFG = dict(bb=8, c=8, rows=512, s5_steps=8, s5_rows=32)


def kernel(x_prompt, x_sample, state_ret, state_hgrn, state_ssm, state_conv, state_s5_re, state_s5_im,
           norm_mix_pre, norm_mix_post, norm_ffn_pre, norm_ffn_post, w_in_even, w_out_even, ret_norm_w,
           hgrn_lower_bounds, hgrn_norm_w, w_in_odd, conv_w, conv_b, dt_bias, a_log, d_ssm, ssm_norm_w,
           s5_lam_re, s5_lam_im, s5_log_step, s5_b_re, s5_b_im, s5_c_re, s5_c_im, s5_d, w_glu, b_glu,
           w_out_odd, w_ffn_up, w_ffn_down):
    p = dict(norm_mix_pre=norm_mix_pre, norm_mix_post=norm_mix_post, norm_ffn_pre=norm_ffn_pre,
             norm_ffn_post=norm_ffn_post, w_in_even=w_in_even, w_out_even=w_out_even, ret_norm_w=ret_norm_w,
             hgrn_lower_bounds=hgrn_lower_bounds, hgrn_norm_w=hgrn_norm_w, w_in_odd=w_in_odd, conv_w=conv_w,
             conv_b=conv_b, dt_bias=dt_bias, a_log=a_log, d_ssm=d_ssm, ssm_norm_w=ssm_norm_w,
             s5_lam_re=s5_lam_re, s5_lam_im=s5_lam_im, s5_log_step=s5_log_step, s5_b_re=s5_b_re,
             s5_b_im=s5_b_im, s5_c_re=s5_c_re, s5_c_im=s5_c_im, s5_d=s5_d, w_glu=w_glu, b_glu=b_glu,
             w_out_odd=w_out_odd, w_ffn_up=w_ffn_up, w_ffn_down=w_ffn_down)
    past_len = 16384
    states = dict(ret=state_ret, hgrn=state_hgrn, ssm=state_ssm, conv=state_conv,
                  s5_re=state_s5_re, s5_im=state_s5_im)
    y_p, ret_p, hg_p, ssm_p, conv_p, re_p, im_p = _forward(x_prompt, 0, None, p, PROMPT_CFG)
    y_s, ret_s, hg_s, ssm_s, conv_s, re_s, im_s = _forward(x_sample, past_len, states, p, SAMPLE_CFG)
    return (y_p, y_s, ret_p, ret_s, hg_p, hg_s, ssm_p, ssm_s, conv_p, conv_s,
            re_p, re_s, im_p, im_s)
```

```python
import functools
import math

import jax
import jax.numpy as jnp
from jax import lax
from jax.experimental import pallas as pl
from jax.experimental.pallas import tpu as pltpu

F32 = jnp.float32
BF16 = jnp.bfloat16
EPS = 1e-6
ROPE_BASE = 10000.0
LANES = 128
VMEM_LIMIT = 56 * 1024 * 1024

H_RET = 4
H_HG = 4
HD = 128
H_SSD = 16
P_SSD = 64
N_SSD = 128
G_SSD = 2
S5_G = 32
S5_GS = 16
S5_P = 64
HG_BLK = 16


def _rms(x, w):
    return x * lax.rsqrt(jnp.mean(x * x, axis=-1, keepdims=True) + EPS) * w


def _dot(a, b):
    return jnp.dot(a.astype(BF16), b.astype(BF16), preferred_element_type=F32)


def _dot_nt(a, b):
    return lax.dot_general(a.astype(BF16), b.astype(BF16), (((1,), (1,)), ((), ())),
                           preferred_element_type=F32)


def _dot_tn(a, b):
    return lax.dot_general(a.astype(BF16), b.astype(BF16), (((0,), (0,)), ((), ())),
                           preferred_element_type=F32)


def _split3(x):
    hi = x.astype(BF16)
    r1 = x - hi.astype(F32)
    mid = r1.astype(BF16)
    lo = (r1 - mid.astype(F32)).astype(BF16)
    return hi, mid, lo


def _dot3(a_exact, parts):
    acc = None
    for p in parts:
        d = jnp.dot(a_exact, p, preferred_element_type=F32)
        acc = d if acc is None else acc + d
    return acc


def _dot3_tn(parts, b_exact):
    acc = None
    for p in parts:
        d = lax.dot_general(p, b_exact, (((0,), (0,)), ((), ())), preferred_element_type=F32)
        acc = d if acc is None else acc + d
    return acc


def _tri(c, upper=False):
    r = lax.broadcasted_iota(jnp.int32, (c, c), 0)
    s = lax.broadcasted_iota(jnp.int32, (c, c), 1)
    return jnp.where((r <= s) if upper else (r >= s), 1.0, 0.0).astype(BF16)


def _const_spec(shape):
    nd = len(shape)
    return pl.BlockSpec(shape, lambda *_: (0,) * nd, pipeline_mode=pl.Buffered(1))


def _post_kernel(n_mix, ff_chunk, h_ref, *refs):
    mix_refs = refs[:n_mix]
    wout_refs = refs[n_mix:2 * n_mix]
    npost_ref, nfpre_ref, nfpost_ref, wup_ref, wdn_ref, o_ref = refs[2 * n_mix:]
    acc = None
    for m_ref, w_ref in zip(mix_refs, wout_refs):
        d = jnp.dot(m_ref[...], w_ref[...], preferred_element_type=F32)
        acc = d if acc is None else acc + d
    h1 = h_ref[...] + _rms(acc, npost_ref[...])
    hn = _rms(h1, nfpre_ref[...]).astype(BF16)
    ff = None
    for j in range(wup_ref.shape[1] // ff_chunk):
        sl = slice(j * ff_chunk, (j + 1) * ff_chunk)
        up = jnp.dot(hn, wup_ref[:, sl], preferred_element_type=F32)
        act = jnp.square(jnp.maximum(up, 0.0)).astype(BF16)
        d = jnp.dot(act, wdn_ref[sl, :], preferred_element_type=F32)
        ff = d if ff is None else ff + d
    o_ref[...] = h1 + _rms(ff, nfpost_ref[...])


def _post_call(h, mixes, wouts, npost, nfpre, nfpost, wup, wdn, rows):
    n, d = h.shape
    n_mix = len(mixes)
    row_spec = lambda w: pl.BlockSpec((rows, w), lambda i: (i, 0))
    in_specs = ([row_spec(d)] + [row_spec(m.shape[1]) for m in mixes]
                + [_const_spec(w.shape) for w in wouts]
                + [_const_spec((1, d))] * 3 + [_const_spec(wup.shape), _const_spec(wdn.shape)])
    return pl.pallas_call(
        functools.partial(_post_kernel, n_mix, 1024),
        grid=(n // rows,),
        in_specs=in_specs,
        out_specs=row_spec(d),
        out_shape=jax.ShapeDtypeStruct((n, d), F32),
        compiler_params=pltpu.CompilerParams(dimension_semantics=("arbitrary",),
                                             vmem_limit_bytes=VMEM_LIMIT),
        name="post_ffn",
    )(h, *mixes, *wouts, npost, nfpre, nfpost, wup, wdn)


def _even_seq(b, c, proj_s, cos, sin, lb, retw_ref, hgw_ref, oret_ref, ohg_ref, mix_ref):
    hw = H_RET * HD
    blk = min(HG_BLK, c)
    nblk = c // blk
    tri = _tri(c)
    ti = lax.broadcasted_iota(jnp.int32, (c, c), 0)
    si = lax.broadcasted_iota(jnp.int32, (c, c), 1)
    tl = lax.broadcasted_iota(jnp.int32, (c, HD), 0).astype(F32)

    base = 4 * hw
    f = lb + (1.0 - lb) * jax.nn.sigmoid(proj_s[b, :, base + hw:base + 2 * hw])
    lf_parts = _split3(jnp.log(f))
    cum = _dot3(tri, lf_parts)
    dcol = _dot3_tn(lf_parts, jnp.ones((c, HD), BF16))

    r_a, r_qs, r_kv, r_v = [], [], [], []
    for h in range(H_RET):
        col = lambda off: slice(off + h * HD, off + (h + 1) * HD)
        lg = math.log1p(-(2.0 ** (-5.0 - h)))
        q = proj_s[b, :, col(0)]
        k = proj_s[b, :, col(hw)]
        v = proj_s[b, :, col(2 * hw)].astype(BF16)
        q = (q * cos + pltpu.roll(q, HD // 2, axis=1) * sin).astype(BF16)
        k = (k * cos + pltpu.roll(k, HD // 2, axis=1) * sin) * (HD ** -0.5)
        r_a.append(_dot_nt(q, k))
        r_qs.append(_dot(q, oret_ref[b, h]))
        r_kv.append(_dot_tn(k * jnp.exp((float(c - 1) - tl) * lg), v))
        r_v.append(v)

    g_a, g_qs, g_kv, g_v = [], [], [], []
    for h in range(H_HG):
        hs = slice(h * HD, (h + 1) * HD)
        col = lambda off: slice(base + off + h * HD, base + off + (h + 1) * HD)
        q = proj_s[b, :, col(0)]
        v = proj_s[b, :, col(2 * hw)]
        k = 1.0 - f[:, hs]
        cum_h = cum[:, hs]
        cl = cum_h[c - 1:c, :]
        g_qs.append(_dot(q * jnp.exp(cum_h), ohg_ref[b, h]))
        g_kv.append(_dot_tn(k * jnp.exp(cl - cum_h), v))
        rows = []
        for i in range(nblk):
            lo, hi = i * blk, (i + 1) * blk
            ref = cum_h[lo + blk // 2 - 1:lo + blk // 2, :]
            qh = q[lo:hi] * jnp.exp(cum_h[lo:hi] - ref)
            kh = k[:hi] * jnp.exp(ref - cum_h[:hi])
            rows.append(_dot_nt(qh, kh))
        g_a.append(rows)
        g_v.append(v)

    r_o = []
    for h in range(H_RET):
        lg = math.log1p(-(2.0 ** (-5.0 - h)))
        dm = jnp.where(ti >= si, jnp.exp((ti - si).astype(F32) * lg), 0.0)
        r_o.append(_dot(r_a[h] * dm, r_v[h]) + r_qs[h] * jnp.exp((tl + 1.0) * lg))
        oret_ref[b, h] = oret_ref[b, h] * math.exp(c * lg) + r_kv[h]

    g_o = []
    for h in range(H_HG):
        outs = []
        for i in range(nblk):
            lo, hi = i * blk, (i + 1) * blk
            tb = lax.broadcasted_iota(jnp.int32, (blk, hi), 0) + lo
            sb = lax.broadcasted_iota(jnp.int32, (blk, hi), 1)
            outs.append(_dot(jnp.where(tb >= sb, g_a[h][i], 0.0), g_v[h][:hi]))
        o = outs[0] if nblk == 1 else jnp.concatenate(outs, axis=0)
        g_o.append(o + g_qs[h])
        ohg_ref[b, h] = ohg_ref[b, h] * jnp.exp(dcol[h * HD:(h + 1) * HD, :]) + g_kv[h]

    for h in range(H_RET):
        col = lambda off: slice(off + h * HD, off + (h + 1) * HD)
        g = proj_s[b, :, col(3 * hw)]
        mu = jnp.mean(r_o[h], axis=-1, keepdims=True)
        oc = r_o[h] - mu
        var = jnp.mean(oc * oc, axis=-1, keepdims=True)
        o = oc * lax.rsqrt(var + EPS) * retw_ref[:, col(0)]
        mix_ref[b, :, col(0)] = (o * (g * jax.nn.sigmoid(g))).astype(mix_ref.dtype)
    for h in range(H_HG):
        hs = slice(h * HD, (h + 1) * HD)
        g = proj_s[b, :, base + 3 * hw + h * HD:base + 3 * hw + (h + 1) * HD]
        o = _rms(g_o[h], hgw_ref[:, hs])
        mix_ref[b, :, hw + h * HD:hw + (h + 1) * HD] = (o * (g * jax.nn.sigmoid(g))).astype(mix_ref.dtype)


def _even_kernel(has_state, x_ref, cos_ref, sin_ref, npre_ref, win_ref, retw_ref, hgw_ref,
                 lb_ref, *refs):
    if has_state:
        sret_ref, shg_ref, mix_ref, oret_ref, ohg_ref, proj_s = refs
    else:
        mix_ref, oret_ref, ohg_ref, proj_s = refs
    bb, c, d = x_ref.shape
    n_in = win_ref.shape[1]

    @pl.when(pl.program_id(1) == 0)
    def _():
        if has_state:
            oret_ref[...] = sret_ref[...]
            ohg_ref[...] = shg_ref[...]
        else:
            oret_ref[...] = jnp.zeros(oret_ref.shape, F32)
            ohg_ref[...] = jnp.zeros(ohg_ref.shape, F32)

    hn = _rms(x_ref[...].reshape(bb * c, d), npre_ref[...]).astype(BF16)
    for j in range(n_in // 512):
        sl = slice(j * 512, (j + 1) * 512)
        proj_s[:, :, sl] = jnp.dot(hn, win_ref[:, sl], preferred_element_type=F32).reshape(bb, c, 512)

    cos = cos_ref[...]
    sin = sin_ref[...]
    lb = lb_ref[...]

    def seq_body(b, carry):
        _even_seq(b, c, proj_s, cos, sin, lb, retw_ref, hgw_ref, oret_ref, ohg_ref, mix_ref)
        return carry

    if bb == 1:
        seq_body(0, 0)
    else:
        lax.fori_loop(0, bb, seq_body, 0)


def _even_call(x, cos, sin, npre, win, retw, hgw, lb, states, bb, c):
    nb_total, t, d = x.shape
    has_state = states is not None
    n_in = win.shape[1]
    grid = (nb_total // bb, t // c)
    st_spec = pl.BlockSpec((bb, H_RET, HD, HD), lambda b, i: (b, 0, 0, 0))
    in_specs = [pl.BlockSpec((bb, c, d), lambda b, i: (b, i, 0)),
                pl.BlockSpec((c, HD), lambda b, i: (i, 0)),
                pl.BlockSpec((c, HD), lambda b, i: (i, 0)),
                _const_spec((1, d)), _const_spec(win.shape),
                _const_spec((1, H_RET * HD)), _const_spec((1, H_HG * HD)), _const_spec((1, H_HG * HD))]
    args = [x, cos, sin, npre, win, retw, hgw, lb]
    if has_state:
        in_specs += [st_spec, st_spec]
        args += list(states)
    mix_w = (H_RET + H_HG) * HD
    return pl.pallas_call(
        functools.partial(_even_kernel, has_state),
        grid=grid,
        in_specs=in_specs,
        out_specs=[pl.BlockSpec((bb, c, mix_w), lambda b, i: (b, i, 0)), st_spec, st_spec],
        out_shape=[jax.ShapeDtypeStruct((nb_total, t, mix_w), BF16),
                   jax.ShapeDtypeStruct((nb_total, H_RET, HD, HD), F32),
                   jax.ShapeDtypeStruct((nb_total, H_HG, HD, HD), F32)],
        scratch_shapes=[pltpu.VMEM((bb, c, n_in), F32)],
        compiler_params=pltpu.CompilerParams(dimension_semantics=("arbitrary", "arbitrary"),
                                             vmem_limit_bytes=VMEM_LIMIT),
        name="even_mixer",
    )(*args)


def _odd_kernel(has_state, x_ref, npre_ref, win_ref, convw_ref, convb_ref, dtb_ref, alog_ref,
                dssm_ref, ssmw_ref, *refs):
    if has_state:
        sconv_ref, sssm_ref, y_ref, u_ref, oconv_ref, ossm_ref, proj_s, tail_s, sst_s = refs
    else:
        y_ref, u_ref, oconv_ref, ossm_ref, proj_s, tail_s, sst_s = refs
    bb, c, d = x_ref.shape
    n_in = win_ref.shape[1]
    d_inner = H_SSD * P_SSD
    conv_dim = d_inner + 2 * G_SSD * N_SSD
    conv_w = convw_ref.shape[0]
    halo = conv_w - 1
    u_w = u_ref.shape[2]

    @pl.when(pl.program_id(1) == 0)
    def _():
        tail_s[...] = jnp.zeros(tail_s.shape, F32)
        if has_state:
            def in_body(b, carry):
                tail_s[b, 8 - halo:8, :] = sconv_ref[b]
                for h in range(H_SSD):
                    g, hh = divmod(h, H_SSD // G_SSD)
                    sst_s[b, g, :, hh * P_SSD:(hh + 1) * P_SSD] = sssm_ref[b, h]
                return carry
            lax.fori_loop(0, bb, in_body, 0)
        else:
            sst_s[...] = jnp.zeros(sst_s.shape, F32)

    hn = _rms(x_ref[...].reshape(bb * c, d), npre_ref[...]).astype(BF16)
    col = 0
    while col < n_in:
        w = min(512, n_in - col)
        proj_s[:, :, col:col + w] = jnp.dot(hn, win_ref[:, col:col + w],
                                            preferred_element_type=F32).reshape(bb, c, w)
        col += w

    tri = _tri(c)
    triu = _tri(c, upper=True)
    causal = (lax.broadcasted_iota(jnp.int32, (c, c), 0) >= lax.broadcasted_iota(jnp.int32, (c, c), 1))
    lo_half = lax.broadcasted_iota(jnp.int32, (c, LANES), 1) < P_SSD
    row8 = lax.broadcasted_iota(jnp.int32, (8, conv_dim), 0)
    a_row = -jnp.exp(alog_ref[...])
    pairs_per_group = H_SSD // G_SSD // 2
    gw = d_inner // G_SSD

    def seq_body(b, carry):
        z = proj_s[b, :, 0:d_inner]
        xbc = proj_s[b, :, d_inner:d_inner + conv_dim]
        u_ref[b] = proj_s[b, :, d_inner + conv_dim:d_inner + conv_dim + u_w]
        dt_raw = proj_s[b, :, d_inner + conv_dim + u_w:d_inner + conv_dim + u_w + LANES]
        tail = tail_s[b]
        conv = convb_ref[...] + xbc * convw_ref[halo:halo + 1, :]
        for s in range(1, conv_w):
            rolled = pltpu.roll(xbc, s, axis=0)
            head = jnp.where(row8 < s, pltpu.roll(tail, s, axis=0), rolled[0:8])
            shifted = head if c == 8 else jnp.concatenate([head, rolled[8:]], axis=0)
            conv = conv + shifted * convw_ref[halo - s:halo - s + 1, :]
        tail_s[b] = xbc[c - 8:c]
        act = conv * jax.nn.sigmoid(conv)
        xs = act[:, :d_inner]

        dt = jax.nn.softplus(dt_raw + dtb_ref[...])
        la_parts = _split3(dt * a_row)
        cum = _dot3(tri, la_parts)
        cum_t = _dot3_tn(la_parts, triu)
        bms = [act[:, d_inner + g * N_SSD:d_inner + (g + 1) * N_SSD] for g in range(G_SSD)]
        cms = [act[:, d_inner + (G_SSD + g) * N_SSD:d_inner + (G_SSD + g + 1) * N_SSD]
               for g in range(G_SSD)]
        scores = [_dot_nt(cms[g], bms[g]) for g in range(G_SSD)]

        bc_cum = [jnp.broadcast_to(cum[:, h:h + 1], (c, LANES)) for h in range(H_SSD)]
        bc_dt = [jnp.broadcast_to(dt[:, h:h + 1], (c, LANES)) for h in range(H_SSD)]
        for g in range(G_SSD):
            xdts, xws, cums, cls = [], [], [], []
            for pp in range(pairs_per_group):
                p = g * pairs_per_group + pp
                cum_p = jnp.where(lo_half, bc_cum[2 * p], bc_cum[2 * p + 1])
                xdt = xs[:, p * LANES:(p + 1) * LANES] * jnp.where(lo_half, bc_dt[2 * p], bc_dt[2 * p + 1])
                cl = cum_p[c - 1:c, :]
                xdts.append(xdt)
                xws.append(xdt * jnp.exp(cl - cum_p))
                cums.append(cum_p)
                cls.append(cl)
            y_state = _dot(cms[g], sst_s[b, g])
            kv = _dot_tn(bms[g], jnp.concatenate(xws, axis=1))
            y_intra = []
            for pp in range(pairs_per_group):
                p = g * pairs_per_group + pp
                a_pair = []
                for h in (2 * p, 2 * p + 1):
                    diff = bc_cum[h][:, :c] - cum_t[h:h + 1, :]
                    a_pair.append(scores[g] * jnp.where(causal, jnp.exp(jnp.where(causal, diff, 0.0)), 0.0))
                both = _dot(jnp.concatenate(a_pair, axis=0), xdts[pp])
                y_intra.append(jnp.where(lo_half, both[:c], both[c:]))
            sst_s[b, g] = sst_s[b, g] * jnp.exp(jnp.concatenate(cls, axis=1)) + kv
            y = (jnp.concatenate(y_intra, axis=1) + y_state * jnp.exp(jnp.concatenate(cums, axis=1))
                 + dssm_ref[:, g * gw:(g + 1) * gw] * xs[:, g * gw:(g + 1) * gw])
            zg = z[:, g * gw:(g + 1) * gw]
            y = y * (zg * jax.nn.sigmoid(zg))
            y_ref[b, :, g * gw:(g + 1) * gw] = _rms(y, ssmw_ref[:, g * gw:(g + 1) * gw]).astype(y_ref.dtype)
        return carry

    if bb == 1:
        seq_body(0, 0)
    else:
        lax.fori_loop(0, bb, seq_body, 0)

    @pl.when(pl.program_id(1) == pl.num_programs(1) - 1)
    def _():
        def out_body(b, carry):
            oconv_ref[b] = tail_s[b, 8 - halo:8, :]
            for h in range(H_SSD):
                g, hh = divmod(h, H_SSD // G_SSD)
                ossm_ref[b, h] = sst_s[b, g, :, hh * P_SSD:(hh + 1) * P_SSD]
            return carry
        lax.fori_loop(0, bb, out_body, 0)


def _odd_call(x, npre, win, convw, convb, dtb, alog, dssm, ssmw, states, bb, c):
    nb_total, t, d = x.shape
    has_state = states is not None
    n_in = win.shape[1]
    d_inner = H_SSD * P_SSD
    conv_dim = d_inner + 2 * G_SSD * N_SSD
    u_w = S5_G * S5_GS
    halo = convw.shape[0] - 1
    grid = (nb_total // bb, t // c)
    conv_spec = pl.BlockSpec((bb, halo, conv_dim), lambda b, i: (b, 0, 0))
    ssm_spec = pl.BlockSpec((bb, H_SSD, N_SSD, P_SSD), lambda b, i: (b, 0, 0, 0))
    in_specs = [pl.BlockSpec((bb, c, d), lambda b, i: (b, i, 0)),
                _const_spec((1, d)), _const_spec(win.shape), _const_spec(convw.shape),
                _const_spec((1, conv_dim)), _const_spec((1, LANES)), _const_spec((1, LANES)),
                _const_spec((1, d_inner)), _const_spec((1, d_inner))]
    args = [x, npre, win, convw, convb, dtb, alog, dssm, ssmw]
    if has_state:
        in_specs += [conv_spec, ssm_spec]
        args += list(states)
    return pl.pallas_call(
        functools.partial(_odd_kernel, has_state),
        grid=grid,
        in_specs=in_specs,
        out_specs=[pl.BlockSpec((bb, c, d_inner), lambda b, i: (b, i, 0)),
                   pl.BlockSpec((bb, c, u_w), lambda b, i: (b, i, 0)),
                   conv_spec, ssm_spec],
        out_shape=[jax.ShapeDtypeStruct((nb_total, t, d_inner), BF16),
                   jax.ShapeDtypeStruct((nb_total, t, u_w), F32),
                   jax.ShapeDtypeStruct((nb_total, halo, conv_dim), F32),
                   jax.ShapeDtypeStruct((nb_total, H_SSD, N_SSD, P_SSD), F32)],
        scratch_shapes=[pltpu.VMEM((bb, c, n_in), F32),
                        pltpu.VMEM((bb, 8, conv_dim), F32),
                        pltpu.VMEM((bb, G_SSD, N_SSD, d_inner // G_SSD), F32)],
        compiler_params=pltpu.CompilerParams(dimension_semantics=("arbitrary", "arbitrary"),
                                             vmem_limit_bytes=VMEM_LIMIT),
        name="odd_mixer",
    )(*args)


S5_HALF_IN = S5_G * S5_GS // 2
S5_HALF_ST = S5_G * S5_P // 2
S5_SCAN_VREGS = 4


def _s5_pitch(steps):
    return steps if steps % 16 == 8 else steps + 8


def _s5_kernel(u_ref, bmat_ref, cmat_ref, lre_ref, lim_ref, d_ref, wglu_ref, bglu_ref,
               h0re_ref, h0im_ref, yd_ref, ore_ref, oim_ref, buf_s, slab_s, ut_s):
    r, steps, uw = u_ref.shape
    hs = S5_HALF_ST
    scan_w = max(LANES, S5_SCAN_VREGS * 8 * LANES // r)
    pitch = _s5_pitch(steps)
    nslab = uw // LANES

    @pl.when(pl.program_id(1) == 0)
    def _():
        ore_ref[...] = h0re_ref[...]
        oim_ref[...] = h0im_ref[...]

    for b in range(r):
        for j in range(nslab):
            slab_s[j, b * pitch:b * pitch + steps, :] = u_ref[b, :, j * LANES:(j + 1) * LANES]
    for t in range(steps):
        for j in range(nslab):
            ut_s[t * r:(t + 1) * r, j * LANES:(j + 1) * LANES] = slab_s[j, pl.ds(t, r, stride=pitch), :]
    u = ut_s[...]
    ub = u.astype(BF16)
    for half in range(2):
        buf_s[:, half * 2 * hs:(half + 1) * 2 * hs] = jnp.dot(
            ub[:, half * S5_HALF_IN:(half + 1) * S5_HALF_IN], bmat_ref[half],
            preferred_element_type=F32)

    for half in range(2):
        for j in range(hs // scan_w):
            st = slice(half * hs + j * scan_w, half * hs + (j + 1) * scan_w)
            cre = half * 2 * hs + j * scan_w
            cim = cre + hs
            lre = jnp.broadcast_to(lre_ref[:, st], (r, scan_w))
            lim = jnp.broadcast_to(lim_ref[:, st], (r, scan_w))

            def step(t, carry):
                hre, him = carry
                rows = pl.ds(pl.multiple_of(t * r, r), r)
                nre = lre * hre - lim * him + buf_s[rows, cre:cre + scan_w]
                nim = lre * him + lim * hre + buf_s[rows, cim:cim + scan_w]
                buf_s[rows, cre:cre + scan_w] = nre
                buf_s[rows, cim:cim + scan_w] = nim
                return nre, nim

            hre, him = lax.fori_loop(0, steps, step, (ore_ref[:, st], oim_ref[:, st]))
            ore_ref[:, st] = hre
            oim_ref[:, st] = him

    ys = []
    for half in range(2):
        ys.append(jnp.dot(buf_s[:, half * 2 * hs:(half + 1) * 2 * hs].astype(BF16), cmat_ref[half],
                          preferred_element_type=F32))
    y = jnp.concatenate(ys, axis=1) + d_ref[...] * u
    g = jax.nn.gelu(y)
    gate = jax.nn.sigmoid(jnp.dot(g.astype(BF16), wglu_ref[...], preferred_element_type=F32)
                          + bglu_ref[...])
    yd = g * gate
    for t in range(steps):
        for j in range(nslab):
            slab_s[j, pl.ds(t, r, stride=pitch), :] = yd[t * r:(t + 1) * r, j * LANES:(j + 1) * LANES]
    for b in range(r):
        for j in range(nslab):
            yd_ref[b, :, j * LANES:(j + 1) * LANES] = slab_s[j, b * pitch:b * pitch + steps, :].astype(
                yd_ref.dtype)


def _s5_call(u, bmat, cmat, lre, lim, dvec, wglu, bglu, h0re, h0im, steps, r):
    nb_total, t, uw = u.shape
    ns = S5_G * S5_P
    grid = (nb_total // r, t // steps)
    st_spec = pl.BlockSpec((r, ns), lambda b, i: (b, 0))
    return pl.pallas_call(
        _s5_kernel,
        grid=grid,
        in_specs=[pl.BlockSpec((r, steps, uw), lambda b, i: (b, i, 0)),
                  _const_spec(bmat.shape), _const_spec(cmat.shape),
                  _const_spec((1, ns)), _const_spec((1, ns)), _const_spec((1, uw)),
                  _const_spec(wglu.shape), _const_spec((1, uw)), st_spec, st_spec],
        out_specs=[pl.BlockSpec((r, steps, uw), lambda b, i: (b, i, 0)), st_spec, st_spec],
        out_shape=[jax.ShapeDtypeStruct((nb_total, t, uw), BF16),
                   jax.ShapeDtypeStruct((nb_total, ns), F32),
                   jax.ShapeDtypeStruct((nb_total, ns), F32)],
        scratch_shapes=[pltpu.VMEM((steps * r, 2 * ns), F32),
                        pltpu.VMEM((uw // LANES, r * _s5_pitch(steps), LANES), F32),
                        pltpu.VMEM((steps * r, uw), F32)],
        compiler_params=pltpu.CompilerParams(dimension_semantics=("arbitrary", "arbitrary"),
                                             vmem_limit_bytes=VMEM_LIMIT),
        name="s5_mixer",
    )(u, bmat, cmat, lre, lim, dvec, wglu, bglu, h0re, h0im)


def _rope_tables(pos):
    half = HD // 2
    inv_freq = ROPE_BASE ** (-jnp.arange(half, dtype=F32) / half)
    ang = pos[:, None] * inv_freq[None, :]
    cos, sin = jnp.cos(ang), jnp.sin(ang)
    return jnp.concatenate([cos, cos], axis=-1), jnp.concatenate([-sin, sin], axis=-1)


def _pad_lanes(v):
    return jnp.pad(v.astype(F32), (0, LANES - v.shape[0]))[None, :]


def _s5_tables(lam_re, lam_im, log_step, b_re, b_im, c_re, c_im):
    lr, li = lam_re.astype(F32), lam_im.astype(F32)
    dt = jnp.exp(log_step.astype(F32))[:, None]
    mag = jnp.exp(lr * dt)
    bar_re, bar_im = mag * jnp.cos(li * dt), mag * jnp.sin(li * dt)
    den = lr * lr + li * li
    cf_re = ((bar_re - 1.0) * lr + bar_im * li) / den
    cf_im = (bar_im * lr - (bar_re - 1.0) * li) / den
    bb_re = cf_re[..., None] * b_re.astype(F32) - cf_im[..., None] * b_im.astype(F32)
    bb_im = cf_re[..., None] * b_im.astype(F32) + cf_im[..., None] * b_re.astype(F32)
    gh = S5_G // 2
    eye = jnp.eye(gh, dtype=F32)

    def in_mat(b):
        return jnp.einsum('gpc,gk->gckp', b, eye).reshape(gh * S5_GS, gh * S5_P)

    def out_mat(cc):
        return jnp.einsum('gcp,gk->gpkc', cc, eye).reshape(gh * S5_P, gh * S5_GS)

    bmats, cmats = [], []
    for half in range(2):
        sl = slice(half * gh, (half + 1) * gh)
        bmats.append(jnp.concatenate([in_mat(bb_re[sl]), in_mat(bb_im[sl])], axis=1))
        cmats.append(jnp.concatenate([out_mat(c_re.astype(F32)[sl]), out_mat(-c_im.astype(F32)[sl])],
                                     axis=0))
    return (jnp.stack(bmats).astype(BF16), jnp.stack(cmats).astype(BF16),
            bar_re.reshape(1, -1), bar_im.reshape(1, -1))


def _row(v):
    return v.astype(F32).reshape(1, -1)


def _forward(x, pos_offset, states, p, cfg):
    nb, t, d = x.shape
    n = nb * t
    has_state = states is not None
    pos = jnp.arange(pos_offset, pos_offset + t, dtype=F32)
    cos, sin = _rope_tables(pos)
    lb = jnp.cumsum(jax.nn.softmax(p['hgrn_lower_bounds'].astype(F32), axis=0), axis=0)[0]
    mix, ret_o, hg_o = _even_call(
        x, cos, sin, _row(p['norm_mix_pre'][0]), p['w_in_even'][0].astype(BF16),
        _row(p['ret_norm_w'][0]), _row(p['hgrn_norm_w'][0]), _row(lb),
        (states['ret'][0], states['hgrn'][0]) if has_state else None, cfg['bb'], cfg['c'])
    h = _post_call(x.reshape(n, d), [mix.reshape(n, -1)], [p['w_out_even'][0].astype(BF16)],
                   _row(p['norm_mix_post'][0]), _row(p['norm_ffn_pre'][0]),
                   _row(p['norm_ffn_post'][0]), p['w_ffn_up'][0].astype(BF16),
                   p['w_ffn_down'][0].astype(BF16), cfg['rows'])
    d_inner = H_SSD * P_SSD
    conv_dim = d_inner + 2 * G_SSD * N_SSD
    u_w = S5_G * S5_GS
    w = p['w_in_odd'][0]
    c0 = d_inner + conv_dim
    w1 = jnp.concatenate([w[:, :c0], w[:, c0 + H_SSD:], w[:, c0:c0 + H_SSD],
                          jnp.zeros((d, LANES - H_SSD), w.dtype)], axis=1).astype(BF16)
    y, u, conv_o, ssm_o = _odd_call(
        h.reshape(nb, t, d), _row(p['norm_mix_pre'][1]), w1, p['conv_w'][0].astype(F32),
        _row(p['conv_b'][0]), _pad_lanes(p['dt_bias'][0]), _pad_lanes(p['a_log'][0]),
        _row(jnp.repeat(p['d_ssm'][0], P_SSD)), _row(p['ssm_norm_w'][0]),
        (states['conv'][0], states['ssm'][0]) if has_state else None, cfg['bb'], cfg['c'])
    bmat, cmat, lre, lim = _s5_tables(p['s5_lam_re'][0], p['s5_lam_im'][0], p['s5_log_step'][0],
                                      p['s5_b_re'][0], p['s5_b_im'][0], p['s5_c_re'][0],
                                      p['s5_c_im'][0])
    ns = S5_G * S5_P
    if has_state:
        h0re = states['s5_re'][0].reshape(nb, ns)
        h0im = states['s5_im'][0].reshape(nb, ns)
    else:
        h0re = jnp.zeros((nb, ns), F32)
        h0im = jnp.zeros((nb, ns), F32)
    yd, re_o, im_o = _s5_call(u, bmat, cmat, lre, lim, _row(p['s5_d'][0]),
                              p['w_glu'][0].astype(BF16), _row(p['b_glu'][0]), h0re, h0im,
                              cfg['s5_steps'], cfg['s5_rows'])
    yd = yd.reshape(n, u_w)
    wo = p['w_out_odd'][0].astype(BF16)
    h = _post_call(h, [y.reshape(n, d_inner), yd], [wo[:d_inner], wo[d_inner:]],
                   _row(p['norm_mix_post'][1]), _row(p['norm_ffn_pre'][1]),
                   _row(p['norm_ffn_post'][1]), p['w_ffn_up'][1].astype(BF16),
                   p['w_ffn_down'][1].astype(BF16), cfg['rows'])
    return (h.reshape(nb, t, d), ret_o[None], hg_o[None], ssm_o[None], conv_o[None],
            re_o.reshape(1, nb, S5_G, S5_P), im_o.reshape(1, nb, S5_G, S5_P))


PROMPT_CFG = dict(bb=1, c=128, rows=512, s5_steps=32, s5_rows=8)
SAMPLE_CFG = dict(bb=8, c=8, rows=512, s5_steps=8, s5_rows=32)


def kernel(x_prompt, x_sample, state_ret, state_hgrn, state_ssm, state_conv, state_s5_re, state_s5_im,
           norm_mix_pre, norm_mix_post, norm_ffn_pre, norm_ffn_post, w_in_even, w_out_even, ret_norm_w,
           hgrn_lower_bounds, hgrn_norm_w, w_in_odd, conv_w, conv_b, dt_bias, a_log, d_ssm, ssm_norm_w,
           s5_lam_re, s5_lam_im, s5_log_step, s5_b_re, s5_b_im, s5_c_re, s5_c_im, s5_d, w_glu, b_glu,
           w_out_odd, w_ffn_up, w_ffn_down):
    p = dict(norm_mix_pre=norm_mix_pre, norm_mix_post=norm_mix_post, norm_ffn_pre=norm_ffn_pre,
             norm_ffn_post=norm_ffn_post, w_in_even=w_in_even, w_out_even=w_out_even, ret_norm_w=ret_norm_w,
             hgrn_lower_bounds=hgrn_lower_bounds, hgrn_norm_w=hgrn_norm_w, w_in_odd=w_in_odd, conv_w=conv_w,
             conv_b=conv_b, dt_bias=dt_bias, a_log=a_log, d_ssm=d_ssm, ssm_norm_w=ssm_norm_w,
             s5_lam_re=s5_lam_re, s5_lam_im=s5_lam_im, s5_log_step=s5_log_step, s5_b_re=s5_b_re,
             s5_b_im=s5_b_im, s5_c_re=s5_c_re, s5_c_im=s5_c_im, s5_d=s5_d, w_glu=w_glu, b_glu=b_glu,
             w_out_odd=w_out_odd, w_ffn_up=w_ffn_up, w_ffn_down=w_ffn_down)
    past_len = 16384
    states = dict(ret=state_ret, hgrn=state_hgrn, ssm=state_ssm, conv=state_conv,
                  s5_re=state_s5_re, s5_im=state_s5_im)
    y_p, ret_p, hg_p, ssm_p, conv_p, re_p, im_p = _forward(x_prompt, 0, None, p, PROMPT_CFG)
    y_s, ret_s, hg_s, ssm_s, conv_s, re_s, im_s = _forward(x_sample, past_len, states, p, SAMPLE_CFG)
    return (y_p, y_s, ret_p, ret_s, hg_p, hg_s, ssm_p, ssm_s, conv_p, conv_s,
            re_p, re_s, im_p, im_s)
```

```python
import functools
import math

import jax
import jax.numpy as jnp
from jax import lax
from jax.experimental import pallas as pl
from jax.experimental.pallas import tpu as pltpu

F32 = jnp.float32
BF16 = jnp.bfloat16
EPS = 1e-6
ROPE_BASE = 10000.0
LANES = 128
SUBLANES = 8
VMEM_LIMIT = 56 * 1024 * 1024

H_RET = 4
H_HG = 4
HD = 128
H_SSD = 16
P_SSD = 64
N_SSD = 128
G_SSD = 2
S5_G = 32
S5_GS = 16
S5_P = 64
HG_BLK = 16


def _rms(x, w):
    return x * lax.rsqrt(jnp.mean(x * x, axis=-1, keepdims=True) + EPS) * w


def _silu(x):
    return x * jax.nn.sigmoid(x)


def _dot(a, b):
    return jnp.dot(a.astype(BF16), b.astype(BF16), preferred_element_type=F32)


def _dot_nt(a, b):
    return lax.dot_general(a.astype(BF16), b.astype(BF16), (((1,), (1,)), ((), ())),
                           preferred_element_type=F32)


def _dot_tn(a, b):
    return lax.dot_general(a.astype(BF16), b.astype(BF16), (((0,), (0,)), ((), ())),
                           preferred_element_type=F32)


def _split3(x):
    hi = x.astype(BF16)
    r1 = x - hi.astype(F32)
    mid = r1.astype(BF16)
    lo = (r1 - mid.astype(F32)).astype(BF16)
    return hi, mid, lo


def _dot3(a_exact, parts):
    acc = None
    for p in parts:
        d = jnp.dot(a_exact, p, preferred_element_type=F32)
        acc = d if acc is None else acc + d
    return acc


def _dot3_tn(parts, b_exact):
    acc = None
    for p in parts:
        d = lax.dot_general(p, b_exact, (((0,), (0,)), ((), ())), preferred_element_type=F32)
        acc = d if acc is None else acc + d
    return acc


def _ones_where(mask):
    return jnp.where(mask, 1.0, 0.0).astype(BF16)


def _seq_masks(rows, seq_len):
    sh = seq_len.bit_length() - 1
    ti = lax.broadcasted_iota(jnp.int32, (rows, rows), 0)
    si = lax.broadcasted_iota(jnp.int32, (rows, rows), 1)
    same = lax.shift_right_logical(ti, sh) == lax.shift_right_logical(si, sh)
    return ti, si, same, same & (ti >= si)


def _const_spec(shape):
    nd = len(shape)
    return pl.BlockSpec(shape, lambda *_: (0,) * nd, pipeline_mode=pl.Buffered(1))


def _post_kernel(n_mix, ff_chunk, h_ref, *refs):
    mix_refs = refs[:n_mix]
    wout_refs = refs[n_mix:2 * n_mix]
    npost_ref, nfpre_ref, nfpost_ref, wup_ref, wdn_ref, o_ref = refs[2 * n_mix:]
    acc = None
    for m_ref, w_ref in zip(mix_refs, wout_refs):
        d = jnp.dot(m_ref[...], w_ref[...], preferred_element_type=F32)
        acc = d if acc is None else acc + d
    h1 = h_ref[...] + _rms(acc, npost_ref[...])
    hn = _rms(h1, nfpre_ref[...]).astype(BF16)
    ff = None
    for j in range(wup_ref.shape[1] // ff_chunk):
        sl = slice(j * ff_chunk, (j + 1) * ff_chunk)
        up = jnp.dot(hn, wup_ref[:, sl], preferred_element_type=F32)
        act = jnp.square(jnp.maximum(up, 0.0)).astype(BF16)
        d = jnp.dot(act, wdn_ref[sl, :], preferred_element_type=F32)
        ff = d if ff is None else ff + d
    o_ref[...] = h1 + _rms(ff, nfpost_ref[...])


def _post_call(h, mixes, wouts, npost, nfpre, nfpost, wup, wdn, rows):
    n, d = h.shape
    n_mix = len(mixes)
    row_spec = lambda w: pl.BlockSpec((rows, w), lambda i: (i, 0))
    in_specs = ([row_spec(d)] + [row_spec(m.shape[1]) for m in mixes]
                + [_const_spec(w.shape) for w in wouts]
                + [_const_spec((1, d))] * 3 + [_const_spec(wup.shape), _const_spec(wdn.shape)])
    return pl.pallas_call(
        functools.partial(_post_kernel, n_mix, 1024),
        grid=(n // rows,),
        in_specs=in_specs,
        out_specs=row_spec(d),
        out_shape=jax.ShapeDtypeStruct((n, d), F32),
        compiler_params=pltpu.CompilerParams(dimension_semantics=("arbitrary",),
                                             vmem_limit_bytes=VMEM_LIMIT),
        name="post_ffn",
    )(h, *mixes, *wouts, npost, nfpre, nfpost, wup, wdn)


def _even_block(bb, seq_len, proj_s, cos, sin, lb, retw_ref, hgw_ref, oret_ref, ohg_ref, mix_ref):
    rows = bb * seq_len
    hw = H_RET * HD
    base = 4 * hw
    sh = seq_len.bit_length() - 1
    ti, si, same, causal = _seq_masks(rows, seq_len)
    tl = (lax.broadcasted_iota(jnp.int32, (rows, HD), 0) & (seq_len - 1)).astype(F32)
    seqs = [slice(b * seq_len, (b + 1) * seq_len) for b in range(bb)]
    one_shot = seq_len <= HG_BLK

    f = lb + (1.0 - lb) * jax.nn.sigmoid(proj_s[:, base + hw:base + 2 * hw])
    lf_parts = _split3(jnp.log(f))
    cum = _dot3(_ones_where(causal), lf_parts)
    if bb == 1:
        dtot = _dot3_tn(lf_parts, jnp.ones((rows, HD), BF16))
        cl = cum[rows - 1:rows, :]
    else:
        r_seq = lax.shift_right_logical(lax.broadcasted_iota(jnp.int32, (rows, HD), 0), sh)
        ind = _ones_where(r_seq == lax.broadcasted_iota(jnp.int32, (rows, HD), 1))
        dtot = _dot3_tn(lf_parts, ind)
        cl = _dot3(_ones_where(same), lf_parts)
    if one_shot:
        refc = _dot3(_ones_where(same & ((si & (seq_len - 1)) < seq_len // 2)), lf_parts)

    r_a, r_qs, r_kv, r_v = [], [], [], []
    for h in range(H_RET):
        col = lambda off: slice(off + h * HD, off + (h + 1) * HD)
        lg = math.log1p(-(2.0 ** (-5.0 - h)))
        q = proj_s[:, col(0)]
        k = proj_s[:, col(hw)]
        v = proj_s[:, col(2 * hw)]
        q = q * cos + pltpu.roll(q, HD // 2, axis=1) * sin
        k = (k * cos + pltpu.roll(k, HD // 2, axis=1) * sin) * (HD ** -0.5)
        kd = k * jnp.exp((float(seq_len - 1) - tl) * lg)
        r_a.append(_dot_nt(q, k))
        r_qs.append([_dot(q[s], oret_ref[b, h]) for b, s in enumerate(seqs)])
        r_kv.append([_dot_tn(kd[s], v[s]) for s in seqs])
        r_v.append(v)

    g_a, g_qs, g_kv, g_v = [], [], [], []
    for h in range(H_HG):
        hs = slice(h * HD, (h + 1) * HD)
        col = lambda off: slice(base + off + h * HD, base + off + (h + 1) * HD)
        q = proj_s[:, col(0)]
        v = proj_s[:, col(2 * hw)]
        k = 1.0 - f[:, hs]
        cum_h = cum[:, hs]
        qe = q * jnp.exp(cum_h)
        ke = k * jnp.exp(cl[:, hs] - cum_h)
        g_qs.append([_dot(qe[s], ohg_ref[b, h]) for b, s in enumerate(seqs)])
        g_kv.append([_dot_tn(ke[s], v[s]) for s in seqs])
        if one_shot:
            ref = refc[:, hs]
            g_a.append([_dot_nt(q * jnp.exp(cum_h - ref), k * jnp.exp(ref - cum_h))])
        else:
            blocks = []
            for i in range(rows // HG_BLK):
                lo, hi = i * HG_BLK, (i + 1) * HG_BLK
                ref = cum_h[lo + HG_BLK // 2 - 1:lo + HG_BLK // 2, :]
                blocks.append(_dot_nt(q[lo:hi] * jnp.exp(cum_h[lo:hi] - ref),
                                      k[:hi] * jnp.exp(ref - cum_h[:hi])))
            g_a.append(blocks)
        g_v.append(v)

    r_o = []
    for h in range(H_RET):
        lg = math.log1p(-(2.0 ** (-5.0 - h)))
        dm = jnp.where(causal, jnp.exp((ti - si).astype(F32) * lg), 0.0)
        qs = r_qs[h][0] if bb == 1 else jnp.concatenate(r_qs[h], axis=0)
        r_o.append(_dot(r_a[h] * dm, r_v[h]) + qs * jnp.exp((tl + 1.0) * lg))
        for b in range(bb):
            oret_ref[b, h] = oret_ref[b, h] * math.exp(seq_len * lg) + r_kv[h][b]

    g_o = []
    for h in range(H_HG):
        if one_shot:
            o = _dot(jnp.where(causal, g_a[h][0], 0.0), g_v[h])
        else:
            outs = []
            for i in range(rows // HG_BLK):
                lo, hi = i * HG_BLK, (i + 1) * HG_BLK
                tb = lax.broadcasted_iota(jnp.int32, (HG_BLK, hi), 0) + lo
                sb = lax.broadcasted_iota(jnp.int32, (HG_BLK, hi), 1)
                outs.append(_dot(jnp.where(tb >= sb, g_a[h][i], 0.0), g_v[h][:hi]))
            o = jnp.concatenate(outs, axis=0)
        qs = g_qs[h][0] if bb == 1 else jnp.concatenate(g_qs[h], axis=0)
        g_o.append(o + qs)
        for b in range(bb):
            dcol = dtot[h * HD:(h + 1) * HD, :]
            if bb > 1:
                dcol = jnp.broadcast_to(dcol[:, b:b + 1], (HD, HD))
            ohg_ref[b, h] = ohg_ref[b, h] * jnp.exp(dcol) + g_kv[h][b]

    for h in range(H_RET):
        col = lambda off: slice(off + h * HD, off + (h + 1) * HD)
        mu = jnp.mean(r_o[h], axis=-1, keepdims=True)
        oc = r_o[h] - mu
        var = jnp.mean(oc * oc, axis=-1, keepdims=True)
        o = oc * lax.rsqrt(var + EPS) * retw_ref[:, col(0)] * _silu(proj_s[:, col(3 * hw)])
        mix_ref[:, :, col(0)] = o.reshape(bb, seq_len, HD).astype(mix_ref.dtype)
    for h in range(H_HG):
        hs = slice(h * HD, (h + 1) * HD)
        g = proj_s[:, base + 3 * hw + h * HD:base + 3 * hw + (h + 1) * HD]
        o = _rms(g_o[h], hgw_ref[:, hs]) * _silu(g)
        mix_ref[:, :, hw + h * HD:hw + (h + 1) * HD] = o.reshape(bb, seq_len, HD).astype(mix_ref.dtype)


def _even_kernel(has_state, x_ref, cos_ref, sin_ref, npre_ref, win_ref, retw_ref, hgw_ref,
                 lb_ref, *refs):
    if has_state:
        sret_ref, shg_ref, mix_ref, oret_ref, ohg_ref, proj_s = refs
    else:
        mix_ref, oret_ref, ohg_ref, proj_s = refs
    bb, seq_len, d = x_ref.shape
    n_in = win_ref.shape[1]

    @pl.when(pl.program_id(1) == 0)
    def _():
        if has_state:
            oret_ref[...] = sret_ref[...]
            ohg_ref[...] = shg_ref[...]
        else:
            oret_ref[...] = jnp.zeros(oret_ref.shape, F32)
            ohg_ref[...] = jnp.zeros(ohg_ref.shape, F32)

    hn = _rms(x_ref[...].reshape(bb * seq_len, d), npre_ref[...]).astype(BF16)
    for j in range(n_in // 512):
        sl = slice(j * 512, (j + 1) * 512)
        proj_s[:, sl] = jnp.dot(hn, win_ref[:, sl], preferred_element_type=F32)
    _even_block(bb, seq_len, proj_s, cos_ref[...], sin_ref[...], lb_ref[...], retw_ref, hgw_ref,
                oret_ref, ohg_ref, mix_ref)


def _even_call(x, cos, sin, npre, win, retw, hgw, lb, states, bb, c):
    nb_total, t, d = x.shape
    has_state = states is not None
    n_in = win.shape[1]
    rows = bb * c
    grid = (nb_total // bb, t // c)
    st_spec = pl.BlockSpec((bb, H_RET, HD, HD), lambda b, i: (b, 0, 0, 0))
    in_specs = [pl.BlockSpec((bb, c, d), lambda b, i: (b, i, 0)),
                pl.BlockSpec((rows, HD), lambda b, i: (i, 0)),
                pl.BlockSpec((rows, HD), lambda b, i: (i, 0)),
                _const_spec((1, d)), _const_spec(win.shape),
                _const_spec((1, H_RET * HD)), _const_spec((1, H_HG * HD)), _const_spec((1, H_HG * HD))]
    args = [x, cos, sin, npre, win, retw, hgw, lb]
    if has_state:
        in_specs += [st_spec, st_spec]
        args += list(states)
    mix_w = (H_RET + H_HG) * HD
    return pl.pallas_call(
        functools.partial(_even_kernel, has_state),
        grid=grid,
        in_specs=in_specs,
        out_specs=[pl.BlockSpec((bb, c, mix_w), lambda b, i: (b, i, 0)), st_spec, st_spec],
        out_shape=[jax.ShapeDtypeStruct((nb_total, t, mix_w), BF16),
                   jax.ShapeDtypeStruct((nb_total, H_RET, HD, HD), F32),
                   jax.ShapeDtypeStruct((nb_total, H_HG, HD, HD), F32)],
        scratch_shapes=[pltpu.VMEM((rows, n_in), F32)],
        compiler_params=pltpu.CompilerParams(dimension_semantics=("arbitrary", "arbitrary"),
                                             vmem_limit_bytes=VMEM_LIMIT),
        name="even_mixer",
    )(*args)


def _odd_kernel(has_state, x_ref, npre_ref, win_ref, convw_ref, convb_ref, dtb_ref, alog_ref,
                dssm_ref, ssmw_ref, *refs):
    if has_state:
        sconv_ref, sst_ref, y_ref, u_ref, oconv_ref, ost_ref, proj_s, tail_s = refs
    else:
        y_ref, u_ref, oconv_ref, ost_ref, proj_s, tail_s = refs
    bb, seq_len, d = x_ref.shape
    rows = bb * seq_len
    n_in = win_ref.shape[1]
    d_inner = H_SSD * P_SSD
    conv_dim = d_inner + 2 * G_SSD * N_SSD
    conv_w = convw_ref.shape[0]
    halo = conv_w - 1
    u_w = u_ref.shape[2]
    gw = d_inner // G_SSD
    heads_per_group = H_SSD // G_SSD
    pairs_per_group = heads_per_group // 2
    seqs = [slice(b * seq_len, (b + 1) * seq_len) for b in range(bb)]

    @pl.when(pl.program_id(1) == 0)
    def _():
        tail_s[...] = jnp.zeros(tail_s.shape, F32)
        if has_state:
            for b in range(bb):
                tail_s[(b + 1) * SUBLANES - halo:(b + 1) * SUBLANES, :] = sconv_ref[b]
            ost_ref[...] = sst_ref[...]
        else:
            ost_ref[...] = jnp.zeros(ost_ref.shape, F32)

    hn = _rms(x_ref[...].reshape(rows, d), npre_ref[...]).astype(BF16)
    col = 0
    while col < n_in:
        w = min(512, n_in - col)
        proj_s[:, col:col + w] = jnp.dot(hn, win_ref[:, col:col + w], preferred_element_type=F32)
        col += w

    z = proj_s[:, 0:d_inner]
    xbc = proj_s[:, d_inner:d_inner + conv_dim]
    u_ref[...] = proj_s[:, d_inner + conv_dim:d_inner + conv_dim + u_w].reshape(bb, seq_len, u_w)
    dt_raw = proj_s[:, d_inner + conv_dim + u_w:d_inner + conv_dim + u_w + LANES]

    tail = tail_s[...]
    conv = convb_ref[...] + xbc * convw_ref[halo:halo + 1, :]
    for s in range(1, conv_w):
        rolled = pltpu.roll(xbc, s, axis=0)
        if seq_len == SUBLANES:
            local = lax.broadcasted_iota(jnp.int32, (rows, conv_dim), 0) & (SUBLANES - 1)
            shifted = jnp.where(local < s, pltpu.roll(tail, rows - SUBLANES + s, axis=0), rolled)
        else:
            local = lax.broadcasted_iota(jnp.int32, (SUBLANES, conv_dim), 0)
            head = jnp.where(local < s, pltpu.roll(tail, s, axis=0), rolled[0:SUBLANES])
            shifted = jnp.concatenate([head, rolled[SUBLANES:]], axis=0)
        conv = conv + shifted * convw_ref[halo - s:halo - s + 1, :]
    tail_s[...] = xbc if seq_len == SUBLANES else xbc[rows - SUBLANES:rows]
    act = _silu(conv)
    xs = act[:, :d_inner]

    ti, si, same, causal = _seq_masks(rows, seq_len)
    lo_half = lax.broadcasted_iota(jnp.int32, (rows, LANES), 1) < P_SSD
    dt = jax.nn.softplus(dt_raw + dtb_ref[...])
    la_parts = _split3(dt * (-jnp.exp(alog_ref[...])))
    cum = _dot3(_ones_where(causal), la_parts)
    cum_t = _dot3_tn(la_parts, _ones_where(same & (ti <= si)))
    rest = _dot3(_ones_where(same & (si > ti)), la_parts)
    bms = [act[:, d_inner + g * N_SSD:d_inner + (g + 1) * N_SSD] for g in range(G_SSD)]
    cms = [act[:, d_inner + (G_SSD + g) * N_SSD:d_inner + (G_SSD + g + 1) * N_SSD]
           for g in range(G_SSD)]
    scores = [_dot_nt(cms[g], bms[g]) for g in range(G_SSD)]

    bc = lambda a, h: jnp.broadcast_to(a[:, h:h + 1], (rows, LANES))
    bc_cum = [bc(cum, h) for h in range(H_SSD)]
    pair = lambda a, p: jnp.where(lo_half, bc(a, 2 * p), bc(a, 2 * p + 1))

    for g in range(G_SSD):
        xdts, xws, cums = [], [], []
        for pp in range(pairs_per_group):
            p = g * pairs_per_group + pp
            xdt = xs[:, p * LANES:(p + 1) * LANES] * pair(dt, p)
            xdts.append(xdt)
            xws.append(xdt * jnp.exp(pair(rest, p)))
            cums.append(jnp.where(lo_half, bc_cum[2 * p], bc_cum[2 * p + 1]))
        xw = jnp.concatenate(xws, axis=1)
        y_st = [_dot_nt(cms[g][s], ost_ref[b, g]) for b, s in enumerate(seqs)]
        kv_t = [_dot_tn(xw[s], bms[g][s]) for s in seqs]
        y_intra = []
        for pp in range(pairs_per_group):
            p = g * pairs_per_group + pp
            a_pair = []
            for h in (2 * p, 2 * p + 1):
                diff = bc_cum[h][:, :rows] - cum_t[h:h + 1, :]
                a_pair.append(scores[g] * jnp.where(causal, jnp.exp(jnp.where(causal, diff, 0.0)), 0.0))
            both = _dot(jnp.concatenate(a_pair, axis=0), xdts[pp])
            y_intra.append(jnp.where(lo_half, both[:rows], both[rows:]))
        for b in range(bb):
            last = (b + 1) * seq_len - 1
            dec = [jnp.broadcast_to(jnp.exp(bc_cum[g * heads_per_group + hh][last:last + 1, :]),
                                    (P_SSD, N_SSD)) for hh in range(heads_per_group)]
            ost_ref[b, g] = ost_ref[b, g] * jnp.concatenate(dec, axis=0) + kv_t[b]
        y_state = y_st[0] if bb == 1 else jnp.concatenate(y_st, axis=0)
        gs = slice(g * gw, (g + 1) * gw)
        y = (jnp.concatenate(y_intra, axis=1) + y_state * jnp.exp(jnp.concatenate(cums, axis=1))
             + dssm_ref[:, gs] * xs[:, gs])
        y = _rms(y * _silu(z[:, gs]), ssmw_ref[:, gs])
        y_ref[:, :, gs] = y.reshape(bb, seq_len, gw).astype(y_ref.dtype)

    @pl.when(pl.program_id(1) == pl.num_programs(1) - 1)
    def _():
        for b in range(bb):
            oconv_ref[b] = tail_s[(b + 1) * SUBLANES - halo:(b + 1) * SUBLANES, :]


def _odd_call(x, npre, win, convw, convb, dtb, alog, dssm, ssmw, states, bb, c):
    nb_total, t, d = x.shape
    has_state = states is not None
    n_in = win.shape[1]
    d_inner = H_SSD * P_SSD
    conv_dim = d_inner + 2 * G_SSD * N_SSD
    u_w = S5_G * S5_GS
    halo = convw.shape[0] - 1
    rows = bb * c
    grid = (nb_total // bb, t // c)
    st_shape = (G_SSD, d_inner // G_SSD, N_SSD)
    conv_spec = pl.BlockSpec((bb, halo, conv_dim), lambda b, i: (b, 0, 0))
    ssm_spec = pl.BlockSpec((bb,) + st_shape, lambda b, i: (b, 0, 0, 0))
    in_specs = [pl.BlockSpec((bb, c, d), lambda b, i: (b, i, 0)),
                _const_spec((1, d)), _const_spec(win.shape), _const_spec(convw.shape),
                _const_spec((1, conv_dim)), _const_spec((1, LANES)), _const_spec((1, LANES)),
                _const_spec((1, d_inner)), _const_spec((1, d_inner))]
    args = [x, npre, win, convw, convb, dtb, alog, dssm, ssmw]
    if has_state:
        in_specs += [conv_spec, ssm_spec]
        args += list(states)
    return pl.pallas_call(
        functools.partial(_odd_kernel, has_state),
        grid=grid,
        in_specs=in_specs,
        out_specs=[pl.BlockSpec((bb, c, d_inner), lambda b, i: (b, i, 0)),
                   pl.BlockSpec((bb, c, u_w), lambda b, i: (b, i, 0)),
                   conv_spec, ssm_spec],
        out_shape=[jax.ShapeDtypeStruct((nb_total, t, d_inner), BF16),
                   jax.ShapeDtypeStruct((nb_total, t, u_w), F32),
                   jax.ShapeDtypeStruct((nb_total, halo, conv_dim), F32),
                   jax.ShapeDtypeStruct((nb_total,) + st_shape, F32)],
        scratch_shapes=[pltpu.VMEM((rows, n_in), F32),
                        pltpu.VMEM((bb * SUBLANES, conv_dim), F32)],
        compiler_params=pltpu.CompilerParams(dimension_semantics=("arbitrary", "arbitrary"),
                                             vmem_limit_bytes=VMEM_LIMIT),
        name="odd_mixer",
    )(*args)


S5_HALF_IN = S5_G * S5_GS // 2
S5_HALF_ST = S5_G * S5_P // 2
S5_SCAN_VREGS = 4


def _s5_pitch(steps):
    return steps if steps % 16 == 8 else steps + 8


def _s5_kernel(u_ref, bmat_ref, cmat_ref, lre_ref, lim_ref, d_ref, wglu_ref, bglu_ref,
               h0re_ref, h0im_ref, yd_ref, ore_ref, oim_ref, buf_s, slab_s, ut_s):
    r, steps, uw = u_ref.shape
    hs = S5_HALF_ST
    scan_w = max(LANES, S5_SCAN_VREGS * SUBLANES * LANES // r)
    pitch = _s5_pitch(steps)
    nslab = uw // LANES

    @pl.when(pl.program_id(1) == 0)
    def _():
        ore_ref[...] = h0re_ref[...]
        oim_ref[...] = h0im_ref[...]

    for b in range(r):
        for j in range(nslab):
            slab_s[j, b * pitch:b * pitch + steps, :] = u_ref[b, :, j * LANES:(j + 1) * LANES]
    for t in range(steps):
        for j in range(nslab):
            ut_s[t * r:(t + 1) * r, j * LANES:(j + 1) * LANES] = slab_s[j, pl.ds(t, r, stride=pitch), :]
    u = ut_s[...]
    ub = u.astype(BF16)
    for half in range(2):
        buf_s[:, half * 2 * hs:(half + 1) * 2 * hs] = jnp.dot(
            ub[:, half * S5_HALF_IN:(half + 1) * S5_HALF_IN], bmat_ref[half],
            preferred_element_type=F32)

    for half in range(2):
        for j in range(hs // scan_w):
            st = slice(half * hs + j * scan_w, half * hs + (j + 1) * scan_w)
            cre = half * 2 * hs + j * scan_w
            cim = cre + hs
            lre = jnp.broadcast_to(lre_ref[:, st], (r, scan_w))
            lim = jnp.broadcast_to(lim_ref[:, st], (r, scan_w))

            def step(t, carry):
                hre, him = carry
                rws = pl.ds(pl.multiple_of(t * r, r), r)
                nre = lre * hre - lim * him + buf_s[rws, cre:cre + scan_w]
                nim = lre * him + lim * hre + buf_s[rws, cim:cim + scan_w]
                buf_s[rws, cre:cre + scan_w] = nre
                buf_s[rws, cim:cim + scan_w] = nim
                return nre, nim

            hre, him = lax.fori_loop(0, steps, step, (ore_ref[:, st], oim_ref[:, st]))
            ore_ref[:, st] = hre
            oim_ref[:, st] = him

    ys = []
    for half in range(2):
        ys.append(jnp.dot(buf_s[:, half * 2 * hs:(half + 1) * 2 * hs].astype(BF16), cmat_ref[half],
                          preferred_element_type=F32))
    y = jnp.concatenate(ys, axis=1) + d_ref[...] * u
    g = jax.nn.gelu(y)
    gate = jax.nn.sigmoid(jnp.dot(g.astype(BF16), wglu_ref[...], preferred_element_type=F32)
                          + bglu_ref[...])
    yd = g * gate
    for t in range(steps):
        for j in range(nslab):
            slab_s[j, pl.ds(t, r, stride=pitch), :] = yd[t * r:(t + 1) * r, j * LANES:(j + 1) * LANES]
    for b in range(r):
        for j in range(nslab):
            yd_ref[b, :, j * LANES:(j + 1) * LANES] = slab_s[j, b * pitch:b * pitch + steps, :].astype(
                yd_ref.dtype)


def _s5_call(u, bmat, cmat, lre, lim, dvec, wglu, bglu, h0re, h0im, steps, r):
    nb_total, t, uw = u.shape
    ns = S5_G * S5_P
    grid = (nb_total // r, t // steps)
    st_spec = pl.BlockSpec((r, ns), lambda b, i: (b, 0))
    return pl.pallas_call(
        _s5_kernel,
        grid=grid,
        in_specs=[pl.BlockSpec((r, steps, uw), lambda b, i: (b, i, 0)),
                  _const_spec(bmat.shape), _const_spec(cmat.shape),
                  _const_spec((1, ns)), _const_spec((1, ns)), _const_spec((1, uw)),
                  _const_spec(wglu.shape), _const_spec((1, uw)), st_spec, st_spec],
        out_specs=[pl.BlockSpec((r, steps, uw), lambda b, i: (b, i, 0)), st_spec, st_spec],
        out_shape=[jax.ShapeDtypeStruct((nb_total, t, uw), BF16),
                   jax.ShapeDtypeStruct((nb_total, ns), F32),
                   jax.ShapeDtypeStruct((nb_total, ns), F32)],
        scratch_shapes=[pltpu.VMEM((steps * r, 2 * ns), F32),
                        pltpu.VMEM((uw // LANES, r * _s5_pitch(steps), LANES), F32),
                        pltpu.VMEM((steps * r, uw), F32)],
        compiler_params=pltpu.CompilerParams(dimension_semantics=("arbitrary", "arbitrary"),
                                             vmem_limit_bytes=VMEM_LIMIT),
        name="s5_mixer",
    )(u, bmat, cmat, lre, lim, dvec, wglu, bglu, h0re, h0im)


def _rope_tables(pos):
    half = HD // 2
    inv_freq = ROPE_BASE ** (-jnp.arange(half, dtype=F32) / half)
    ang = pos[:, None] * inv_freq[None, :]
    cos, sin = jnp.cos(ang), jnp.sin(ang)
    return jnp.concatenate([cos, cos], axis=-1), jnp.concatenate([-sin, sin], axis=-1)


def _pad_lanes(v):
    return jnp.pad(v.astype(F32), (0, LANES - v.shape[0]))[None, :]


def _s5_tables(lam_re, lam_im, log_step, b_re, b_im, c_re, c_im):
    lr, li = lam_re.astype(F32), lam_im.astype(F32)
    dt = jnp.exp(log_step.astype(F32))[:, None]
    mag = jnp.exp(lr * dt)
    bar_re, bar_im = mag * jnp.cos(li * dt), mag * jnp.sin(li * dt)
    den = lr * lr + li * li
    cf_re = ((bar_re - 1.0) * lr + bar_im * li) / den
    cf_im = (bar_im * lr - (bar_re - 1.0) * li) / den
    bb_re = cf_re[..., None] * b_re.astype(F32) - cf_im[..., None] * b_im.astype(F32)
    bb_im = cf_re[..., None] * b_im.astype(F32) + cf_im[..., None] * b_re.astype(F32)
    gh = S5_G // 2
    eye = jnp.eye(gh, dtype=F32)

    def in_mat(b):
        return jnp.einsum('gpc,gk->gckp', b, eye).reshape(gh * S5_GS, gh * S5_P)

    def out_mat(cc):
        return jnp.einsum('gcp,gk->gpkc', cc, eye).reshape(gh * S5_P, gh * S5_GS)

    bmats, cmats = [], []
    for half in range(2):
        sl = slice(half * gh, (half + 1) * gh)
        bmats.append(jnp.concatenate([in_mat(bb_re[sl]), in_mat(bb_im[sl])], axis=1))
        cmats.append(jnp.concatenate([out_mat(c_re.astype(F32)[sl]), out_mat(-c_im.astype(F32)[sl])],
                                     axis=0))
    return (jnp.stack(bmats).astype(BF16), jnp.stack(cmats).astype(BF16),
            bar_re.reshape(1, -1), bar_im.reshape(1, -1))


def _row(v):
    return v.astype(F32).reshape(1, -1)


def _forward(x, pos_offset, states, p, cfg):
    nb, t, d = x.shape
    n = nb * t
    bb, c = cfg['bb'], cfg['c']
    has_state = states is not None
    pos = jnp.arange(pos_offset, pos_offset + t, dtype=F32)
    cos, sin = _rope_tables(pos)
    tile_rows = lambda a: jnp.tile(a.reshape(t // c, 1, c, HD), (1, bb, 1, 1)).reshape(-1, HD)
    lb = jnp.cumsum(jax.nn.softmax(p['hgrn_lower_bounds'].astype(F32), axis=0), axis=0)[0]
    mix, ret_o, hg_o = _even_call(
        x, tile_rows(cos), tile_rows(sin), _row(p['norm_mix_pre'][0]), p['w_in_even'][0].astype(BF16),
        _row(p['ret_norm_w'][0]), _row(p['hgrn_norm_w'][0]), _row(lb),
        (states['ret'][0], states['hgrn'][0]) if has_state else None, bb, c)
    h = _post_call(x.reshape(n, d), [mix.reshape(n, -1)], [p['w_out_even'][0].astype(BF16)],
                   _row(p['norm_mix_post'][0]), _row(p['norm_ffn_pre'][0]),
                   _row(p['norm_ffn_post'][0]), p['w_ffn_up'][0].astype(BF16),
                   p['w_ffn_down'][0].astype(BF16), cfg['rows'])
    d_inner = H_SSD * P_SSD
    conv_dim = d_inner + 2 * G_SSD * N_SSD
    u_w = S5_G * S5_GS
    w = p['w_in_odd'][0]
    c0 = d_inner + conv_dim
    w1 = jnp.concatenate([w[:, :c0], w[:, c0 + H_SSD:], w[:, c0:c0 + H_SSD],
                          jnp.zeros((d, LANES - H_SSD), w.dtype)], axis=1).astype(BF16)
    st_shape = (nb, G_SSD, d_inner // G_SSD, N_SSD)
    odd_states = None
    if has_state:
        odd_states = (states['conv'][0], jnp.swapaxes(states['ssm'][0], -1, -2).reshape(st_shape))
    y, u, conv_o, sst_o = _odd_call(
        h.reshape(nb, t, d), _row(p['norm_mix_pre'][1]), w1, p['conv_w'][0].astype(F32),
        _row(p['conv_b'][0]), _pad_lanes(p['dt_bias'][0]), _pad_lanes(p['a_log'][0]),
        _row(jnp.repeat(p['d_ssm'][0], P_SSD)), _row(p['ssm_norm_w'][0]), odd_states, bb, c)
    ssm_o = jnp.swapaxes(sst_o.reshape(nb, H_SSD, P_SSD, N_SSD), -1, -2)
    bmat, cmat, lre, lim = _s5_tables(p['s5_lam_re'][0], p['s5_lam_im'][0], p['s5_log_step'][0],
                                      p['s5_b_re'][0], p['s5_b_im'][0], p['s5_c_re'][0],
                                      p['s5_c_im'][0])
    ns = S5_G * S5_P
    if has_state:
        h0re = states['s5_re'][0].reshape(nb, ns)
        h0im = states['s5_im'][0].reshape(nb, ns)
    else:
        h0re = jnp.zeros((nb, ns), F32)
        h0im = jnp.zeros((nb, ns), F32)
    yd, re_o, im_o = _s5_call(u, bmat, cmat, lre, lim, _row(p['s5_d'][0]),
                              p['w_glu'][0].astype(BF16), _row(p['b_glu'][0]), h0re, h0im,
                              cfg['s5_steps'], cfg['s5_rows'])
    yd = yd.reshape(n, u_w)
    wo = p['w_out_odd'][0].astype(BF16)
    h = _post_call(h, [y.reshape(n, d_inner), yd], [wo[:d_inner], wo[d_inner:]],
                   _row(p['norm_mix_post'][1]), _row(p['norm_ffn_pre'][1]),
                   _row(p['norm_ffn_post'][1]), p['w_ffn_up'][1].astype(BF16),
                   p['w_ffn_down'][1].astype(BF16), cfg['rows'])
    return (h.reshape(nb, t, d), ret_o[None], hg_o[None], ssm_o[None], conv_o[None],
            re_o.reshape(1, nb, S5_G, S5_P), im_o.reshape(1, nb, S5_G, S5_P))


PROMPT_CFG = dict(bb=1, c=128, rows=512, s5_steps=32, s5_rows=8)
SAMPLE_CFG = dict(bb=8, c=8, rows=512, s5_steps=8, s5_rows=32)


def kernel(x_prompt, x_sample, state_ret, state_hgrn, state_ssm, state_conv, state_s5_re, state_s5_im,
           norm_mix_pre, norm_mix_post, norm_ffn_pre, norm_ffn_post, w_in_even, w_out_even, ret_norm_w,
           hgrn_lower_bounds, hgrn_norm_w, w_in_odd, conv_w, conv_b, dt_bias, a_log, d_ssm, ssm_norm_w,
           s5_lam_re, s5_lam_im, s5_log_step, s5_b_re, s5_b_im, s5_c_re, s5_c_im, s5_d, w_glu, b_glu,
           w_out_odd, w_ffn_up, w_ffn_down):
    p = dict(norm_mix_pre=norm_mix_pre, norm_mix_post=norm_mix_post, norm_ffn_pre=norm_ffn_pre,
             norm_ffn_post=norm_ffn_post, w_in_even=w_in_even, w_out_even=w_out_even, ret_norm_w=ret_norm_w,
             hgrn_lower_bounds=hgrn_lower_bounds, hgrn_norm_w=hgrn_norm_w, w_in_odd=w_in_odd, conv_w=conv_w,
             conv_b=conv_b, dt_bias=dt_bias, a_log=a_log, d_ssm=d_ssm, ssm_norm_w=ssm_norm_w,
             s5_lam_re=s5_lam_re, s5_lam_im=s5_lam_im, s5_log_step=s5_log_step, s5_b_re=s5_b_re,
             s5_b_im=s5_b_im, s5_c_re=s5_c_re, s5_c_im=s5_c_im, s5_d=s5_d, w_glu=w_glu, b_glu=b_glu,
             w_out_odd=w_out_odd, w_ffn_up=w_ffn_up, w_ffn_down=w_ffn_down)
    past_len = 16384
    states = dict(ret=state_ret, hgrn=state_hgrn, ssm=state_ssm, conv=state_conv,
                  s5_re=state_s5_re, s5_im=state_s5_im)
    y_p, ret_p, hg_p, ssm_p, conv_p, re_p, im_p = _forward(x_prompt, 0, None, p, PROMPT_CFG)
    y_s, ret_s, hg_s, ssm_s, conv_s, re_s, im_s = _forward(x_sample, past_len, states, p, SAMPLE_CFG)
    return (y_p, y_s, ret_p, ret_s, hg_p, hg_s, ssm_p, ssm_s, conv_p, conv_s,
            re_p, re_s, im_p, im_s)
```

```python
import functools
import math

import jax
import jax.numpy as jnp
from jax import lax
from jax.experimental import pallas as pl
from jax.experimental.pallas import tpu as pltpu

F32 = jnp.float32
BF16 = jnp.bfloat16
EPS = 1e-6
ROPE_BASE = 10000.0
LANES = 128
SUBLANES = 8
VMEM_LIMIT = 56 * 1024 * 1024

H_RET = 4
H_HG = 4
HD = 128
H_SSD = 16
P_SSD = 64
N_SSD = 128
G_SSD = 2
S5_G = 32
S5_GS = 16
S5_P = 64
HG_BLK = 16


def _rms(x, w):
    return x * lax.rsqrt(jnp.mean(x * x, axis=-1, keepdims=True) + EPS) * w


def _silu(x):
    return x * jax.nn.sigmoid(x)


def _dot(a, b):
    return jnp.dot(a.astype(BF16), b.astype(BF16), preferred_element_type=F32)


def _dot_nt(a, b):
    return lax.dot_general(a.astype(BF16), b.astype(BF16), (((1,), (1,)), ((), ())),
                           preferred_element_type=F32)


def _dot_tn(a, b):
    return lax.dot_general(a.astype(BF16), b.astype(BF16), (((0,), (0,)), ((), ())),
                           preferred_element_type=F32)


def _split3(x):
    hi = x.astype(BF16)
    r1 = x - hi.astype(F32)
    mid = r1.astype(BF16)
    lo = (r1 - mid.astype(F32)).astype(BF16)
    return hi, mid, lo


def _dot3(a_exact, parts):
    acc = None
    for p in parts:
        d = jnp.dot(a_exact, p, preferred_element_type=F32)
        acc = d if acc is None else acc + d
    return acc


def _dot3_tn(parts, b_exact):
    acc = None
    for p in parts:
        d = lax.dot_general(p, b_exact, (((0,), (0,)), ((), ())), preferred_element_type=F32)
        acc = d if acc is None else acc + d
    return acc


def _ones_where(mask):
    return jnp.where(mask, 1.0, 0.0).astype(BF16)


def _seq_masks(rows, seq_len):
    sh = seq_len.bit_length() - 1
    ti = lax.broadcasted_iota(jnp.int32, (rows, rows), 0)
    si = lax.broadcasted_iota(jnp.int32, (rows, rows), 1)
    same = lax.shift_right_logical(ti, sh) == lax.shift_right_logical(si, sh)
    return ti, si, same, same & (ti >= si)


def _const_spec(shape):
    nd = len(shape)
    return pl.BlockSpec(shape, lambda *_: (0,) * nd, pipeline_mode=pl.Buffered(1))


def _post_kernel(n_mix, ff_chunk, h_ref, *refs):
    mix_refs = refs[:n_mix]
    wout_refs = refs[n_mix:2 * n_mix]
    npost_ref, nfpre_ref, nfpost_ref, wup_ref, wdn_ref, o_ref = refs[2 * n_mix:]
    acc = None
    for m_ref, w_ref in zip(mix_refs, wout_refs):
        d = jnp.dot(m_ref[...], w_ref[...], preferred_element_type=F32)
        acc = d if acc is None else acc + d
    h1 = h_ref[...] + _rms(acc, npost_ref[...])
    hn = _rms(h1, nfpre_ref[...]).astype(BF16)
    ff = None
    for j in range(wup_ref.shape[1] // ff_chunk):
        sl = slice(j * ff_chunk, (j + 1) * ff_chunk)
        up = jnp.dot(hn, wup_ref[:, sl], preferred_element_type=F32)
        act = jnp.square(jnp.maximum(up, 0.0)).astype(BF16)
        d = jnp.dot(act, wdn_ref[sl, :], preferred_element_type=F32)
        ff = d if ff is None else ff + d
    o_ref[...] = h1 + _rms(ff, nfpost_ref[...])


def _post_call(h, mixes, wouts, npost, nfpre, nfpost, wup, wdn, rows):
    n, d = h.shape
    n_mix = len(mixes)
    row_spec = lambda w: pl.BlockSpec((rows, w), lambda i: (i, 0))
    in_specs = ([row_spec(d)] + [row_spec(m.shape[1]) for m in mixes]
                + [_const_spec(w.shape) for w in wouts]
                + [_const_spec((1, d))] * 3 + [_const_spec(wup.shape), _const_spec(wdn.shape)])
    return pl.pallas_call(
        functools.partial(_post_kernel, n_mix, 1024),
        grid=(n // rows,),
        in_specs=in_specs,
        out_specs=row_spec(d),
        out_shape=jax.ShapeDtypeStruct((n, d), F32),
        compiler_params=pltpu.CompilerParams(dimension_semantics=("arbitrary",),
                                             vmem_limit_bytes=VMEM_LIMIT),
        name="post_ffn",
    )(h, *mixes, *wouts, npost, nfpre, nfpost, wup, wdn)


def _even_block(bb, seq_len, proj_s, cos, sin, lb, retw_ref, hgw_ref, oret_ref, ohg_ref, mix_ref):
    rows = bb * seq_len
    hw = H_RET * HD
    base = 4 * hw
    sh = seq_len.bit_length() - 1
    ti, si, same, causal = _seq_masks(rows, seq_len)
    tl = (lax.broadcasted_iota(jnp.int32, (rows, HD), 0) & (seq_len - 1)).astype(F32)
    seqs = [slice(b * seq_len, (b + 1) * seq_len) for b in range(bb)]
    one_shot = seq_len <= HG_BLK

    f = lb + (1.0 - lb) * jax.nn.sigmoid(proj_s[:, base + hw:base + 2 * hw])
    lf_parts = _split3(jnp.log(f))
    cum = _dot3(_ones_where(causal), lf_parts)
    if bb == 1:
        dtot = _dot3_tn(lf_parts, jnp.ones((rows, HD), BF16))
        cl = cum[rows - 1:rows, :]
    else:
        r_seq = lax.shift_right_logical(lax.broadcasted_iota(jnp.int32, (rows, HD), 0), sh)
        ind = _ones_where(r_seq == lax.broadcasted_iota(jnp.int32, (rows, HD), 1))
        dtot = _dot3_tn(lf_parts, ind)
        cl = _dot3(_ones_where(same), lf_parts)
    if one_shot:
        refc = _dot3(_ones_where(same & ((si & (seq_len - 1)) < seq_len // 2)), lf_parts)

    r_a, r_qs, r_kv, r_v = [], [], [], []
    for h in range(H_RET):
        col = lambda off: slice(off + h * HD, off + (h + 1) * HD)
        lg = math.log1p(-(2.0 ** (-5.0 - h)))
        q = proj_s[:, col(0)]
        k = proj_s[:, col(hw)]
        v = proj_s[:, col(2 * hw)]
        q = q * cos + pltpu.roll(q, HD // 2, axis=1) * sin
        k = (k * cos + pltpu.roll(k, HD // 2, axis=1) * sin) * (HD ** -0.5)
        kd = k * jnp.exp((float(seq_len - 1) - tl) * lg)
        r_a.append(_dot_nt(q, k))
        r_qs.append([_dot(q[s], oret_ref[b, h]) for b, s in enumerate(seqs)])
        r_kv.append([_dot_tn(kd[s], v[s]) for s in seqs])
        r_v.append(v)

    g_a, g_qs, g_kv, g_v = [], [], [], []
    for h in range(H_HG):
        hs = slice(h * HD, (h + 1) * HD)
        col = lambda off: slice(base + off + h * HD, base + off + (h + 1) * HD)
        q = proj_s[:, col(0)]
        v = proj_s[:, col(2 * hw)]
        k = 1.0 - f[:, hs]
        cum_h = cum[:, hs]
        qe = q * jnp.exp(cum_h)
        ke = k * jnp.exp(cl[:, hs] - cum_h)
        g_qs.append([_dot(qe[s], ohg_ref[b, h]) for b, s in enumerate(seqs)])
        g_kv.append([_dot_tn(ke[s], v[s]) for s in seqs])
        if one_shot:
            ref = refc[:, hs]
            g_a.append([_dot_nt(q * jnp.exp(cum_h - ref), k * jnp.exp(ref - cum_h))])
        else:
            blocks = []
            for i in range(rows // HG_BLK):
                lo, hi = i * HG_BLK, (i + 1) * HG_BLK
                ref = cum_h[lo + HG_BLK // 2 - 1:lo + HG_BLK // 2, :]
                blocks.append(_dot_nt(q[lo:hi] * jnp.exp(cum_h[lo:hi] - ref),
                                      k[:hi] * jnp.exp(ref - cum_h[:hi])))
            g_a.append(blocks)
        g_v.append(v)

    r_o = []
    for h in range(H_RET):
        lg = math.log1p(-(2.0 ** (-5.0 - h)))
        dm = jnp.where(causal, jnp.exp((ti - si).astype(F32) * lg), 0.0)
        qs = r_qs[h][0] if bb == 1 else jnp.concatenate(r_qs[h], axis=0)
        r_o.append(_dot(r_a[h] * dm, r_v[h]) + qs * jnp.exp((tl + 1.0) * lg))
        for b in range(bb):
            oret_ref[b, h] = oret_ref[b, h] * math.exp(seq_len * lg) + r_kv[h][b]

    g_o = []
    for h in range(H_HG):
        if one_shot:
            o = _dot(jnp.where(causal, g_a[h][0], 0.0), g_v[h])
        else:
            outs = []
            for i in range(rows // HG_BLK):
                lo, hi = i * HG_BLK, (i + 1) * HG_BLK
                tb = lax.broadcasted_iota(jnp.int32, (HG_BLK, hi), 0) + lo
                sb = lax.broadcasted_iota(jnp.int32, (HG_BLK, hi), 1)
                outs.append(_dot(jnp.where(tb >= sb, g_a[h][i], 0.0), g_v[h][:hi]))
            o = jnp.concatenate(outs, axis=0)
        qs = g_qs[h][0] if bb == 1 else jnp.concatenate(g_qs[h], axis=0)
        g_o.append(o + qs)
        for b in range(bb):
            dcol = dtot[h * HD:(h + 1) * HD, :]
            if bb > 1:
                dcol = jnp.broadcast_to(dcol[:, b:b + 1], (HD, HD))
            ohg_ref[b, h] = ohg_ref[b, h] * jnp.exp(dcol) + g_kv[h][b]

    for h in range(H_RET):
        col = lambda off: slice(off + h * HD, off + (h + 1) * HD)
        mu = jnp.mean(r_o[h], axis=-1, keepdims=True)
        oc = r_o[h] - mu
        var = jnp.mean(oc * oc, axis=-1, keepdims=True)
        o = oc * lax.rsqrt(var + EPS) * retw_ref[:, col(0)] * _silu(proj_s[:, col(3 * hw)])
        mix_ref[:, :, col(0)] = o.reshape(bb, seq_len, HD).astype(mix_ref.dtype)
    for h in range(H_HG):
        hs = slice(h * HD, (h + 1) * HD)
        g = proj_s[:, base + 3 * hw + h * HD:base + 3 * hw + (h + 1) * HD]
        o = _rms(g_o[h], hgw_ref[:, hs]) * _silu(g)
        mix_ref[:, :, hw + h * HD:hw + (h + 1) * HD] = o.reshape(bb, seq_len, HD).astype(mix_ref.dtype)


PROJ_COLS = 512


def _project(x_ref, npre_ref, win_ref, proj_s):
    bb, seq_len, d = x_ref.shape
    hn = _rms(x_ref[...].reshape(bb * seq_len, d), npre_ref[...]).astype(BF16)
    n_in = win_ref.shape[1]
    for lo in range(0, n_in, PROJ_COLS):
        sl = slice(lo, min(lo + PROJ_COLS, n_in))
        proj_s[:, sl] = jnp.dot(hn, win_ref[:, sl], preferred_element_type=F32)


def _even_kernel(has_state, x_ref, cos_ref, sin_ref, npre_ref, win_ref, retw_ref, hgw_ref,
                 lb_ref, *refs):
    if has_state:
        sret_ref, shg_ref, mix_ref, oret_ref, ohg_ref, proj_s = refs
    else:
        mix_ref, oret_ref, ohg_ref, proj_s = refs
    bb, seq_len, _ = x_ref.shape

    @pl.when(pl.program_id(1) == 0)
    def _():
        if has_state:
            oret_ref[...] = sret_ref[...]
            ohg_ref[...] = shg_ref[...]
        else:
            oret_ref[...] = jnp.zeros(oret_ref.shape, F32)
            ohg_ref[...] = jnp.zeros(ohg_ref.shape, F32)

    _project(x_ref, npre_ref, win_ref, proj_s)
    _even_block(bb, seq_len, proj_s, cos_ref[...], sin_ref[...], lb_ref[...], retw_ref, hgw_ref,
                oret_ref, ohg_ref, mix_ref)


def _even_call(x, cos, sin, npre, win, retw, hgw, lb, states, bb, c):
    nb_total, t, d = x.shape
    has_state = states is not None
    n_in = win.shape[1]
    rows = bb * c
    grid = (nb_total // bb, t // c)
    st_spec = pl.BlockSpec((bb, H_RET, HD, HD), lambda b, i: (b, 0, 0, 0))
    in_specs = [pl.BlockSpec((bb, c, d), lambda b, i: (b, i, 0)),
                pl.BlockSpec((rows, HD), lambda b, i: (i, 0)),
                pl.BlockSpec((rows, HD), lambda b, i: (i, 0)),
                _const_spec((1, d)), _const_spec(win.shape),
                _const_spec((1, H_RET * HD)), _const_spec((1, H_HG * HD)), _const_spec((1, H_HG * HD))]
    args = [x, cos, sin, npre, win, retw, hgw, lb]
    if has_state:
        in_specs += [st_spec, st_spec]
        args += list(states)
    mix_w = (H_RET + H_HG) * HD
    return pl.pallas_call(
        functools.partial(_even_kernel, has_state),
        grid=grid,
        in_specs=in_specs,
        out_specs=[pl.BlockSpec((bb, c, mix_w), lambda b, i: (b, i, 0)), st_spec, st_spec],
        out_shape=[jax.ShapeDtypeStruct((nb_total, t, mix_w), BF16),
                   jax.ShapeDtypeStruct((nb_total, H_RET, HD, HD), F32),
                   jax.ShapeDtypeStruct((nb_total, H_HG, HD, HD), F32)],
        scratch_shapes=[pltpu.VMEM((rows, n_in), F32)],
        compiler_params=pltpu.CompilerParams(dimension_semantics=("arbitrary", "arbitrary"),
                                             vmem_limit_bytes=VMEM_LIMIT),
        name="even_mixer",
    )(*args)


def _odd_kernel(has_state, x_ref, npre_ref, win_ref, convw_ref, convb_ref, dtb_ref, alog_ref,
                dssm_ref, ssmw_ref, *refs):
    if has_state:
        sconv_ref, sst_ref, y_ref, u_ref, oconv_ref, ost_ref, proj_s, tail_s = refs
    else:
        y_ref, u_ref, oconv_ref, ost_ref, proj_s, tail_s, sst_s = refs
    bb, seq_len, d = x_ref.shape
    rows = bb * seq_len
    n_in = win_ref.shape[1]
    d_inner = H_SSD * P_SSD
    conv_dim = d_inner + 2 * G_SSD * N_SSD
    conv_w = convw_ref.shape[0]
    halo = conv_w - 1
    u_w = u_ref.shape[2]
    gw = d_inner // G_SSD
    heads_per_group = H_SSD // G_SSD
    pairs_per_group = heads_per_group // 2
    seqs = [slice(b * seq_len, (b + 1) * seq_len) for b in range(bb)]

    @pl.when(pl.program_id(1) == 0)
    def _():
        tail_s[...] = jnp.zeros(tail_s.shape, F32)
        if has_state:
            for b in range(bb):
                tail_s[(b + 1) * SUBLANES - halo:(b + 1) * SUBLANES, :] = sconv_ref[b]
            ost_ref[...] = sst_ref[...]
        else:
            sst_s[...] = jnp.zeros(sst_s.shape, F32)

    _project(x_ref, npre_ref, win_ref, proj_s)
    z = proj_s[:, 0:d_inner]
    xbc = proj_s[:, d_inner:d_inner + conv_dim]
    u_ref[...] = proj_s[:, d_inner + conv_dim:d_inner + conv_dim + u_w].reshape(bb, seq_len, u_w)
    dt_raw = proj_s[:, d_inner + conv_dim + u_w:d_inner + conv_dim + u_w + LANES]

    tail = tail_s[...]
    conv = convb_ref[...] + xbc * convw_ref[halo:halo + 1, :]
    for s in range(1, conv_w):
        rolled = pltpu.roll(xbc, s, axis=0)
        if seq_len == SUBLANES:
            local = lax.broadcasted_iota(jnp.int32, (rows, conv_dim), 0) & (SUBLANES - 1)
            shifted = jnp.where(local < s, pltpu.roll(tail, rows - SUBLANES + s, axis=0), rolled)
        else:
            local = lax.broadcasted_iota(jnp.int32, (SUBLANES, conv_dim), 0)
            head = jnp.where(local < s, pltpu.roll(tail, s, axis=0), rolled[0:SUBLANES])
            shifted = jnp.concatenate([head, rolled[SUBLANES:]], axis=0)
        conv = conv + shifted * convw_ref[halo - s:halo - s + 1, :]
    tail_s[...] = xbc if seq_len == SUBLANES else xbc[rows - SUBLANES:rows]
    act = _silu(conv)
    xs = act[:, :d_inner]

    ti, si, same, causal = _seq_masks(rows, seq_len)
    lo_half = lax.broadcasted_iota(jnp.int32, (rows, LANES), 1) < P_SSD
    dt = jax.nn.softplus(dt_raw + dtb_ref[...])
    la_parts = _split3(dt * (-jnp.exp(alog_ref[...])))
    cum = _dot3(_ones_where(causal), la_parts)
    cum_t = _dot3_tn(la_parts, _ones_where(same & (ti <= si)))
    if bb > 1:
        rest = _dot3(_ones_where(same & (si > ti)), la_parts)
    bms = [act[:, d_inner + g * N_SSD:d_inner + (g + 1) * N_SSD] for g in range(G_SSD)]
    cms = [act[:, d_inner + (G_SSD + g) * N_SSD:d_inner + (G_SSD + g + 1) * N_SSD]
           for g in range(G_SSD)]
    scores = [_dot_nt(cms[g], bms[g]) for g in range(G_SSD)]

    bc = lambda a, h: jnp.broadcast_to(a[:, h:h + 1], (rows, LANES))
    bc_cum = [bc(cum, h) for h in range(H_SSD)]
    pair = lambda a, p: jnp.where(lo_half, bc(a, 2 * p), bc(a, 2 * p + 1))

    for g in range(G_SSD):
        xdts, xws, cums = [], [], []
        for pp in range(pairs_per_group):
            p = g * pairs_per_group + pp
            xdt = xs[:, p * LANES:(p + 1) * LANES] * pair(dt, p)
            cum_p = jnp.where(lo_half, bc_cum[2 * p], bc_cum[2 * p + 1])
            rest_p = cum_p[rows - 1:rows, :] - cum_p if bb == 1 else pair(rest, p)
            xdts.append(xdt)
            xws.append(xdt * jnp.exp(rest_p))
            cums.append(cum_p)
        xw = jnp.concatenate(xws, axis=1)
        if has_state:
            y_st = [_dot_nt(cms[g][s], ost_ref[b, g]) for b, s in enumerate(seqs)]
            kv = [_dot_tn(xw[s], bms[g][s]) for s in seqs]
        else:
            y_st = [_dot(cms[g][s], sst_s[b, g]) for b, s in enumerate(seqs)]
            kv = [_dot_tn(bms[g][s], xw[s]) for s in seqs]
        y_intra = []
        for pp in range(pairs_per_group):
            p = g * pairs_per_group + pp
            a_pair = []
            for h in (2 * p, 2 * p + 1):
                diff = bc_cum[h][:, :rows] - cum_t[h:h + 1, :]
                a_pair.append(scores[g] * jnp.where(causal, jnp.exp(jnp.where(causal, diff, 0.0)), 0.0))
            both = _dot(jnp.concatenate(a_pair, axis=0), xdts[pp])
            y_intra.append(jnp.where(lo_half, both[:rows], both[rows:]))
        for b in range(bb):
            last = (b + 1) * seq_len - 1
            if has_state:
                dec = [jnp.broadcast_to(jnp.exp(bc_cum[g * heads_per_group + hh][last:last + 1, :]),
                                        (P_SSD, N_SSD)) for hh in range(heads_per_group)]
                ost_ref[b, g] = ost_ref[b, g] * jnp.concatenate(dec, axis=0) + kv[b]
            else:
                dec = jnp.exp(jnp.concatenate([c_p[last:last + 1, :] for c_p in cums], axis=1))
                sst_s[b, g] = sst_s[b, g] * dec + kv[b]
        y_state = y_st[0] if bb == 1 else jnp.concatenate(y_st, axis=0)
        gs = slice(g * gw, (g + 1) * gw)
        y = (jnp.concatenate(y_intra, axis=1) + y_state * jnp.exp(jnp.concatenate(cums, axis=1))
             + dssm_ref[:, gs] * xs[:, gs])
        y = _rms(y * _silu(z[:, gs]), ssmw_ref[:, gs])
        y_ref[:, :, gs] = y.reshape(bb, seq_len, gw).astype(y_ref.dtype)

    @pl.when(pl.program_id(1) == pl.num_programs(1) - 1)
    def _():
        for b in range(bb):
            oconv_ref[b] = tail_s[(b + 1) * SUBLANES - halo:(b + 1) * SUBLANES, :]
            if not has_state:
                for g in range(G_SSD):
                    for pp in range(pairs_per_group):
                        ps = slice(pp * LANES, (pp + 1) * LANES)
                        ost_ref[b, g, ps, :] = sst_s[b, g, :, ps].T


def _odd_call(x, npre, win, convw, convb, dtb, alog, dssm, ssmw, states, bb, c):
    nb_total, t, d = x.shape
    has_state = states is not None
    n_in = win.shape[1]
    d_inner = H_SSD * P_SSD
    conv_dim = d_inner + 2 * G_SSD * N_SSD
    u_w = S5_G * S5_GS
    halo = convw.shape[0] - 1
    rows = bb * c
    grid = (nb_total // bb, t // c)
    st_shape = (G_SSD, d_inner // G_SSD, N_SSD)
    conv_spec = pl.BlockSpec((bb, halo, conv_dim), lambda b, i: (b, 0, 0))
    ssm_spec = pl.BlockSpec((bb,) + st_shape, lambda b, i: (b, 0, 0, 0))
    in_specs = [pl.BlockSpec((bb, c, d), lambda b, i: (b, i, 0)),
                _const_spec((1, d)), _const_spec(win.shape), _const_spec(convw.shape),
                _const_spec((1, conv_dim)), _const_spec((1, LANES)), _const_spec((1, LANES)),
                _const_spec((1, d_inner)), _const_spec((1, d_inner))]
    args = [x, npre, win, convw, convb, dtb, alog, dssm, ssmw]
    scratch = [pltpu.VMEM((rows, n_in), F32), pltpu.VMEM((bb * SUBLANES, conv_dim), F32)]
    if has_state:
        in_specs += [conv_spec, ssm_spec]
        args += list(states)
    else:
        scratch.append(pltpu.VMEM((bb, G_SSD, N_SSD, d_inner // G_SSD), F32))
    return pl.pallas_call(
        functools.partial(_odd_kernel, has_state),
        grid=grid,
        in_specs=in_specs,
        out_specs=[pl.BlockSpec((bb, c, d_inner), lambda b, i: (b, i, 0)),
                   pl.BlockSpec((bb, c, u_w), lambda b, i: (b, i, 0)),
                   conv_spec, ssm_spec],
        out_shape=[jax.ShapeDtypeStruct((nb_total, t, d_inner), BF16),
                   jax.ShapeDtypeStruct((nb_total, t, u_w), F32),
                   jax.ShapeDtypeStruct((nb_total, halo, conv_dim), F32),
                   jax.ShapeDtypeStruct((nb_total,) + st_shape, F32)],
        scratch_shapes=scratch,
        compiler_params=pltpu.CompilerParams(dimension_semantics=("arbitrary", "arbitrary"),
                                             vmem_limit_bytes=VMEM_LIMIT),
        name="odd_mixer",
    )(*args)


S5_HALF_IN = S5_G * S5_GS // 2
S5_HALF_ST = S5_G * S5_P // 2
S5_SCAN_VREGS = 4


def _s5_pitch(steps):
    return steps if steps % 16 == 8 else steps + 8


def _s5_kernel(u_ref, bmat_ref, cmat_ref, lre_ref, lim_ref, d_ref, wglu_ref, bglu_ref,
               h0re_ref, h0im_ref, yd_ref, ore_ref, oim_ref, buf_s, slab_s, ut_s):
    r, steps, uw = u_ref.shape
    hs = S5_HALF_ST
    scan_w = max(LANES, S5_SCAN_VREGS * SUBLANES * LANES // r)
    pitch = _s5_pitch(steps)
    nslab = uw // LANES

    @pl.when(pl.program_id(1) == 0)
    def _():
        ore_ref[...] = h0re_ref[...]
        oim_ref[...] = h0im_ref[...]

    for b in range(r):
        for j in range(nslab):
            slab_s[j, b * pitch:b * pitch + steps, :] = u_ref[b, :, j * LANES:(j + 1) * LANES]
    for t in range(steps):
        for j in range(nslab):
            ut_s[t * r:(t + 1) * r, j * LANES:(j + 1) * LANES] = slab_s[j, pl.ds(t, r, stride=pitch), :]
    u = ut_s[...]
    ub = u.astype(BF16)
    for half in range(2):
        buf_s[:, half * 2 * hs:(half + 1) * 2 * hs] = jnp.dot(
            ub[:, half * S5_HALF_IN:(half + 1) * S5_HALF_IN], bmat_ref[half],
            preferred_element_type=F32)

    for half in range(2):
        for j in range(hs // scan_w):
            st = slice(half * hs + j * scan_w, half * hs + (j + 1) * scan_w)
            cre = half * 2 * hs + j * scan_w
            cim = cre + hs
            lre = jnp.broadcast_to(lre_ref[:, st], (r, scan_w))
            lim = jnp.broadcast_to(lim_ref[:, st], (r, scan_w))
            hre, him = ore_ref[:, st], oim_ref[:, st]
            for t in range(steps):
                rws = slice(t * r, (t + 1) * r)
                hre, him = (lre * hre - lim * him + buf_s[rws, cre:cre + scan_w],
                            lre * him + lim * hre + buf_s[rws, cim:cim + scan_w])
                buf_s[rws, cre:cre + scan_w] = hre
                buf_s[rws, cim:cim + scan_w] = him
            ore_ref[:, st] = hre
            oim_ref[:, st] = him

    ys = []
    for half in range(2):
        ys.append(jnp.dot(buf_s[:, half * 2 * hs:(half + 1) * 2 * hs].astype(BF16), cmat_ref[half],
                          preferred_element_type=F32))
    y = jnp.concatenate(ys, axis=1) + d_ref[...] * u
    g = jax.nn.gelu(y)
    gate = jax.nn.sigmoid(jnp.dot(g.astype(BF16), wglu_ref[...], preferred_element_type=F32)
                          + bglu_ref[...])
    yd = g * gate
    for t in range(steps):
        for j in range(nslab):
            slab_s[j, pl.ds(t, r, stride=pitch), :] = yd[t * r:(t + 1) * r, j * LANES:(j + 1) * LANES]
    for b in range(r):
        for j in range(nslab):
            yd_ref[b, :, j * LANES:(j + 1) * LANES] = slab_s[j, b * pitch:b * pitch + steps, :].astype(
                yd_ref.dtype)


def _s5_call(u, bmat, cmat, lre, lim, dvec, wglu, bglu, h0re, h0im, steps, r):
    nb_total, t, uw = u.shape
    ns = S5_G * S5_P
    grid = (nb_total // r, t // steps)
    st_spec = pl.BlockSpec((r, ns), lambda b, i: (b, 0))
    return pl.pallas_call(
        _s5_kernel,
        grid=grid,
        in_specs=[pl.BlockSpec((r, steps, uw), lambda b, i: (b, i, 0)),
                  _const_spec(bmat.shape), _const_spec(cmat.shape),
                  _const_spec((1, ns)), _const_spec((1, ns)), _const_spec((1, uw)),
                  _const_spec(wglu.shape), _const_spec((1, uw)), st_spec, st_spec],
        out_specs=[pl.BlockSpec((r, steps, uw), lambda b, i: (b, i, 0)), st_spec, st_spec],
        out_shape=[jax.ShapeDtypeStruct((nb_total, t, uw), BF16),
                   jax.ShapeDtypeStruct((nb_total, ns), F32),
                   jax.ShapeDtypeStruct((nb_total, ns), F32)],
        scratch_shapes=[pltpu.VMEM((steps * r, 2 * ns), F32),
                        pltpu.VMEM((uw // LANES, r * _s5_pitch(steps), LANES), F32),
                        pltpu.VMEM((steps * r, uw), F32)],
        compiler_params=pltpu.CompilerParams(dimension_semantics=("arbitrary", "arbitrary"),
                                             vmem_limit_bytes=VMEM_LIMIT),
        name="s5_mixer",
    )(u, bmat, cmat, lre, lim, dvec, wglu, bglu, h0re, h0im)


def _rope_tables(pos):
    half = HD // 2
    inv_freq = ROPE_BASE ** (-jnp.arange(half, dtype=F32) / half)
    ang = pos[:, None] * inv_freq[None, :]
    cos, sin = jnp.cos(ang), jnp.sin(ang)
    return jnp.concatenate([cos, cos], axis=-1), jnp.concatenate([-sin, sin], axis=-1)


def _pad_lanes(v):
    return jnp.pad(v.astype(F32), (0, LANES - v.shape[0]))[None, :]


def _s5_tables(lam_re, lam_im, log_step, b_re, b_im, c_re, c_im):
    lr, li = lam_re.astype(F32), lam_im.astype(F32)
    dt = jnp.exp(log_step.astype(F32))[:, None]
    mag = jnp.exp(lr * dt)
    bar_re, bar_im = mag * jnp.cos(li * dt), mag * jnp.sin(li * dt)
    den = lr * lr + li * li
    cf_re = ((bar_re - 1.0) * lr + bar_im * li) / den
    cf_im = (bar_im * lr - (bar_re - 1.0) * li) / den
    bb_re = cf_re[..., None] * b_re.astype(F32) - cf_im[..., None] * b_im.astype(F32)
    bb_im = cf_re[..., None] * b_im.astype(F32) + cf_im[..., None] * b_re.astype(F32)
    gh = S5_G // 2
    eye = jnp.eye(gh, dtype=F32)

    def in_mat(b):
        return jnp.einsum('gpc,gk->gckp', b, eye).reshape(gh * S5_GS, gh * S5_P)

    def out_mat(cc):
        return jnp.einsum('gcp,gk->gpkc', cc, eye).reshape(gh * S5_P, gh * S5_GS)

    bmats, cmats = [], []
    for half in range(2):
        sl = slice(half * gh, (half + 1) * gh)
        bmats.append(jnp.concatenate([in_mat(bb_re[sl]), in_mat(bb_im[sl])], axis=1))
        cmats.append(jnp.concatenate([out_mat(c_re.astype(F32)[sl]), out_mat(-c_im.astype(F32)[sl])],
                                     axis=0))
    return (jnp.stack(bmats).astype(BF16), jnp.stack(cmats).astype(BF16),
            bar_re.reshape(1, -1), bar_im.reshape(1, -1))


def _row(v):
    return v.astype(F32).reshape(1, -1)


def _forward(x, pos_offset, states, p, cfg):
    nb, t, d = x.shape
    n = nb * t
    bb, c = cfg['bb'], cfg['c']
    has_state = states is not None
    pos = jnp.arange(pos_offset, pos_offset + t, dtype=F32)
    cos, sin = _rope_tables(pos)
    tile_rows = lambda a: jnp.tile(a.reshape(t // c, 1, c, HD), (1, bb, 1, 1)).reshape(-1, HD)
    lb = jnp.cumsum(jax.nn.softmax(p['hgrn_lower_bounds'].astype(F32), axis=0), axis=0)[0]
    mix, ret_o, hg_o = _even_call(
        x, tile_rows(cos), tile_rows(sin), _row(p['norm_mix_pre'][0]), p['w_in_even'][0].astype(BF16),
        _row(p['ret_norm_w'][0]), _row(p['hgrn_norm_w'][0]), _row(lb),
        (states['ret'][0], states['hgrn'][0]) if has_state else None, bb, c)
    h = _post_call(x.reshape(n, d), [mix.reshape(n, -1)], [p['w_out_even'][0].astype(BF16)],
                   _row(p['norm_mix_post'][0]), _row(p['norm_ffn_pre'][0]),
                   _row(p['norm_ffn_post'][0]), p['w_ffn_up'][0].astype(BF16),
                   p['w_ffn_down'][0].astype(BF16), cfg['rows'])
    d_inner = H_SSD * P_SSD
    conv_dim = d_inner + 2 * G_SSD * N_SSD
    u_w = S5_G * S5_GS
    w = p['w_in_odd'][0]
    c0 = d_inner + conv_dim
    w1 = jnp.concatenate([w[:, :c0], w[:, c0 + H_SSD:], w[:, c0:c0 + H_SSD],
                          jnp.zeros((d, LANES - H_SSD), w.dtype)], axis=1).astype(BF16)
    st_shape = (nb, G_SSD, d_inner // G_SSD, N_SSD)
    odd_states = None
    if has_state:
        odd_states = (states['conv'][0], jnp.swapaxes(states['ssm'][0], -1, -2).reshape(st_shape))
    y, u, conv_o, sst_o = _odd_call(
        h.reshape(nb, t, d), _row(p['norm_mix_pre'][1]), w1, p['conv_w'][0].astype(F32),
        _row(p['conv_b'][0]), _pad_lanes(p['dt_bias'][0]), _pad_lanes(p['a_log'][0]),
        _row(jnp.repeat(p['d_ssm'][0], P_SSD)), _row(p['ssm_norm_w'][0]), odd_states, bb, c)
    ssm_o = jnp.swapaxes(sst_o.reshape(nb, H_SSD, P_SSD, N_SSD), -1, -2)
    bmat, cmat, lre, lim = _s5_tables(p['s5_lam_re'][0], p['s5_lam_im'][0], p['s5_log_step'][0],
                                      p['s5_b_re'][0], p['s5_b_im'][0], p['s5_c_re'][0],
                                      p['s5_c_im'][0])
    ns = S5_G * S5_P
    if has_state:
        h0re = states['s5_re'][0].reshape(nb, ns)
        h0im = states['s5_im'][0].reshape(nb, ns)
    else:
        h0re = jnp.zeros((nb, ns), F32)
        h0im = jnp.zeros((nb, ns), F32)
    yd, re_o, im_o = _s5_call(u, bmat, cmat, lre, lim, _row(p['s5_d'][0]),
                              p['w_glu'][0].astype(BF16), _row(p['b_glu'][0]), h0re, h0im,
                              cfg['s5_steps'], cfg['s5_rows'])
    yd = yd.reshape(n, u_w)
    wo = p['w_out_odd'][0].astype(BF16)
    h = _post_call(h, [y.reshape(n, d_inner), yd], [wo[:d_inner], wo[d_inner:]],
                   _row(p['norm_mix_post'][1]), _row(p['norm_ffn_pre'][1]),
                   _row(p['norm_ffn_post'][1]), p['w_ffn_up'][1].astype(BF16),
                   p['w_ffn_down'][1].astype(BF16), cfg['rows'])
    return (h.reshape(nb, t, d), ret_o[None], hg_o[None], ssm_o[None], conv_o[None],
            re_o.reshape(1, nb, S5_G, S5_P), im_o.reshape(1, nb, S5_G, S5_P))


PROMPT_CFG = dict(bb=1, c=128, rows=512, s5_steps=32, s5_rows=8)
SAMPLE_CFG = dict(bb=8, c=8, rows=512, s5_steps=8, s5_rows=32)


def kernel(x_prompt, x_sample, state_ret, state_hgrn, state_ssm, state_conv, state_s5_re, state_s5_im,
           norm_mix_pre, norm_mix_post, norm_ffn_pre, norm_ffn_post, w_in_even, w_out_even, ret_norm_w,
           hgrn_lower_bounds, hgrn_norm_w, w_in_odd, conv_w, conv_b, dt_bias, a_log, d_ssm, ssm_norm_w,
           s5_lam_re, s5_lam_im, s5_log_step, s5_b_re, s5_b_im, s5_c_re, s5_c_im, s5_d, w_glu, b_glu,
           w_out_odd, w_ffn_up, w_ffn_down):
    p = dict(norm_mix_pre=norm_mix_pre, norm_mix_post=norm_mix_post, norm_ffn_pre=norm_ffn_pre,
             norm_ffn_post=norm_ffn_post, w_in_even=w_in_even, w_out_even=w_out_even, ret_norm_w=ret_norm_w,
             hgrn_lower_bounds=hgrn_lower_bounds, hgrn_norm_w=hgrn_norm_w, w_in_odd=w_in_odd, conv_w=conv_w,
             conv_b=conv_b, dt_bias=dt_bias, a_log=a_log, d_ssm=d_ssm, ssm_norm_w=ssm_norm_w,
             s5_lam_re=s5_lam_re, s5_lam_im=s5_lam_im, s5_log_step=s5_log_step, s5_b_re=s5_b_re,
             s5_b_im=s5_b_im, s5_c_re=s5_c_re, s5_c_im=s5_c_im, s5_d=s5_d, w_glu=w_glu, b_glu=b_glu,
             w_out_odd=w_out_odd, w_ffn_up=w_ffn_up, w_ffn_down=w_ffn_down)
    past_len = 16384
    states = dict(ret=state_ret, hgrn=state_hgrn, ssm=state_ssm, conv=state_conv,
                  s5_re=state_s5_re, s5_im=state_s5_im)
    y_p, ret_p, hg_p, ssm_p, conv_p, re_p, im_p = _forward(x_prompt, 0, None, p, PROMPT_CFG)
    y_s, ret_s, hg_s, ssm_s, conv_s, re_s, im_s = _forward(x_sample, past_len, states, p, SAMPLE_CFG)
    return (y_p, y_s, ret_p, ret_s, hg_p, hg_s, ssm_p, ssm_s, conv_p, conv_s,
            re_p, re_s, im_p, im_s)
```

```python
import functools
import math

import jax
import jax.numpy as jnp
import numpy as np
from jax import lax
from jax.experimental import pallas as pl
from jax.experimental.pallas import tpu as pltpu

F32 = jnp.float32
BF16 = jnp.bfloat16
EPS = 1e-6
ROPE_BASE = 10000.0
LANES = 128
SUBLANES = 8
VMEM_LIMIT = 56 * 1024 * 1024

H_RET = 4
H_HG = 4
HD = 128
H_SSD = 16
P_SSD = 64
N_SSD = 128
G_SSD = 2
S5_G = 32
S5_GS = 16
S5_P = 64
HG_BLK = 16


def _rms(x, w):
    return x * lax.rsqrt(jnp.mean(x * x, axis=-1, keepdims=True) + EPS) * w


def _silu(x):
    return x * jax.nn.sigmoid(x)


def _dot(a, b):
    return jnp.dot(a.astype(BF16), b.astype(BF16), preferred_element_type=F32)


def _dot_nt(a, b):
    return lax.dot_general(a.astype(BF16), b.astype(BF16), (((1,), (1,)), ((), ())),
                           preferred_element_type=F32)


def _dot_tn(a, b):
    return lax.dot_general(a.astype(BF16), b.astype(BF16), (((0,), (0,)), ((), ())),
                           preferred_element_type=F32)


def _split3(x):
    hi = x.astype(BF16)
    r1 = x - hi.astype(F32)
    mid = r1.astype(BF16)
    lo = (r1 - mid.astype(F32)).astype(BF16)
    return hi, mid, lo


def _dot3(a_exact, parts):
    acc = None
    for p in parts:
        d = jnp.dot(a_exact, p, preferred_element_type=F32)
        acc = d if acc is None else acc + d
    return acc


def _dot3_tn(parts, b_exact):
    acc = None
    for p in parts:
        d = lax.dot_general(p, b_exact, (((0,), (0,)), ((), ())), preferred_element_type=F32)
        acc = d if acc is None else acc + d
    return acc


def _ones_where(mask):
    return jnp.where(mask, 1.0, 0.0).astype(BF16)


def _seq_masks(rows, seq_len):
    sh = seq_len.bit_length() - 1
    ti = lax.broadcasted_iota(jnp.int32, (rows, rows), 0)
    si = lax.broadcasted_iota(jnp.int32, (rows, rows), 1)
    same = lax.shift_right_logical(ti, sh) == lax.shift_right_logical(si, sh)
    return ti, si, same, same & (ti >= si)


def _interleave(units):
    live = list(units)
    while live:
        still = []
        for g in live:
            try:
                next(g)
                still.append(g)
            except StopIteration:
                pass
        live = still


def _const_spec(shape):
    nd = len(shape)
    return pl.BlockSpec(shape, lambda *_: (0,) * nd, pipeline_mode=pl.Buffered(1))


def _post_kernel(n_mix, ff_chunk, h_ref, *refs):
    mix_refs = refs[:n_mix]
    wout_refs = refs[n_mix:2 * n_mix]
    npost_ref, nfpre_ref, nfpost_ref, wup_ref, wdn_ref, o_ref = refs[2 * n_mix:]
    acc = None
    for m_ref, w_ref in zip(mix_refs, wout_refs):
        d = jnp.dot(m_ref[...], w_ref[...], preferred_element_type=F32)
        acc = d if acc is None else acc + d
    h1 = h_ref[...] + _rms(acc, npost_ref[...])
    hn = _rms(h1, nfpre_ref[...]).astype(BF16)
    ff = None
    for j in range(wup_ref.shape[1] // ff_chunk):
        sl = slice(j * ff_chunk, (j + 1) * ff_chunk)
        up = jnp.dot(hn, wup_ref[:, sl], preferred_element_type=F32)
        act = jnp.square(jnp.maximum(up, 0.0)).astype(BF16)
        d = jnp.dot(act, wdn_ref[sl, :], preferred_element_type=F32)
        ff = d if ff is None else ff + d
    o_ref[...] = h1 + _rms(ff, nfpost_ref[...])


def _post_call(h, mixes, wouts, npost, nfpre, nfpost, wup, wdn, rows):
    n, d = h.shape
    n_mix = len(mixes)
    row_spec = lambda w: pl.BlockSpec((rows, w), lambda i: (i, 0))
    in_specs = ([row_spec(d)] + [row_spec(m.shape[1]) for m in mixes]
                + [_const_spec(w.shape) for w in wouts]
                + [_const_spec((1, d))] * 3 + [_const_spec(wup.shape), _const_spec(wdn.shape)])
    return pl.pallas_call(
        functools.partial(_post_kernel, n_mix, 1024),
        grid=(n // rows,),
        in_specs=in_specs,
        out_specs=row_spec(d),
        out_shape=jax.ShapeDtypeStruct((n, d), F32),
        compiler_params=pltpu.CompilerParams(dimension_semantics=("arbitrary",),
                                             vmem_limit_bytes=VMEM_LIMIT),
        name="post_ffn",
    )(h, *mixes, *wouts, npost, nfpre, nfpost, wup, wdn)


def _ret_decay_tables(bb, seq_len):
    rows = bb * seq_len
    t = np.arange(rows)
    tl = t % seq_len
    valid = (t[:, None] // seq_len == t[None, :] // seq_len) & (t[:, None] >= t[None, :])
    pair, to_row, to_end = [], [], []
    for h in range(H_RET):
        lg = math.log1p(-(2.0 ** (-5.0 - h)))
        pair.append(np.where(valid, np.exp((t[:, None] - t[None, :]) * lg), 0.0))
        to_row.append(np.broadcast_to(np.exp((tl + 1.0) * lg)[:, None], (rows, HD)))
        to_end.append(np.broadcast_to(np.exp((seq_len - 1.0 - tl) * lg)[:, None], (rows, HD)))
    return tuple(jnp.asarray(np.stack(a), F32) for a in (pair, to_row, to_end))


def _even_block(bb, seq_len, proj_s, cos, sin, lb, retw_ref, hgw_ref, dec_refs, oret_ref, ohg_ref,
                mix_ref):
    rows = bb * seq_len
    hw = H_RET * HD
    base = 4 * hw
    sh = seq_len.bit_length() - 1
    ti, si, same, causal = _seq_masks(rows, seq_len)
    pair_ref, to_row_ref, to_end_ref = dec_refs
    seqs = [slice(b * seq_len, (b + 1) * seq_len) for b in range(bb)]
    one_shot = seq_len <= HG_BLK

    f = lb + (1.0 - lb) * jax.nn.sigmoid(proj_s[:, base + hw:base + 2 * hw])
    lf_parts = _split3(jnp.log(f))
    cum = _dot3(_ones_where(causal), lf_parts)
    if bb == 1:
        dtot = _dot3_tn(lf_parts, jnp.ones((rows, HD), BF16))
        cl = cum[rows - 1:rows, :]
    else:
        r_seq = lax.shift_right_logical(lax.broadcasted_iota(jnp.int32, (rows, HD), 0), sh)
        ind = _ones_where(r_seq == lax.broadcasted_iota(jnp.int32, (rows, HD), 1))
        dtot = _dot3_tn(lf_parts, ind)
        cl = _dot3(_ones_where(same), lf_parts)
    if one_shot:
        refc = _dot3(_ones_where(same & ((si & (seq_len - 1)) < seq_len // 2)), lf_parts)

    yield
    r_a, r_qs, r_kv, r_v = [], [], [], []
    for h in range(H_RET):
        col = lambda off: slice(off + h * HD, off + (h + 1) * HD)
        q = proj_s[:, col(0)]
        k = proj_s[:, col(hw)]
        v = proj_s[:, col(2 * hw)]
        q = q * cos + pltpu.roll(q, HD // 2, axis=1) * sin
        k = (k * cos + pltpu.roll(k, HD // 2, axis=1) * sin) * (HD ** -0.5)
        kd = k * to_end_ref[h]
        r_a.append(_dot_nt(q, k))
        r_qs.append([_dot(q[s], oret_ref[b, h]) for b, s in enumerate(seqs)])
        r_kv.append([_dot_tn(kd[s], v[s]) for s in seqs])
        r_v.append(v)

    g_a, g_qs, g_kv, g_v = [], [], [], []
    for h in range(H_HG):
        hs = slice(h * HD, (h + 1) * HD)
        col = lambda off: slice(base + off + h * HD, base + off + (h + 1) * HD)
        q = proj_s[:, col(0)]
        v = proj_s[:, col(2 * hw)]
        k = 1.0 - f[:, hs]
        cum_h = cum[:, hs]
        qe = q * jnp.exp(cum_h)
        ke = k * jnp.exp(cl[:, hs] - cum_h)
        g_qs.append([_dot(qe[s], ohg_ref[b, h]) for b, s in enumerate(seqs)])
        g_kv.append([_dot_tn(ke[s], v[s]) for s in seqs])
        if one_shot:
            ref = refc[:, hs]
            g_a.append([_dot_nt(q * jnp.exp(cum_h - ref), k * jnp.exp(ref - cum_h))])
        else:
            blocks = []
            for i in range(rows // HG_BLK):
                lo, hi = i * HG_BLK, (i + 1) * HG_BLK
                ref = cum_h[lo + HG_BLK // 2 - 1:lo + HG_BLK // 2, :]
                blocks.append(_dot_nt(q[lo:hi] * jnp.exp(cum_h[lo:hi] - ref),
                                      k[:hi] * jnp.exp(ref - cum_h[:hi])))
            g_a.append(blocks)
        g_v.append(v)

    yield
    r_o = []
    for h in range(H_RET):
        lg = math.log1p(-(2.0 ** (-5.0 - h)))
        qs = r_qs[h][0] if bb == 1 else jnp.concatenate(r_qs[h], axis=0)
        r_o.append(_dot(r_a[h] * pair_ref[h], r_v[h]) + qs * to_row_ref[h])
        for b in range(bb):
            oret_ref[b, h] = oret_ref[b, h] * math.exp(seq_len * lg) + r_kv[h][b]

    g_o = []
    for h in range(H_HG):
        if one_shot:
            o = _dot(jnp.where(causal, g_a[h][0], 0.0), g_v[h])
        else:
            outs = []
            for i in range(rows // HG_BLK):
                lo, hi = i * HG_BLK, (i + 1) * HG_BLK
                tb = lax.broadcasted_iota(jnp.int32, (HG_BLK, hi), 0) + lo
                sb = lax.broadcasted_iota(jnp.int32, (HG_BLK, hi), 1)
                outs.append(_dot(jnp.where(tb >= sb, g_a[h][i], 0.0), g_v[h][:hi]))
            o = jnp.concatenate(outs, axis=0)
        qs = g_qs[h][0] if bb == 1 else jnp.concatenate(g_qs[h], axis=0)
        g_o.append(o + qs)
        for b in range(bb):
            dcol = dtot[h * HD:(h + 1) * HD, :]
            if bb > 1:
                dcol = jnp.broadcast_to(dcol[:, b:b + 1], (HD, HD))
            ohg_ref[b, h] = ohg_ref[b, h] * jnp.exp(dcol) + g_kv[h][b]

    yield
    for h in range(H_RET):
        col = lambda off: slice(off + h * HD, off + (h + 1) * HD)
        mu = jnp.mean(r_o[h], axis=-1, keepdims=True)
        oc = r_o[h] - mu
        var = jnp.mean(oc * oc, axis=-1, keepdims=True)
        o = oc * lax.rsqrt(var + EPS) * retw_ref[:, col(0)] * _silu(proj_s[:, col(3 * hw)])
        mix_ref[:, :, col(0)] = o.reshape(bb, seq_len, HD).astype(mix_ref.dtype)
    for h in range(H_HG):
        hs = slice(h * HD, (h + 1) * HD)
        g = proj_s[:, base + 3 * hw + h * HD:base + 3 * hw + (h + 1) * HD]
        o = _rms(g_o[h], hgw_ref[:, hs]) * _silu(g)
        mix_ref[:, :, hw + h * HD:hw + (h + 1) * HD] = o.reshape(bb, seq_len, HD).astype(mix_ref.dtype)


PROJ_COLS = 512


def _project(x_ref, npre_ref, win_ref, proj_s):
    bb, seq_len, d = x_ref.shape
    hn = _rms(x_ref[...].reshape(bb * seq_len, d), npre_ref[...]).astype(BF16)
    n_in = win_ref.shape[1]
    for lo in range(0, n_in, PROJ_COLS):
        sl = slice(lo, min(lo + PROJ_COLS, n_in))
        proj_s[:, sl] = jnp.dot(hn, win_ref[:, sl], preferred_element_type=F32)


def _even_kernel(has_state, sub, x_ref, cos_ref, sin_ref, npre_ref, win_ref, retw_ref, hgw_ref,
                 lb_ref, pair_ref, to_row_ref, to_end_ref, *refs):
    if has_state:
        sret_ref, shg_ref, mix_ref, oret_ref, ohg_ref, proj_s = refs
    else:
        mix_ref, oret_ref, ohg_ref, proj_s = refs
    bb, seq_len, _ = x_ref.shape
    dec_refs = (pair_ref, to_row_ref, to_end_ref)

    @pl.when(pl.program_id(1) == 0)
    def _():
        if has_state:
            oret_ref[...] = sret_ref[...]
            ohg_ref[...] = shg_ref[...]
        else:
            oret_ref[...] = jnp.zeros(oret_ref.shape, F32)
            ohg_ref[...] = jnp.zeros(ohg_ref.shape, F32)

    _project(x_ref, npre_ref, win_ref, proj_s)
    consts = (lb_ref[...], retw_ref, hgw_ref, dec_refs)
    if seq_len > sub:
        assert bb == 1 and seq_len % sub == 0

        def chunk(j, carry):
            rs = pl.ds(pl.multiple_of(j * sub, sub), sub)
            _interleave([_even_block(1, sub, proj_s.at[rs], cos_ref[rs, :], sin_ref[rs, :], *consts,
                                     oret_ref, ohg_ref, mix_ref.at[:, rs])])
            return carry

        lax.fori_loop(0, seq_len // sub, chunk, 0)
    elif bb > 1 and seq_len > HG_BLK:
        units = []
        for u in range(bb):
            rs, one = pl.ds(u * seq_len, seq_len), pl.ds(u, 1)
            units.append(_even_block(1, seq_len, proj_s.at[rs], cos_ref[rs, :], sin_ref[rs, :], *consts,
                                     oret_ref.at[one], ohg_ref.at[one], mix_ref.at[one]))
        _interleave(units)
    else:
        _interleave([_even_block(bb, seq_len, proj_s, cos_ref[...], sin_ref[...], *consts,
                                 oret_ref, ohg_ref, mix_ref)])


def _even_call(x, cos, sin, npre, win, retw, hgw, lb, states, bb, c, sub):
    nb_total, t, d = x.shape
    has_state = states is not None
    n_in = win.shape[1]
    rows = bb * c
    grid = (nb_total // bb, t // c)
    st_spec = pl.BlockSpec((bb, H_RET, HD, HD), lambda b, i: (b, 0, 0, 0))
    in_specs = [pl.BlockSpec((bb, c, d), lambda b, i: (b, i, 0)),
                pl.BlockSpec((rows, HD), lambda b, i: (i, 0)),
                pl.BlockSpec((rows, HD), lambda b, i: (i, 0)),
                _const_spec((1, d)), _const_spec(win.shape),
                _const_spec((1, H_RET * HD)), _const_spec((1, H_HG * HD)), _const_spec((1, H_HG * HD))]
    tables = _ret_decay_tables(bb if sub <= HG_BLK else 1, sub)
    in_specs += [_const_spec(a.shape) for a in tables]
    args = [x, cos, sin, npre, win, retw, hgw, lb, *tables]
    if has_state:
        in_specs += [st_spec, st_spec]
        args += list(states)
    mix_w = (H_RET + H_HG) * HD
    return pl.pallas_call(
        functools.partial(_even_kernel, has_state, sub),
        grid=grid,
        in_specs=in_specs,
        out_specs=[pl.BlockSpec((bb, c, mix_w), lambda b, i: (b, i, 0)), st_spec, st_spec],
        out_shape=[jax.ShapeDtypeStruct((nb_total, t, mix_w), BF16),
                   jax.ShapeDtypeStruct((nb_total, H_RET, HD, HD), F32),
                   jax.ShapeDtypeStruct((nb_total, H_HG, HD, HD), F32)],
        scratch_shapes=[pltpu.VMEM((rows, n_in), F32)],
        compiler_params=pltpu.CompilerParams(dimension_semantics=("arbitrary", "arbitrary"),
                                             vmem_limit_bytes=VMEM_LIMIT),
        name="even_mixer",
    )(*args)


def _odd_block(has_state, bb, seq_len, proj_s, tail_s, ost_ref, sst_s, convw_ref, convb_ref, dtb_ref,
               alog_ref, dssm_ref, ssmw_ref, y_ref, u_ref):
    rows = bb * seq_len
    d_inner = H_SSD * P_SSD
    conv_dim = d_inner + 2 * G_SSD * N_SSD
    conv_w = convw_ref.shape[0]
    halo = conv_w - 1
    u_w = u_ref.shape[2]
    gw = d_inner // G_SSD
    heads_per_group = H_SSD // G_SSD
    pairs_per_group = heads_per_group // 2
    seqs = [slice(b * seq_len, (b + 1) * seq_len) for b in range(bb)]

    z = proj_s[:, 0:d_inner]
    xbc = proj_s[:, d_inner:d_inner + conv_dim]
    c0 = d_inner + conv_dim
    dt_raw = proj_s[:, c0:c0 + LANES]
    u_ref[...] = proj_s[:, c0 + H_SSD:c0 + H_SSD + u_w].reshape(bb, seq_len, u_w)

    tail = tail_s[...]
    conv = convb_ref[...] + xbc * convw_ref[halo:halo + 1, :]
    for s in range(1, conv_w):
        rolled = pltpu.roll(xbc, s, axis=0)
        if seq_len == SUBLANES:
            local = lax.broadcasted_iota(jnp.int32, (rows, conv_dim), 0) & (SUBLANES - 1)
            shifted = jnp.where(local < s, pltpu.roll(tail, rows - SUBLANES + s, axis=0), rolled)
        else:
            local = lax.broadcasted_iota(jnp.int32, (SUBLANES, conv_dim), 0)
            head = jnp.where(local < s, pltpu.roll(tail, s, axis=0), rolled[0:SUBLANES])
            shifted = jnp.concatenate([head, rolled[SUBLANES:]], axis=0)
        conv = conv + shifted * convw_ref[halo - s:halo - s + 1, :]
    tail_s[...] = xbc if seq_len == SUBLANES else xbc[rows - SUBLANES:rows]
    act = _silu(conv)
    xs = act[:, :d_inner]

    ti, si, same, causal = _seq_masks(rows, seq_len)
    lo_half = lax.broadcasted_iota(jnp.int32, (rows, LANES), 1) < P_SSD
    dt = jax.nn.softplus(dt_raw + dtb_ref[...])
    la_parts = _split3(dt * (-jnp.exp(alog_ref[...])))
    cum = _dot3(_ones_where(causal), la_parts)
    cum_t = _dot3_tn(la_parts, _ones_where(same & (ti <= si)))
    if bb > 1:
        rest = _dot3(_ones_where(same & (si > ti)), la_parts)
    bms = [act[:, d_inner + g * N_SSD:d_inner + (g + 1) * N_SSD] for g in range(G_SSD)]
    cms = [act[:, d_inner + (G_SSD + g) * N_SSD:d_inner + (G_SSD + g + 1) * N_SSD]
           for g in range(G_SSD)]
    scores = [_dot_nt(cms[g], bms[g]) for g in range(G_SSD)]
    yield

    bc = lambda a, h: jnp.broadcast_to(a[:, h:h + 1], (rows, LANES))
    bc_cum = [bc(cum, h) for h in range(H_SSD)]
    pair = lambda a, p: jnp.where(lo_half, bc(a, 2 * p), bc(a, 2 * p + 1))

    for g in range(G_SSD):
        xdts, xws, cums = [], [], []
        for pp in range(pairs_per_group):
            p = g * pairs_per_group + pp
            xdt = xs[:, p * LANES:(p + 1) * LANES] * pair(dt, p)
            cum_p = jnp.where(lo_half, bc_cum[2 * p], bc_cum[2 * p + 1])
            rest_p = cum_p[rows - 1:rows, :] - cum_p if bb == 1 else pair(rest, p)
            xdts.append(xdt)
            xws.append(xdt * jnp.exp(rest_p))
            cums.append(cum_p)
        xw = jnp.concatenate(xws, axis=1)
        if has_state:
            y_st = [_dot_nt(cms[g][s], ost_ref[b, g]) for b, s in enumerate(seqs)]
            kv = [_dot_tn(xw[s], bms[g][s]) for s in seqs]
        else:
            y_st = [_dot(cms[g][s], sst_s[b, g]) for b, s in enumerate(seqs)]
            kv = [_dot_tn(bms[g][s], xw[s]) for s in seqs]
        y_intra = []
        for pp in range(pairs_per_group):
            p = g * pairs_per_group + pp
            a_pair = []
            for h in (2 * p, 2 * p + 1):
                diff = bc_cum[h][:, :rows] - cum_t[h:h + 1, :]
                a_pair.append(scores[g] * jnp.where(causal, jnp.exp(jnp.where(causal, diff, 0.0)), 0.0))
            both = _dot(jnp.concatenate(a_pair, axis=0), xdts[pp])
            y_intra.append(jnp.where(lo_half, both[:rows], both[rows:]))
        yield
        for b in range(bb):
            last = (b + 1) * seq_len - 1
            if has_state:
                dec = [jnp.broadcast_to(jnp.exp(bc_cum[g * heads_per_group + hh][last:last + 1, :]),
                                        (P_SSD, N_SSD)) for hh in range(heads_per_group)]
                ost_ref[b, g] = ost_ref[b, g] * jnp.concatenate(dec, axis=0) + kv[b]
            else:
                dec = jnp.exp(jnp.concatenate([c_p[last:last + 1, :] for c_p in cums], axis=1))
                sst_s[b, g] = sst_s[b, g] * dec + kv[b]
        y_state = y_st[0] if bb == 1 else jnp.concatenate(y_st, axis=0)
        gs = slice(g * gw, (g + 1) * gw)
        y = (jnp.concatenate(y_intra, axis=1) + y_state * jnp.exp(jnp.concatenate(cums, axis=1))
             + dssm_ref[:, gs] * xs[:, gs])
        y = _rms(y * _silu(z[:, gs]), ssmw_ref[:, gs])
        y_ref[:, :, gs] = y.reshape(bb, seq_len, gw).astype(y_ref.dtype)


def _odd_kernel(has_state, sub, x_ref, npre_ref, win_ref, convw_ref, convb_ref, dtb_ref, alog_ref,
                dssm_ref, ssmw_ref, *refs):
    if has_state:
        sconv_ref, sst_ref, y_ref, u_ref, oconv_ref, ost_ref, proj_s, tail_s = refs
        sst_s = None
    else:
        y_ref, u_ref, oconv_ref, ost_ref, proj_s, tail_s, sst_s = refs
    bb, seq_len, _ = x_ref.shape
    halo = convw_ref.shape[0] - 1

    @pl.when(pl.program_id(1) == 0)
    def _():
        tail_s[...] = jnp.zeros(tail_s.shape, F32)
        if has_state:
            for b in range(bb):
                tail_s[(b + 1) * SUBLANES - halo:(b + 1) * SUBLANES, :] = sconv_ref[b]
            ost_ref[...] = sst_ref[...]
        else:
            sst_s[...] = jnp.zeros(sst_s.shape, F32)

    _project(x_ref, npre_ref, win_ref, proj_s)
    consts = (convw_ref, convb_ref, dtb_ref, alog_ref, dssm_ref, ssmw_ref)
    if seq_len > sub:
        assert bb == 1 and seq_len % sub == 0

        def chunk(j, carry):
            rs = pl.ds(pl.multiple_of(j * sub, sub), sub)
            _interleave([_odd_block(has_state, 1, sub, proj_s.at[rs], tail_s, ost_ref, sst_s, *consts,
                                    y_ref.at[:, rs], u_ref.at[:, rs])])
            return carry

        lax.fori_loop(0, seq_len // sub, chunk, 0)
    elif bb > 1 and seq_len > SUBLANES:
        units = []
        for u in range(bb):
            one = pl.ds(u, 1)
            units.append(_odd_block(
                has_state, 1, seq_len, proj_s.at[pl.ds(u * seq_len, seq_len)],
                tail_s.at[pl.ds(u * SUBLANES, SUBLANES)], ost_ref.at[one],
                None if sst_s is None else sst_s.at[one], *consts, y_ref.at[one], u_ref.at[one]))
        _interleave(units)
    else:
        _interleave([_odd_block(has_state, bb, seq_len, proj_s, tail_s, ost_ref, sst_s, *consts,
                                y_ref, u_ref)])

    @pl.when(pl.program_id(1) == pl.num_programs(1) - 1)
    def _():
        for b in range(bb):
            oconv_ref[b] = tail_s[(b + 1) * SUBLANES - halo:(b + 1) * SUBLANES, :]
            if not has_state:
                for g in range(G_SSD):
                    for ps in range(0, ost_ref.shape[2], LANES):
                        ost_ref[b, g, ps:ps + LANES, :] = sst_s[b, g, :, ps:ps + LANES].T


def _odd_call(x, npre, win, convw, convb, dtb, alog, dssm, ssmw, states, bb, c, sub):
    nb_total, t, d = x.shape
    has_state = states is not None
    n_in = win.shape[1]
    d_inner = H_SSD * P_SSD
    conv_dim = d_inner + 2 * G_SSD * N_SSD
    u_w = S5_G * S5_GS
    halo = convw.shape[0] - 1
    rows = bb * c
    grid = (nb_total // bb, t // c)
    st_shape = (G_SSD, d_inner // G_SSD, N_SSD)
    conv_spec = pl.BlockSpec((bb, halo, conv_dim), lambda b, i: (b, 0, 0))
    ssm_spec = pl.BlockSpec((bb,) + st_shape, lambda b, i: (b, 0, 0, 0))
    in_specs = [pl.BlockSpec((bb, c, d), lambda b, i: (b, i, 0)),
                _const_spec((1, d)), _const_spec(win.shape), _const_spec(convw.shape),
                _const_spec((1, conv_dim)), _const_spec((1, LANES)), _const_spec((1, LANES)),
                _const_spec((1, d_inner)), _const_spec((1, d_inner))]
    args = [x, npre, win, convw, convb, dtb, alog, dssm, ssmw]
    scratch = [pltpu.VMEM((rows, n_in), F32), pltpu.VMEM((bb * SUBLANES, conv_dim), F32)]
    if has_state:
        in_specs += [conv_spec, ssm_spec]
        args += list(states)
    else:
        scratch.append(pltpu.VMEM((bb, G_SSD, N_SSD, d_inner // G_SSD), F32))
    return pl.pallas_call(
        functools.partial(_odd_kernel, has_state, sub),
        grid=grid,
        in_specs=in_specs,
        out_specs=[pl.BlockSpec((bb, c, d_inner), lambda b, i: (b, i, 0)),
                   pl.BlockSpec((bb, c, u_w), lambda b, i: (b, i, 0)),
                   conv_spec, ssm_spec],
        out_shape=[jax.ShapeDtypeStruct((nb_total, t, d_inner), BF16),
                   jax.ShapeDtypeStruct((nb_total, t, u_w), F32),
                   jax.ShapeDtypeStruct((nb_total, halo, conv_dim), F32),
                   jax.ShapeDtypeStruct((nb_total,) + st_shape, F32)],
        scratch_shapes=scratch,
        compiler_params=pltpu.CompilerParams(dimension_semantics=("arbitrary", "arbitrary"),
                                             vmem_limit_bytes=VMEM_LIMIT),
        name="odd_mixer",
    )(*args)


S5_PARTS = 4
S5_PART_IN = S5_G * S5_GS // S5_PARTS
S5_PART_ST = S5_G * S5_P // S5_PARTS
S5_SCAN_VREGS = 4


def _s5_pitch(steps):
    return steps if steps % 16 == 8 else steps + 8


def _s5_kernel(u_ref, bmat_ref, cmat_ref, lre_ref, lim_ref, d_ref, wglu_ref, bglu_ref,
               h0re_ref, h0im_ref, yd_ref, ore_ref, oim_ref, buf_s, slab_s, ut_s):
    r, steps, uw = u_ref.shape
    hs = S5_PART_ST
    scan_w = min(hs, max(LANES, S5_SCAN_VREGS * SUBLANES * LANES // r))
    pitch = _s5_pitch(steps)
    nslab = uw // LANES

    @pl.when(pl.program_id(1) == 0)
    def _():
        ore_ref[...] = h0re_ref[...]
        oim_ref[...] = h0im_ref[...]

    for b in range(r):
        for j in range(nslab):
            slab_s[j, b * pitch:b * pitch + steps, :] = u_ref[b, :, j * LANES:(j + 1) * LANES]
    for t in range(steps):
        for j in range(nslab):
            ut_s[t * r:(t + 1) * r, j * LANES:(j + 1) * LANES] = slab_s[j, pl.ds(t, r, stride=pitch), :]
    u = ut_s[...]
    ub = u.astype(BF16)
    for half in range(S5_PARTS):
        buf_s[:, half * 2 * hs:(half + 1) * 2 * hs] = jnp.dot(
            ub[:, half * S5_PART_IN:(half + 1) * S5_PART_IN], bmat_ref[half],
            preferred_element_type=F32)

    for half in range(S5_PARTS):
        for j in range(hs // scan_w):
            st = slice(half * hs + j * scan_w, half * hs + (j + 1) * scan_w)
            cre = half * 2 * hs + j * scan_w
            cim = cre + hs
            lre = jnp.broadcast_to(lre_ref[:, st], (r, scan_w))
            lim = jnp.broadcast_to(lim_ref[:, st], (r, scan_w))
            hre, him = ore_ref[:, st], oim_ref[:, st]
            for t in range(steps):
                rws = slice(t * r, (t + 1) * r)
                hre, him = (lre * hre - lim * him + buf_s[rws, cre:cre + scan_w],
                            lre * him + lim * hre + buf_s[rws, cim:cim + scan_w])
                buf_s[rws, cre:cre + scan_w] = hre
                buf_s[rws, cim:cim + scan_w] = him
            ore_ref[:, st] = hre
            oim_ref[:, st] = him

    ys = []
    for half in range(S5_PARTS):
        ys.append(jnp.dot(buf_s[:, half * 2 * hs:(half + 1) * 2 * hs].astype(BF16), cmat_ref[half],
                          preferred_element_type=F32))
    y = jnp.concatenate(ys, axis=1) + d_ref[...] * u
    g = jax.nn.gelu(y)
    gate = jax.nn.sigmoid(jnp.dot(g.astype(BF16), wglu_ref[...], preferred_element_type=F32)
                          + bglu_ref[...])
    yd = g * gate
    for t in range(steps):
        for j in range(nslab):
            slab_s[j, pl.ds(t, r, stride=pitch), :] = yd[t * r:(t + 1) * r, j * LANES:(j + 1) * LANES]
    for b in range(r):
        for j in range(nslab):
            yd_ref[b, :, j * LANES:(j + 1) * LANES] = slab_s[j, b * pitch:b * pitch + steps, :].astype(
                yd_ref.dtype)


def _s5_call(u, bmat, cmat, lre, lim, dvec, wglu, bglu, h0re, h0im, steps, r):
    nb_total, t, uw = u.shape
    ns = S5_G * S5_P
    grid = (nb_total // r, t // steps)
    st_spec = pl.BlockSpec((r, ns), lambda b, i: (b, 0))
    return pl.pallas_call(
        _s5_kernel,
        grid=grid,
        in_specs=[pl.BlockSpec((r, steps, uw), lambda b, i: (b, i, 0)),
                  _const_spec(bmat.shape), _const_spec(cmat.shape),
                  _const_spec((1, ns)), _const_spec((1, ns)), _const_spec((1, uw)),
                  _const_spec(wglu.shape), _const_spec((1, uw)), st_spec, st_spec],
        out_specs=[pl.BlockSpec((r, steps, uw), lambda b, i: (b, i, 0)), st_spec, st_spec],
        out_shape=[jax.ShapeDtypeStruct((nb_total, t, uw), BF16),
                   jax.ShapeDtypeStruct((nb_total, ns), F32),
                   jax.ShapeDtypeStruct((nb_total, ns), F32)],
        scratch_shapes=[pltpu.VMEM((steps * r, 2 * ns), F32),
                        pltpu.VMEM((uw // LANES, r * _s5_pitch(steps), LANES), F32),
                        pltpu.VMEM((steps * r, uw), F32)],
        compiler_params=pltpu.CompilerParams(dimension_semantics=("arbitrary", "arbitrary"),
                                             vmem_limit_bytes=VMEM_LIMIT),
        name="s5_mixer",
    )(u, bmat, cmat, lre, lim, dvec, wglu, bglu, h0re, h0im)


def _rope_tables(pos):
    half = HD // 2
    inv_freq = ROPE_BASE ** (-jnp.arange(half, dtype=F32) / half)
    ang = pos[:, None] * inv_freq[None, :]
    cos, sin = jnp.cos(ang), jnp.sin(ang)
    return jnp.concatenate([cos, cos], axis=-1), jnp.concatenate([-sin, sin], axis=-1)


def _pad_lanes(v):
    return jnp.pad(v.astype(F32), (0, LANES - v.shape[0]))[None, :]


def _s5_tables(lam_re, lam_im, log_step, b_re, b_im, c_re, c_im):
    lr, li = lam_re.astype(F32), lam_im.astype(F32)
    dt = jnp.exp(log_step.astype(F32))[:, None]
    mag = jnp.exp(lr * dt)
    bar_re, bar_im = mag * jnp.cos(li * dt), mag * jnp.sin(li * dt)
    den = lr * lr + li * li
    cf_re = ((bar_re - 1.0) * lr + bar_im * li) / den
    cf_im = (bar_im * lr - (bar_re - 1.0) * li) / den
    bb_re = cf_re[..., None] * b_re.astype(F32) - cf_im[..., None] * b_im.astype(F32)
    bb_im = cf_re[..., None] * b_im.astype(F32) + cf_im[..., None] * b_re.astype(F32)
    gh = S5_G // S5_PARTS
    eye = jnp.eye(gh, dtype=F32)

    def in_mat(b):
        return jnp.einsum('gpc,gk->gckp', b, eye).reshape(gh * S5_GS, gh * S5_P)

    def out_mat(cc):
        return jnp.einsum('gcp,gk->gpkc', cc, eye).reshape(gh * S5_P, gh * S5_GS)

    bmats, cmats = [], []
    for half in range(S5_PARTS):
        sl = slice(half * gh, (half + 1) * gh)
        bmats.append(jnp.concatenate([in_mat(bb_re[sl]), in_mat(bb_im[sl])], axis=1))
        cmats.append(jnp.concatenate([out_mat(c_re.astype(F32)[sl]), out_mat(-c_im.astype(F32)[sl])],
                                     axis=0))
    return (jnp.stack(bmats).astype(BF16), jnp.stack(cmats).astype(BF16),
            bar_re.reshape(1, -1), bar_im.reshape(1, -1))


def _row(v):
    return v.astype(F32).reshape(1, -1)


def _forward(x, pos_offset, states, p, cfg):
    nb, t, d = x.shape
    n = nb * t
    bb, c = cfg['bb'], cfg['c']
    has_state = states is not None
    pos = jnp.arange(pos_offset, pos_offset + t, dtype=F32)
    cos, sin = _rope_tables(pos)
    tile_rows = lambda a: jnp.tile(a.reshape(t // c, 1, c, HD), (1, bb, 1, 1)).reshape(-1, HD)
    lb = jnp.cumsum(jax.nn.softmax(p['hgrn_lower_bounds'].astype(F32), axis=0), axis=0)[0]
    mix, ret_o, hg_o = _even_call(
        x, tile_rows(cos), tile_rows(sin), _row(p['norm_mix_pre'][0]), p['w_in_even'][0].astype(BF16),
        _row(p['ret_norm_w'][0]), _row(p['hgrn_norm_w'][0]), _row(lb),
        (states['ret'][0], states['hgrn'][0]) if has_state else None, bb, c, cfg['sub'])
    h = _post_call(x.reshape(n, d), [mix.reshape(n, -1)], [p['w_out_even'][0].astype(BF16)],
                   _row(p['norm_mix_post'][0]), _row(p['norm_ffn_pre'][0]),
                   _row(p['norm_ffn_post'][0]), p['w_ffn_up'][0].astype(BF16),
                   p['w_ffn_down'][0].astype(BF16), cfg['rows'])
    d_inner = H_SSD * P_SSD
    conv_dim = d_inner + 2 * G_SSD * N_SSD
    u_w = S5_G * S5_GS
    w1 = p['w_in_odd'][0].astype(BF16)
    st_shape = (nb, G_SSD, d_inner // G_SSD, N_SSD)
    odd_states = None
    if has_state:
        odd_states = (states['conv'][0], jnp.swapaxes(states['ssm'][0], -1, -2).reshape(st_shape))
    y, u, conv_o, sst_o = _odd_call(
        h.reshape(nb, t, d), _row(p['norm_mix_pre'][1]), w1, p['conv_w'][0].astype(F32),
        _row(p['conv_b'][0]), _pad_lanes(p['dt_bias'][0]), _pad_lanes(p['a_log'][0]),
        _row(jnp.repeat(p['d_ssm'][0], P_SSD)), _row(p['ssm_norm_w'][0]), odd_states, bb, c, cfg['sub'])
    ssm_o = jnp.swapaxes(sst_o.reshape(nb, H_SSD, P_SSD, N_SSD), -1, -2)
    bmat, cmat, lre, lim = _s5_tables(p['s5_lam_re'][0], p['s5_lam_im'][0], p['s5_log_step'][0],
                                      p['s5_b_re'][0], p['s5_b_im'][0], p['s5_c_re'][0],
                                      p['s5_c_im'][0])
    ns = S5_G * S5_P
    if has_state:
        h0re = states['s5_re'][0].reshape(nb, ns)
        h0im = states['s5_im'][0].reshape(nb, ns)
    else:
        h0re = jnp.zeros((nb, ns), F32)
        h0im = jnp.zeros((nb, ns), F32)
    yd, re_o, im_o = _s5_call(u, bmat, cmat, lre, lim, _row(p['s5_d'][0]),
                              p['w_glu'][0].astype(BF16), _row(p['b_glu'][0]), h0re, h0im,
                              cfg['s5_steps'], cfg['s5_rows'])
    yd = yd.reshape(n, u_w)
    wo = p['w_out_odd'][0].astype(BF16)
    h = _post_call(h, [y.reshape(n, d_inner), yd], [wo[:d_inner], wo[d_inner:]],
                   _row(p['norm_mix_post'][1]), _row(p['norm_ffn_pre'][1]),
                   _row(p['norm_ffn_post'][1]), p['w_ffn_up'][1].astype(BF16),
                   p['w_ffn_down'][1].astype(BF16), cfg['rows'])
    return (h.reshape(nb, t, d), ret_o[None], hg_o[None], ssm_o[None], conv_o[None],
            re_o.reshape(1, nb, S5_G, S5_P), im_o.reshape(1, nb, S5_G, S5_P))


PROMPT_CFG = dict(bb=2, c=128, sub=128, rows=1024, s5_steps=32, s5_rows=8)
SAMPLE_CFG = dict(bb=8, c=8, sub=8, rows=1024, s5_steps=8, s5_rows=32)


def kernel(x_prompt, x_sample, state_ret, state_hgrn, state_ssm, state_conv, state_s5_re, state_s5_im,
           norm_mix_pre, norm_mix_post, norm_ffn_pre, norm_ffn_post, w_in_even, w_out_even, ret_norm_w,
           hgrn_lower_bounds, hgrn_norm_w, w_in_odd, conv_w, conv_b, dt_bias, a_log, d_ssm, ssm_norm_w,
           s5_lam_re, s5_lam_im, s5_log_step, s5_b_re, s5_b_im, s5_c_re, s5_c_im, s5_d, w_glu, b_glu,
           w_out_odd, w_ffn_up, w_ffn_down):
    p = dict(norm_mix_pre=norm_mix_pre, norm_mix_post=norm_mix_post, norm_ffn_pre=norm_ffn_pre,
             norm_ffn_post=norm_ffn_post, w_in_even=w_in_even, w_out_even=w_out_even, ret_norm_w=ret_norm_w,
             hgrn_lower_bounds=hgrn_lower_bounds, hgrn_norm_w=hgrn_norm_w, w_in_odd=w_in_odd, conv_w=conv_w,
             conv_b=conv_b, dt_bias=dt_bias, a_log=a_log, d_ssm=d_ssm, ssm_norm_w=ssm_norm_w,
             s5_lam_re=s5_lam_re, s5_lam_im=s5_lam_im, s5_log_step=s5_log_step, s5_b_re=s5_b_re,
             s5_b_im=s5_b_im, s5_c_re=s5_c_re, s5_c_im=s5_c_im, s5_d=s5_d, w_glu=w_glu, b_glu=b_glu,
             w_out_odd=w_out_odd, w_ffn_up=w_ffn_up, w_ffn_down=w_ffn_down)
    past_len = 16384
    states = dict(ret=state_ret, hgrn=state_hgrn, ssm=state_ssm, conv=state_conv,
                  s5_re=state_s5_re, s5_im=state_s5_im)
    y_p, ret_p, hg_p, ssm_p, conv_p, re_p, im_p = _forward(x_prompt, 0, None, p, PROMPT_CFG)
    y_s, ret_s, hg_s, ssm_s, conv_s, re_s, im_s = _forward(x_sample, past_len, states, p, SAMPLE_CFG)
    return (y_p, y_s, ret_p, ret_s, hg_p, hg_s, ssm_p, ssm_s, conv_p, conv_s,
            re_p, re_s, im_p, im_s)
```

```python
import functools
import math

import jax
import jax.numpy as jnp
import numpy as np
from jax import lax
from jax.experimental import pallas as pl
from jax.experimental.pallas import tpu as pltpu

F32 = jnp.float32
BF16 = jnp.bfloat16
EPS = 1e-6
ROPE_BASE = 10000.0
LANES = 128
SUBLANES = 8
VMEM_LIMIT = 56 * 1024 * 1024

H_RET = 4
H_HG = 4
HD = 128
H_SSD = 16
P_SSD = 64
N_SSD = 128
G_SSD = 2
S5_G = 32
S5_GS = 16
S5_P = 64
HG_BLK = 16


def _rms(x, w):
    return x * lax.rsqrt(jnp.mean(x * x, axis=-1, keepdims=True) + EPS) * w


def _silu(x):
    return x * jax.nn.sigmoid(x)


def _dot(a, b):
    return jnp.dot(a.astype(BF16), b.astype(BF16), preferred_element_type=F32)


def _dot_nt(a, b):
    return lax.dot_general(a.astype(BF16), b.astype(BF16), (((1,), (1,)), ((), ())),
                           preferred_element_type=F32)


def _dot_tn(a, b):
    return lax.dot_general(a.astype(BF16), b.astype(BF16), (((0,), (0,)), ((), ())),
                           preferred_element_type=F32)


def _split3(x):
    hi = x.astype(BF16)
    r1 = x - hi.astype(F32)
    mid = r1.astype(BF16)
    lo = (r1 - mid.astype(F32)).astype(BF16)
    return hi, mid, lo


def _dot3(a_exact, parts):
    acc = None
    for p in parts:
        d = jnp.dot(a_exact, p, preferred_element_type=F32)
        acc = d if acc is None else acc + d
    return acc


def _dot3_tn(parts, b_exact):
    acc = None
    for p in parts:
        d = lax.dot_general(p, b_exact, (((0,), (0,)), ((), ())), preferred_element_type=F32)
        acc = d if acc is None else acc + d
    return acc


def _ones_where(mask):
    return jnp.where(mask, 1.0, 0.0).astype(BF16)


def _seq_masks(rows, seq_len):
    sh = seq_len.bit_length() - 1
    ti = lax.broadcasted_iota(jnp.int32, (rows, rows), 0)
    si = lax.broadcasted_iota(jnp.int32, (rows, rows), 1)
    same = lax.shift_right_logical(ti, sh) == lax.shift_right_logical(si, sh)
    return ti, si, same, same & (ti >= si)


def _interleave(units):
    live = list(units)
    while live:
        still = []
        for g in live:
            try:
                next(g)
                still.append(g)
            except StopIteration:
                pass
        live = still


def _const_spec(shape):
    nd = len(shape)
    return pl.BlockSpec(shape, lambda *_: (0,) * nd, pipeline_mode=pl.Buffered(1))


def _post_kernel(n_mix, ff_chunk, h_ref, *refs):
    mix_refs = refs[:n_mix]
    wout_refs = refs[n_mix:2 * n_mix]
    npost_ref, nfpre_ref, nfpost_ref, wup_ref, wdn_ref, o_ref = refs[2 * n_mix:]
    acc = None
    for m_ref, w_ref in zip(mix_refs, wout_refs):
        d = jnp.dot(m_ref[...], w_ref[...], preferred_element_type=F32)
        acc = d if acc is None else acc + d
    h1 = h_ref[...] + _rms(acc, npost_ref[...])
    hn = _rms(h1, nfpre_ref[...]).astype(BF16)
    ff = None
    for j in range(wup_ref.shape[1] // ff_chunk):
        sl = slice(j * ff_chunk, (j + 1) * ff_chunk)
        up = jnp.dot(hn, wup_ref[:, sl], preferred_element_type=F32)
        act = jnp.square(jnp.maximum(up, 0.0)).astype(BF16)
        d = jnp.dot(act, wdn_ref[sl, :], preferred_element_type=F32)
        ff = d if ff is None else ff + d
    o_ref[...] = h1 + _rms(ff, nfpost_ref[...])


def _post_call(h, mixes, wouts, npost, nfpre, nfpost, wup, wdn, rows):
    n, d = h.shape
    n_mix = len(mixes)
    row_spec = lambda w: pl.BlockSpec((rows, w), lambda i: (i, 0))
    in_specs = ([row_spec(d)] + [row_spec(m.shape[1]) for m in mixes]
                + [_const_spec(w.shape) for w in wouts]
                + [_const_spec((1, d))] * 3 + [_const_spec(wup.shape), _const_spec(wdn.shape)])
    return pl.pallas_call(
        functools.partial(_post_kernel, n_mix, 1024),
        grid=(n // rows,),
        in_specs=in_specs,
        out_specs=row_spec(d),
        out_shape=jax.ShapeDtypeStruct((n, d), F32),
        compiler_params=pltpu.CompilerParams(dimension_semantics=("arbitrary",),
                                             vmem_limit_bytes=VMEM_LIMIT),
        name="post_ffn",
    )(h, *mixes, *wouts, npost, nfpre, nfpost, wup, wdn)


def _ret_decay_tables(bb, seq_len):
    rows = bb * seq_len
    t = np.arange(rows)
    tl = t % seq_len
    valid = (t[:, None] // seq_len == t[None, :] // seq_len) & (t[:, None] >= t[None, :])
    pair, to_row, to_end = [], [], []
    for h in range(H_RET):
        lg = math.log1p(-(2.0 ** (-5.0 - h)))
        pair.append(np.where(valid, np.exp((t[:, None] - t[None, :]) * lg), 0.0))
        to_row.append(np.broadcast_to(np.exp((tl + 1.0) * lg)[:, None], (rows, HD)))
        to_end.append(np.broadcast_to(np.exp((seq_len - 1.0 - tl) * lg)[:, None], (rows, HD)))
    return tuple(jnp.asarray(np.stack(a), F32) for a in (pair, to_row, to_end))


def _even_block(bb, seq_len, proj_s, cos, sin, lb, retw_ref, hgw_ref, dec_refs, oret_ref, ohg_ref,
                mix_ref):
    rows = bb * seq_len
    hw = H_RET * HD
    base = 4 * hw
    sh = seq_len.bit_length() - 1
    ti, si, same, causal = _seq_masks(rows, seq_len)
    pair_ref, to_row_ref, to_end_ref = dec_refs
    seqs = [slice(b * seq_len, (b + 1) * seq_len) for b in range(bb)]
    one_shot = seq_len <= HG_BLK

    f = lb + (1.0 - lb) * jax.nn.sigmoid(proj_s[:, base + hw:base + 2 * hw])
    lf_parts = _split3(jnp.log(f))
    cum = _dot3(_ones_where(causal), lf_parts)
    if bb == 1:
        dtot = _dot3_tn(lf_parts, jnp.ones((rows, HD), BF16))
        cl = cum[rows - 1:rows, :]
    else:
        r_seq = lax.shift_right_logical(lax.broadcasted_iota(jnp.int32, (rows, HD), 0), sh)
        ind = _ones_where(r_seq == lax.broadcasted_iota(jnp.int32, (rows, HD), 1))
        dtot = _dot3_tn(lf_parts, ind)
        cl = _dot3(_ones_where(same), lf_parts)
    if one_shot:
        refc = _dot3(_ones_where(same & ((si & (seq_len - 1)) < seq_len // 2)), lf_parts)

    yield
    r_a, r_qs, r_kv, r_v = [], [], [], []
    for h in range(H_RET):
        col = lambda off: slice(off + h * HD, off + (h + 1) * HD)
        q = proj_s[:, col(0)]
        k = proj_s[:, col(hw)]
        v = proj_s[:, col(2 * hw)]
        q = q * cos + pltpu.roll(q, HD // 2, axis=1) * sin
        k = (k * cos + pltpu.roll(k, HD // 2, axis=1) * sin) * (HD ** -0.5)
        kd = k * to_end_ref[h]
        r_a.append(_dot_nt(q, k))
        r_qs.append([_dot(q[s], oret_ref[b, h]) for b, s in enumerate(seqs)])
        r_kv.append([_dot_tn(kd[s], v[s]) for s in seqs])
        r_v.append(v)

    g_a, g_qs, g_kv, g_v = [], [], [], []
    for h in range(H_HG):
        hs = slice(h * HD, (h + 1) * HD)
        col = lambda off: slice(base + off + h * HD, base + off + (h + 1) * HD)
        q = proj_s[:, col(0)]
        v = proj_s[:, col(2 * hw)]
        k = 1.0 - f[:, hs]
        cum_h = cum[:, hs]
        qe = q * jnp.exp(cum_h)
        ke = k * jnp.exp(cl[:, hs] - cum_h)
        g_qs.append([_dot(qe[s], ohg_ref[b, h]) for b, s in enumerate(seqs)])
        g_kv.append([_dot_tn(ke[s], v[s]) for s in seqs])
        if one_shot:
            ref = refc[:, hs]
            g_a.append([_dot_nt(q * jnp.exp(cum_h - ref), k * jnp.exp(ref - cum_h))])
        else:
            blocks = []
            for i in range(rows // HG_BLK):
                lo, hi = i * HG_BLK, (i + 1) * HG_BLK
                ref = cum_h[lo + HG_BLK // 2 - 1:lo + HG_BLK // 2, :]
                blocks.append(_dot_nt(q[lo:hi] * jnp.exp(cum_h[lo:hi] - ref),
                                      k[:hi] * jnp.exp(ref - cum_h[:hi])))
            g_a.append(blocks)
        g_v.append(v)

    yield
    r_o = []
    for h in range(H_RET):
        lg = math.log1p(-(2.0 ** (-5.0 - h)))
        qs = r_qs[h][0] if bb == 1 else jnp.concatenate(r_qs[h], axis=0)
        r_o.append(_dot(r_a[h] * pair_ref[h], r_v[h]) + qs * to_row_ref[h])
        for b in range(bb):
            oret_ref[b, h] = oret_ref[b, h] * math.exp(seq_len * lg) + r_kv[h][b]

    g_o = []
    for h in range(H_HG):
        if one_shot:
            o = _dot(jnp.where(causal, g_a[h][0], 0.0), g_v[h])
        else:
            outs = []
            for i in range(rows // HG_BLK):
                lo, hi = i * HG_BLK, (i + 1) * HG_BLK
                tb = lax.broadcasted_iota(jnp.int32, (HG_BLK, hi), 0) + lo
                sb = lax.broadcasted_iota(jnp.int32, (HG_BLK, hi), 1)
                outs.append(_dot(jnp.where(tb >= sb, g_a[h][i], 0.0), g_v[h][:hi]))
            o = jnp.concatenate(outs, axis=0)
        qs = g_qs[h][0] if bb == 1 else jnp.concatenate(g_qs[h], axis=0)
        g_o.append(o + qs)
        for b in range(bb):
            dcol = dtot[h * HD:(h + 1) * HD, :]
            if bb > 1:
                dcol = jnp.broadcast_to(dcol[:, b:b + 1], (HD, HD))
            ohg_ref[b, h] = ohg_ref[b, h] * jnp.exp(dcol) + g_kv[h][b]

    yield
    for h in range(H_RET):
        col = lambda off: slice(off + h * HD, off + (h + 1) * HD)
        mu = jnp.mean(r_o[h], axis=-1, keepdims=True)
        oc = r_o[h] - mu
        var = jnp.mean(oc * oc, axis=-1, keepdims=True)
        o = oc * lax.rsqrt(var + EPS) * retw_ref[:, col(0)] * _silu(proj_s[:, col(3 * hw)])
        mix_ref[:, :, col(0)] = o.reshape(bb, seq_len, HD).astype(mix_ref.dtype)
    for h in range(H_HG):
        hs = slice(h * HD, (h + 1) * HD)
        g = proj_s[:, base + 3 * hw + h * HD:base + 3 * hw + (h + 1) * HD]
        o = _rms(g_o[h], hgw_ref[:, hs]) * _silu(g)
        mix_ref[:, :, hw + h * HD:hw + (h + 1) * HD] = o.reshape(bb, seq_len, HD).astype(mix_ref.dtype)


PROJ_COLS = 512


def _project(x_ref, npre_ref, win_ref, proj_s):
    bb, seq_len, d = x_ref.shape
    hn = _rms(x_ref[...].reshape(bb * seq_len, d), npre_ref[...]).astype(BF16)
    n_in = win_ref.shape[1]
    for lo in range(0, n_in, PROJ_COLS):
        sl = slice(lo, min(lo + PROJ_COLS, n_in))
        proj_s[:, sl] = jnp.dot(hn, win_ref[:, sl], preferred_element_type=F32)


def _even_kernel(has_state, sub, x_ref, cos_ref, sin_ref, npre_ref, win_ref, retw_ref, hgw_ref,
                 lb_ref, pair_ref, to_row_ref, to_end_ref, *refs):
    if has_state:
        sret_ref, shg_ref, mix_ref, oret_ref, ohg_ref, proj_s = refs
    else:
        mix_ref, oret_ref, ohg_ref, proj_s = refs
    bb, seq_len, _ = x_ref.shape
    dec_refs = (pair_ref, to_row_ref, to_end_ref)

    @pl.when(pl.program_id(1) == 0)
    def _():
        if has_state:
            oret_ref[...] = sret_ref[...]
            ohg_ref[...] = shg_ref[...]
        else:
            oret_ref[...] = jnp.zeros(oret_ref.shape, F32)
            ohg_ref[...] = jnp.zeros(ohg_ref.shape, F32)

    _project(x_ref, npre_ref, win_ref, proj_s)
    consts = (lb_ref[...], retw_ref, hgw_ref, dec_refs)
    if seq_len > sub:
        assert bb == 1 and seq_len % sub == 0

        def chunk(j, carry):
            rs = pl.ds(pl.multiple_of(j * sub, sub), sub)
            _interleave([_even_block(1, sub, proj_s.at[rs], cos_ref[rs, :], sin_ref[rs, :], *consts,
                                     oret_ref, ohg_ref, mix_ref.at[:, rs])])
            return carry

        lax.fori_loop(0, seq_len // sub, chunk, 0)
    elif bb > 1 and seq_len > HG_BLK:
        units = []
        for u in range(bb):
            rs, one = pl.ds(u * seq_len, seq_len), pl.ds(u, 1)
            units.append(_even_block(1, seq_len, proj_s.at[rs], cos_ref[rs, :], sin_ref[rs, :], *consts,
                                     oret_ref.at[one], ohg_ref.at[one], mix_ref.at[one]))
        _interleave(units)
    else:
        _interleave([_even_block(bb, seq_len, proj_s, cos_ref[...], sin_ref[...], *consts,
                                 oret_ref, ohg_ref, mix_ref)])


def _even_call(x, cos, sin, npre, win, retw, hgw, lb, states, bb, c, sub):
    nb_total, t, d = x.shape
    has_state = states is not None
    n_in = win.shape[1]
    rows = bb * c
    grid = (nb_total // bb, t // c)
    st_spec = pl.BlockSpec((bb, H_RET, HD, HD), lambda b, i: (b, 0, 0, 0))
    in_specs = [pl.BlockSpec((bb, c, d), lambda b, i: (b, i, 0)),
                pl.BlockSpec((rows, HD), lambda b, i: (i, 0)),
                pl.BlockSpec((rows, HD), lambda b, i: (i, 0)),
                _const_spec((1, d)), _const_spec(win.shape),
                _const_spec((1, H_RET * HD)), _const_spec((1, H_HG * HD)), _const_spec((1, H_HG * HD))]
    tables = _ret_decay_tables(bb if sub <= HG_BLK else 1, sub)
    in_specs += [_const_spec(a.shape) for a in tables]
    args = [x, cos, sin, npre, win, retw, hgw, lb, *tables]
    if has_state:
        in_specs += [st_spec, st_spec]
        args += list(states)
    mix_w = (H_RET + H_HG) * HD
    return pl.pallas_call(
        functools.partial(_even_kernel, has_state, sub),
        grid=grid,
        in_specs=in_specs,
        out_specs=[pl.BlockSpec((bb, c, mix_w), lambda b, i: (b, i, 0)), st_spec, st_spec],
        out_shape=[jax.ShapeDtypeStruct((nb_total, t, mix_w), BF16),
                   jax.ShapeDtypeStruct((nb_total, H_RET, HD, HD), F32),
                   jax.ShapeDtypeStruct((nb_total, H_HG, HD, HD), F32)],
        scratch_shapes=[pltpu.VMEM((rows, n_in), F32)],
        compiler_params=pltpu.CompilerParams(dimension_semantics=("arbitrary", "arbitrary"),
                                             vmem_limit_bytes=VMEM_LIMIT),
        name="even_mixer",
    )(*args)


def _odd_block(has_state, bb, seq_len, proj_s, tail_s, ost_ref, sst_s, convw_ref, convb_ref, dtb_ref,
               alog_ref, dssm_ref, ssmw_ref, y_ref, u_ref):
    rows = bb * seq_len
    d_inner = H_SSD * P_SSD
    conv_dim = d_inner + 2 * G_SSD * N_SSD
    conv_w = convw_ref.shape[0]
    halo = conv_w - 1
    u_w = u_ref.shape[2]
    gw = d_inner // G_SSD
    heads_per_group = H_SSD // G_SSD
    pairs_per_group = heads_per_group // 2
    seqs = [slice(b * seq_len, (b + 1) * seq_len) for b in range(bb)]

    z = proj_s[:, 0:d_inner]
    xbc = proj_s[:, d_inner:d_inner + conv_dim]
    c0 = d_inner + conv_dim
    dt_raw = proj_s[:, c0:c0 + LANES]
    u_ref[...] = proj_s[:, c0 + H_SSD:c0 + H_SSD + u_w].reshape(bb, seq_len, u_w)

    tail = tail_s[...]
    conv = convb_ref[...] + xbc * convw_ref[halo:halo + 1, :]
    for s in range(1, conv_w):
        rolled = pltpu.roll(xbc, s, axis=0)
        if seq_len == SUBLANES:
            local = lax.broadcasted_iota(jnp.int32, (rows, conv_dim), 0) & (SUBLANES - 1)
            shifted = jnp.where(local < s, pltpu.roll(tail, rows - SUBLANES + s, axis=0), rolled)
        else:
            local = lax.broadcasted_iota(jnp.int32, (SUBLANES, conv_dim), 0)
            head = jnp.where(local < s, pltpu.roll(tail, s, axis=0), rolled[0:SUBLANES])
            shifted = jnp.concatenate([head, rolled[SUBLANES:]], axis=0)
        conv = conv + shifted * convw_ref[halo - s:halo - s + 1, :]
    tail_s[...] = xbc if seq_len == SUBLANES else xbc[rows - SUBLANES:rows]
    act = _silu(conv)
    xs = act[:, :d_inner]

    ti, si, same, causal = _seq_masks(rows, seq_len)
    lo_half = lax.broadcasted_iota(jnp.int32, (rows, LANES), 1) < P_SSD
    dt = jax.nn.softplus(dt_raw + dtb_ref[...])
    la_parts = _split3(dt * (-jnp.exp(alog_ref[...])))
    cum = _dot3(_ones_where(causal), la_parts)
    cum_t = _dot3_tn(la_parts, _ones_where(same & (ti <= si)))
    if bb > 1:
        rest = _dot3(_ones_where(same & (si > ti)), la_parts)
    bms = [act[:, d_inner + g * N_SSD:d_inner + (g + 1) * N_SSD] for g in range(G_SSD)]
    cms = [act[:, d_inner + (G_SSD + g) * N_SSD:d_inner + (G_SSD + g + 1) * N_SSD]
           for g in range(G_SSD)]
    scores = [_dot_nt(cms[g], bms[g]) for g in range(G_SSD)]
    yield

    bc = lambda a, h: jnp.broadcast_to(a[:, h:h + 1], (rows, LANES))
    bc_cum = [bc(cum, h) for h in range(H_SSD)]
    pair = lambda a, p: jnp.where(lo_half, bc(a, 2 * p), bc(a, 2 * p + 1))

    for g in range(G_SSD):
        xdts, xws, cums = [], [], []
        for pp in range(pairs_per_group):
            p = g * pairs_per_group + pp
            xdt = xs[:, p * LANES:(p + 1) * LANES] * pair(dt, p)
            cum_p = jnp.where(lo_half, bc_cum[2 * p], bc_cum[2 * p + 1])
            rest_p = cum_p[rows - 1:rows, :] - cum_p if bb == 1 else pair(rest, p)
            xdts.append(xdt)
            xws.append(xdt * jnp.exp(rest_p))
            cums.append(cum_p)
        xw = jnp.concatenate(xws, axis=1)
        if has_state:
            y_st = [_dot_nt(cms[g][s], ost_ref[b, g]) for b, s in enumerate(seqs)]
            kv = [_dot_tn(xw[s], bms[g][s]) for s in seqs]
        else:
            y_st = [_dot(cms[g][s], sst_s[b, g]) for b, s in enumerate(seqs)]
            kv = [_dot_tn(bms[g][s], xw[s]) for s in seqs]
        y_intra = []
        for pp in range(pairs_per_group):
            p = g * pairs_per_group + pp
            a_pair = []
            for h in (2 * p, 2 * p + 1):
                diff = bc_cum[h][:, :rows] - cum_t[h:h + 1, :]
                a_pair.append(scores[g] * jnp.where(causal, jnp.exp(jnp.where(causal, diff, 0.0)), 0.0))
            both = _dot(jnp.concatenate(a_pair, axis=0), xdts[pp])
            y_intra.append(jnp.where(lo_half, both[:rows], both[rows:]))
        yield
        for b in range(bb):
            last = (b + 1) * seq_len - 1
            if has_state:
                dec = [jnp.broadcast_to(jnp.exp(bc_cum[g * heads_per_group + hh][last:last + 1, :]),
                                        (P_SSD, N_SSD)) for hh in range(heads_per_group)]
                ost_ref[b, g] = ost_ref[b, g] * jnp.concatenate(dec, axis=0) + kv[b]
            else:
                dec = jnp.exp(jnp.concatenate([c_p[last:last + 1, :] for c_p in cums], axis=1))
                sst_s[b, g] = sst_s[b, g] * dec + kv[b]
        y_state = y_st[0] if bb == 1 else jnp.concatenate(y_st, axis=0)
        gs = slice(g * gw, (g + 1) * gw)
        y = (jnp.concatenate(y_intra, axis=1) + y_state * jnp.exp(jnp.concatenate(cums, axis=1))
             + dssm_ref[:, gs] * xs[:, gs])
        y = _rms(y * _silu(z[:, gs]), ssmw_ref[:, gs])
        y_ref[:, :, gs] = y.reshape(bb, seq_len, gw).astype(y_ref.dtype)


def _odd_kernel(has_state, sub, x_ref, npre_ref, win_ref, convw_ref, convb_ref, dtb_ref, alog_ref,
                dssm_ref, ssmw_ref, *refs):
    if has_state:
        sconv_ref, sst_ref, y_ref, u_ref, oconv_ref, ost_ref, proj_s, tail_s = refs
        sst_s = None
    else:
        y_ref, u_ref, oconv_ref, ost_ref, proj_s, tail_s, sst_s = refs
    bb, seq_len, _ = x_ref.shape
    halo = convw_ref.shape[0] - 1

    @pl.when(pl.program_id(1) == 0)
    def _():
        tail_s[...] = jnp.zeros(tail_s.shape, F32)
        if has_state:
            for b in range(bb):
                tail_s[(b + 1) * SUBLANES - halo:(b + 1) * SUBLANES, :] = sconv_ref[b]
            ost_ref[...] = sst_ref[...]
        else:
            sst_s[...] = jnp.zeros(sst_s.shape, F32)

    _project(x_ref, npre_ref, win_ref, proj_s)
    consts = (convw_ref, convb_ref, dtb_ref, alog_ref, dssm_ref, ssmw_ref)
    if seq_len > sub:
        assert bb == 1 and seq_len % sub == 0

        def chunk(j, carry):
            rs = pl.ds(pl.multiple_of(j * sub, sub), sub)
            _interleave([_odd_block(has_state, 1, sub, proj_s.at[rs], tail_s, ost_ref, sst_s, *consts,
                                    y_ref.at[:, rs], u_ref.at[:, rs])])
            return carry

        lax.fori_loop(0, seq_len // sub, chunk, 0)
    elif bb > 1 and seq_len > SUBLANES:
        units = []
        for u in range(bb):
            one = pl.ds(u, 1)
            units.append(_odd_block(
                has_state, 1, seq_len, proj_s.at[pl.ds(u * seq_len, seq_len)],
                tail_s.at[pl.ds(u * SUBLANES, SUBLANES)], ost_ref.at[one],
                None if sst_s is None else sst_s.at[one], *consts, y_ref.at[one], u_ref.at[one]))
        _interleave(units)
    else:
        _interleave([_odd_block(has_state, bb, seq_len, proj_s, tail_s, ost_ref, sst_s, *consts,
                                y_ref, u_ref)])

    @pl.when(pl.program_id(1) == pl.num_programs(1) - 1)
    def _():
        for b in range(bb):
            oconv_ref[b] = tail_s[(b + 1) * SUBLANES - halo:(b + 1) * SUBLANES, :]
            if not has_state:
                for g in range(G_SSD):
                    for ps in range(0, ost_ref.shape[2], LANES):
                        ost_ref[b, g, ps:ps + LANES, :] = sst_s[b, g, :, ps:ps + LANES].T


def _odd_call(x, npre, win, convw, convb, dtb, alog, dssm, ssmw, states, bb, c, sub):
    nb_total, t, d = x.shape
    has_state = states is not None
    n_in = win.shape[1]
    d_inner = H_SSD * P_SSD
    conv_dim = d_inner + 2 * G_SSD * N_SSD
    u_w = S5_G * S5_GS
    halo = convw.shape[0] - 1
    rows = bb * c
    grid = (nb_total // bb, t // c)
    st_shape = (G_SSD, d_inner // G_SSD, N_SSD)
    conv_spec = pl.BlockSpec((bb, halo, conv_dim), lambda b, i: (b, 0, 0))
    ssm_spec = pl.BlockSpec((bb,) + st_shape, lambda b, i: (b, 0, 0, 0))
    in_specs = [pl.BlockSpec((bb, c, d), lambda b, i: (b, i, 0)),
                _const_spec((1, d)), _const_spec(win.shape), _const_spec(convw.shape),
                _const_spec((1, conv_dim)), _const_spec((1, LANES)), _const_spec((1, LANES)),
                _const_spec((1, d_inner)), _const_spec((1, d_inner))]
    args = [x, npre, win, convw, convb, dtb, alog, dssm, ssmw]
    scratch = [pltpu.VMEM((rows, n_in), F32), pltpu.VMEM((bb * SUBLANES, conv_dim), F32)]
    if has_state:
        in_specs += [conv_spec, ssm_spec]
        args += list(states)
    else:
        scratch.append(pltpu.VMEM((bb, G_SSD, N_SSD, d_inner // G_SSD), F32))
    return pl.pallas_call(
        functools.partial(_odd_kernel, has_state, sub),
        grid=grid,
        in_specs=in_specs,
        out_specs=[pl.BlockSpec((bb, c, d_inner), lambda b, i: (b, i, 0)),
                   pl.BlockSpec((bb, c, u_w), lambda b, i: (b, i, 0)),
                   conv_spec, ssm_spec],
        out_shape=[jax.ShapeDtypeStruct((nb_total, t, d_inner), BF16),
                   jax.ShapeDtypeStruct((nb_total, t, u_w), F32),
                   jax.ShapeDtypeStruct((nb_total, halo, conv_dim), F32),
                   jax.ShapeDtypeStruct((nb_total,) + st_shape, F32)],
        scratch_shapes=scratch,
        compiler_params=pltpu.CompilerParams(dimension_semantics=("arbitrary", "arbitrary"),
                                             vmem_limit_bytes=VMEM_LIMIT),
        name="odd_mixer",
    )(*args)


S5_PARTS = 4
S5_PART_IN = S5_G * S5_GS // S5_PARTS
S5_PART_ST = S5_G * S5_P // S5_PARTS
S5_SCAN_VREGS = 4


def _s5_pitch(steps):
    return steps if steps % 16 == 8 else steps + 8


def _s5_kernel(u_ref, bmat_ref, cmat_ref, lre_ref, lim_ref, d_ref, wglu_ref, bglu_ref,
               h0re_ref, h0im_ref, yd_ref, ore_ref, oim_ref, buf_s, slab_s, ut_s):
    r, steps, uw = u_ref.shape
    hs = S5_PART_ST
    scan_w = min(hs, max(LANES, S5_SCAN_VREGS * SUBLANES * LANES // r))
    pitch = _s5_pitch(steps)
    nslab = uw // LANES

    @pl.when(pl.program_id(1) == 0)
    def _():
        ore_ref[...] = h0re_ref[...]
        oim_ref[...] = h0im_ref[...]

    for b in range(r):
        for j in range(nslab):
            slab_s[j, b * pitch:b * pitch + steps, :] = u_ref[b, :, j * LANES:(j + 1) * LANES]
    for t in range(steps):
        for j in range(nslab):
            ut_s[t * r:(t + 1) * r, j * LANES:(j + 1) * LANES] = slab_s[j, pl.ds(t, r, stride=pitch), :]
    u = ut_s[...]
    ub = u.astype(BF16)
    for half in range(S5_PARTS):
        buf_s[:, half * 2 * hs:(half + 1) * 2 * hs] = jnp.dot(
            ub[:, half * S5_PART_IN:(half + 1) * S5_PART_IN], bmat_ref[half],
            preferred_element_type=F32)

    for half in range(S5_PARTS):
        for j in range(hs // scan_w):
            st = slice(half * hs + j * scan_w, half * hs + (j + 1) * scan_w)
            cre = half * 2 * hs + j * scan_w
            cim = cre + hs
            lre = jnp.broadcast_to(lre_ref[:, st], (r, scan_w))
            lim = jnp.broadcast_to(lim_ref[:, st], (r, scan_w))
            hre, him = ore_ref[:, st], oim_ref[:, st]
            for t in range(steps):
                rws = slice(t * r, (t + 1) * r)
                hre, him = (lre * hre - lim * him + buf_s[rws, cre:cre + scan_w],
                            lre * him + lim * hre + buf_s[rws, cim:cim + scan_w])
                buf_s[rws, cre:cre + scan_w] = hre
                buf_s[rws, cim:cim + scan_w] = him
            ore_ref[:, st] = hre
            oim_ref[:, st] = him

    ys = []
    for half in range(S5_PARTS):
        ys.append(jnp.dot(buf_s[:, half * 2 * hs:(half + 1) * 2 * hs].astype(BF16), cmat_ref[half],
                          preferred_element_type=F32))
    y = jnp.concatenate(ys, axis=1) + d_ref[...] * u
    g = jax.nn.gelu(y)
    gate = jax.nn.sigmoid(jnp.dot(g.astype(BF16), wglu_ref[...], preferred_element_type=F32)
                          + bglu_ref[...])
    yd = g * gate
    for t in range(steps):
        for j in range(nslab):
            slab_s[j, pl.ds(t, r, stride=pitch), :] = yd[t * r:(t + 1) * r, j * LANES:(j + 1) * LANES]
    for b in range(r):
        for j in range(nslab):
            yd_ref[b, :, j * LANES:(j + 1) * LANES] = slab_s[j, b * pitch:b * pitch + steps, :].astype(
                yd_ref.dtype)


def _s5_call(u, bmat, cmat, lre, lim, dvec, wglu, bglu, h0re, h0im, steps, r):
    nb_total, t, uw = u.shape
    ns = S5_G * S5_P
    grid = (nb_total // r, t // steps)
    st_spec = pl.BlockSpec((r, ns), lambda b, i: (b, 0))
    return pl.pallas_call(
        _s5_kernel,
        grid=grid,
        in_specs=[pl.BlockSpec((r, steps, uw), lambda b, i: (b, i, 0)),
                  _const_spec(bmat.shape), _const_spec(cmat.shape),
                  _const_spec((1, ns)), _const_spec((1, ns)), _const_spec((1, uw)),
                  _const_spec(wglu.shape), _const_spec((1, uw)), st_spec, st_spec],
        out_specs=[pl.BlockSpec((r, steps, uw), lambda b, i: (b, i, 0)), st_spec, st_spec],
        out_shape=[jax.ShapeDtypeStruct((nb_total, t, uw), BF16),
                   jax.ShapeDtypeStruct((nb_total, ns), F32),
                   jax.ShapeDtypeStruct((nb_total, ns), F32)],
        scratch_shapes=[pltpu.VMEM((steps * r, 2 * ns), F32),
                        pltpu.VMEM((uw // LANES, r * _s5_pitch(steps), LANES), F32),
                        pltpu.VMEM((steps * r, uw), F32)],
        compiler_params=pltpu.CompilerParams(dimension_semantics=("arbitrary", "arbitrary"),
                                             vmem_limit_bytes=VMEM_LIMIT),
        name="s5_mixer",
    )(u, bmat, cmat, lre, lim, dvec, wglu, bglu, h0re, h0im)


def _rope_tables(pos):
    half = HD // 2
    inv_freq = ROPE_BASE ** (-jnp.arange(half, dtype=F32) / half)
    ang = pos[:, None] * inv_freq[None, :]
    cos, sin = jnp.cos(ang), jnp.sin(ang)
    return jnp.concatenate([cos, cos], axis=-1), jnp.concatenate([-sin, sin], axis=-1)


def _pad_lanes(v):
    return jnp.pad(v.astype(F32), (0, LANES - v.shape[0]))[None, :]


def _s5_tables(lam_re, lam_im, log_step, b_re, b_im, c_re, c_im):
    lr, li = lam_re.astype(F32), lam_im.astype(F32)
    dt = jnp.exp(log_step.astype(F32))[:, None]
    mag = jnp.exp(lr * dt)
    bar_re, bar_im = mag * jnp.cos(li * dt), mag * jnp.sin(li * dt)
    den = lr * lr + li * li
    cf_re = ((bar_re - 1.0) * lr + bar_im * li) / den
    cf_im = (bar_im * lr - (bar_re - 1.0) * li) / den
    bb_re = cf_re[..., None] * b_re.astype(F32) - cf_im[..., None] * b_im.astype(F32)
    bb_im = cf_re[..., None] * b_im.astype(F32) + cf_im[..., None] * b_re.astype(F32)
    gh = S5_G // S5_PARTS
    eye = jnp.eye(gh, dtype=F32)

    def in_mat(b):
        return jnp.einsum('gpc,gk->gckp', b, eye).reshape(gh * S5_GS, gh * S5_P)

    def out_mat(cc):
        return jnp.einsum('gcp,gk->gpkc', cc, eye).reshape(gh * S5_P, gh * S5_GS)

    bmats, cmats = [], []
    for half in range(S5_PARTS):
        sl = slice(half * gh, (half + 1) * gh)
        bmats.append(jnp.concatenate([in_mat(bb_re[sl]), in_mat(bb_im[sl])], axis=1))
        cmats.append(jnp.concatenate([out_mat(c_re.astype(F32)[sl]), out_mat(-c_im.astype(F32)[sl])],
                                     axis=0))
    return (jnp.stack(bmats).astype(BF16), jnp.stack(cmats).astype(BF16),
            bar_re.reshape(1, -1), bar_im.reshape(1, -1))


def _row(v):
    return v.astype(F32).reshape(1, -1)


def _forward(x, pos_offset, states, p, cfg):
    nb, t, d = x.shape
    n = nb * t
    bb, c = cfg['bb'], cfg['c']
    has_state = states is not None
    pos = jnp.arange(pos_offset, pos_offset + t, dtype=F32)
    cos, sin = _rope_tables(pos)
    tile_rows = lambda a: jnp.tile(a.reshape(t // c, 1, c, HD), (1, bb, 1, 1)).reshape(-1, HD)
    lb = jnp.cumsum(jax.nn.softmax(p['hgrn_lower_bounds'].astype(F32), axis=0), axis=0)[0]
    mix, ret_o, hg_o = _even_call(
        x, tile_rows(cos), tile_rows(sin), _row(p['norm_mix_pre'][0]), p['w_in_even'][0].astype(BF16),
        _row(p['ret_norm_w'][0]), _row(p['hgrn_norm_w'][0]), _row(lb),
        (states['ret'][0], states['hgrn'][0]) if has_state else None, bb, c, cfg['sub'])
    h = _post_call(x.reshape(n, d), [mix.reshape(n, -1)], [p['w_out_even'][0].astype(BF16)],
                   _row(p['norm_mix_post'][0]), _row(p['norm_ffn_pre'][0]),
                   _row(p['norm_ffn_post'][0]), p['w_ffn_up'][0].astype(BF16),
                   p['w_ffn_down'][0].astype(BF16), cfg['rows'])
    d_inner = H_SSD * P_SSD
    conv_dim = d_inner + 2 * G_SSD * N_SSD
    u_w = S5_G * S5_GS
    w1 = p['w_in_odd'][0].astype(BF16)
    st_shape = (nb, G_SSD, d_inner // G_SSD, N_SSD)
    odd_states = None
    if has_state:
        odd_states = (states['conv'][0], jnp.swapaxes(states['ssm'][0], -1, -2).reshape(st_shape))
    y, u, conv_o, sst_o = _odd_call(
        h.reshape(nb, t, d), _row(p['norm_mix_pre'][1]), w1, p['conv_w'][0].astype(F32),
        _row(p['conv_b'][0]), _pad_lanes(p['dt_bias'][0]), _pad_lanes(p['a_log'][0]),
        _row(jnp.repeat(p['d_ssm'][0], P_SSD)), _row(p['ssm_norm_w'][0]), odd_states, bb, c, cfg['sub'])
    ssm_o = jnp.swapaxes(sst_o.reshape(nb, H_SSD, P_SSD, N_SSD), -1, -2)
    bmat, cmat, lre, lim = _s5_tables(p['s5_lam_re'][0], p['s5_lam_im'][0], p['s5_log_step'][0],
                                      p['s5_b_re'][0], p['s5_b_im'][0], p['s5_c_re'][0],
                                      p['s5_c_im'][0])
    ns = S5_G * S5_P
    if has_state:
        h0re = states['s5_re'][0].reshape(nb, ns)
        h0im = states['s5_im'][0].reshape(nb, ns)
    else:
        h0re = jnp.zeros((nb, ns), F32)
        h0im = jnp.zeros((nb, ns), F32)
    yd, re_o, im_o = _s5_call(u, bmat, cmat, lre, lim, _row(p['s5_d'][0]),
                              p['w_glu'][0].astype(BF16), _row(p['b_glu'][0]), h0re, h0im,
                              cfg['s5_steps'], cfg['s5_rows'])
    yd = yd.reshape(n, u_w)
    wo = p['w_out_odd'][0].astype(BF16)
    h = _post_call(h, [y.reshape(n, d_inner), yd], [wo[:d_inner], wo[d_inner:]],
                   _row(p['norm_mix_post'][1]), _row(p['norm_ffn_pre'][1]),
                   _row(p['norm_ffn_post'][1]), p['w_ffn_up'][1].astype(BF16),
                   p['w_ffn_down'][1].astype(BF16), cfg['rows'])
    return (h.reshape(nb, t, d), ret_o[None], hg_o[None], ssm_o[None], conv_o[None],
            re_o.reshape(1, nb, S5_G, S5_P), im_o.reshape(1, nb, S5_G, S5_P))


PROMPT_CFG = dict(bb=4, c=128, sub=128, rows=1024, s5_steps=64, s5_rows=8)
SAMPLE_CFG = dict(bb=8, c=8, sub=8, rows=1024, s5_steps=8, s5_rows=32)


def kernel(x_prompt, x_sample, state_ret, state_hgrn, state_ssm, state_conv, state_s5_re, state_s5_im,
           norm_mix_pre, norm_mix_post, norm_ffn_pre, norm_ffn_post, w_in_even, w_out_even, ret_norm_w,
           hgrn_lower_bounds, hgrn_norm_w, w_in_odd, conv_w, conv_b, dt_bias, a_log, d_ssm, ssm_norm_w,
           s5_lam_re, s5_lam_im, s5_log_step, s5_b_re, s5_b_im, s5_c_re, s5_c_im, s5_d, w_glu, b_glu,
           w_out_odd, w_ffn_up, w_ffn_down):
    p = dict(norm_mix_pre=norm_mix_pre, norm_mix_post=norm_mix_post, norm_ffn_pre=norm_ffn_pre,
             norm_ffn_post=norm_ffn_post, w_in_even=w_in_even, w_out_even=w_out_even, ret_norm_w=ret_norm_w,
             hgrn_lower_bounds=hgrn_lower_bounds, hgrn_norm_w=hgrn_norm_w, w_in_odd=w_in_odd, conv_w=conv_w,
             conv_b=conv_b, dt_bias=dt_bias, a_log=a_log, d_ssm=d_ssm, ssm_norm_w=ssm_norm_w,
             s5_lam_re=s5_lam_re, s5_lam_im=s5_lam_im, s5_log_step=s5_log_step, s5_b_re=s5_b_re,
             s5_b_im=s5_b_im, s5_c_re=s5_c_re, s5_c_im=s5_c_im, s5_d=s5_d, w_glu=w_glu, b_glu=b_glu,
             w_out_odd=w_out_odd, w_ffn_up=w_ffn_up, w_ffn_down=w_ffn_down)
    past_len = 16384
    states = dict(ret=state_ret, hgrn=state_hgrn, ssm=state_ssm, conv=state_conv,
                  s5_re=state_s5_re, s5_im=state_s5_im)
    y_p, ret_p, hg_p, ssm_p, conv_p, re_p, im_p = _forward(x_prompt, 0, None, p, PROMPT_CFG)
    y_s, ret_s, hg_s, ssm_s, conv_s, re_s, im_s = _forward(x_sample, past_len, states, p, SAMPLE_CFG)
    return (y_p, y_s, ret_p, ret_s, hg_p, hg_s, ssm_p, ssm_s, conv_p, conv_s,
            re_p, re_s, im_p, im_s)
```

```python
import functools
import math

import jax
import jax.numpy as jnp
import numpy as np
from jax import lax
from jax.experimental import pallas as pl
from jax.experimental.pallas import tpu as pltpu

F32 = jnp.float32
BF16 = jnp.bfloat16
EPS = 1e-6
ROPE_BASE = 10000.0
LANES = 128
SUBLANES = 8
VMEM_LIMIT = 56 * 1024 * 1024

H_RET = 4
H_HG = 4
HD = 128
H_SSD = 16
P_SSD = 64
N_SSD = 128
G_SSD = 2
S5_G = 32
S5_GS = 16
S5_P = 64
HG_BLK = 16


def _rms(x, w):
    return x * lax.rsqrt(jnp.mean(x * x, axis=-1, keepdims=True) + EPS) * w


def _silu(x):
    return x * jax.nn.sigmoid(x)


def _dot(a, b):
    return jnp.dot(a.astype(BF16), b.astype(BF16), preferred_element_type=F32)


def _dot_nt(a, b):
    return lax.dot_general(a.astype(BF16), b.astype(BF16), (((1,), (1,)), ((), ())),
                           preferred_element_type=F32)


def _dot_tn(a, b):
    return lax.dot_general(a.astype(BF16), b.astype(BF16), (((0,), (0,)), ((), ())),
                           preferred_element_type=F32)


def _split3(x):
    hi = x.astype(BF16)
    r1 = x - hi.astype(F32)
    mid = r1.astype(BF16)
    lo = (r1 - mid.astype(F32)).astype(BF16)
    return hi, mid, lo


def _dot3(a_exact, parts):
    acc = None
    for p in parts:
        d = jnp.dot(a_exact, p, preferred_element_type=F32)
        acc = d if acc is None else acc + d
    return acc


def _dot3_tn(parts, b_exact):
    acc = None
    for p in parts:
        d = lax.dot_general(p, b_exact, (((0,), (0,)), ((), ())), preferred_element_type=F32)
        acc = d if acc is None else acc + d
    return acc


def _ones_where(mask):
    return jnp.where(mask, 1.0, 0.0).astype(BF16)


def _seq_masks(rows, seq_len):
    sh = seq_len.bit_length() - 1
    ti = lax.broadcasted_iota(jnp.int32, (rows, rows), 0)
    si = lax.broadcasted_iota(jnp.int32, (rows, rows), 1)
    same = lax.shift_right_logical(ti, sh) == lax.shift_right_logical(si, sh)
    return ti, si, same, same & (ti >= si)


def _interleave(units):
    live = list(units)
    while live:
        still = []
        for g in live:
            try:
                next(g)
                still.append(g)
            except StopIteration:
                pass
        live = still


def _const_spec(shape):
    nd = len(shape)
    return pl.BlockSpec(shape, lambda *_: (0,) * nd, pipeline_mode=pl.Buffered(1))


def _post_kernel(n_mix, ff_chunk, n_parts, h_ref, *refs):
    mix_refs = refs[:n_mix]
    wout_ref, npost_ref, nfpre_ref, nfpost_ref, wup_ref, wdn_ref, o_ref = refs[n_mix:]
    part = h_ref.shape[0] // n_parts
    groups = [slice(i * part, (i + 1) * part) for i in range(n_parts)]
    accs = []
    for rs in groups:
        acc, off = None, 0
        for m_ref in mix_refs:
            w = m_ref.shape[1]
            d = jnp.dot(m_ref[rs, :], wout_ref[off:off + w, :], preferred_element_type=F32)
            acc = d if acc is None else acc + d
            off += w
        accs.append(acc)
    for rs, acc in zip(groups, accs):
        h1 = h_ref[rs, :] + _rms(acc, npost_ref[...])
        hn = _rms(h1, nfpre_ref[...]).astype(BF16)
        ff = None
        for j in range(wup_ref.shape[1] // ff_chunk):
            sl = slice(j * ff_chunk, (j + 1) * ff_chunk)
            up = jnp.dot(hn, wup_ref[:, sl], preferred_element_type=F32)
            act = jnp.square(jnp.maximum(up, 0.0)).astype(BF16)
            d = jnp.dot(act, wdn_ref[sl, :], preferred_element_type=F32)
            ff = d if ff is None else ff + d
        o_ref[rs, :] = h1 + _rms(ff, nfpost_ref[...])


def _layer_spec(shape, layer):
    nd = len(shape) - 1
    return pl.BlockSpec((None,) + tuple(shape[1:]), lambda *_: (layer,) + (0,) * nd,
                        pipeline_mode=pl.Buffered(1))


def _post_call(h, mixes, wout, npost, nfpre, nfpost, wup, wdn, layer, rows):
    n, d = h.shape
    n_mix = len(mixes)
    row_spec = lambda w: pl.BlockSpec((rows, w), lambda i: (i, 0))
    in_specs = ([row_spec(d)] + [row_spec(m.shape[1]) for m in mixes] + [_const_spec(wout.shape)]
                + [_layer_spec(a.shape, layer) for a in (npost, nfpre, nfpost, wup, wdn)])
    return pl.pallas_call(
        functools.partial(_post_kernel, n_mix, 1024, 2),
        grid=(n // rows,),
        in_specs=in_specs,
        out_specs=row_spec(d),
        out_shape=jax.ShapeDtypeStruct((n, d), F32),
        compiler_params=pltpu.CompilerParams(dimension_semantics=("arbitrary",),
                                             vmem_limit_bytes=VMEM_LIMIT),
        name="post_ffn",
    )(h, *mixes, wout, npost, nfpre, nfpost, wup, wdn)


def _ret_decay_tables(bb, seq_len):
    rows = bb * seq_len
    t = np.arange(rows)
    tl = t % seq_len
    valid = (t[:, None] // seq_len == t[None, :] // seq_len) & (t[:, None] >= t[None, :])
    pair, to_row, to_end = [], [], []
    for h in range(H_RET):
        lg = math.log1p(-(2.0 ** (-5.0 - h)))
        pair.append(np.where(valid, np.exp((t[:, None] - t[None, :]) * lg), 0.0))
        to_row.append(np.broadcast_to(np.exp((tl + 1.0) * lg)[:, None], (rows, HD)))
        to_end.append(np.broadcast_to(np.exp((seq_len - 1.0 - tl) * lg)[:, None], (rows, HD)))
    return tuple(jnp.asarray(np.stack(a), F32) for a in (pair, to_row, to_end))


def _even_block(bb, seq_len, proj_s, cos, sin, lb, retw_ref, hgw_ref, dec_refs, oret_ref, ohg_ref,
                mix_ref):
    rows = bb * seq_len
    hw = H_RET * HD
    base = 4 * hw
    sh = seq_len.bit_length() - 1
    ti, si, same, causal = _seq_masks(rows, seq_len)
    pair_ref, to_row_ref, to_end_ref = dec_refs
    seqs = [slice(b * seq_len, (b + 1) * seq_len) for b in range(bb)]
    one_shot = seq_len <= HG_BLK

    f = lb + (1.0 - lb) * jax.nn.sigmoid(proj_s[:, base + hw:base + 2 * hw])
    lf_parts = _split3(jnp.log(f))
    cum = _dot3(_ones_where(causal), lf_parts)
    if bb == 1:
        dtot = _dot3_tn(lf_parts, jnp.ones((rows, HD), BF16))
        cl = cum[rows - 1:rows, :]
    else:
        r_seq = lax.shift_right_logical(lax.broadcasted_iota(jnp.int32, (rows, HD), 0), sh)
        ind = _ones_where(r_seq == lax.broadcasted_iota(jnp.int32, (rows, HD), 1))
        dtot = _dot3_tn(lf_parts, ind)
        cl = _dot3(_ones_where(same), lf_parts)
    if one_shot:
        refc = _dot3(_ones_where(same & ((si & (seq_len - 1)) < seq_len // 2)), lf_parts)

    yield
    r_a, r_qs, r_kv, r_v = [], [], [], []
    for h in range(H_RET):
        col = lambda off: slice(off + h * HD, off + (h + 1) * HD)
        q = proj_s[:, col(0)]
        k = proj_s[:, col(hw)]
        v = proj_s[:, col(2 * hw)]
        q = q * cos + pltpu.roll(q, HD // 2, axis=1) * sin
        k = (k * cos + pltpu.roll(k, HD // 2, axis=1) * sin) * (HD ** -0.5)
        kd = k * to_end_ref[h]
        r_a.append(_dot_nt(q, k))
        r_qs.append([_dot(q[s], oret_ref[b, h]) for b, s in enumerate(seqs)])
        r_kv.append([_dot_tn(kd[s], v[s]) for s in seqs])
        r_v.append(v)

    g_a, g_qs, g_kv, g_v = [], [], [], []
    for h in range(H_HG):
        hs = slice(h * HD, (h + 1) * HD)
        col = lambda off: slice(base + off + h * HD, base + off + (h + 1) * HD)
        q = proj_s[:, col(0)]
        v = proj_s[:, col(2 * hw)]
        k = 1.0 - f[:, hs]
        cum_h = cum[:, hs]
        qe = q * jnp.exp(cum_h)
        ke = k * jnp.exp(cl[:, hs] - cum_h)
        g_qs.append([_dot(qe[s], ohg_ref[b, h]) for b, s in enumerate(seqs)])
        g_kv.append([_dot_tn(ke[s], v[s]) for s in seqs])
        if one_shot:
            ref = refc[:, hs]
            g_a.append([_dot_nt(q * jnp.exp(cum_h - ref), k * jnp.exp(ref - cum_h))])
        else:
            blocks = []
            for i in range(rows // HG_BLK):
                lo, hi = i * HG_BLK, (i + 1) * HG_BLK
                ref = cum_h[lo + HG_BLK // 2 - 1:lo + HG_BLK // 2, :]
                blocks.append(_dot_nt(q[lo:hi] * jnp.exp(cum_h[lo:hi] - ref),
                                      k[:hi] * jnp.exp(ref - cum_h[:hi])))
            g_a.append(blocks)
        g_v.append(v)

    yield
    r_o = []
    for h in range(H_RET):
        lg = math.log1p(-(2.0 ** (-5.0 - h)))
        qs = r_qs[h][0] if bb == 1 else jnp.concatenate(r_qs[h], axis=0)
        r_o.append(_dot(r_a[h] * pair_ref[h], r_v[h]) + qs * to_row_ref[h])
        for b in range(bb):
            oret_ref[b, h] = oret_ref[b, h] * math.exp(seq_len * lg) + r_kv[h][b]

    g_o = []
    for h in range(H_HG):
        if one_shot:
            o = _dot(jnp.where(causal, g_a[h][0], 0.0), g_v[h])
        else:
            outs = []
            for i in range(rows // HG_BLK):
                lo, hi = i * HG_BLK, (i + 1) * HG_BLK
                tb = lax.broadcasted_iota(jnp.int32, (HG_BLK, hi), 0) + lo
                sb = lax.broadcasted_iota(jnp.int32, (HG_BLK, hi), 1)
                outs.append(_dot(jnp.where(tb >= sb, g_a[h][i], 0.0), g_v[h][:hi]))
            o = jnp.concatenate(outs, axis=0)
        qs = g_qs[h][0] if bb == 1 else jnp.concatenate(g_qs[h], axis=0)
        g_o.append(o + qs)
        for b in range(bb):
            dcol = dtot[h * HD:(h + 1) * HD, :]
            if bb > 1:
                dcol = jnp.broadcast_to(dcol[:, b:b + 1], (HD, HD))
            ohg_ref[b, h] = ohg_ref[b, h] * jnp.exp(dcol) + g_kv[h][b]

    yield
    for h in range(H_RET):
        col = lambda off: slice(off + h * HD, off + (h + 1) * HD)
        mu = jnp.mean(r_o[h], axis=-1, keepdims=True)
        oc = r_o[h] - mu
        var = jnp.mean(oc * oc, axis=-1, keepdims=True)
        o = oc * lax.rsqrt(var + EPS) * retw_ref[:, col(0)] * _silu(proj_s[:, col(3 * hw)])
        mix_ref[:, :, col(0)] = o.reshape(bb, seq_len, HD).astype(mix_ref.dtype)
    for h in range(H_HG):
        hs = slice(h * HD, (h + 1) * HD)
        g = proj_s[:, base + 3 * hw + h * HD:base + 3 * hw + (h + 1) * HD]
        o = _rms(g_o[h], hgw_ref[:, hs]) * _silu(g)
        mix_ref[:, :, hw + h * HD:hw + (h + 1) * HD] = o.reshape(bb, seq_len, HD).astype(mix_ref.dtype)


PROJ_COLS = 512


def _project(x_ref, npre_ref, win_ref, proj_s):
    bb, seq_len, d = x_ref.shape
    hn = _rms(x_ref[...].reshape(bb * seq_len, d), npre_ref[...]).astype(BF16)
    n_in = win_ref.shape[1]
    for lo in range(0, n_in, PROJ_COLS):
        sl = slice(lo, min(lo + PROJ_COLS, n_in))
        proj_s[:, sl] = jnp.dot(hn, win_ref[:, sl], preferred_element_type=F32)


def _even_kernel(has_state, sub, x_ref, cos_ref, sin_ref, npre_ref, win_ref, retw_ref, hgw_ref,
                 lb_ref, pair_ref, to_row_ref, to_end_ref, *refs):
    if has_state:
        sret_ref, shg_ref, mix_ref, oret_ref, ohg_ref, proj_s = refs
    else:
        mix_ref, oret_ref, ohg_ref, proj_s = refs
    bb, seq_len, _ = x_ref.shape
    dec_refs = (pair_ref, to_row_ref, to_end_ref)

    @pl.when(pl.program_id(1) == 0)
    def _():
        if has_state:
            oret_ref[...] = sret_ref[...]
            ohg_ref[...] = shg_ref[...]
        else:
            oret_ref[...] = jnp.zeros(oret_ref.shape, F32)
            ohg_ref[...] = jnp.zeros(ohg_ref.shape, F32)

    _project(x_ref, npre_ref, win_ref, proj_s)
    consts = (lb_ref[...], retw_ref, hgw_ref, dec_refs)
    if seq_len > sub:
        assert bb == 1 and seq_len % sub == 0

        def chunk(j, carry):
            rs = pl.ds(pl.multiple_of(j * sub, sub), sub)
            _interleave([_even_block(1, sub, proj_s.at[rs], cos_ref[rs, :], sin_ref[rs, :], *consts,
                                     oret_ref, ohg_ref, mix_ref.at[:, rs])])
            return carry

        lax.fori_loop(0, seq_len // sub, chunk, 0)
    elif bb > 1 and seq_len > HG_BLK:
        units = []
        for u in range(bb):
            rs, one = pl.ds(u * seq_len, seq_len), pl.ds(u, 1)
            units.append(_even_block(1, seq_len, proj_s.at[rs], cos_ref[rs, :], sin_ref[rs, :], *consts,
                                     oret_ref.at[one], ohg_ref.at[one], mix_ref.at[one]))
        _interleave(units)
    else:
        _interleave([_even_block(bb, seq_len, proj_s, cos_ref[...], sin_ref[...], *consts,
                                 oret_ref, ohg_ref, mix_ref)])


def _even_call(x, cos, sin, npre, win, retw, hgw, lb, states, bb, c, sub):
    nb_total, t, d = x.shape
    has_state = states is not None
    n_in = win.shape[1]
    rows = bb * c
    grid = (nb_total // bb, t // c)
    st_spec = pl.BlockSpec((bb, H_RET, HD, HD), lambda b, i: (b, 0, 0, 0))
    in_specs = [pl.BlockSpec((bb, c, d), lambda b, i: (b, i, 0)),
                pl.BlockSpec((rows, HD), lambda b, i: (i, 0)),
                pl.BlockSpec((rows, HD), lambda b, i: (i, 0)),
                _const_spec((1, d)), _const_spec(win.shape),
                _const_spec((1, H_RET * HD)), _const_spec((1, H_HG * HD)), _const_spec((1, H_HG * HD))]
    tables = _ret_decay_tables(bb if sub <= HG_BLK else 1, sub)
    in_specs += [_const_spec(a.shape) for a in tables]
    args = [x, cos, sin, npre, win, retw, hgw, lb, *tables]
    if has_state:
        in_specs += [st_spec, st_spec]
        args += list(states)
    mix_w = (H_RET + H_HG) * HD
    return pl.pallas_call(
        functools.partial(_even_kernel, has_state, sub),
        grid=grid,
        in_specs=in_specs,
        out_specs=[pl.BlockSpec((bb, c, mix_w), lambda b, i: (b, i, 0)), st_spec, st_spec],
        out_shape=[jax.ShapeDtypeStruct((nb_total, t, mix_w), BF16),
                   jax.ShapeDtypeStruct((nb_total, H_RET, HD, HD), F32),
                   jax.ShapeDtypeStruct((nb_total, H_HG, HD, HD), F32)],
        scratch_shapes=[pltpu.VMEM((rows, n_in), F32)],
        compiler_params=pltpu.CompilerParams(dimension_semantics=("arbitrary", "arbitrary"),
                                             vmem_limit_bytes=VMEM_LIMIT),
        name="even_mixer",
    )(*args)


def _odd_block(has_state, bb, seq_len, proj_s, tail_s, ost_ref, sst_s, convw_ref, convb_ref, dtb_ref,
               alog_ref, dssm_ref, ssmw_ref, y_ref, u_ref):
    rows = bb * seq_len
    d_inner = H_SSD * P_SSD
    conv_dim = d_inner + 2 * G_SSD * N_SSD
    conv_w = convw_ref.shape[0]
    halo = conv_w - 1
    u_w = u_ref.shape[2]
    gw = d_inner // G_SSD
    heads_per_group = H_SSD // G_SSD
    pairs_per_group = heads_per_group // 2
    seqs = [slice(b * seq_len, (b + 1) * seq_len) for b in range(bb)]

    z = proj_s[:, 0:d_inner]
    xbc = proj_s[:, d_inner:d_inner + conv_dim]
    c0 = d_inner + conv_dim
    dt_raw = proj_s[:, c0:c0 + LANES]
    u_ref[...] = proj_s[:, c0 + H_SSD:c0 + H_SSD + u_w].reshape(bb, seq_len, u_w)

    tail = tail_s[...]
    conv = convb_ref[...] + xbc * convw_ref[halo:halo + 1, :]
    for s in range(1, conv_w):
        rolled = pltpu.roll(xbc, s, axis=0)
        if seq_len == SUBLANES:
            local = lax.broadcasted_iota(jnp.int32, (rows, conv_dim), 0) & (SUBLANES - 1)
            shifted = jnp.where(local < s, pltpu.roll(tail, rows - SUBLANES + s, axis=0), rolled)
        else:
            local = lax.broadcasted_iota(jnp.int32, (SUBLANES, conv_dim), 0)
            head = jnp.where(local < s, pltpu.roll(tail, s, axis=0), rolled[0:SUBLANES])
            shifted = jnp.concatenate([head, rolled[SUBLANES:]], axis=0)
        conv = conv + shifted * convw_ref[halo - s:halo - s + 1, :]
    tail_s[...] = xbc if seq_len == SUBLANES else xbc[rows - SUBLANES:rows]
    act = _silu(conv)
    xs = act[:, :d_inner]

    ti, si, same, causal = _seq_masks(rows, seq_len)
    lo_half = lax.broadcasted_iota(jnp.int32, (rows, LANES), 1) < P_SSD
    dt = jax.nn.softplus(dt_raw + dtb_ref[...])
    la_parts = _split3(dt * (-jnp.exp(alog_ref[...])))
    cum = _dot3(_ones_where(causal), la_parts)
    cum_t = _dot3_tn(la_parts, _ones_where(same & (ti <= si)))
    if bb > 1:
        rest = _dot3(_ones_where(same & (si > ti)), la_parts)
    bms = [act[:, d_inner + g * N_SSD:d_inner + (g + 1) * N_SSD] for g in range(G_SSD)]
    cms = [act[:, d_inner + (G_SSD + g) * N_SSD:d_inner + (G_SSD + g + 1) * N_SSD]
           for g in range(G_SSD)]
    scores = [_dot_nt(cms[g], bms[g]) for g in range(G_SSD)]
    yield

    bc = lambda a, h: jnp.broadcast_to(a[:, h:h + 1], (rows, LANES))
    bc_cum = [bc(cum, h) for h in range(H_SSD)]
    pair = lambda a, p: jnp.where(lo_half, bc(a, 2 * p), bc(a, 2 * p + 1))

    for g in range(G_SSD):
        xdts, xws, cums = [], [], []
        for pp in range(pairs_per_group):
            p = g * pairs_per_group + pp
            xdt = xs[:, p * LANES:(p + 1) * LANES] * pair(dt, p)
            cum_p = jnp.where(lo_half, bc_cum[2 * p], bc_cum[2 * p + 1])
            rest_p = cum_p[rows - 1:rows, :] - cum_p if bb == 1 else pair(rest, p)
            xdts.append(xdt)
            xws.append(xdt * jnp.exp(rest_p))
            cums.append(cum_p)
        xw = jnp.concatenate(xws, axis=1)
        if has_state:
            y_st = [_dot_nt(cms[g][s], ost_ref[b, g]) for b, s in enumerate(seqs)]
            kv = [_dot_tn(xw[s], bms[g][s]) for s in seqs]
        else:
            y_st = [_dot(cms[g][s], sst_s[b, g]) for b, s in enumerate(seqs)]
            kv = [_dot_tn(bms[g][s], xw[s]) for s in seqs]
        y_intra = []
        for pp in range(pairs_per_group):
            p = g * pairs_per_group + pp
            a_pair = []
            for h in (2 * p, 2 * p + 1):
                diff = bc_cum[h][:, :rows] - cum_t[h:h + 1, :]
                a_pair.append(scores[g] * jnp.where(causal, jnp.exp(jnp.where(causal, diff, 0.0)), 0.0))
            both = _dot(jnp.concatenate(a_pair, axis=0), xdts[pp])
            y_intra.append(jnp.where(lo_half, both[:rows], both[rows:]))
        yield
        for b in range(bb):
            last = (b + 1) * seq_len - 1
            if has_state:
                dec = [jnp.broadcast_to(jnp.exp(bc_cum[g * heads_per_group + hh][last:last + 1, :]),
                                        (P_SSD, N_SSD)) for hh in range(heads_per_group)]
                ost_ref[b, g] = ost_ref[b, g] * jnp.concatenate(dec, axis=0) + kv[b]
            else:
                dec = jnp.exp(jnp.concatenate([c_p[last:last + 1, :] for c_p in cums], axis=1))
                sst_s[b, g] = sst_s[b, g] * dec + kv[b]
        y_state = y_st[0] if bb == 1 else jnp.concatenate(y_st, axis=0)
        gs = slice(g * gw, (g + 1) * gw)
        y = (jnp.concatenate(y_intra, axis=1) + y_state * jnp.exp(jnp.concatenate(cums, axis=1))
             + dssm_ref[:, gs] * xs[:, gs])
        y = _rms(y * _silu(z[:, gs]), ssmw_ref[:, gs])
        y_ref[:, :, gs] = y.reshape(bb, seq_len, gw).astype(y_ref.dtype)


def _odd_kernel(has_state, sub, x_ref, npre_ref, win_ref, convw_ref, convb_ref, dtb_ref, alog_ref,
                dssm_ref, ssmw_ref, *refs):
    if has_state:
        sconv_ref, sst_ref, y_ref, u_ref, oconv_ref, ost_ref, proj_s, tail_s = refs
        sst_s = None
    else:
        y_ref, u_ref, oconv_ref, ost_ref, proj_s, tail_s, sst_s = refs
    bb, seq_len, _ = x_ref.shape
    halo = convw_ref.shape[0] - 1

    @pl.when(pl.program_id(1) == 0)
    def _():
        tail_s[...] = jnp.zeros(tail_s.shape, F32)
        if has_state:
            for b in range(bb):
                tail_s[(b + 1) * SUBLANES - halo:(b + 1) * SUBLANES, :] = sconv_ref[b]
            ost_ref[...] = sst_ref[...]
        else:
            sst_s[...] = jnp.zeros(sst_s.shape, F32)

    _project(x_ref, npre_ref, win_ref, proj_s)
    consts = (convw_ref, convb_ref, dtb_ref, alog_ref, dssm_ref, ssmw_ref)
    if seq_len > sub:
        assert bb == 1 and seq_len % sub == 0

        def chunk(j, carry):
            rs = pl.ds(pl.multiple_of(j * sub, sub), sub)
            _interleave([_odd_block(has_state, 1, sub, proj_s.at[rs], tail_s, ost_ref, sst_s, *consts,
                                    y_ref.at[:, rs], u_ref.at[:, rs])])
            return carry

        lax.fori_loop(0, seq_len // sub, chunk, 0)
    elif bb > 1 and seq_len > SUBLANES:
        units = []
        for u in range(bb):
            one = pl.ds(u, 1)
            units.append(_odd_block(
                has_state, 1, seq_len, proj_s.at[pl.ds(u * seq_len, seq_len)],
                tail_s.at[pl.ds(u * SUBLANES, SUBLANES)], ost_ref.at[one],
                None if sst_s is None else sst_s.at[one], *consts, y_ref.at[one], u_ref.at[one]))
        _interleave(units)
    else:
        _interleave([_odd_block(has_state, bb, seq_len, proj_s, tail_s, ost_ref, sst_s, *consts,
                                y_ref, u_ref)])

    @pl.when(pl.program_id(1) == pl.num_programs(1) - 1)
    def _():
        for b in range(bb):
            oconv_ref[b] = tail_s[(b + 1) * SUBLANES - halo:(b + 1) * SUBLANES, :]
            if not has_state:
                for g in range(G_SSD):
                    for ps in range(0, ost_ref.shape[2], LANES):
                        ost_ref[b, g, ps:ps + LANES, :] = sst_s[b, g, :, ps:ps + LANES].T


def _odd_call(x, npre, win, convw, convb, dtb, alog, dssm, ssmw, states, bb, c, sub):
    nb_total, t, d = x.shape
    has_state = states is not None
    n_in = win.shape[1]
    d_inner = H_SSD * P_SSD
    conv_dim = d_inner + 2 * G_SSD * N_SSD
    u_w = S5_G * S5_GS
    halo = convw.shape[0] - 1
    rows = bb * c
    grid = (nb_total // bb, t // c)
    st_shape = (G_SSD, d_inner // G_SSD, N_SSD)
    conv_spec = pl.BlockSpec((bb, halo, conv_dim), lambda b, i: (b, 0, 0))
    ssm_spec = pl.BlockSpec((bb,) + st_shape, lambda b, i: (b, 0, 0, 0))
    in_specs = [pl.BlockSpec((bb, c, d), lambda b, i: (b, i, 0)),
                _const_spec((1, d)), _const_spec(win.shape), _const_spec(convw.shape),
                _const_spec((1, conv_dim)), _const_spec((1, LANES)), _const_spec((1, LANES)),
                _const_spec((1, d_inner)), _const_spec((1, d_inner))]
    args = [x, npre, win, convw, convb, dtb, alog, dssm, ssmw]
    scratch = [pltpu.VMEM((rows, n_in), F32), pltpu.VMEM((bb * SUBLANES, conv_dim), F32)]
    if has_state:
        in_specs += [conv_spec, ssm_spec]
        args += list(states)
    else:
        scratch.append(pltpu.VMEM((bb, G_SSD, N_SSD, d_inner // G_SSD), F32))
    return pl.pallas_call(
        functools.partial(_odd_kernel, has_state, sub),
        grid=grid,
        in_specs=in_specs,
        out_specs=[pl.BlockSpec((bb, c, d_inner), lambda b, i: (b, i, 0)),
                   pl.BlockSpec((bb, c, u_w), lambda b, i: (b, i, 0)),
                   conv_spec, ssm_spec],
        out_shape=[jax.ShapeDtypeStruct((nb_total, t, d_inner), BF16),
                   jax.ShapeDtypeStruct((nb_total, t, u_w), F32),
                   jax.ShapeDtypeStruct((nb_total, halo, conv_dim), F32),
                   jax.ShapeDtypeStruct((nb_total,) + st_shape, F32)],
        scratch_shapes=scratch,
        compiler_params=pltpu.CompilerParams(dimension_semantics=("arbitrary", "arbitrary"),
                                             vmem_limit_bytes=VMEM_LIMIT),
        name="odd_mixer",
    )(*args)


S5_PARTS = 4
S5_PART_IN = S5_G * S5_GS // S5_PARTS
S5_PART_ST = S5_G * S5_P // S5_PARTS
S5_SCAN_VREGS = 4


def _s5_pitch(steps):
    return steps if steps % 16 == 8 else steps + 8


def _s5_kernel(u_ref, bmat_ref, cmat_ref, lre_ref, lim_ref, d_ref, wglu_ref, bglu_ref,
               h0re_ref, h0im_ref, yd_ref, ore_ref, oim_ref, buf_s, slab_s, ut_s):
    r, steps, uw = u_ref.shape
    hs = S5_PART_ST
    scan_w = min(hs, max(LANES, S5_SCAN_VREGS * SUBLANES * LANES // r))
    pitch = _s5_pitch(steps)
    nslab = uw // LANES

    @pl.when(pl.program_id(1) == 0)
    def _():
        ore_ref[...] = h0re_ref[...]
        oim_ref[...] = h0im_ref[...]

    for b in range(r):
        for j in range(nslab):
            slab_s[j, b * pitch:b * pitch + steps, :] = u_ref[b, :, j * LANES:(j + 1) * LANES]
    for t in range(steps):
        for j in range(nslab):
            ut_s[t * r:(t + 1) * r, j * LANES:(j + 1) * LANES] = slab_s[j, pl.ds(t, r, stride=pitch), :]
    u = ut_s[...]
    ub = u.astype(BF16)
    for half in range(S5_PARTS):
        buf_s[:, half * 2 * hs:(half + 1) * 2 * hs] = jnp.dot(
            ub[:, half * S5_PART_IN:(half + 1) * S5_PART_IN], bmat_ref[half],
            preferred_element_type=F32)

    for half in range(S5_PARTS):
        for j in range(hs // scan_w):
            st = slice(half * hs + j * scan_w, half * hs + (j + 1) * scan_w)
            cre = half * 2 * hs + j * scan_w
            cim = cre + hs
            lre = jnp.broadcast_to(lre_ref[:, st], (r, scan_w))
            lim = jnp.broadcast_to(lim_ref[:, st], (r, scan_w))
            hre, him = ore_ref[:, st], oim_ref[:, st]
            for t in range(steps):
                rws = slice(t * r, (t + 1) * r)
                hre, him = (lre * hre - lim * him + buf_s[rws, cre:cre + scan_w],
                            lre * him + lim * hre + buf_s[rws, cim:cim + scan_w])
                buf_s[rws, cre:cre + scan_w] = hre
                buf_s[rws, cim:cim + scan_w] = him
            ore_ref[:, st] = hre
            oim_ref[:, st] = him

    ys = []
    for half in range(S5_PARTS):
        ys.append(jnp.dot(buf_s[:, half * 2 * hs:(half + 1) * 2 * hs].astype(BF16), cmat_ref[half],
                          preferred_element_type=F32))
    y = jnp.concatenate(ys, axis=1) + d_ref[...] * u
    g = jax.nn.gelu(y)
    gate = jax.nn.sigmoid(jnp.dot(g.astype(BF16), wglu_ref[...], preferred_element_type=F32)
                          + bglu_ref[...])
    yd = g * gate
    for t in range(steps):
        for j in range(nslab):
            slab_s[j, pl.ds(t, r, stride=pitch), :] = yd[t * r:(t + 1) * r, j * LANES:(j + 1) * LANES]
    for b in range(r):
        for j in range(nslab):
            yd_ref[b, :, j * LANES:(j + 1) * LANES] = slab_s[j, b * pitch:b * pitch + steps, :].astype(
                yd_ref.dtype)


def _s5_call(u, bmat, cmat, lre, lim, dvec, wglu, bglu, h0re, h0im, steps, r):
    nb_total, t, uw = u.shape
    ns = S5_G * S5_P
    grid = (nb_total // r, t // steps)
    st_spec = pl.BlockSpec((r, ns), lambda b, i: (b, 0))
    return pl.pallas_call(
        _s5_kernel,
        grid=grid,
        in_specs=[pl.BlockSpec((r, steps, uw), lambda b, i: (b, i, 0)),
                  _const_spec(bmat.shape), _const_spec(cmat.shape),
                  _const_spec((1, ns)), _const_spec((1, ns)), _const_spec((1, uw)),
                  _const_spec(wglu.shape), _const_spec((1, uw)), st_spec, st_spec],
        out_specs=[pl.BlockSpec((r, steps, uw), lambda b, i: (b, i, 0)), st_spec, st_spec],
        out_shape=[jax.ShapeDtypeStruct((nb_total, t, uw), BF16),
                   jax.ShapeDtypeStruct((nb_total, ns), F32),
                   jax.ShapeDtypeStruct((nb_total, ns), F32)],
        scratch_shapes=[pltpu.VMEM((steps * r, 2 * ns), F32),
                        pltpu.VMEM((uw // LANES, r * _s5_pitch(steps), LANES), F32),
                        pltpu.VMEM((steps * r, uw), F32)],
        compiler_params=pltpu.CompilerParams(dimension_semantics=("arbitrary", "arbitrary"),
                                             vmem_limit_bytes=VMEM_LIMIT),
        name="s5_mixer",
    )(u, bmat, cmat, lre, lim, dvec, wglu, bglu, h0re, h0im)


def _rope_tables(pos):
    half = HD // 2
    inv_freq = ROPE_BASE ** (-jnp.arange(half, dtype=F32) / half)
    ang = pos[:, None] * inv_freq[None, :]
    cos, sin = jnp.cos(ang), jnp.sin(ang)
    return jnp.concatenate([cos, cos], axis=-1), jnp.concatenate([-sin, sin], axis=-1)


def _pad_lanes(v):
    return jnp.pad(v.astype(F32), (0, LANES - v.shape[0]))[None, :]


def _s5_tables(lam_re, lam_im, log_step, b_re, b_im, c_re, c_im):
    lr, li = lam_re.astype(F32), lam_im.astype(F32)
    dt = jnp.exp(log_step.astype(F32))[:, None]
    mag = jnp.exp(lr * dt)
    bar_re, bar_im = mag * jnp.cos(li * dt), mag * jnp.sin(li * dt)
    den = lr * lr + li * li
    cf_re = ((bar_re - 1.0) * lr + bar_im * li) / den
    cf_im = (bar_im * lr - (bar_re - 1.0) * li) / den
    bb_re = cf_re[..., None] * b_re.astype(F32) - cf_im[..., None] * b_im.astype(F32)
    bb_im = cf_re[..., None] * b_im.astype(F32) + cf_im[..., None] * b_re.astype(F32)
    gh = S5_G // S5_PARTS
    eye = jnp.eye(gh, dtype=BF16)
    b_ri = jnp.stack([bb_re, bb_im]).astype(BF16).reshape(2, S5_PARTS, gh, S5_P, S5_GS)
    bmat = jnp.einsum('rqgpc,gk->qgcrkp', b_ri, eye).reshape(S5_PARTS, gh * S5_GS, 2 * gh * S5_P)
    c_ri = jnp.stack([c_re.astype(F32), -c_im.astype(F32)]).astype(BF16).reshape(
        2, S5_PARTS, gh, S5_GS, S5_P)
    cmat = jnp.einsum('rqgcp,gk->qrgpkc', c_ri, eye).reshape(S5_PARTS, 2 * gh * S5_P, gh * S5_GS)
    return bmat, cmat, bar_re.reshape(1, -1), bar_im.reshape(1, -1)


def _row(v):
    return v.astype(F32).reshape(1, -1)


def _forward(x, pos_offset, states, p, cfg):
    nb, t, d = x.shape
    n = nb * t
    bb, c = cfg['bb'], cfg['c']
    has_state = states is not None
    pos = jnp.arange(pos_offset, pos_offset + t, dtype=F32)
    cos, sin = _rope_tables(pos)
    tile_rows = lambda a: jnp.tile(a.reshape(t // c, 1, c, HD), (1, bb, 1, 1)).reshape(-1, HD)
    lb = jnp.cumsum(jax.nn.softmax(p['hgrn_lower_bounds'].astype(F32), axis=0), axis=0)[0]
    mix, ret_o, hg_o = _even_call(
        x, tile_rows(cos), tile_rows(sin), _row(p['norm_mix_pre'][0]), p['w_in_even'][0].astype(BF16),
        _row(p['ret_norm_w'][0]), _row(p['hgrn_norm_w'][0]), _row(lb),
        (states['ret'][0], states['hgrn'][0]) if has_state else None, bb, c, cfg['sub'])
    stacked_rows = lambda a: a.astype(F32)[:, None, :]
    ffn = (stacked_rows(p['norm_mix_post']), stacked_rows(p['norm_ffn_pre']),
           stacked_rows(p['norm_ffn_post']), p['w_ffn_up'].astype(BF16), p['w_ffn_down'].astype(BF16))
    h = _post_call(x.reshape(n, d), [mix.reshape(n, -1)], p['w_out_even'][0].astype(BF16), *ffn, 0,
                   cfg['rows'])
    d_inner = H_SSD * P_SSD
    conv_dim = d_inner + 2 * G_SSD * N_SSD
    u_w = S5_G * S5_GS
    w1 = p['w_in_odd'][0].astype(BF16)
    st_shape = (nb, G_SSD, d_inner // G_SSD, N_SSD)
    odd_states = None
    if has_state:
        odd_states = (states['conv'][0], jnp.swapaxes(states['ssm'][0], -1, -2).reshape(st_shape))
    y, u, conv_o, sst_o = _odd_call(
        h.reshape(nb, t, d), _row(p['norm_mix_pre'][1]), w1, p['conv_w'][0].astype(F32),
        _row(p['conv_b'][0]), _pad_lanes(p['dt_bias'][0]), _pad_lanes(p['a_log'][0]),
        _row(jnp.repeat(p['d_ssm'][0], P_SSD)), _row(p['ssm_norm_w'][0]), odd_states, bb, c, cfg['sub'])
    ssm_o = jnp.swapaxes(sst_o.reshape(nb, H_SSD, P_SSD, N_SSD), -1, -2)
    bmat, cmat, lre, lim = _s5_tables(p['s5_lam_re'][0], p['s5_lam_im'][0], p['s5_log_step'][0],
                                      p['s5_b_re'][0], p['s5_b_im'][0], p['s5_c_re'][0],
                                      p['s5_c_im'][0])
    ns = S5_G * S5_P
    if has_state:
        h0re = states['s5_re'][0].reshape(nb, ns)
        h0im = states['s5_im'][0].reshape(nb, ns)
    else:
        h0re = jnp.zeros((nb, ns), F32)
        h0im = jnp.zeros((nb, ns), F32)
    yd, re_o, im_o = _s5_call(u, bmat, cmat, lre, lim, _row(p['s5_d'][0]),
                              p['w_glu'][0].astype(BF16), _row(p['b_glu'][0]), h0re, h0im,
                              cfg['s5_steps'], cfg['s5_rows'])
    yd = yd.reshape(n, u_w)
    h = _post_call(h, [y.reshape(n, d_inner), yd], p['w_out_odd'][0].astype(BF16), *ffn, 1, cfg['rows'])
    return (h.reshape(nb, t, d), ret_o[None], hg_o[None], ssm_o[None], conv_o[None],
            re_o.reshape(1, nb, S5_G, S5_P), im_o.reshape(1, nb, S5_G, S5_P))


PROMPT_CFG = dict(bb=4, c=128, sub=128, rows=1024, s5_steps=64, s5_rows=8)
SAMPLE_CFG = dict(bb=8, c=8, sub=8, rows=1024, s5_steps=8, s5_rows=32)


def kernel(x_prompt, x_sample, state_ret, state_hgrn, state_ssm, state_conv, state_s5_re, state_s5_im,
           norm_mix_pre, norm_mix_post, norm_ffn_pre, norm_ffn_post, w_in_even, w_out_even, ret_norm_w,
           hgrn_lower_bounds, hgrn_norm_w, w_in_odd, conv_w, conv_b, dt_bias, a_log, d_ssm, ssm_norm_w,
           s5_lam_re, s5_lam_im, s5_log_step, s5_b_re, s5_b_im, s5_c_re, s5_c_im, s5_d, w_glu, b_glu,
           w_out_odd, w_ffn_up, w_ffn_down):
    p = dict(norm_mix_pre=norm_mix_pre, norm_mix_post=norm_mix_post, norm_ffn_pre=norm_ffn_pre,
             norm_ffn_post=norm_ffn_post, w_in_even=w_in_even, w_out_even=w_out_even, ret_norm_w=ret_norm_w,
             hgrn_lower_bounds=hgrn_lower_bounds, hgrn_norm_w=hgrn_norm_w, w_in_odd=w_in_odd, conv_w=conv_w,
             conv_b=conv_b, dt_bias=dt_bias, a_log=a_log, d_ssm=d_ssm, ssm_norm_w=ssm_norm_w,
             s5_lam_re=s5_lam_re, s5_lam_im=s5_lam_im, s5_log_step=s5_log_step, s5_b_re=s5_b_re,
             s5_b_im=s5_b_im, s5_c_re=s5_c_re, s5_c_im=s5_c_im, s5_d=s5_d, w_glu=w_glu, b_glu=b_glu,
             w_out_odd=w_out_odd, w_ffn_up=w_ffn_up, w_ffn_down=w_ffn_down)
    past_len = 16384
    states = dict(ret=state_ret, hgrn=state_hgrn, ssm=state_ssm, conv=state_conv,
                  s5_re=state_s5_re, s5_im=state_s5_im)
    y_p, ret_p, hg_p, ssm_p, conv_p, re_p, im_p = _forward(x_prompt, 0, None, p, PROMPT_CFG)
    y_s, ret_s, hg_s, ssm_s, conv_s, re_s, im_s = _forward(x_sample, past_len, states, p, SAMPLE_CFG)
    return (y_p, y_s, ret_p, ret_s, hg_p, hg_s, ssm_p, ssm_s, conv_p, conv_s,
            re_p, re_s, im_p, im_s)
```

```python
import functools
import math

import jax
import jax.numpy as jnp
import numpy as np
from jax import lax
from jax.experimental import pallas as pl
from jax.experimental.pallas import tpu as pltpu

F32 = jnp.float32
BF16 = jnp.bfloat16
EPS = 1e-6
ROPE_BASE = 10000.0
LANES = 128
SUBLANES = 8
VMEM_LIMIT = 56 * 1024 * 1024

H_RET = 4
H_HG = 4
HD = 128
H_SSD = 16
P_SSD = 64
N_SSD = 128
G_SSD = 2
S5_G = 32
S5_GS = 16
S5_P = 64
HG_BLK = 16


def _rms(x, w):
    return x * lax.rsqrt(jnp.mean(x * x, axis=-1, keepdims=True) + EPS) * w


def _silu(x):
    return x * jax.nn.sigmoid(x)


def _dot(a, b):
    return jnp.dot(a.astype(BF16), b.astype(BF16), preferred_element_type=F32)


def _dot_nt(a, b):
    return lax.dot_general(a.astype(BF16), b.astype(BF16), (((1,), (1,)), ((), ())),
                           preferred_element_type=F32)


def _dot_tn(a, b):
    return lax.dot_general(a.astype(BF16), b.astype(BF16), (((0,), (0,)), ((), ())),
                           preferred_element_type=F32)


def _split3(x):
    hi = x.astype(BF16)
    r1 = x - hi.astype(F32)
    mid = r1.astype(BF16)
    lo = (r1 - mid.astype(F32)).astype(BF16)
    return hi, mid, lo


def _dot3(a_exact, parts):
    acc = None
    for p in parts:
        d = jnp.dot(a_exact, p, preferred_element_type=F32)
        acc = d if acc is None else acc + d
    return acc


def _dot3_tn(parts, b_exact):
    acc = None
    for p in parts:
        d = lax.dot_general(p, b_exact, (((0,), (0,)), ((), ())), preferred_element_type=F32)
        acc = d if acc is None else acc + d
    return acc


def _ones_where(mask):
    return jnp.where(mask, 1.0, 0.0).astype(BF16)


def _seq_masks(rows, seq_len):
    sh = seq_len.bit_length() - 1
    ti = lax.broadcasted_iota(jnp.int32, (rows, rows), 0)
    si = lax.broadcasted_iota(jnp.int32, (rows, rows), 1)
    same = lax.shift_right_logical(ti, sh) == lax.shift_right_logical(si, sh)
    return ti, si, same, same & (ti >= si)


def _interleave(units):
    live = list(units)
    while live:
        still = []
        for g in live:
            try:
                next(g)
                still.append(g)
            except StopIteration:
                pass
        live = still


def _const_spec(shape):
    nd = len(shape)
    return pl.BlockSpec(shape, lambda *_: (0,) * nd, pipeline_mode=pl.Buffered(1))


def _post_kernel(n_mix, ff_chunk, n_parts, h_ref, *refs):
    mix_refs = refs[:n_mix]
    wout_ref, npost_ref, nfpre_ref, nfpost_ref, wup_ref, wdn_ref, o_ref = refs[n_mix:]
    part = h_ref.shape[0] // n_parts
    groups = [slice(i * part, (i + 1) * part) for i in range(n_parts)]
    accs = []
    for rs in groups:
        acc, off = None, 0
        for m_ref in mix_refs:
            w = m_ref.shape[1]
            d = jnp.dot(m_ref[rs, :], wout_ref[off:off + w, :], preferred_element_type=F32)
            acc = d if acc is None else acc + d
            off += w
        accs.append(acc)
    for rs, acc in zip(groups, accs):
        h1 = h_ref[rs, :] + _rms(acc, npost_ref[...])
        hn = _rms(h1, nfpre_ref[...]).astype(BF16)
        ff = None
        for j in range(wup_ref.shape[1] // ff_chunk):
            sl = slice(j * ff_chunk, (j + 1) * ff_chunk)
            up = jnp.dot(hn, wup_ref[:, sl], preferred_element_type=F32)
            act = jnp.square(jnp.maximum(up, 0.0)).astype(BF16)
            d = jnp.dot(act, wdn_ref[sl, :], preferred_element_type=F32)
            ff = d if ff is None else ff + d
        o_ref[rs, :] = h1 + _rms(ff, nfpost_ref[...])


def _layer_spec(shape, layer):
    nd = len(shape) - 1
    return pl.BlockSpec((None,) + tuple(shape[1:]), lambda *_: (layer,) + (0,) * nd,
                        pipeline_mode=pl.Buffered(1))


def _post_call(h, mixes, wout, npost, nfpre, nfpost, wup, wdn, layer, rows):
    n, d = h.shape
    n_mix = len(mixes)
    row_spec = lambda w: pl.BlockSpec((rows, w), lambda i: (i, 0))
    in_specs = ([row_spec(d)] + [row_spec(m.shape[1]) for m in mixes] + [_const_spec(wout.shape)]
                + [_layer_spec(a.shape, layer) for a in (npost, nfpre, nfpost, wup, wdn)])
    return pl.pallas_call(
        functools.partial(_post_kernel, n_mix, 1024, 2),
        grid=(n // rows,),
        in_specs=in_specs,
        out_specs=row_spec(d),
        out_shape=jax.ShapeDtypeStruct((n, d), F32),
        compiler_params=pltpu.CompilerParams(dimension_semantics=("arbitrary",),
                                             vmem_limit_bytes=VMEM_LIMIT),
        name="post_ffn",
    )(h, *mixes, wout, npost, nfpre, nfpost, wup, wdn)


def _ret_decay_tables(bb, seq_len):
    rows = bb * seq_len
    t = np.arange(rows)
    tl = t % seq_len
    valid = (t[:, None] // seq_len == t[None, :] // seq_len) & (t[:, None] >= t[None, :])
    pair, to_row, to_end = [], [], []
    for h in range(H_RET):
        lg = math.log1p(-(2.0 ** (-5.0 - h)))
        pair.append(np.where(valid, np.exp((t[:, None] - t[None, :]) * lg), 0.0))
        to_row.append(np.broadcast_to(np.exp((tl + 1.0) * lg)[:, None], (rows, HD)))
        to_end.append(np.broadcast_to(np.exp((seq_len - 1.0 - tl) * lg)[:, None], (rows, HD)))
    return tuple(jnp.asarray(np.stack(a), F32) for a in (pair, to_row, to_end))


def _even_block(bb, seq_len, proj_s, cos, sin, lb, retw_ref, hgw_ref, dec_refs, oret_ref, ohg_ref,
                mix_ref):
    rows = bb * seq_len
    hw = H_RET * HD
    base = 4 * hw
    sh = seq_len.bit_length() - 1
    ti, si, same, causal = _seq_masks(rows, seq_len)
    pair_ref, to_row_ref, to_end_ref = dec_refs
    seqs = [slice(b * seq_len, (b + 1) * seq_len) for b in range(bb)]
    one_shot = seq_len <= HG_BLK

    f = lb + (1.0 - lb) * jax.nn.sigmoid(proj_s[:, base + hw:base + 2 * hw])
    lf_parts = _split3(jnp.log(f))
    cum = _dot3(_ones_where(causal), lf_parts)
    if bb == 1:
        dtot = _dot3_tn(lf_parts, jnp.ones((rows, HD), BF16))
        cl = cum[rows - 1:rows, :]
    else:
        r_seq = lax.shift_right_logical(lax.broadcasted_iota(jnp.int32, (rows, HD), 0), sh)
        ind = _ones_where(r_seq == lax.broadcasted_iota(jnp.int32, (rows, HD), 1))
        dtot = _dot3_tn(lf_parts, ind)
        cl = _dot3(_ones_where(same), lf_parts)
    if one_shot:
        refc = _dot3(_ones_where(same & ((si & (seq_len - 1)) < seq_len // 2)), lf_parts)

    yield
    r_a, r_qs, r_kv, r_v = [], [], [], []
    for h in range(H_RET):
        col = lambda off: slice(off + h * HD, off + (h + 1) * HD)
        q = proj_s[:, col(0)]
        k = proj_s[:, col(hw)]
        v = proj_s[:, col(2 * hw)]
        q = q * cos + pltpu.roll(q, HD // 2, axis=1) * sin
        k = (k * cos + pltpu.roll(k, HD // 2, axis=1) * sin) * (HD ** -0.5)
        kd = k * to_end_ref[h]
        r_a.append(_dot_nt(q, k))
        r_qs.append([_dot(q[s], oret_ref[b, h]) for b, s in enumerate(seqs)])
        r_kv.append([_dot_tn(kd[s], v[s]) for s in seqs])
        r_v.append(v)

    g_a, g_qs, g_kv, g_v = [], [], [], []
    for h in range(H_HG):
        hs = slice(h * HD, (h + 1) * HD)
        col = lambda off: slice(base + off + h * HD, base + off + (h + 1) * HD)
        q = proj_s[:, col(0)]
        v = proj_s[:, col(2 * hw)]
        k = 1.0 - f[:, hs]
        cum_h = cum[:, hs]
        qe = q * jnp.exp(cum_h)
        ke = k * jnp.exp(cl[:, hs] - cum_h)
        g_qs.append([_dot(qe[s], ohg_ref[b, h]) for b, s in enumerate(seqs)])
        g_kv.append([_dot_tn(ke[s], v[s]) for s in seqs])
        if one_shot:
            ref = refc[:, hs]
            g_a.append([_dot_nt(q * jnp.exp(cum_h - ref), k * jnp.exp(ref - cum_h))])
        else:
            blocks = []
            for i in range(rows // HG_BLK):
                lo, hi = i * HG_BLK, (i + 1) * HG_BLK
                ref = cum_h[lo + HG_BLK // 2 - 1:lo + HG_BLK // 2, :]
                blocks.append(_dot_nt(q[lo:hi] * jnp.exp(cum_h[lo:hi] - ref),
                                      k[:hi] * jnp.exp(ref - cum_h[:hi])))
            g_a.append(blocks)
        g_v.append(v)

    yield
    r_o = []
    for h in range(H_RET):
        lg = math.log1p(-(2.0 ** (-5.0 - h)))
        qs = r_qs[h][0] if bb == 1 else jnp.concatenate(r_qs[h], axis=0)
        r_o.append(_dot(r_a[h] * pair_ref[h], r_v[h]) + qs * to_row_ref[h])
        for b in range(bb):
            oret_ref[b, h] = oret_ref[b, h] * math.exp(seq_len * lg) + r_kv[h][b]

    g_o = []
    for h in range(H_HG):
        if one_shot:
            o = _dot(jnp.where(causal, g_a[h][0], 0.0), g_v[h])
        else:
            outs = []
            for i in range(rows // HG_BLK):
                lo, hi = i * HG_BLK, (i + 1) * HG_BLK
                tb = lax.broadcasted_iota(jnp.int32, (HG_BLK, hi), 0) + lo
                sb = lax.broadcasted_iota(jnp.int32, (HG_BLK, hi), 1)
                outs.append(_dot(jnp.where(tb >= sb, g_a[h][i], 0.0), g_v[h][:hi]))
            o = jnp.concatenate(outs, axis=0)
        qs = g_qs[h][0] if bb == 1 else jnp.concatenate(g_qs[h], axis=0)
        g_o.append(o + qs)
        for b in range(bb):
            dcol = dtot[h * HD:(h + 1) * HD, :]
            if bb > 1:
                dcol = jnp.broadcast_to(dcol[:, b:b + 1], (HD, HD))
            ohg_ref[b, h] = ohg_ref[b, h] * jnp.exp(dcol) + g_kv[h][b]

    yield
    for h in range(H_RET):
        col = lambda off: slice(off + h * HD, off + (h + 1) * HD)
        mu = jnp.mean(r_o[h], axis=-1, keepdims=True)
        oc = r_o[h] - mu
        var = jnp.mean(oc * oc, axis=-1, keepdims=True)
        o = oc * lax.rsqrt(var + EPS) * retw_ref[:, col(0)] * _silu(proj_s[:, col(3 * hw)])
        mix_ref[:, :, col(0)] = o.reshape(bb, seq_len, HD).astype(mix_ref.dtype)
    for h in range(H_HG):
        hs = slice(h * HD, (h + 1) * HD)
        g = proj_s[:, base + 3 * hw + h * HD:base + 3 * hw + (h + 1) * HD]
        o = _rms(g_o[h], hgw_ref[:, hs]) * _silu(g)
        mix_ref[:, :, hw + h * HD:hw + (h + 1) * HD] = o.reshape(bb, seq_len, HD).astype(mix_ref.dtype)


PROJ_COLS = 512


def _project(x_ref, npre_ref, win_ref, proj_s):
    bb, seq_len, d = x_ref.shape
    hn = _rms(x_ref[...].reshape(bb * seq_len, d), npre_ref[...]).astype(BF16)
    n_in = win_ref.shape[1]
    for lo in range(0, n_in, PROJ_COLS):
        sl = slice(lo, min(lo + PROJ_COLS, n_in))
        proj_s[:, sl] = jnp.dot(hn, win_ref[:, sl], preferred_element_type=F32)


def _even_kernel(has_state, sub, x_ref, cos_ref, sin_ref, npre_ref, win_ref, retw_ref, hgw_ref,
                 lb_ref, pair_ref, to_row_ref, to_end_ref, *refs):
    if has_state:
        sret_ref, shg_ref, mix_ref, oret_ref, ohg_ref, proj_s = refs
    else:
        mix_ref, oret_ref, ohg_ref, proj_s = refs
    bb, seq_len, _ = x_ref.shape
    dec_refs = (pair_ref, to_row_ref, to_end_ref)

    @pl.when(pl.program_id(1) == 0)
    def _():
        if has_state:
            oret_ref[...] = sret_ref[...]
            ohg_ref[...] = shg_ref[...]
        else:
            oret_ref[...] = jnp.zeros(oret_ref.shape, F32)
            ohg_ref[...] = jnp.zeros(ohg_ref.shape, F32)

    _project(x_ref, npre_ref, win_ref, proj_s)
    consts = (lb_ref[...], retw_ref, hgw_ref, dec_refs)
    if seq_len > sub:
        assert bb == 1 and seq_len % sub == 0

        def chunk(j, carry):
            rs = pl.ds(pl.multiple_of(j * sub, sub), sub)
            _interleave([_even_block(1, sub, proj_s.at[rs], cos_ref[rs, :], sin_ref[rs, :], *consts,
                                     oret_ref, ohg_ref, mix_ref.at[:, rs])])
            return carry

        lax.fori_loop(0, seq_len // sub, chunk, 0)
    elif bb > 1 and seq_len > HG_BLK:
        units = []
        for u in range(bb):
            rs, one = pl.ds(u * seq_len, seq_len), pl.ds(u, 1)
            units.append(_even_block(1, seq_len, proj_s.at[rs], cos_ref[rs, :], sin_ref[rs, :], *consts,
                                     oret_ref.at[one], ohg_ref.at[one], mix_ref.at[one]))
        _interleave(units)
    else:
        _interleave([_even_block(bb, seq_len, proj_s, cos_ref[...], sin_ref[...], *consts,
                                 oret_ref, ohg_ref, mix_ref)])


def _even_call(x, cos, sin, npre, win, retw, hgw, lb, states, bb, c, sub):
    nb_total, t, d = x.shape
    has_state = states is not None
    n_in = win.shape[1]
    rows = bb * c
    grid = (nb_total // bb, t // c)
    st_spec = pl.BlockSpec((bb, H_RET, HD, HD), lambda b, i: (b, 0, 0, 0))
    in_specs = [pl.BlockSpec((bb, c, d), lambda b, i: (b, i, 0)),
                pl.BlockSpec((rows, HD), lambda b, i: (i, 0)),
                pl.BlockSpec((rows, HD), lambda b, i: (i, 0)),
                _const_spec((1, d)), _const_spec(win.shape),
                _const_spec((1, H_RET * HD)), _const_spec((1, H_HG * HD)), _const_spec((1, H_HG * HD))]
    tables = _ret_decay_tables(bb if sub <= HG_BLK else 1, sub)
    in_specs += [_const_spec(a.shape) for a in tables]
    args = [x, cos, sin, npre, win, retw, hgw, lb, *tables]
    if has_state:
        in_specs += [st_spec, st_spec]
        args += list(states)
    mix_w = (H_RET + H_HG) * HD
    return pl.pallas_call(
        functools.partial(_even_kernel, has_state, sub),
        grid=grid,
        in_specs=in_specs,
        out_specs=[pl.BlockSpec((bb, c, mix_w), lambda b, i: (b, i, 0)), st_spec, st_spec],
        out_shape=[jax.ShapeDtypeStruct((nb_total, t, mix_w), BF16),
                   jax.ShapeDtypeStruct((nb_total, H_RET, HD, HD), F32),
                   jax.ShapeDtypeStruct((nb_total, H_HG, HD, HD), F32)],
        scratch_shapes=[pltpu.VMEM((rows, n_in), F32)],
        compiler_params=pltpu.CompilerParams(dimension_semantics=("arbitrary", "arbitrary"),
                                             vmem_limit_bytes=VMEM_LIMIT),
        name="even_mixer",
    )(*args)


def _odd_block(has_state, bb, seq_len, proj_s, tail_s, ost_ref, sst_s, convw_ref, convb_ref, dtb_ref,
               alog_ref, dssm_ref, ssmw_ref, y_ref, u_ref):
    rows = bb * seq_len
    d_inner = H_SSD * P_SSD
    conv_dim = d_inner + 2 * G_SSD * N_SSD
    conv_w = convw_ref.shape[0]
    halo = conv_w - 1
    u_w = u_ref.shape[2]
    gw = d_inner // G_SSD
    heads_per_group = H_SSD // G_SSD
    pairs_per_group = heads_per_group // 2
    seqs = [slice(b * seq_len, (b + 1) * seq_len) for b in range(bb)]

    z = proj_s[:, 0:d_inner]
    xbc = proj_s[:, d_inner:d_inner + conv_dim]
    c0 = d_inner + conv_dim
    dt_raw = proj_s[:, c0:c0 + LANES]
    u_ref[...] = proj_s[:, c0 + H_SSD:c0 + H_SSD + u_w].reshape(bb, seq_len, u_w)

    tail = tail_s[...]
    conv = convb_ref[...] + xbc * convw_ref[halo:halo + 1, :]
    for s in range(1, conv_w):
        rolled = pltpu.roll(xbc, s, axis=0)
        if seq_len == SUBLANES:
            local = lax.broadcasted_iota(jnp.int32, (rows, conv_dim), 0) & (SUBLANES - 1)
            shifted = jnp.where(local < s, pltpu.roll(tail, rows - SUBLANES + s, axis=0), rolled)
        else:
            local = lax.broadcasted_iota(jnp.int32, (SUBLANES, conv_dim), 0)
            head = jnp.where(local < s, pltpu.roll(tail, s, axis=0), rolled[0:SUBLANES])
            shifted = jnp.concatenate([head, rolled[SUBLANES:]], axis=0)
        conv = conv + shifted * convw_ref[halo - s:halo - s + 1, :]
    tail_s[...] = xbc if seq_len == SUBLANES else xbc[rows - SUBLANES:rows]
    act = _silu(conv)
    xs = act[:, :d_inner]

    ti, si, same, causal = _seq_masks(rows, seq_len)
    lo_half = lax.broadcasted_iota(jnp.int32, (rows, LANES), 1) < P_SSD
    dt = jax.nn.softplus(dt_raw + dtb_ref[...])
    la_parts = _split3(dt * (-jnp.exp(alog_ref[...])))
    cum = _dot3(_ones_where(causal), la_parts)
    cum_t = _dot3_tn(la_parts, _ones_where(same & (ti <= si)))
    if bb > 1:
        rest = _dot3(_ones_where(same & (si > ti)), la_parts)
    bms = [act[:, d_inner + g * N_SSD:d_inner + (g + 1) * N_SSD] for g in range(G_SSD)]
    cms = [act[:, d_inner + (G_SSD + g) * N_SSD:d_inner + (G_SSD + g + 1) * N_SSD]
           for g in range(G_SSD)]
    scores = [_dot_nt(cms[g], bms[g]) for g in range(G_SSD)]
    yield

    bc = lambda a, h: jnp.broadcast_to(a[:, h:h + 1], (rows, LANES))
    bc_cum = [bc(cum, h) for h in range(H_SSD)]
    pair = lambda a, p: jnp.where(lo_half, bc(a, 2 * p), bc(a, 2 * p + 1))

    for g in range(G_SSD):
        xdts, xws, cums = [], [], []
        for pp in range(pairs_per_group):
            p = g * pairs_per_group + pp
            xdt = xs[:, p * LANES:(p + 1) * LANES] * pair(dt, p)
            cum_p = jnp.where(lo_half, bc_cum[2 * p], bc_cum[2 * p + 1])
            rest_p = cum_p[rows - 1:rows, :] - cum_p if bb == 1 else pair(rest, p)
            xdts.append(xdt)
            xws.append(xdt * jnp.exp(rest_p))
            cums.append(cum_p)
        xw = jnp.concatenate(xws, axis=1)
        if has_state:
            y_st = [_dot_nt(cms[g][s], ost_ref[b, g]) for b, s in enumerate(seqs)]
            kv = [_dot_tn(xw[s], bms[g][s]) for s in seqs]
        else:
            y_st = [_dot(cms[g][s], sst_s[b, g]) for b, s in enumerate(seqs)]
            kv = [_dot_tn(bms[g][s], xw[s]) for s in seqs]
        y_intra = []
        for pp in range(pairs_per_group):
            p = g * pairs_per_group + pp
            a_pair = []
            for h in (2 * p, 2 * p + 1):
                diff = bc_cum[h][:, :rows] - cum_t[h:h + 1, :]
                a_pair.append(scores[g] * jnp.where(causal, jnp.exp(jnp.where(causal, diff, 0.0)), 0.0))
            both = _dot(jnp.concatenate(a_pair, axis=0), xdts[pp])
            y_intra.append(jnp.where(lo_half, both[:rows], both[rows:]))
        yield
        for b in range(bb):
            last = (b + 1) * seq_len - 1
            if has_state:
                dec = [jnp.broadcast_to(jnp.exp(bc_cum[g * heads_per_group + hh][last:last + 1, :]),
                                        (P_SSD, N_SSD)) for hh in range(heads_per_group)]
                ost_ref[b, g] = ost_ref[b, g] * jnp.concatenate(dec, axis=0) + kv[b]
            else:
                dec = jnp.exp(jnp.concatenate([c_p[last:last + 1, :] for c_p in cums], axis=1))
                sst_s[b, g] = sst_s[b, g] * dec + kv[b]
        y_state = y_st[0] if bb == 1 else jnp.concatenate(y_st, axis=0)
        gs = slice(g * gw, (g + 1) * gw)
        y = (jnp.concatenate(y_intra, axis=1) + y_state * jnp.exp(jnp.concatenate(cums, axis=1))
             + dssm_ref[:, gs] * xs[:, gs])
        y = _rms(y * _silu(z[:, gs]), ssmw_ref[:, gs])
        y_ref[:, :, gs] = y.reshape(bb, seq_len, gw).astype(y_ref.dtype)


def _odd_kernel(has_state, sub, x_ref, npre_ref, win_ref, convw_ref, convb_ref, dtb_ref, alog_ref,
                dssm_ref, ssmw_ref, *refs):
    if has_state:
        sconv_ref, sst_ref, y_ref, u_ref, oconv_ref, ost_ref, proj_s, tail_s = refs
        sst_s = None
    else:
        y_ref, u_ref, oconv_ref, ost_ref, proj_s, tail_s, sst_s = refs
    bb, seq_len, _ = x_ref.shape
    halo = convw_ref.shape[0] - 1

    @pl.when(pl.program_id(1) == 0)
    def _():
        tail_s[...] = jnp.zeros(tail_s.shape, F32)
        if has_state:
            for b in range(bb):
                tail_s[(b + 1) * SUBLANES - halo:(b + 1) * SUBLANES, :] = sconv_ref[b]
            ost_ref[...] = sst_ref[...]
        else:
            sst_s[...] = jnp.zeros(sst_s.shape, F32)

    _project(x_ref, npre_ref, win_ref, proj_s)
    consts = (convw_ref, convb_ref, dtb_ref, alog_ref, dssm_ref, ssmw_ref)
    if seq_len > sub:
        assert bb == 1 and seq_len % sub == 0

        def chunk(j, carry):
            rs = pl.ds(pl.multiple_of(j * sub, sub), sub)
            _interleave([_odd_block(has_state, 1, sub, proj_s.at[rs], tail_s, ost_ref, sst_s, *consts,
                                    y_ref.at[:, rs], u_ref.at[:, rs])])
            return carry

        lax.fori_loop(0, seq_len // sub, chunk, 0)
    elif bb > 1 and seq_len > SUBLANES:
        units = []
        for u in range(bb):
            one = pl.ds(u, 1)
            units.append(_odd_block(
                has_state, 1, seq_len, proj_s.at[pl.ds(u * seq_len, seq_len)],
                tail_s.at[pl.ds(u * SUBLANES, SUBLANES)], ost_ref.at[one],
                None if sst_s is None else sst_s.at[one], *consts, y_ref.at[one], u_ref.at[one]))
        _interleave(units)
    else:
        _interleave([_odd_block(has_state, bb, seq_len, proj_s, tail_s, ost_ref, sst_s, *consts,
                                y_ref, u_ref)])

    @pl.when(pl.program_id(1) == pl.num_programs(1) - 1)
    def _():
        for b in range(bb):
            oconv_ref[b] = tail_s[(b + 1) * SUBLANES - halo:(b + 1) * SUBLANES, :]
            if not has_state:
                for g in range(G_SSD):
                    for ps in range(0, ost_ref.shape[2], LANES):
                        ost_ref[b, g, ps:ps + LANES, :] = sst_s[b, g, :, ps:ps + LANES].T


def _odd_call(x, npre, win, convw, convb, dtb, alog, dssm, ssmw, states, bb, c, sub):
    nb_total, t, d = x.shape
    has_state = states is not None
    n_in = win.shape[1]
    d_inner = H_SSD * P_SSD
    conv_dim = d_inner + 2 * G_SSD * N_SSD
    u_w = S5_G * S5_GS
    halo = convw.shape[0] - 1
    rows = bb * c
    grid = (nb_total // bb, t // c)
    st_shape = (G_SSD, d_inner // G_SSD, N_SSD)
    conv_spec = pl.BlockSpec((bb, halo, conv_dim), lambda b, i: (b, 0, 0))
    ssm_spec = pl.BlockSpec((bb,) + st_shape, lambda b, i: (b, 0, 0, 0))
    in_specs = [pl.BlockSpec((bb, c, d), lambda b, i: (b, i, 0)),
                _const_spec((1, d)), _const_spec(win.shape), _const_spec(convw.shape),
                _const_spec((1, conv_dim)), _const_spec((1, LANES)), _const_spec((1, LANES)),
                _const_spec((1, d_inner)), _const_spec((1, d_inner))]
    args = [x, npre, win, convw, convb, dtb, alog, dssm, ssmw]
    scratch = [pltpu.VMEM((rows, n_in), F32), pltpu.VMEM((bb * SUBLANES, conv_dim), F32)]
    if has_state:
        in_specs += [conv_spec, ssm_spec]
        args += list(states)
    else:
        scratch.append(pltpu.VMEM((bb, G_SSD, N_SSD, d_inner // G_SSD), F32))
    return pl.pallas_call(
        functools.partial(_odd_kernel, has_state, sub),
        grid=grid,
        in_specs=in_specs,
        out_specs=[pl.BlockSpec((bb, c, d_inner), lambda b, i: (b, i, 0)),
                   pl.BlockSpec((bb, c, u_w), lambda b, i: (b, i, 0)),
                   conv_spec, ssm_spec],
        out_shape=[jax.ShapeDtypeStruct((nb_total, t, d_inner), BF16),
                   jax.ShapeDtypeStruct((nb_total, t, u_w), F32),
                   jax.ShapeDtypeStruct((nb_total, halo, conv_dim), F32),
                   jax.ShapeDtypeStruct((nb_total,) + st_shape, F32)],
        scratch_shapes=scratch,
        compiler_params=pltpu.CompilerParams(dimension_semantics=("arbitrary", "arbitrary"),
                                             vmem_limit_bytes=VMEM_LIMIT),
        name="odd_mixer",
    )(*args)


S5_PARTS = 4
S5_PART_IN = S5_G * S5_GS // S5_PARTS
S5_PART_ST = S5_G * S5_P // S5_PARTS
S5_SCAN_VREGS = 4


def _s5_pitch(steps):
    return steps if steps % 16 == 8 else steps + 8


def _s5_kernel(u_ref, bmat_ref, cmat_ref, lre_ref, lim_ref, d_ref, wglu_ref, bglu_ref,
               h0re_ref, h0im_ref, yd_ref, ore_ref, oim_ref, buf_s, slab_s, ut_s):
    r, steps, uw = u_ref.shape
    hs = S5_PART_ST
    scan_w = min(hs, max(LANES, S5_SCAN_VREGS * SUBLANES * LANES // r))
    pitch = _s5_pitch(steps)
    nslab = uw // LANES

    @pl.when(pl.program_id(1) == 0)
    def _():
        ore_ref[...] = h0re_ref[...]
        oim_ref[...] = h0im_ref[...]

    for b in range(r):
        for j in range(nslab):
            slab_s[j, b * pitch:b * pitch + steps, :] = u_ref[b, :, j * LANES:(j + 1) * LANES]
    for t in range(steps):
        for j in range(nslab):
            ut_s[t * r:(t + 1) * r, j * LANES:(j + 1) * LANES] = slab_s[j, pl.ds(t, r, stride=pitch), :]
    u = ut_s[...]
    ub = u.astype(BF16)
    for half in range(S5_PARTS):
        buf_s[:, half * 2 * hs:(half + 1) * 2 * hs] = jnp.dot(
            ub[:, half * S5_PART_IN:(half + 1) * S5_PART_IN], bmat_ref[half],
            preferred_element_type=F32)

    for half in range(S5_PARTS):
        for j in range(hs // scan_w):
            st = slice(half * hs + j * scan_w, half * hs + (j + 1) * scan_w)
            cre = half * 2 * hs + j * scan_w
            cim = cre + hs
            lre = jnp.broadcast_to(lre_ref[:, st], (r, scan_w))
            lim = jnp.broadcast_to(lim_ref[:, st], (r, scan_w))
            hre, him = ore_ref[:, st], oim_ref[:, st]
            for t in range(steps):
                rws = slice(t * r, (t + 1) * r)
                hre, him = (lre * hre - lim * him + buf_s[rws, cre:cre + scan_w],
                            lre * him + lim * hre + buf_s[rws, cim:cim + scan_w])
                buf_s[rws, cre:cre + scan_w] = hre
                buf_s[rws, cim:cim + scan_w] = him
            ore_ref[:, st] = hre
            oim_ref[:, st] = him

    ys = []
    for half in range(S5_PARTS):
        ys.append(jnp.dot(buf_s[:, half * 2 * hs:(half + 1) * 2 * hs].astype(BF16), cmat_ref[half],
                          preferred_element_type=F32))
    y = jnp.concatenate(ys, axis=1) + d_ref[...] * u
    g = jax.nn.gelu(y)
    gate = jax.nn.sigmoid(jnp.dot(g.astype(BF16), wglu_ref[...], preferred_element_type=F32)
                          + bglu_ref[...])
    yd = g * gate
    for t in range(steps):
        for j in range(nslab):
            slab_s[j, pl.ds(t, r, stride=pitch), :] = yd[t * r:(t + 1) * r, j * LANES:(j + 1) * LANES]
    for b in range(r):
        for j in range(nslab):
            yd_ref[b, :, j * LANES:(j + 1) * LANES] = slab_s[j, b * pitch:b * pitch + steps, :].astype(
                yd_ref.dtype)


def _s5_call(u, bmat, cmat, lre, lim, dvec, wglu, bglu, h0re, h0im, steps, r):
    nb_total, t, uw = u.shape
    ns = S5_G * S5_P
    grid = (nb_total // r, t // steps)
    st_spec = pl.BlockSpec((r, ns), lambda b, i: (b, 0))
    return pl.pallas_call(
        _s5_kernel,
        grid=grid,
        in_specs=[pl.BlockSpec((r, steps, uw), lambda b, i: (b, i, 0)),
                  _const_spec(bmat.shape), _const_spec(cmat.shape),
                  _const_spec((1, ns)), _const_spec((1, ns)), _const_spec((1, uw)),
                  _const_spec(wglu.shape), _const_spec((1, uw)), st_spec, st_spec],
        out_specs=[pl.BlockSpec((r, steps, uw), lambda b, i: (b, i, 0)), st_spec, st_spec],
        out_shape=[jax.ShapeDtypeStruct((nb_total, t, uw), BF16),
                   jax.ShapeDtypeStruct((nb_total, ns), F32),
                   jax.ShapeDtypeStruct((nb_total, ns), F32)],
        scratch_shapes=[pltpu.VMEM((steps * r, 2 * ns), F32),
                        pltpu.VMEM((uw // LANES, r * _s5_pitch(steps), LANES), F32),
                        pltpu.VMEM((steps * r, uw), F32)],
        compiler_params=pltpu.CompilerParams(dimension_semantics=("arbitrary", "arbitrary"),
                                             vmem_limit_bytes=VMEM_LIMIT),
        name="s5_mixer",
    )(u, bmat, cmat, lre, lim, dvec, wglu, bglu, h0re, h0im)


def _rope_tables(pos):
    half = HD // 2
    inv_freq = ROPE_BASE ** (-jnp.arange(half, dtype=F32) / half)
    ang = pos[:, None] * inv_freq[None, :]
    cos, sin = jnp.cos(ang), jnp.sin(ang)
    return jnp.concatenate([cos, cos], axis=-1), jnp.concatenate([-sin, sin], axis=-1)


def _pad_lanes(v):
    return jnp.pad(v.astype(F32), (0, LANES - v.shape[0]))[None, :]


def _s5_tables(lam_re, lam_im, log_step, b_re, b_im, c_re, c_im):
    lr, li = lam_re.astype(F32), lam_im.astype(F32)
    dt = jnp.exp(log_step.astype(F32))[:, None]
    mag = jnp.exp(lr * dt)
    bar_re, bar_im = mag * jnp.cos(li * dt), mag * jnp.sin(li * dt)
    den = lr * lr + li * li
    cf_re = ((bar_re - 1.0) * lr + bar_im * li) / den
    cf_im = (bar_im * lr - (bar_re - 1.0) * li) / den
    bb_re = cf_re[..., None] * b_re.astype(F32) - cf_im[..., None] * b_im.astype(F32)
    bb_im = cf_re[..., None] * b_im.astype(F32) + cf_im[..., None] * b_re.astype(F32)
    gh = S5_G // S5_PARTS
    eye = jnp.eye(gh, dtype=BF16)
    b_ri = jnp.stack([bb_re, bb_im]).astype(BF16).reshape(2, S5_PARTS, gh, S5_P, S5_GS)
    bmat = jnp.einsum('rqgpc,gk->qgcrkp', b_ri, eye).reshape(S5_PARTS, gh * S5_GS, 2 * gh * S5_P)
    c_ri = jnp.stack([c_re.astype(F32), -c_im.astype(F32)]).astype(BF16).reshape(
        2, S5_PARTS, gh, S5_GS, S5_P)
    cmat = jnp.einsum('rqgcp,gk->qrgpkc', c_ri, eye).reshape(S5_PARTS, 2 * gh * S5_P, gh * S5_GS)
    return bmat, cmat, bar_re.reshape(1, -1), bar_im.reshape(1, -1)


def _row(v):
    return v.astype(F32).reshape(1, -1)


def _forward(x, pos_offset, states, p, cfg):
    nb, t, d = x.shape
    n = nb * t
    bb, c = cfg['bb'], cfg['c']
    has_state = states is not None
    pos = jnp.arange(pos_offset, pos_offset + t, dtype=F32)
    cos, sin = _rope_tables(pos)
    tile_rows = lambda a: jnp.tile(a.reshape(t // c, 1, c, HD), (1, bb, 1, 1)).reshape(-1, HD)
    lb = jnp.cumsum(jax.nn.softmax(p['hgrn_lower_bounds'].astype(F32), axis=0), axis=0)[0]
    mix, ret_o, hg_o = _even_call(
        x, tile_rows(cos), tile_rows(sin), _row(p['norm_mix_pre'][0]), p['w_in_even'][0].astype(BF16),
        _row(p['ret_norm_w'][0]), _row(p['hgrn_norm_w'][0]), _row(lb),
        (states['ret'][0], states['hgrn'][0]) if has_state else None, bb, c, cfg['sub'])
    stacked_rows = lambda a: a.astype(F32)[:, None, :]
    ffn = (stacked_rows(p['norm_mix_post']), stacked_rows(p['norm_ffn_pre']),
           stacked_rows(p['norm_ffn_post']), p['w_ffn_up'].astype(BF16), p['w_ffn_down'].astype(BF16))
    h = _post_call(x.reshape(n, d), [mix.reshape(n, -1)], p['w_out_even'][0].astype(BF16), *ffn, 0,
                   cfg['rows'])
    d_inner = H_SSD * P_SSD
    conv_dim = d_inner + 2 * G_SSD * N_SSD
    u_w = S5_G * S5_GS
    w1 = p['w_in_odd'][0].astype(BF16)
    st_shape = (nb, G_SSD, d_inner // G_SSD, N_SSD)
    odd_states = None
    if has_state:
        odd_states = (states['conv'][0], jnp.swapaxes(states['ssm'][0], -1, -2).reshape(st_shape))
    y, u, conv_o, sst_o = _odd_call(
        h.reshape(nb, t, d), _row(p['norm_mix_pre'][1]), w1, p['conv_w'][0].astype(F32),
        _row(p['conv_b'][0]), _pad_lanes(p['dt_bias'][0]), _pad_lanes(p['a_log'][0]),
        _row(jnp.repeat(p['d_ssm'][0], P_SSD)), _row(p['ssm_norm_w'][0]), odd_states, bb, c, cfg['sub'])
    ssm_o = jnp.swapaxes(sst_o.reshape(nb, H_SSD, P_SSD, N_SSD), -1, -2)
    bmat, cmat, lre, lim = _s5_tables(p['s5_lam_re'][0], p['s5_lam_im'][0], p['s5_log_step'][0],
                                      p['s5_b_re'][0], p['s5_b_im'][0], p['s5_c_re'][0],
                                      p['s5_c_im'][0])
    ns = S5_G * S5_P
    if has_state:
        h0re = states['s5_re'][0].reshape(nb, ns)
        h0im = states['s5_im'][0].reshape(nb, ns)
    else:
        h0re = jnp.zeros((nb, ns), F32)
        h0im = jnp.zeros((nb, ns), F32)
    yd, re_o, im_o = _s5_call(u, bmat, cmat, lre, lim, _row(p['s5_d'][0]),
                              p['w_glu'][0].astype(BF16), _row(p['b_glu'][0]), h0re, h0im,
                              cfg['s5_steps'], cfg['s5_rows'])
    yd = yd.reshape(n, u_w)
    h = _post_call(h, [y.reshape(n, d_inner), yd], p['w_out_odd'][0].astype(BF16), *ffn, 1, cfg['rows'])
    return (h.reshape(nb, t, d), ret_o[None], hg_o[None], ssm_o[None], conv_o[None],
            re_o.reshape(1, nb, S5_G, S5_P), im_o.reshape(1, nb, S5_G, S5_P))


PROMPT_CFG = dict(bb=4, c=128, sub=128, rows=1024, s5_steps=64, s5_rows=8)
SAMPLE_CFG = dict(bb=16, c=8, sub=8, rows=1024, s5_steps=8, s5_rows=32)


def kernel(x_prompt, x_sample, state_ret, state_hgrn, state_ssm, state_conv, state_s5_re, state_s5_im,
           norm_mix_pre, norm_mix_post, norm_ffn_pre, norm_ffn_post, w_in_even, w_out_even, ret_norm_w,
           hgrn_lower_bounds, hgrn_norm_w, w_in_odd, conv_w, conv_b, dt_bias, a_log, d_ssm, ssm_norm_w,
           s5_lam_re, s5_lam_im, s5_log_step, s5_b_re, s5_b_im, s5_c_re, s5_c_im, s5_d, w_glu, b_glu,
           w_out_odd, w_ffn_up, w_ffn_down):
    p = dict(norm_mix_pre=norm_mix_pre, norm_mix_post=norm_mix_post, norm_ffn_pre=norm_ffn_pre,
             norm_ffn_post=norm_ffn_post, w_in_even=w_in_even, w_out_even=w_out_even, ret_norm_w=ret_norm_w,
             hgrn_lower_bounds=hgrn_lower_bounds, hgrn_norm_w=hgrn_norm_w, w_in_odd=w_in_odd, conv_w=conv_w,
             conv_b=conv_b, dt_bias=dt_bias, a_log=a_log, d_ssm=d_ssm, ssm_norm_w=ssm_norm_w,
             s5_lam_re=s5_lam_re, s5_lam_im=s5_lam_im, s5_log_step=s5_log_step, s5_b_re=s5_b_re,
             s5_b_im=s5_b_im, s5_c_re=s5_c_re, s5_c_im=s5_c_im, s5_d=s5_d, w_glu=w_glu, b_glu=b_glu,
             w_out_odd=w_out_odd, w_ffn_up=w_ffn_up, w_ffn_down=w_ffn_down)
    past_len = 16384
    states = dict(ret=state_ret, hgrn=state_hgrn, ssm=state_ssm, conv=state_conv,
                  s5_re=state_s5_re, s5_im=state_s5_im)
    y_p, ret_p, hg_p, ssm_p, conv_p, re_p, im_p = _forward(x_prompt, 0, None, p, PROMPT_CFG)
    y_s, ret_s, hg_s, ssm_s, conv_s, re_s, im_s = _forward(x_sample, past_len, states, p, SAMPLE_CFG)
    return (y_p, y_s, ret_p, ret_s, hg_p, hg_s, ssm_p, ssm_s, conv_p, conv_s,
            re_p, re_s, im_p, im_s)
```

```python
import functools
import math

import jax
import jax.numpy as jnp
import numpy as np
from jax import lax
from jax.experimental import pallas as pl
from jax.experimental.pallas import tpu as pltpu

F32 = jnp.float32
BF16 = jnp.bfloat16
EPS = 1e-6
ROPE_BASE = 10000.0
LANES = 128
SUBLANES = 8
VMEM_LIMIT = 56 * 1024 * 1024

H_RET = 4
H_HG = 4
HD = 128
H_SSD = 16
P_SSD = 64
N_SSD = 128
G_SSD = 2
S5_G = 32
S5_GS = 16
S5_P = 64
HG_BLK = 16


def _rms(x, w):
    return x * lax.rsqrt(jnp.mean(x * x, axis=-1, keepdims=True) + EPS) * w


def _silu(x):
    return x * jax.nn.sigmoid(x)


def _dot(a, b):
    return jnp.dot(a.astype(BF16), b.astype(BF16), preferred_element_type=F32)


def _dot_nt(a, b):
    return lax.dot_general(a.astype(BF16), b.astype(BF16), (((1,), (1,)), ((), ())),
                           preferred_element_type=F32)


def _dot_tn(a, b):
    return lax.dot_general(a.astype(BF16), b.astype(BF16), (((0,), (0,)), ((), ())),
                           preferred_element_type=F32)


def _split3(x):
    hi = x.astype(BF16)
    r1 = x - hi.astype(F32)
    mid = r1.astype(BF16)
    lo = (r1 - mid.astype(F32)).astype(BF16)
    return hi, mid, lo


def _dot3(a_exact, parts):
    acc = None
    for p in parts:
        d = jnp.dot(a_exact, p, preferred_element_type=F32)
        acc = d if acc is None else acc + d
    return acc


def _dot3_tn(parts, b_exact):
    acc = None
    for p in parts:
        d = lax.dot_general(p, b_exact, (((0,), (0,)), ((), ())), preferred_element_type=F32)
        acc = d if acc is None else acc + d
    return acc


def _ones_where(mask):
    return jnp.where(mask, 1.0, 0.0).astype(BF16)


def _seq_masks(rows, seq_len):
    sh = seq_len.bit_length() - 1
    ti = lax.broadcasted_iota(jnp.int32, (rows, rows), 0)
    si = lax.broadcasted_iota(jnp.int32, (rows, rows), 1)
    same = lax.shift_right_logical(ti, sh) == lax.shift_right_logical(si, sh)
    return ti, si, same, same & (ti >= si)


FILLERS_PER_ROUND = 2


def _interleave(units, fillers=()):
    live = list(units)
    fillers = list(fillers)
    while live:
        still = []
        for g in live:
            try:
                next(g)
                still.append(g)
            except StopIteration:
                pass
        live = still
        for f in fillers[:FILLERS_PER_ROUND]:
            f()
        fillers = fillers[FILLERS_PER_ROUND:]
    for f in fillers:
        f()


def _const_spec(shape):
    nd = len(shape)
    return pl.BlockSpec(shape, lambda *_: (0,) * nd, pipeline_mode=pl.Buffered(1))


def _post_kernel(n_mix, ff_chunk, n_parts, h_ref, *refs):
    mix_refs = refs[:n_mix]
    wout_ref, npost_ref, nfpre_ref, nfpost_ref, wup_ref, wdn_ref, o_ref = refs[n_mix:]
    part = h_ref.shape[0] // n_parts
    groups = [slice(i * part, (i + 1) * part) for i in range(n_parts)]
    accs = []
    for rs in groups:
        acc, off = None, 0
        for m_ref in mix_refs:
            w = m_ref.shape[1]
            d = jnp.dot(m_ref[rs, :], wout_ref[off:off + w, :], preferred_element_type=F32)
            acc = d if acc is None else acc + d
            off += w
        accs.append(acc)
    for rs, acc in zip(groups, accs):
        h1 = h_ref[rs, :] + _rms(acc, npost_ref[...])
        hn = _rms(h1, nfpre_ref[...]).astype(BF16)
        ff = None
        for j in range(wup_ref.shape[1] // ff_chunk):
            sl = slice(j * ff_chunk, (j + 1) * ff_chunk)
            up = jnp.dot(hn, wup_ref[:, sl], preferred_element_type=F32)
            act = jnp.square(jnp.maximum(up, 0.0)).astype(BF16)
            d = jnp.dot(act, wdn_ref[sl, :], preferred_element_type=F32)
            ff = d if ff is None else ff + d
        o_ref[rs, :] = h1 + _rms(ff, nfpost_ref[...])


def _layer_spec(shape, layer):
    nd = len(shape) - 1
    return pl.BlockSpec((None,) + tuple(shape[1:]), lambda *_: (layer,) + (0,) * nd,
                        pipeline_mode=pl.Buffered(1))


def _post_call(h, mixes, wout, npost, nfpre, nfpost, wup, wdn, layer, rows):
    n, d = h.shape
    n_mix = len(mixes)
    row_spec = lambda w: pl.BlockSpec((rows, w), lambda i: (i, 0))
    in_specs = ([row_spec(d)] + [row_spec(m.shape[1]) for m in mixes] + [_const_spec(wout.shape)]
                + [_layer_spec(a.shape, layer) for a in (npost, nfpre, nfpost, wup, wdn)])
    return pl.pallas_call(
        functools.partial(_post_kernel, n_mix, 1024, 2),
        grid=(n // rows,),
        in_specs=in_specs,
        out_specs=row_spec(d),
        out_shape=jax.ShapeDtypeStruct((n, d), F32),
        compiler_params=pltpu.CompilerParams(dimension_semantics=("arbitrary",),
                                             vmem_limit_bytes=VMEM_LIMIT),
        name="post_ffn",
    )(h, *mixes, wout, npost, nfpre, nfpost, wup, wdn)


def _ret_decay_tables(bb, seq_len):
    rows = bb * seq_len
    t = np.arange(rows)
    tl = t % seq_len
    valid = (t[:, None] // seq_len == t[None, :] // seq_len) & (t[:, None] >= t[None, :])
    pair, to_row, to_end = [], [], []
    for h in range(H_RET):
        lg = math.log1p(-(2.0 ** (-5.0 - h)))
        pair.append(np.where(valid, np.exp((t[:, None] - t[None, :]) * lg), 0.0))
        to_row.append(np.broadcast_to(np.exp((tl + 1.0) * lg)[:, None], (rows, HD)))
        to_end.append(np.broadcast_to(np.exp((seq_len - 1.0 - tl) * lg)[:, None], (rows, HD)))
    return tuple(jnp.asarray(np.stack(a), F32) for a in (pair, to_row, to_end))


def _even_block(bb, seq_len, proj_s, cos, sin, lb, retw_ref, hgw_ref, dec_refs, oret_ref, ohg_ref,
                mix_ref):
    rows = bb * seq_len
    hw = H_RET * HD
    base = 4 * hw
    sh = seq_len.bit_length() - 1
    ti, si, same, causal = _seq_masks(rows, seq_len)
    pair_ref, to_row_ref, to_end_ref = dec_refs
    seqs = [slice(b * seq_len, (b + 1) * seq_len) for b in range(bb)]
    one_shot = seq_len <= HG_BLK

    f = lb + (1.0 - lb) * jax.nn.sigmoid(proj_s[:, base + hw:base + 2 * hw])
    lf_parts = _split3(jnp.log(f))
    cum = _dot3(_ones_where(causal), lf_parts)
    if bb == 1:
        dtot = _dot3_tn(lf_parts, jnp.ones((rows, HD), BF16))
        cl = cum[rows - 1:rows, :]
    else:
        r_seq = lax.shift_right_logical(lax.broadcasted_iota(jnp.int32, (rows, HD), 0), sh)
        ind = _ones_where(r_seq == lax.broadcasted_iota(jnp.int32, (rows, HD), 1))
        dtot = _dot3_tn(lf_parts, ind)
        cl = _dot3(_ones_where(same), lf_parts)
    if one_shot:
        refc = _dot3(_ones_where(same & ((si & (seq_len - 1)) < seq_len // 2)), lf_parts)

    yield
    r_a, r_qs, r_kv, r_v = [], [], [], []
    for h in range(H_RET):
        col = lambda off: slice(off + h * HD, off + (h + 1) * HD)
        q = proj_s[:, col(0)]
        k = proj_s[:, col(hw)]
        v = proj_s[:, col(2 * hw)]
        q = q * cos + pltpu.roll(q, HD // 2, axis=1) * sin
        k = (k * cos + pltpu.roll(k, HD // 2, axis=1) * sin) * (HD ** -0.5)
        kd = k * to_end_ref[h]
        r_a.append(_dot_nt(q, k))
        r_qs.append([_dot(q[s], oret_ref[b, h]) for b, s in enumerate(seqs)])
        r_kv.append([_dot_tn(kd[s], v[s]) for s in seqs])
        r_v.append(v)

    g_a, g_qs, g_kv, g_v = [], [], [], []
    for h in range(H_HG):
        hs = slice(h * HD, (h + 1) * HD)
        col = lambda off: slice(base + off + h * HD, base + off + (h + 1) * HD)
        q = proj_s[:, col(0)]
        v = proj_s[:, col(2 * hw)]
        k = 1.0 - f[:, hs]
        cum_h = cum[:, hs]
        qe = q * jnp.exp(cum_h)
        ke = k * jnp.exp(cl[:, hs] - cum_h)
        g_qs.append([_dot(qe[s], ohg_ref[b, h]) for b, s in enumerate(seqs)])
        g_kv.append([_dot_tn(ke[s], v[s]) for s in seqs])
        if one_shot:
            ref = refc[:, hs]
            g_a.append([_dot_nt(q * jnp.exp(cum_h - ref), k * jnp.exp(ref - cum_h))])
        else:
            blocks = []
            for i in range(rows // HG_BLK):
                lo, hi = i * HG_BLK, (i + 1) * HG_BLK
                ref = cum_h[lo + HG_BLK // 2 - 1:lo + HG_BLK // 2, :]
                blocks.append(_dot_nt(q[lo:hi] * jnp.exp(cum_h[lo:hi] - ref),
                                      k[:hi] * jnp.exp(ref - cum_h[:hi])))
            g_a.append(blocks)
        g_v.append(v)

    yield
    r_o = []
    for h in range(H_RET):
        lg = math.log1p(-(2.0 ** (-5.0 - h)))
        qs = r_qs[h][0] if bb == 1 else jnp.concatenate(r_qs[h], axis=0)
        r_o.append(_dot(r_a[h] * pair_ref[h], r_v[h]) + qs * to_row_ref[h])
        for b in range(bb):
            oret_ref[b, h] = oret_ref[b, h] * math.exp(seq_len * lg) + r_kv[h][b]

    g_o = []
    for h in range(H_HG):
        if one_shot:
            o = _dot(jnp.where(causal, g_a[h][0], 0.0), g_v[h])
        else:
            outs = []
            for i in range(rows // HG_BLK):
                lo, hi = i * HG_BLK, (i + 1) * HG_BLK
                tb = lax.broadcasted_iota(jnp.int32, (HG_BLK, hi), 0) + lo
                sb = lax.broadcasted_iota(jnp.int32, (HG_BLK, hi), 1)
                outs.append(_dot(jnp.where(tb >= sb, g_a[h][i], 0.0), g_v[h][:hi]))
            o = jnp.concatenate(outs, axis=0)
        qs = g_qs[h][0] if bb == 1 else jnp.concatenate(g_qs[h], axis=0)
        g_o.append(o + qs)
        for b in range(bb):
            dcol = dtot[h * HD:(h + 1) * HD, :]
            if bb > 1:
                dcol = jnp.broadcast_to(dcol[:, b:b + 1], (HD, HD))
            ohg_ref[b, h] = ohg_ref[b, h] * jnp.exp(dcol) + g_kv[h][b]

    yield
    for h in range(H_RET):
        col = lambda off: slice(off + h * HD, off + (h + 1) * HD)
        mu = jnp.mean(r_o[h], axis=-1, keepdims=True)
        oc = r_o[h] - mu
        var = jnp.mean(oc * oc, axis=-1, keepdims=True)
        o = oc * lax.rsqrt(var + EPS) * retw_ref[:, col(0)] * _silu(proj_s[:, col(3 * hw)])
        mix_ref[:, :, col(0)] = o.reshape(bb, seq_len, HD).astype(mix_ref.dtype)
    for h in range(H_HG):
        hs = slice(h * HD, (h + 1) * HD)
        g = proj_s[:, base + 3 * hw + h * HD:base + 3 * hw + (h + 1) * HD]
        o = _rms(g_o[h], hgw_ref[:, hs]) * _silu(g)
        mix_ref[:, :, hw + h * HD:hw + (h + 1) * HD] = o.reshape(bb, seq_len, HD).astype(mix_ref.dtype)


PROJ_COLS = 512


def _project(x_ref, npre_ref, win_ref, proj_s):
    bb, seq_len, d = x_ref.shape
    hn = _rms(x_ref[...].reshape(bb * seq_len, d), npre_ref[...]).astype(BF16)
    n_in = win_ref.shape[1]

    def chunk(lo):
        def run():
            sl = slice(lo, min(lo + PROJ_COLS, n_in))
            proj_s[:, sl] = jnp.dot(hn, win_ref[:, sl], preferred_element_type=F32)
        return run

    return [chunk(lo) for lo in range(0, n_in, PROJ_COLS)]


def _projection_ahead(x_ref, xn_ref, npre_ref, win_ref, proj_s, proj_next_s):
    @pl.when((pl.program_id(0) == 0) & (pl.program_id(1) == 0))
    def _():
        for f in _project(x_ref, npre_ref, win_ref, proj_s):
            f()

    return _project(xn_ref, npre_ref, win_ref, proj_next_s)


def _carry_projection(proj_s, proj_next_s):
    n_in = proj_s.shape[1]
    for lo in range(0, n_in, PROJ_COLS):
        sl = slice(lo, min(lo + PROJ_COLS, n_in))
        proj_s[:, sl] = proj_next_s[:, sl]


def _next_block_map(n_blocks, nt):
    def imap(b, i):
        s = jnp.minimum(b * nt + i + 1, n_blocks * nt - 1)
        return (s // nt, s % nt, 0)
    return imap


def _even_kernel(has_state, x_ref, xn_ref, cos_ref, sin_ref, npre_ref, win_ref, retw_ref, hgw_ref,
                 lb_ref, pair_ref, to_row_ref, to_end_ref, *refs):
    if has_state:
        sret_ref, shg_ref, mix_ref, oret_ref, ohg_ref, proj_s, proj_next_s = refs
    else:
        mix_ref, oret_ref, ohg_ref, proj_s, proj_next_s = refs
    bb, seq_len, _ = x_ref.shape
    dec_refs = (pair_ref, to_row_ref, to_end_ref)

    @pl.when(pl.program_id(1) == 0)
    def _():
        if has_state:
            oret_ref[...] = sret_ref[...]
            ohg_ref[...] = shg_ref[...]
        else:
            oret_ref[...] = jnp.zeros(oret_ref.shape, F32)
            ohg_ref[...] = jnp.zeros(ohg_ref.shape, F32)

    fillers = _projection_ahead(x_ref, xn_ref, npre_ref, win_ref, proj_s, proj_next_s)
    consts = (lb_ref[...], retw_ref, hgw_ref, dec_refs)
    if seq_len > HG_BLK:
        units = []
        for u in range(bb):
            rs, one = pl.ds(u * seq_len, seq_len), pl.ds(u, 1)
            units.append(_even_block(1, seq_len, proj_s.at[rs], cos_ref[rs, :], sin_ref[rs, :], *consts,
                                     oret_ref.at[one], ohg_ref.at[one], mix_ref.at[one]))
    else:
        units = [_even_block(bb, seq_len, proj_s, cos_ref[...], sin_ref[...], *consts,
                             oret_ref, ohg_ref, mix_ref)]
    _interleave(units, fillers)
    _carry_projection(proj_s, proj_next_s)


def _even_call(x, cos, sin, npre, win, retw, hgw, lb, states, bb, c):
    nb_total, t, d = x.shape
    has_state = states is not None
    n_in = win.shape[1]
    rows = bb * c
    grid = (nb_total // bb, t // c)
    st_spec = pl.BlockSpec((bb, H_RET, HD, HD), lambda b, i: (b, 0, 0, 0))
    in_specs = [pl.BlockSpec((bb, c, d), lambda b, i: (b, i, 0)),
                pl.BlockSpec((bb, c, d), _next_block_map(*grid)),
                pl.BlockSpec((rows, HD), lambda b, i: (i, 0)),
                pl.BlockSpec((rows, HD), lambda b, i: (i, 0)),
                _const_spec((1, d)), _const_spec(win.shape),
                _const_spec((1, H_RET * HD)), _const_spec((1, H_HG * HD)), _const_spec((1, H_HG * HD))]
    tables = _ret_decay_tables(bb if c <= HG_BLK else 1, c)
    in_specs += [_const_spec(a.shape) for a in tables]
    args = [x, x, cos, sin, npre, win, retw, hgw, lb, *tables]
    if has_state:
        in_specs += [st_spec, st_spec]
        args += list(states)
    mix_w = (H_RET + H_HG) * HD
    return pl.pallas_call(
        functools.partial(_even_kernel, has_state),
        grid=grid,
        in_specs=in_specs,
        out_specs=[pl.BlockSpec((bb, c, mix_w), lambda b, i: (b, i, 0)), st_spec, st_spec],
        out_shape=[jax.ShapeDtypeStruct((nb_total, t, mix_w), BF16),
                   jax.ShapeDtypeStruct((nb_total, H_RET, HD, HD), F32),
                   jax.ShapeDtypeStruct((nb_total, H_HG, HD, HD), F32)],
        scratch_shapes=[pltpu.VMEM((rows, n_in), F32), pltpu.VMEM((rows, n_in), F32)],
        compiler_params=pltpu.CompilerParams(dimension_semantics=("arbitrary", "arbitrary"),
                                             vmem_limit_bytes=VMEM_LIMIT),
        name="even_mixer",
    )(*args)


def _odd_block(has_state, bb, seq_len, proj_s, tail_s, ost_ref, sst_s, convw_ref, convb_ref, dtb_ref,
               alog_ref, dssm_ref, ssmw_ref, y_ref, u_ref):
    rows = bb * seq_len
    d_inner = H_SSD * P_SSD
    conv_dim = d_inner + 2 * G_SSD * N_SSD
    conv_w = convw_ref.shape[0]
    halo = conv_w - 1
    u_w = u_ref.shape[2]
    gw = d_inner // G_SSD
    heads_per_group = H_SSD // G_SSD
    pairs_per_group = heads_per_group // 2
    seqs = [slice(b * seq_len, (b + 1) * seq_len) for b in range(bb)]

    z = proj_s[:, 0:d_inner]
    xbc = proj_s[:, d_inner:d_inner + conv_dim]
    c0 = d_inner + conv_dim
    dt_raw = proj_s[:, c0:c0 + LANES]
    u_ref[...] = proj_s[:, c0 + H_SSD:c0 + H_SSD + u_w].reshape(bb, seq_len, u_w)

    tail = tail_s[...]
    conv = convb_ref[...] + xbc * convw_ref[halo:halo + 1, :]
    for s in range(1, conv_w):
        rolled = pltpu.roll(xbc, s, axis=0)
        if seq_len == SUBLANES:
            local = lax.broadcasted_iota(jnp.int32, (rows, conv_dim), 0) & (SUBLANES - 1)
            shifted = jnp.where(local < s, pltpu.roll(tail, rows - SUBLANES + s, axis=0), rolled)
        else:
            local = lax.broadcasted_iota(jnp.int32, (SUBLANES, conv_dim), 0)
            head = jnp.where(local < s, pltpu.roll(tail, s, axis=0), rolled[0:SUBLANES])
            shifted = jnp.concatenate([head, rolled[SUBLANES:]], axis=0)
        conv = conv + shifted * convw_ref[halo - s:halo - s + 1, :]
    tail_s[...] = xbc if seq_len == SUBLANES else xbc[rows - SUBLANES:rows]
    act = _silu(conv)
    xs = act[:, :d_inner]

    ti, si, same, causal = _seq_masks(rows, seq_len)
    lo_half = lax.broadcasted_iota(jnp.int32, (rows, LANES), 1) < P_SSD
    dt = jax.nn.softplus(dt_raw + dtb_ref[...])
    la_parts = _split3(dt * (-jnp.exp(alog_ref[...])))
    cum = _dot3(_ones_where(causal), la_parts)
    cum_t = _dot3_tn(la_parts, _ones_where(same & (ti <= si)))
    if bb > 1:
        rest = _dot3(_ones_where(same & (si > ti)), la_parts)
    bms = [act[:, d_inner + g * N_SSD:d_inner + (g + 1) * N_SSD] for g in range(G_SSD)]
    cms = [act[:, d_inner + (G_SSD + g) * N_SSD:d_inner + (G_SSD + g + 1) * N_SSD]
           for g in range(G_SSD)]
    scores = [_dot_nt(cms[g], bms[g]) for g in range(G_SSD)]
    yield

    bc = lambda a, h: jnp.broadcast_to(a[:, h:h + 1], (rows, LANES))
    bc_cum = [bc(cum, h) for h in range(H_SSD)]
    pair = lambda a, p: jnp.where(lo_half, bc(a, 2 * p), bc(a, 2 * p + 1))

    for g in range(G_SSD):
        xdts, xws, cums = [], [], []
        for pp in range(pairs_per_group):
            p = g * pairs_per_group + pp
            xdt = xs[:, p * LANES:(p + 1) * LANES] * pair(dt, p)
            cum_p = jnp.where(lo_half, bc_cum[2 * p], bc_cum[2 * p + 1])
            rest_p = cum_p[rows - 1:rows, :] - cum_p if bb == 1 else pair(rest, p)
            xdts.append(xdt)
            xws.append(xdt * jnp.exp(rest_p))
            cums.append(cum_p)
        xw = jnp.concatenate(xws, axis=1)
        if has_state:
            y_st = [_dot_nt(cms[g][s], ost_ref[b, g]) for b, s in enumerate(seqs)]
            kv = [_dot_tn(xw[s], bms[g][s]) for s in seqs]
        else:
            y_st = [_dot(cms[g][s], sst_s[b, g]) for b, s in enumerate(seqs)]
            kv = [_dot_tn(bms[g][s], xw[s]) for s in seqs]
        y_intra = []
        for pp in range(pairs_per_group):
            p = g * pairs_per_group + pp
            a_pair = []
            for h in (2 * p, 2 * p + 1):
                diff = bc_cum[h][:, :rows] - cum_t[h:h + 1, :]
                a_pair.append(scores[g] * jnp.where(causal, jnp.exp(jnp.where(causal, diff, 0.0)), 0.0))
            both = _dot(jnp.concatenate(a_pair, axis=0), xdts[pp])
            y_intra.append(jnp.where(lo_half, both[:rows], both[rows:]))
        yield
        for b in range(bb):
            last = (b + 1) * seq_len - 1
            if has_state:
                dec = [jnp.broadcast_to(jnp.exp(bc_cum[g * heads_per_group + hh][last:last + 1, :]),
                                        (P_SSD, N_SSD)) for hh in range(heads_per_group)]
                ost_ref[b, g] = ost_ref[b, g] * jnp.concatenate(dec, axis=0) + kv[b]
            else:
                dec = jnp.exp(jnp.concatenate([c_p[last:last + 1, :] for c_p in cums], axis=1))
                sst_s[b, g] = sst_s[b, g] * dec + kv[b]
        y_state = y_st[0] if bb == 1 else jnp.concatenate(y_st, axis=0)
        gs = slice(g * gw, (g + 1) * gw)
        y = (jnp.concatenate(y_intra, axis=1) + y_state * jnp.exp(jnp.concatenate(cums, axis=1))
             + dssm_ref[:, gs] * xs[:, gs])
        y = _rms(y * _silu(z[:, gs]), ssmw_ref[:, gs])
        y_ref[:, :, gs] = y.reshape(bb, seq_len, gw).astype(y_ref.dtype)


def _odd_kernel(has_state, x_ref, xn_ref, npre_ref, win_ref, convw_ref, convb_ref, dtb_ref, alog_ref,
                dssm_ref, ssmw_ref, *refs):
    if has_state:
        sconv_ref, sst_ref, y_ref, u_ref, oconv_ref, ost_ref, proj_s, proj_next_s, tail_s = refs
        sst_s = None
    else:
        y_ref, u_ref, oconv_ref, ost_ref, proj_s, proj_next_s, tail_s, sst_s = refs
    bb, seq_len, _ = x_ref.shape
    halo = convw_ref.shape[0] - 1

    @pl.when(pl.program_id(1) == 0)
    def _():
        tail_s[...] = jnp.zeros(tail_s.shape, F32)
        if has_state:
            for b in range(bb):
                tail_s[(b + 1) * SUBLANES - halo:(b + 1) * SUBLANES, :] = sconv_ref[b]
            ost_ref[...] = sst_ref[...]
        else:
            sst_s[...] = jnp.zeros(sst_s.shape, F32)

    fillers = _projection_ahead(x_ref, xn_ref, npre_ref, win_ref, proj_s, proj_next_s)
    consts = (convw_ref, convb_ref, dtb_ref, alog_ref, dssm_ref, ssmw_ref)
    if seq_len > SUBLANES:
        units = []
        for u in range(bb):
            one = pl.ds(u, 1)
            units.append(_odd_block(
                has_state, 1, seq_len, proj_s.at[pl.ds(u * seq_len, seq_len)],
                tail_s.at[pl.ds(u * SUBLANES, SUBLANES)], ost_ref.at[one],
                None if sst_s is None else sst_s.at[one], *consts, y_ref.at[one], u_ref.at[one]))
    else:
        units = [_odd_block(has_state, bb, seq_len, proj_s, tail_s, ost_ref, sst_s, *consts, y_ref, u_ref)]
    _interleave(units, fillers)
    _carry_projection(proj_s, proj_next_s)

    @pl.when(pl.program_id(1) == pl.num_programs(1) - 1)
    def _():
        for b in range(bb):
            oconv_ref[b] = tail_s[(b + 1) * SUBLANES - halo:(b + 1) * SUBLANES, :]
            if not has_state:
                for g in range(G_SSD):
                    for ps in range(0, ost_ref.shape[2], LANES):
                        ost_ref[b, g, ps:ps + LANES, :] = sst_s[b, g, :, ps:ps + LANES].T


def _odd_call(x, npre, win, convw, convb, dtb, alog, dssm, ssmw, states, bb, c):
    nb_total, t, d = x.shape
    has_state = states is not None
    n_in = win.shape[1]
    d_inner = H_SSD * P_SSD
    conv_dim = d_inner + 2 * G_SSD * N_SSD
    u_w = S5_G * S5_GS
    halo = convw.shape[0] - 1
    rows = bb * c
    grid = (nb_total // bb, t // c)
    st_shape = (G_SSD, d_inner // G_SSD, N_SSD)
    conv_spec = pl.BlockSpec((bb, halo, conv_dim), lambda b, i: (b, 0, 0))
    ssm_spec = pl.BlockSpec((bb,) + st_shape, lambda b, i: (b, 0, 0, 0))
    in_specs = [pl.BlockSpec((bb, c, d), lambda b, i: (b, i, 0)),
                pl.BlockSpec((bb, c, d), _next_block_map(*grid)),
                _const_spec((1, d)), _const_spec(win.shape), _const_spec(convw.shape),
                _const_spec((1, conv_dim)), _const_spec((1, LANES)), _const_spec((1, LANES)),
                _const_spec((1, d_inner)), _const_spec((1, d_inner))]
    args = [x, x, npre, win, convw, convb, dtb, alog, dssm, ssmw]
    scratch = [pltpu.VMEM((rows, n_in), F32), pltpu.VMEM((rows, n_in), F32),
               pltpu.VMEM((bb * SUBLANES, conv_dim), F32)]
    if has_state:
        in_specs += [conv_spec, ssm_spec]
        args += list(states)
    else:
        scratch.append(pltpu.VMEM((bb, G_SSD, N_SSD, d_inner // G_SSD), F32))
    return pl.pallas_call(
        functools.partial(_odd_kernel, has_state),
        grid=grid,
        in_specs=in_specs,
        out_specs=[pl.BlockSpec((bb, c, d_inner), lambda b, i: (b, i, 0)),
                   pl.BlockSpec((bb, c, u_w), lambda b, i: (b, i, 0)),
                   conv_spec, ssm_spec],
        out_shape=[jax.ShapeDtypeStruct((nb_total, t, d_inner), BF16),
                   jax.ShapeDtypeStruct((nb_total, t, u_w), F32),
                   jax.ShapeDtypeStruct((nb_total, halo, conv_dim), F32),
                   jax.ShapeDtypeStruct((nb_total,) + st_shape, F32)],
        scratch_shapes=scratch,
        compiler_params=pltpu.CompilerParams(dimension_semantics=("arbitrary", "arbitrary"),
                                             vmem_limit_bytes=VMEM_LIMIT),
        name="odd_mixer",
    )(*args)


S5_PARTS = 4
S5_PART_IN = S5_G * S5_GS // S5_PARTS
S5_PART_ST = S5_G * S5_P // S5_PARTS
S5_SCAN_VREGS = 4


def _s5_pitch(steps):
    return steps if steps % 16 == 8 else steps + 8


def _s5_kernel(u_ref, bmat_ref, cmat_ref, lre_ref, lim_ref, d_ref, wglu_ref, bglu_ref,
               h0re_ref, h0im_ref, yd_ref, ore_ref, oim_ref, buf_s, slab_s, ut_s):
    r, steps, uw = u_ref.shape
    hs = S5_PART_ST
    scan_w = min(hs, max(LANES, S5_SCAN_VREGS * SUBLANES * LANES // r))
    pitch = _s5_pitch(steps)
    nslab = uw // LANES

    @pl.when(pl.program_id(1) == 0)
    def _():
        ore_ref[...] = h0re_ref[...]
        oim_ref[...] = h0im_ref[...]

    for b in range(r):
        for j in range(nslab):
            slab_s[j, b * pitch:b * pitch + steps, :] = u_ref[b, :, j * LANES:(j + 1) * LANES]
    for t in range(steps):
        for j in range(nslab):
            ut_s[t * r:(t + 1) * r, j * LANES:(j + 1) * LANES] = slab_s[j, pl.ds(t, r, stride=pitch), :]
    u = ut_s[...]
    ub = u.astype(BF16)
    for half in range(S5_PARTS):
        buf_s[:, half * 2 * hs:(half + 1) * 2 * hs] = jnp.dot(
            ub[:, half * S5_PART_IN:(half + 1) * S5_PART_IN], bmat_ref[half],
            preferred_element_type=F32)

    for half in range(S5_PARTS):
        for j in range(hs // scan_w):
            st = slice(half * hs + j * scan_w, half * hs + (j + 1) * scan_w)
            cre = half * 2 * hs + j * scan_w
            cim = cre + hs
            lre = jnp.broadcast_to(lre_ref[:, st], (r, scan_w))
            lim = jnp.broadcast_to(lim_ref[:, st], (r, scan_w))
            hre, him = ore_ref[:, st], oim_ref[:, st]
            for t in range(steps):
                rws = slice(t * r, (t + 1) * r)
                hre, him = (lre * hre - lim * him + buf_s[rws, cre:cre + scan_w],
                            lre * him + lim * hre + buf_s[rws, cim:cim + scan_w])
                buf_s[rws, cre:cre + scan_w] = hre
                buf_s[rws, cim:cim + scan_w] = him
            ore_ref[:, st] = hre
            oim_ref[:, st] = him

    ys = []
    for half in range(S5_PARTS):
        ys.append(jnp.dot(buf_s[:, half * 2 * hs:(half + 1) * 2 * hs].astype(BF16), cmat_ref[half],
                          preferred_element_type=F32))
    y = jnp.concatenate(ys, axis=1) + d_ref[...] * u
    g = jax.nn.gelu(y)
    gate = jax.nn.sigmoid(jnp.dot(g.astype(BF16), wglu_ref[...], preferred_element_type=F32)
                          + bglu_ref[...])
    yd = g * gate
    for t in range(steps):
        for j in range(nslab):
            slab_s[j, pl.ds(t, r, stride=pitch), :] = yd[t * r:(t + 1) * r, j * LANES:(j + 1) * LANES]
    for b in range(r):
        for j in range(nslab):
            yd_ref[b, :, j * LANES:(j + 1) * LANES] = slab_s[j, b * pitch:b * pitch + steps, :].astype(
                yd_ref.dtype)


def _s5_call(u, bmat, cmat, lre, lim, dvec, wglu, bglu, h0re, h0im, steps, r):
    nb_total, t, uw = u.shape
    ns = S5_G * S5_P
    grid = (nb_total // r, t // steps)
    st_spec = pl.BlockSpec((r, ns), lambda b, i: (b, 0))
    return pl.pallas_call(
        _s5_kernel,
        grid=grid,
        in_specs=[pl.BlockSpec((r, steps, uw), lambda b, i: (b, i, 0)),
                  _const_spec(bmat.shape), _const_spec(cmat.shape),
                  _const_spec((1, ns)), _const_spec((1, ns)), _const_spec((1, uw)),
                  _const_spec(wglu.shape), _const_spec((1, uw)), st_spec, st_spec],
        out_specs=[pl.BlockSpec((r, steps, uw), lambda b, i: (b, i, 0)), st_spec, st_spec],
        out_shape=[jax.ShapeDtypeStruct((nb_total, t, uw), BF16),
                   jax.ShapeDtypeStruct((nb_total, ns), F32),
                   jax.ShapeDtypeStruct((nb_total, ns), F32)],
        scratch_shapes=[pltpu.VMEM((steps * r, 2 * ns), F32),
                        pltpu.VMEM((uw // LANES, r * _s5_pitch(steps), LANES), F32),
                        pltpu.VMEM((steps * r, uw), F32)],
        compiler_params=pltpu.CompilerParams(dimension_semantics=("arbitrary", "arbitrary"),
                                             vmem_limit_bytes=VMEM_LIMIT),
        name="s5_mixer",
    )(u, bmat, cmat, lre, lim, dvec, wglu, bglu, h0re, h0im)


def _rope_tables(pos):
    half = HD // 2
    inv_freq = ROPE_BASE ** (-jnp.arange(half, dtype=F32) / half)
    ang = pos[:, None] * inv_freq[None, :]
    cos, sin = jnp.cos(ang), jnp.sin(ang)
    return jnp.concatenate([cos, cos], axis=-1), jnp.concatenate([-sin, sin], axis=-1)


def _pad_lanes(v):
    return jnp.pad(v.astype(F32), (0, LANES - v.shape[0]))[None, :]


def _s5_tables(lam_re, lam_im, log_step, b_re, b_im, c_re, c_im):
    lr, li = lam_re.astype(F32), lam_im.astype(F32)
    dt = jnp.exp(log_step.astype(F32))[:, None]
    mag = jnp.exp(lr * dt)
    bar_re, bar_im = mag * jnp.cos(li * dt), mag * jnp.sin(li * dt)
    den = lr * lr + li * li
    cf_re = ((bar_re - 1.0) * lr + bar_im * li) / den
    cf_im = (bar_im * lr - (bar_re - 1.0) * li) / den
    bb_re = cf_re[..., None] * b_re.astype(F32) - cf_im[..., None] * b_im.astype(F32)
    bb_im = cf_re[..., None] * b_im.astype(F32) + cf_im[..., None] * b_re.astype(F32)
    gh = S5_G // S5_PARTS
    eye = jnp.eye(gh, dtype=BF16)
    b_ri = jnp.stack([bb_re, bb_im]).astype(BF16).reshape(2, S5_PARTS, gh, S5_P, S5_GS)
    bmat = jnp.einsum('rqgpc,gk->qgcrkp', b_ri, eye).reshape(S5_PARTS, gh * S5_GS, 2 * gh * S5_P)
    c_ri = jnp.stack([c_re.astype(F32), -c_im.astype(F32)]).astype(BF16).reshape(
        2, S5_PARTS, gh, S5_GS, S5_P)
    cmat = jnp.einsum('rqgcp,gk->qrgpkc', c_ri, eye).reshape(S5_PARTS, 2 * gh * S5_P, gh * S5_GS)
    return bmat, cmat, bar_re.reshape(1, -1), bar_im.reshape(1, -1)


def _row(v):
    return v.astype(F32).reshape(1, -1)


def _forward(x, pos_offset, states, p, cfg):
    nb, t, d = x.shape
    n = nb * t
    bb, c = cfg['bb'], cfg['c']
    has_state = states is not None
    pos = jnp.arange(pos_offset, pos_offset + t, dtype=F32)
    cos, sin = _rope_tables(pos)
    tile_rows = lambda a: jnp.tile(a.reshape(t // c, 1, c, HD), (1, bb, 1, 1)).reshape(-1, HD)
    lb = jnp.cumsum(jax.nn.softmax(p['hgrn_lower_bounds'].astype(F32), axis=0), axis=0)[0]
    mix, ret_o, hg_o = _even_call(
        x, tile_rows(cos), tile_rows(sin), _row(p['norm_mix_pre'][0]), p['w_in_even'][0].astype(BF16),
        _row(p['ret_norm_w'][0]), _row(p['hgrn_norm_w'][0]), _row(lb),
        (states['ret'][0], states['hgrn'][0]) if has_state else None, bb, c)
    stacked_rows = lambda a: a.astype(F32)[:, None, :]
    ffn = (stacked_rows(p['norm_mix_post']), stacked_rows(p['norm_ffn_pre']),
           stacked_rows(p['norm_ffn_post']), p['w_ffn_up'].astype(BF16), p['w_ffn_down'].astype(BF16))
    h = _post_call(x.reshape(n, d), [mix.reshape(n, -1)], p['w_out_even'][0].astype(BF16), *ffn, 0,
                   cfg['rows'])
    d_inner = H_SSD * P_SSD
    conv_dim = d_inner + 2 * G_SSD * N_SSD
    u_w = S5_G * S5_GS
    w1 = p['w_in_odd'][0].astype(BF16)
    st_shape = (nb, G_SSD, d_inner // G_SSD, N_SSD)
    odd_states = None
    if has_state:
        odd_states = (states['conv'][0], jnp.swapaxes(states['ssm'][0], -1, -2).reshape(st_shape))
    y, u, conv_o, sst_o = _odd_call(
        h.reshape(nb, t, d), _row(p['norm_mix_pre'][1]), w1, p['conv_w'][0].astype(F32),
        _row(p['conv_b'][0]), _pad_lanes(p['dt_bias'][0]), _pad_lanes(p['a_log'][0]),
        _row(jnp.repeat(p['d_ssm'][0], P_SSD)), _row(p['ssm_norm_w'][0]), odd_states, bb, c)
    ssm_o = jnp.swapaxes(sst_o.reshape(nb, H_SSD, P_SSD, N_SSD), -1, -2)
    bmat, cmat, lre, lim = _s5_tables(p['s5_lam_re'][0], p['s5_lam_im'][0], p['s5_log_step'][0],
                                      p['s5_b_re'][0], p['s5_b_im'][0], p['s5_c_re'][0],
                                      p['s5_c_im'][0])
    ns = S5_G * S5_P
    if has_state:
        h0re = states['s5_re'][0].reshape(nb, ns)
        h0im = states['s5_im'][0].reshape(nb, ns)
    else:
        h0re = jnp.zeros((nb, ns), F32)
        h0im = jnp.zeros((nb, ns), F32)
    yd, re_o, im_o = _s5_call(u, bmat, cmat, lre, lim, _row(p['s5_d'][0]),
                              p['w_glu'][0].astype(BF16), _row(p['b_glu'][0]), h0re, h0im,
                              cfg['s5_steps'], cfg['s5_rows'])
    yd = yd.reshape(n, u_w)
    h = _post_call(h, [y.reshape(n, d_inner), yd], p['w_out_odd'][0].astype(BF16), *ffn, 1, cfg['rows'])
    return (h.reshape(nb, t, d), ret_o[None], hg_o[None], ssm_o[None], conv_o[None],
            re_o.reshape(1, nb, S5_G, S5_P), im_o.reshape(1, nb, S5_G, S5_P))


PROMPT_CFG = dict(bb=4, c=128, sub=128, rows=1024, s5_steps=64, s5_rows=8)
SAMPLE_CFG = dict(bb=16, c=8, sub=8, rows=1024, s5_steps=8, s5_rows=32)


def kernel(x_prompt, x_sample, state_ret, state_hgrn, state_ssm, state_conv, state_s5_re, state_s5_im,
           norm_mix_pre, norm_mix_post, norm_ffn_pre, norm_ffn_post, w_in_even, w_out_even, ret_norm_w,
           hgrn_lower_bounds, hgrn_norm_w, w_in_odd, conv_w, conv_b, dt_bias, a_log, d_ssm, ssm_norm_w,
           s5_lam_re, s5_lam_im, s5_log_step, s5_b_re, s5_b_im, s5_c_re, s5_c_im, s5_d, w_glu, b_glu,
           w_out_odd, w_ffn_up, w_ffn_down):
    p = dict(norm_mix_pre=norm_mix_pre, norm_mix_post=norm_mix_post, norm_ffn_pre=norm_ffn_pre,
             norm_ffn_post=norm_ffn_post, w_in_even=w_in_even, w_out_even=w_out_even, ret_norm_w=ret_norm_w,
             hgrn_lower_bounds=hgrn_lower_bounds, hgrn_norm_w=hgrn_norm_w, w_in_odd=w_in_odd, conv_w=conv_w,
             conv_b=conv_b, dt_bias=dt_bias, a_log=a_log, d_ssm=d_ssm, ssm_norm_w=ssm_norm_w,
             s5_lam_re=s5_lam_re, s5_lam_im=s5_lam_im, s5_log_step=s5_log_step, s5_b_re=s5_b_re,
             s5_b_im=s5_b_im, s5_c_re=s5_c_re, s5_c_im=s5_c_im, s5_d=s5_d, w_glu=w_glu, b_glu=b_glu,
             w_out_odd=w_out_odd, w_ffn_up=w_ffn_up, w_ffn_down=w_ffn_down)
    past_len = 16384
    states = dict(ret=state_ret, hgrn=state_hgrn, ssm=state_ssm, conv=state_conv,
                  s5_re=state_s5_re, s5_im=state_s5_im)
    y_p, ret_p, hg_p, ssm_p, conv_p, re_p, im_p = _forward(x_prompt, 0, None, p, PROMPT_CFG)
    y_s, ret_s, hg_s, ssm_s, conv_s, re_s, im_s = _forward(x_sample, past_len, states, p, SAMPLE_CFG)
    return (y_p, y_s, ret_p, ret_s, hg_p, hg_s, ssm_p, ssm_s, conv_p, conv_s,
            re_p, re_s, im_p, im_s)
```

```python
import functools
import math

import jax
import jax.numpy as jnp
import numpy as np
from jax import lax
from jax.experimental import pallas as pl
from jax.experimental.pallas import tpu as pltpu

F32 = jnp.float32
BF16 = jnp.bfloat16
EPS = 1e-6
ROPE_BASE = 10000.0
LANES = 128
SUBLANES = 8
VMEM_LIMIT = 56 * 1024 * 1024

H_RET = 4
H_HG = 4
HD = 128
H_SSD = 16
P_SSD = 64
N_SSD = 128
G_SSD = 2
S5_G = 32
S5_GS = 16
S5_P = 64
HG_BLK = 16


def _rms(x, w):
    return x * lax.rsqrt(jnp.mean(x * x, axis=-1, keepdims=True) + EPS) * w


def _silu(x):
    return x * jax.nn.sigmoid(x)


def _dot(a, b):
    return jnp.dot(a.astype(BF16), b.astype(BF16), preferred_element_type=F32)


def _dot_nt(a, b):
    return lax.dot_general(a.astype(BF16), b.astype(BF16), (((1,), (1,)), ((), ())),
                           preferred_element_type=F32)


def _dot_tn(a, b):
    return lax.dot_general(a.astype(BF16), b.astype(BF16), (((0,), (0,)), ((), ())),
                           preferred_element_type=F32)


def _split3(x):
    hi = x.astype(BF16)
    r1 = x - hi.astype(F32)
    mid = r1.astype(BF16)
    lo = (r1 - mid.astype(F32)).astype(BF16)
    return hi, mid, lo


def _dot3(a_exact, parts):
    acc = None
    for p in parts:
        d = jnp.dot(a_exact, p, preferred_element_type=F32)
        acc = d if acc is None else acc + d
    return acc


def _dot3_tn(parts, b_exact):
    acc = None
    for p in parts:
        d = lax.dot_general(p, b_exact, (((0,), (0,)), ((), ())), preferred_element_type=F32)
        acc = d if acc is None else acc + d
    return acc


def _ones_where(mask):
    return jnp.where(mask, 1.0, 0.0).astype(BF16)


def _seq_masks(rows, seq_len):
    sh = seq_len.bit_length() - 1
    ti = lax.broadcasted_iota(jnp.int32, (rows, rows), 0)
    si = lax.broadcasted_iota(jnp.int32, (rows, rows), 1)
    same = lax.shift_right_logical(ti, sh) == lax.shift_right_logical(si, sh)
    return ti, si, same, same & (ti >= si)


def _interleave(units, fillers=(), per_round=2):
    live = list(units)
    fillers = list(fillers)
    while live:
        still = []
        for g in live:
            try:
                next(g)
                still.append(g)
            except StopIteration:
                pass
        live = still
        for f in fillers[:per_round]:
            f()
        fillers = fillers[per_round:]
    for f in fillers:
        f()


def _const_spec(shape):
    nd = len(shape)
    return pl.BlockSpec(shape, lambda *_: (0,) * nd, pipeline_mode=pl.Buffered(1))


def _post_kernel(n_mix, ff_chunk, n_parts, h_ref, *refs):
    mix_refs = refs[:n_mix]
    wout_ref, npost_ref, nfpre_ref, nfpost_ref, wup_ref, wdn_ref, o_ref = refs[n_mix:]
    part = h_ref.shape[0] // n_parts
    groups = [slice(i * part, (i + 1) * part) for i in range(n_parts)]
    accs = []
    for rs in groups:
        acc, off = None, 0
        for m_ref in mix_refs:
            w = m_ref.shape[1]
            d = jnp.dot(m_ref[rs, :], wout_ref[off:off + w, :], preferred_element_type=F32)
            acc = d if acc is None else acc + d
            off += w
        accs.append(acc)
    for rs, acc in zip(groups, accs):
        h1 = h_ref[rs, :] + _rms(acc, npost_ref[...])
        hn = _rms(h1, nfpre_ref[...]).astype(BF16)
        ff = None
        for j in range(wup_ref.shape[1] // ff_chunk):
            sl = slice(j * ff_chunk, (j + 1) * ff_chunk)
            up = jnp.dot(hn, wup_ref[:, sl], preferred_element_type=F32)
            act = jnp.square(jnp.maximum(up, 0.0)).astype(BF16)
            d = jnp.dot(act, wdn_ref[sl, :], preferred_element_type=F32)
            ff = d if ff is None else ff + d
        o_ref[rs, :] = h1 + _rms(ff, nfpost_ref[...])


def _layer_spec(shape, layer):
    nd = len(shape) - 1
    return pl.BlockSpec((None,) + tuple(shape[1:]), lambda *_: (layer,) + (0,) * nd,
                        pipeline_mode=pl.Buffered(1))


def _post_call(h, mixes, wout, npost, nfpre, nfpost, wup, wdn, layer, rows):
    n, d = h.shape
    n_mix = len(mixes)
    row_spec = lambda w: pl.BlockSpec((rows, w), lambda i: (i, 0))
    in_specs = ([row_spec(d)] + [row_spec(m.shape[1]) for m in mixes] + [_const_spec(wout.shape)]
                + [_layer_spec(a.shape, layer) for a in (npost, nfpre, nfpost, wup, wdn)])
    return pl.pallas_call(
        functools.partial(_post_kernel, n_mix, 1024, 2),
        grid=(n // rows,),
        in_specs=in_specs,
        out_specs=row_spec(d),
        out_shape=jax.ShapeDtypeStruct((n, d), F32),
        compiler_params=pltpu.CompilerParams(dimension_semantics=("arbitrary",),
                                             vmem_limit_bytes=VMEM_LIMIT),
        name="post_ffn",
    )(h, *mixes, wout, npost, nfpre, nfpost, wup, wdn)


def _ret_decay_tables(bb, seq_len):
    rows = bb * seq_len
    t = np.arange(rows)
    tl = t % seq_len
    valid = (t[:, None] // seq_len == t[None, :] // seq_len) & (t[:, None] >= t[None, :])
    pair, to_row, to_end = [], [], []
    for h in range(H_RET):
        lg = math.log1p(-(2.0 ** (-5.0 - h)))
        pair.append(np.where(valid, np.exp((t[:, None] - t[None, :]) * lg), 0.0))
        to_row.append(np.broadcast_to(np.exp((tl + 1.0) * lg)[:, None], (rows, HD)))
        to_end.append(np.broadcast_to(np.exp((seq_len - 1.0 - tl) * lg)[:, None], (rows, HD)))
    return tuple(jnp.asarray(np.stack(a), F32) for a in (pair, to_row, to_end))


def _even_block(bb, seq_len, proj_s, cos, sin, lb, retw_ref, hgw_ref, dec_refs, oret_ref, ohg_ref,
                mix_ref):
    rows = bb * seq_len
    hw = H_RET * HD
    base = 4 * hw
    sh = seq_len.bit_length() - 1
    ti, si, same, causal = _seq_masks(rows, seq_len)
    pair_ref, to_row_ref, to_end_ref = dec_refs
    seqs = [slice(b * seq_len, (b + 1) * seq_len) for b in range(bb)]
    one_shot = seq_len <= HG_BLK

    f = lb + (1.0 - lb) * jax.nn.sigmoid(proj_s[:, base + hw:base + 2 * hw])
    lf_parts = _split3(jnp.log(f))
    cum = _dot3(_ones_where(causal), lf_parts)
    if bb == 1:
        dtot = _dot3_tn(lf_parts, jnp.ones((rows, HD), BF16))
        cl = cum[rows - 1:rows, :]
    else:
        r_seq = lax.shift_right_logical(lax.broadcasted_iota(jnp.int32, (rows, HD), 0), sh)
        ind = _ones_where(r_seq == lax.broadcasted_iota(jnp.int32, (rows, HD), 1))
        dtot = _dot3_tn(lf_parts, ind)
        cl = _dot3(_ones_where(same), lf_parts)
    if one_shot:
        refc = _dot3(_ones_where(same & ((si & (seq_len - 1)) < seq_len // 2)), lf_parts)

    yield
    r_a, r_qs, r_kv, r_v = [], [], [], []
    for h in range(H_RET):
        col = lambda off: slice(off + h * HD, off + (h + 1) * HD)
        q = proj_s[:, col(0)]
        k = proj_s[:, col(hw)]
        v = proj_s[:, col(2 * hw)]
        q = q * cos + pltpu.roll(q, HD // 2, axis=1) * sin
        k = (k * cos + pltpu.roll(k, HD // 2, axis=1) * sin) * (HD ** -0.5)
        kd = k * to_end_ref[h]
        r_a.append(_dot_nt(q, k))
        r_qs.append([_dot(q[s], oret_ref[b, h]) for b, s in enumerate(seqs)])
        r_kv.append([_dot_tn(kd[s], v[s]) for s in seqs])
        r_v.append(v)

    g_a, g_qs, g_kv, g_v = [], [], [], []
    for h in range(H_HG):
        hs = slice(h * HD, (h + 1) * HD)
        col = lambda off: slice(base + off + h * HD, base + off + (h + 1) * HD)
        q = proj_s[:, col(0)]
        v = proj_s[:, col(2 * hw)]
        k = 1.0 - f[:, hs]
        cum_h = cum[:, hs]
        qe = q * jnp.exp(cum_h)
        ke = k * jnp.exp(cl[:, hs] - cum_h)
        g_qs.append([_dot(qe[s], ohg_ref[b, h]) for b, s in enumerate(seqs)])
        g_kv.append([_dot_tn(ke[s], v[s]) for s in seqs])
        if one_shot:
            ref = refc[:, hs]
            g_a.append([_dot_nt(q * jnp.exp(cum_h - ref), k * jnp.exp(ref - cum_h))])
        else:
            blocks = []
            for i in range(rows // HG_BLK):
                lo, hi = i * HG_BLK, (i + 1) * HG_BLK
                ref = cum_h[lo + HG_BLK // 2 - 1:lo + HG_BLK // 2, :]
                blocks.append(_dot_nt(q[lo:hi] * jnp.exp(cum_h[lo:hi] - ref),
                                      k[:hi] * jnp.exp(ref - cum_h[:hi])))
            g_a.append(blocks)
        g_v.append(v)

    yield
    r_o = []
    for h in range(H_RET):
        lg = math.log1p(-(2.0 ** (-5.0 - h)))
        qs = r_qs[h][0] if bb == 1 else jnp.concatenate(r_qs[h], axis=0)
        r_o.append(_dot(r_a[h] * pair_ref[h], r_v[h]) + qs * to_row_ref[h])
        for b in range(bb):
            oret_ref[b, h] = oret_ref[b, h] * math.exp(seq_len * lg) + r_kv[h][b]

    g_o = []
    for h in range(H_HG):
        if one_shot:
            o = _dot(jnp.where(causal, g_a[h][0], 0.0), g_v[h])
        else:
            outs = []
            for i in range(rows // HG_BLK):
                lo, hi = i * HG_BLK, (i + 1) * HG_BLK
                tb = lax.broadcasted_iota(jnp.int32, (HG_BLK, hi), 0) + lo
                sb = lax.broadcasted_iota(jnp.int32, (HG_BLK, hi), 1)
                outs.append(_dot(jnp.where(tb >= sb, g_a[h][i], 0.0), g_v[h][:hi]))
            o = jnp.concatenate(outs, axis=0)
        qs = g_qs[h][0] if bb == 1 else jnp.concatenate(g_qs[h], axis=0)
        g_o.append(o + qs)
        for b in range(bb):
            dcol = dtot[h * HD:(h + 1) * HD, :]
            if bb > 1:
                dcol = jnp.broadcast_to(dcol[:, b:b + 1], (HD, HD))
            ohg_ref[b, h] = ohg_ref[b, h] * jnp.exp(dcol) + g_kv[h][b]

    yield
    for h in range(H_RET):
        col = lambda off: slice(off + h * HD, off + (h + 1) * HD)
        mu = jnp.mean(r_o[h], axis=-1, keepdims=True)
        oc = r_o[h] - mu
        var = jnp.mean(oc * oc, axis=-1, keepdims=True)
        o = oc * lax.rsqrt(var + EPS) * retw_ref[:, col(0)] * _silu(proj_s[:, col(3 * hw)])
        mix_ref[:, :, col(0)] = o.reshape(bb, seq_len, HD).astype(mix_ref.dtype)
    for h in range(H_HG):
        hs = slice(h * HD, (h + 1) * HD)
        g = proj_s[:, base + 3 * hw + h * HD:base + 3 * hw + (h + 1) * HD]
        o = _rms(g_o[h], hgw_ref[:, hs]) * _silu(g)
        mix_ref[:, :, hw + h * HD:hw + (h + 1) * HD] = o.reshape(bb, seq_len, HD).astype(mix_ref.dtype)


PROJ_COLS = 512


def _project(x_ref, npre_ref, win_ref, proj_s):
    bb, seq_len, d = x_ref.shape
    hn = _rms(x_ref[...].reshape(bb * seq_len, d), npre_ref[...]).astype(BF16)
    n_in = win_ref.shape[1]

    def chunk(lo):
        def run():
            sl = slice(lo, min(lo + PROJ_COLS, n_in))
            proj_s[:, sl] = jnp.dot(hn, win_ref[:, sl], preferred_element_type=F32)
        return run

    return [chunk(lo) for lo in range(0, n_in, PROJ_COLS)]


def _projection_ahead(x_ref, xn_ref, npre_ref, win_ref, proj_s, proj_next_s):
    @pl.when((pl.program_id(0) == 0) & (pl.program_id(1) == 0))
    def _():
        for f in _project(x_ref, npre_ref, win_ref, proj_s):
            f()

    return _project(xn_ref, npre_ref, win_ref, proj_next_s)


def _carry_projection(proj_s, proj_next_s):
    n_in = proj_s.shape[1]
    for lo in range(0, n_in, PROJ_COLS):
        sl = slice(lo, min(lo + PROJ_COLS, n_in))
        proj_s[:, sl] = proj_next_s[:, sl]


def _next_block_map(n_blocks, nt):
    def imap(b, i):
        s = jnp.minimum(b * nt + i + 1, n_blocks * nt - 1)
        return (s // nt, s % nt, 0)
    return imap


def _even_kernel(has_state, x_ref, xn_ref, cos_ref, sin_ref, npre_ref, win_ref, retw_ref, hgw_ref,
                 lb_ref, pair_ref, to_row_ref, to_end_ref, *refs):
    if has_state:
        sret_ref, shg_ref, mix_ref, oret_ref, ohg_ref, proj_s, proj_next_s = refs
    else:
        mix_ref, oret_ref, ohg_ref, proj_s, proj_next_s = refs
    bb, seq_len, _ = x_ref.shape
    dec_refs = (pair_ref, to_row_ref, to_end_ref)

    @pl.when(pl.program_id(1) == 0)
    def _():
        if has_state:
            oret_ref[...] = sret_ref[...]
            ohg_ref[...] = shg_ref[...]
        else:
            oret_ref[...] = jnp.zeros(oret_ref.shape, F32)
            ohg_ref[...] = jnp.zeros(ohg_ref.shape, F32)

    fillers = _projection_ahead(x_ref, xn_ref, npre_ref, win_ref, proj_s, proj_next_s)
    consts = (lb_ref[...], retw_ref, hgw_ref, dec_refs)
    if seq_len > HG_BLK:
        units = []
        for u in range(bb):
            rs, one = pl.ds(u * seq_len, seq_len), pl.ds(u, 1)
            units.append(_even_block(1, seq_len, proj_s.at[rs], cos_ref[rs, :], sin_ref[rs, :], *consts,
                                     oret_ref.at[one], ohg_ref.at[one], mix_ref.at[one]))
    else:
        units = [_even_block(bb, seq_len, proj_s, cos_ref[...], sin_ref[...], *consts,
                             oret_ref, ohg_ref, mix_ref)]
    _interleave(units, fillers)
    _carry_projection(proj_s, proj_next_s)


def _even_call(x, cos, sin, npre, win, retw, hgw, lb, states, bb, c):
    nb_total, t, d = x.shape
    has_state = states is not None
    n_in = win.shape[1]
    rows = bb * c
    grid = (nb_total // bb, t // c)
    st_spec = pl.BlockSpec((bb, H_RET, HD, HD), lambda b, i: (b, 0, 0, 0))
    in_specs = [pl.BlockSpec((bb, c, d), lambda b, i: (b, i, 0)),
                pl.BlockSpec((bb, c, d), _next_block_map(*grid)),
                pl.BlockSpec((rows, HD), lambda b, i: (i, 0)),
                pl.BlockSpec((rows, HD), lambda b, i: (i, 0)),
                _const_spec((1, d)), _const_spec(win.shape),
                _const_spec((1, H_RET * HD)), _const_spec((1, H_HG * HD)), _const_spec((1, H_HG * HD))]
    tables = _ret_decay_tables(bb if c <= HG_BLK else 1, c)
    in_specs += [_const_spec(a.shape) for a in tables]
    args = [x, x, cos, sin, npre, win, retw, hgw, lb, *tables]
    if has_state:
        in_specs += [st_spec, st_spec]
        args += list(states)
    mix_w = (H_RET + H_HG) * HD
    return pl.pallas_call(
        functools.partial(_even_kernel, has_state),
        grid=grid,
        in_specs=in_specs,
        out_specs=[pl.BlockSpec((bb, c, mix_w), lambda b, i: (b, i, 0)), st_spec, st_spec],
        out_shape=[jax.ShapeDtypeStruct((nb_total, t, mix_w), BF16),
                   jax.ShapeDtypeStruct((nb_total, H_RET, HD, HD), F32),
                   jax.ShapeDtypeStruct((nb_total, H_HG, HD, HD), F32)],
        scratch_shapes=[pltpu.VMEM((rows, n_in), F32), pltpu.VMEM((rows, n_in), F32)],
        compiler_params=pltpu.CompilerParams(dimension_semantics=("arbitrary", "arbitrary"),
                                             vmem_limit_bytes=VMEM_LIMIT),
        name="even_mixer",
    )(*args)


def _odd_block(has_state, bb, seq_len, proj_s, tail_s, ost_ref, sst_s, convw_ref, convb_ref, dtb_ref,
               alog_ref, dssm_ref, ssmw_ref, y_ref, u_ref):
    rows = bb * seq_len
    d_inner = H_SSD * P_SSD
    conv_dim = d_inner + 2 * G_SSD * N_SSD
    conv_w = convw_ref.shape[0]
    halo = conv_w - 1
    u_w = u_ref.shape[2]
    gw = d_inner // G_SSD
    heads_per_group = H_SSD // G_SSD
    pairs_per_group = heads_per_group // 2
    seqs = [slice(b * seq_len, (b + 1) * seq_len) for b in range(bb)]

    z = proj_s[:, 0:d_inner]
    xbc = proj_s[:, d_inner:d_inner + conv_dim]
    c0 = d_inner + conv_dim
    dt_raw = proj_s[:, c0:c0 + LANES]
    u_ref[...] = proj_s[:, c0 + H_SSD:c0 + H_SSD + u_w].reshape(bb, seq_len, u_w)

    tail = tail_s[...]
    conv = convb_ref[...] + xbc * convw_ref[halo:halo + 1, :]
    for s in range(1, conv_w):
        rolled = pltpu.roll(xbc, s, axis=0)
        if seq_len == SUBLANES:
            local = lax.broadcasted_iota(jnp.int32, (rows, conv_dim), 0) & (SUBLANES - 1)
            shifted = jnp.where(local < s, pltpu.roll(tail, rows - SUBLANES + s, axis=0), rolled)
        else:
            local = lax.broadcasted_iota(jnp.int32, (SUBLANES, conv_dim), 0)
            head = jnp.where(local < s, pltpu.roll(tail, s, axis=0), rolled[0:SUBLANES])
            shifted = jnp.concatenate([head, rolled[SUBLANES:]], axis=0)
        conv = conv + shifted * convw_ref[halo - s:halo - s + 1, :]
    tail_s[...] = xbc if seq_len == SUBLANES else xbc[rows - SUBLANES:rows]
    act = _silu(conv)
    xs = act[:, :d_inner]

    ti, si, same, causal = _seq_masks(rows, seq_len)
    lo_half = lax.broadcasted_iota(jnp.int32, (rows, LANES), 1) < P_SSD
    dt = jax.nn.softplus(dt_raw + dtb_ref[...])
    la_parts = _split3(dt * (-jnp.exp(alog_ref[...])))
    cum = _dot3(_ones_where(causal), la_parts)
    cum_t = _dot3_tn(la_parts, _ones_where(same & (ti <= si)))
    if bb > 1:
        rest = _dot3(_ones_where(same & (si > ti)), la_parts)
    bms = [act[:, d_inner + g * N_SSD:d_inner + (g + 1) * N_SSD] for g in range(G_SSD)]
    cms = [act[:, d_inner + (G_SSD + g) * N_SSD:d_inner + (G_SSD + g + 1) * N_SSD]
           for g in range(G_SSD)]
    scores = [_dot_nt(cms[g], bms[g]) for g in range(G_SSD)]
    yield

    bc = lambda a, h: jnp.broadcast_to(a[:, h:h + 1], (rows, LANES))
    bc_cum = [bc(cum, h) for h in range(H_SSD)]
    pair = lambda a, p: jnp.where(lo_half, bc(a, 2 * p), bc(a, 2 * p + 1))

    for g in range(G_SSD):
        xdts, xws, cums = [], [], []
        for pp in range(pairs_per_group):
            p = g * pairs_per_group + pp
            xdt = xs[:, p * LANES:(p + 1) * LANES] * pair(dt, p)
            cum_p = jnp.where(lo_half, bc_cum[2 * p], bc_cum[2 * p + 1])
            rest_p = cum_p[rows - 1:rows, :] - cum_p if bb == 1 else pair(rest, p)
            xdts.append(xdt)
            xws.append(xdt * jnp.exp(rest_p))
            cums.append(cum_p)
        xw = jnp.concatenate(xws, axis=1)
        if has_state:
            y_st = [_dot_nt(cms[g][s], ost_ref[b, g]) for b, s in enumerate(seqs)]
            kv = [_dot_tn(xw[s], bms[g][s]) for s in seqs]
        else:
            y_st = [_dot(cms[g][s], sst_s[b, g]) for b, s in enumerate(seqs)]
            kv = [_dot_tn(bms[g][s], xw[s]) for s in seqs]
        y_intra = []
        for pp in range(pairs_per_group):
            p = g * pairs_per_group + pp
            a_pair = []
            for h in (2 * p, 2 * p + 1):
                diff = bc_cum[h][:, :rows] - cum_t[h:h + 1, :]
                a_pair.append(scores[g] * jnp.where(causal, jnp.exp(jnp.where(causal, diff, 0.0)), 0.0))
            both = _dot(jnp.concatenate(a_pair, axis=0), xdts[pp])
            y_intra.append(jnp.where(lo_half, both[:rows], both[rows:]))
        yield
        for b in range(bb):
            last = (b + 1) * seq_len - 1
            if has_state:
                dec = [jnp.broadcast_to(jnp.exp(bc_cum[g * heads_per_group + hh][last:last + 1, :]),
                                        (P_SSD, N_SSD)) for hh in range(heads_per_group)]
                ost_ref[b, g] = ost_ref[b, g] * jnp.concatenate(dec, axis=0) + kv[b]
            else:
                dec = jnp.exp(jnp.concatenate([c_p[last:last + 1, :] for c_p in cums], axis=1))
                sst_s[b, g] = sst_s[b, g] * dec + kv[b]
        y_state = y_st[0] if bb == 1 else jnp.concatenate(y_st, axis=0)
        gs = slice(g * gw, (g + 1) * gw)
        y = (jnp.concatenate(y_intra, axis=1) + y_state * jnp.exp(jnp.concatenate(cums, axis=1))
             + dssm_ref[:, gs] * xs[:, gs])
        y = _rms(y * _silu(z[:, gs]), ssmw_ref[:, gs])
        y_ref[:, :, gs] = y.reshape(bb, seq_len, gw).astype(y_ref.dtype)


def _odd_kernel(has_state, x_ref, xn_ref, npre_ref, win_ref, convw_ref, convb_ref, dtb_ref, alog_ref,
                dssm_ref, ssmw_ref, *refs):
    if has_state:
        sconv_ref, sst_ref, y_ref, u_ref, oconv_ref, ost_ref, proj_s, proj_next_s, tail_s = refs
        sst_s = None
    else:
        y_ref, u_ref, oconv_ref, ost_ref, proj_s, proj_next_s, tail_s, sst_s = refs
    bb, seq_len, _ = x_ref.shape
    halo = convw_ref.shape[0] - 1

    @pl.when(pl.program_id(1) == 0)
    def _():
        tail_s[...] = jnp.zeros(tail_s.shape, F32)
        if has_state:
            for b in range(bb):
                tail_s[(b + 1) * SUBLANES - halo:(b + 1) * SUBLANES, :] = sconv_ref[b]
            ost_ref[...] = sst_ref[...]
        else:
            sst_s[...] = jnp.zeros(sst_s.shape, F32)

    fillers = _projection_ahead(x_ref, xn_ref, npre_ref, win_ref, proj_s, proj_next_s)
    consts = (convw_ref, convb_ref, dtb_ref, alog_ref, dssm_ref, ssmw_ref)
    if seq_len > SUBLANES:
        units = []
        for u in range(bb):
            one = pl.ds(u, 1)
            units.append(_odd_block(
                has_state, 1, seq_len, proj_s.at[pl.ds(u * seq_len, seq_len)],
                tail_s.at[pl.ds(u * SUBLANES, SUBLANES)], ost_ref.at[one],
                None if sst_s is None else sst_s.at[one], *consts, y_ref.at[one], u_ref.at[one]))
    else:
        units = [_odd_block(has_state, bb, seq_len, proj_s, tail_s, ost_ref, sst_s, *consts, y_ref, u_ref)]
    _interleave(units, fillers, per_round=3)
    _carry_projection(proj_s, proj_next_s)

    @pl.when(pl.program_id(1) == pl.num_programs(1) - 1)
    def _():
        for b in range(bb):
            oconv_ref[b] = tail_s[(b + 1) * SUBLANES - halo:(b + 1) * SUBLANES, :]
            if not has_state:
                for g in range(G_SSD):
                    for ps in range(0, ost_ref.shape[2], LANES):
                        ost_ref[b, g, ps:ps + LANES, :] = sst_s[b, g, :, ps:ps + LANES].T


def _odd_call(x, npre, win, convw, convb, dtb, alog, dssm, ssmw, states, bb, c):
    nb_total, t, d = x.shape
    has_state = states is not None
    n_in = win.shape[1]
    d_inner = H_SSD * P_SSD
    conv_dim = d_inner + 2 * G_SSD * N_SSD
    u_w = S5_G * S5_GS
    halo = convw.shape[0] - 1
    rows = bb * c
    grid = (nb_total // bb, t // c)
    st_shape = (G_SSD, d_inner // G_SSD, N_SSD)
    conv_spec = pl.BlockSpec((bb, halo, conv_dim), lambda b, i: (b, 0, 0))
    ssm_spec = pl.BlockSpec((bb,) + st_shape, lambda b, i: (b, 0, 0, 0))
    in_specs = [pl.BlockSpec((bb, c, d), lambda b, i: (b, i, 0)),
                pl.BlockSpec((bb, c, d), _next_block_map(*grid)),
                _const_spec((1, d)), _const_spec(win.shape), _const_spec(convw.shape),
                _const_spec((1, conv_dim)), _const_spec((1, LANES)), _const_spec((1, LANES)),
                _const_spec((1, d_inner)), _const_spec((1, d_inner))]
    args = [x, x, npre, win, convw, convb, dtb, alog, dssm, ssmw]
    scratch = [pltpu.VMEM((rows, n_in), F32), pltpu.VMEM((rows, n_in), F32),
               pltpu.VMEM((bb * SUBLANES, conv_dim), F32)]
    if has_state:
        in_specs += [conv_spec, ssm_spec]
        args += list(states)
    else:
        scratch.append(pltpu.VMEM((bb, G_SSD, N_SSD, d_inner // G_SSD), F32))
    return pl.pallas_call(
        functools.partial(_odd_kernel, has_state),
        grid=grid,
        in_specs=in_specs,
        out_specs=[pl.BlockSpec((bb, c, d_inner), lambda b, i: (b, i, 0)),
                   pl.BlockSpec((bb, c, u_w), lambda b, i: (b, i, 0)),
                   conv_spec, ssm_spec],
        out_shape=[jax.ShapeDtypeStruct((nb_total, t, d_inner), BF16),
                   jax.ShapeDtypeStruct((nb_total, t, u_w), F32),
                   jax.ShapeDtypeStruct((nb_total, halo, conv_dim), F32),
                   jax.ShapeDtypeStruct((nb_total,) + st_shape, F32)],
        scratch_shapes=scratch,
        compiler_params=pltpu.CompilerParams(dimension_semantics=("arbitrary", "arbitrary"),
                                             vmem_limit_bytes=VMEM_LIMIT),
        name="odd_mixer",
    )(*args)


S5_PARTS = 4
S5_PART_IN = S5_G * S5_GS // S5_PARTS
S5_PART_ST = S5_G * S5_P // S5_PARTS
S5_SCAN_VREGS = 4


def _s5_pitch(steps):
    return steps if steps % 16 == 8 else steps + 8


def _s5_kernel(u_ref, bmat_ref, cmat_ref, lre_ref, lim_ref, d_ref, wglu_ref, bglu_ref,
               h0re_ref, h0im_ref, yd_ref, ore_ref, oim_ref, buf_s, slab_s, ut_s):
    r, steps, uw = u_ref.shape
    hs = S5_PART_ST
    scan_w = min(hs, max(LANES, S5_SCAN_VREGS * SUBLANES * LANES // r))
    pitch = _s5_pitch(steps)
    nslab = uw // LANES

    @pl.when(pl.program_id(1) == 0)
    def _():
        ore_ref[...] = h0re_ref[...]
        oim_ref[...] = h0im_ref[...]

    for b in range(r):
        for j in range(nslab):
            slab_s[j, b * pitch:b * pitch + steps, :] = u_ref[b, :, j * LANES:(j + 1) * LANES]
    for t in range(steps):
        for j in range(nslab):
            ut_s[t * r:(t + 1) * r, j * LANES:(j + 1) * LANES] = slab_s[j, pl.ds(t, r, stride=pitch), :]
    u = ut_s[...]
    ub = u.astype(BF16)
    for half in range(S5_PARTS):
        buf_s[:, half * 2 * hs:(half + 1) * 2 * hs] = jnp.dot(
            ub[:, half * S5_PART_IN:(half + 1) * S5_PART_IN], bmat_ref[half],
            preferred_element_type=F32)

    for half in range(S5_PARTS):
        for j in range(hs // scan_w):
            st = slice(half * hs + j * scan_w, half * hs + (j + 1) * scan_w)
            cre = half * 2 * hs + j * scan_w
            cim = cre + hs
            lre = jnp.broadcast_to(lre_ref[:, st], (r, scan_w))
            lim = jnp.broadcast_to(lim_ref[:, st], (r, scan_w))
            hre, him = ore_ref[:, st], oim_ref[:, st]
            for t in range(steps):
                rws = slice(t * r, (t + 1) * r)
                hre, him = (lre * hre - lim * him + buf_s[rws, cre:cre + scan_w],
                            lre * him + lim * hre + buf_s[rws, cim:cim + scan_w])
                buf_s[rws, cre:cre + scan_w] = hre
                buf_s[rws, cim:cim + scan_w] = him
            ore_ref[:, st] = hre
            oim_ref[:, st] = him

    ys = []
    for half in range(S5_PARTS):
        ys.append(jnp.dot(buf_s[:, half * 2 * hs:(half + 1) * 2 * hs].astype(BF16), cmat_ref[half],
                          preferred_element_type=F32))
    y = jnp.concatenate(ys, axis=1) + d_ref[...] * u
    g = jax.nn.gelu(y)
    gate = jax.nn.sigmoid(jnp.dot(g.astype(BF16), wglu_ref[...], preferred_element_type=F32)
                          + bglu_ref[...])
    yd = g * gate
    for t in range(steps):
        for j in range(nslab):
            slab_s[j, pl.ds(t, r, stride=pitch), :] = yd[t * r:(t + 1) * r, j * LANES:(j + 1) * LANES]
    for b in range(r):
        for j in range(nslab):
            yd_ref[b, :, j * LANES:(j + 1) * LANES] = slab_s[j, b * pitch:b * pitch + steps, :].astype(
                yd_ref.dtype)


def _s5_call(u, bmat, cmat, lre, lim, dvec, wglu, bglu, h0re, h0im, steps, r):
    nb_total, t, uw = u.shape
    ns = S5_G * S5_P
    grid = (nb_total // r, t // steps)
    st_spec = pl.BlockSpec((r, ns), lambda b, i: (b, 0))
    return pl.pallas_call(
        _s5_kernel,
        grid=grid,
        in_specs=[pl.BlockSpec((r, steps, uw), lambda b, i: (b, i, 0)),
                  _const_spec(bmat.shape), _const_spec(cmat.shape),
                  _const_spec((1, ns)), _const_spec((1, ns)), _const_spec((1, uw)),
                  _const_spec(wglu.shape), _const_spec((1, uw)), st_spec, st_spec],
        out_specs=[pl.BlockSpec((r, steps, uw), lambda b, i: (b, i, 0)), st_spec, st_spec],
        out_shape=[jax.ShapeDtypeStruct((nb_total, t, uw), BF16),
                   jax.ShapeDtypeStruct((nb_total, ns), F32),
                   jax.ShapeDtypeStruct((nb_total, ns), F32)],
        scratch_shapes=[pltpu.VMEM((steps * r, 2 * ns), F32),
                        pltpu.VMEM((uw // LANES, r * _s5_pitch(steps), LANES), F32),
                        pltpu.VMEM((steps * r, uw), F32)],
        compiler_params=pltpu.CompilerParams(dimension_semantics=("arbitrary", "arbitrary"),
                                             vmem_limit_bytes=VMEM_LIMIT),
        name="s5_mixer",
    )(u, bmat, cmat, lre, lim, dvec, wglu, bglu, h0re, h0im)


def _rope_tables(pos):
    half = HD // 2
    inv_freq = ROPE_BASE ** (-jnp.arange(half, dtype=F32) / half)
    ang = pos[:, None] * inv_freq[None, :]
    cos, sin = jnp.cos(ang), jnp.sin(ang)
    return jnp.concatenate([cos, cos], axis=-1), jnp.concatenate([-sin, sin], axis=-1)


def _pad_lanes(v):
    return jnp.pad(v.astype(F32), (0, LANES - v.shape[0]))[None, :]


def _s5_tables(lam_re, lam_im, log_step, b_re, b_im, c_re, c_im):
    lr, li = lam_re.astype(F32), lam_im.astype(F32)
    dt = jnp.exp(log_step.astype(F32))[:, None]
    mag = jnp.exp(lr * dt)
    bar_re, bar_im = mag * jnp.cos(li * dt), mag * jnp.sin(li * dt)
    den = lr * lr + li * li
    cf_re = ((bar_re - 1.0) * lr + bar_im * li) / den
    cf_im = (bar_im * lr - (bar_re - 1.0) * li) / den
    bb_re = cf_re[..., None] * b_re.astype(F32) - cf_im[..., None] * b_im.astype(F32)
    bb_im = cf_re[..., None] * b_im.astype(F32) + cf_im[..., None] * b_re.astype(F32)
    gh = S5_G // S5_PARTS
    eye = jnp.eye(gh, dtype=BF16)
    b_ri = jnp.stack([bb_re, bb_im]).astype(BF16).reshape(2, S5_PARTS, gh, S5_P, S5_GS)
    bmat = jnp.einsum('rqgpc,gk->qgcrkp', b_ri, eye).reshape(S5_PARTS, gh * S5_GS, 2 * gh * S5_P)
    c_ri = jnp.stack([c_re.astype(F32), -c_im.astype(F32)]).astype(BF16).reshape(
        2, S5_PARTS, gh, S5_GS, S5_P)
    cmat = jnp.einsum('rqgcp,gk->qrgpkc', c_ri, eye).reshape(S5_PARTS, 2 * gh * S5_P, gh * S5_GS)
    return bmat, cmat, bar_re.reshape(1, -1), bar_im.reshape(1, -1)


def _row(v):
    return v.astype(F32).reshape(1, -1)


def _forward(x, pos_offset, states, p, cfg):
    nb, t, d = x.shape
    n = nb * t
    bb, c = cfg['bb'], cfg['c']
    has_state = states is not None
    pos = jnp.arange(pos_offset, pos_offset + t, dtype=F32)
    cos, sin = _rope_tables(pos)
    tile_rows = lambda a: jnp.tile(a.reshape(t // c, 1, c, HD), (1, bb, 1, 1)).reshape(-1, HD)
    lb = jnp.cumsum(jax.nn.softmax(p['hgrn_lower_bounds'].astype(F32), axis=0), axis=0)[0]
    mix, ret_o, hg_o = _even_call(
        x, tile_rows(cos), tile_rows(sin), _row(p['norm_mix_pre'][0]), p['w_in_even'][0].astype(BF16),
        _row(p['ret_norm_w'][0]), _row(p['hgrn_norm_w'][0]), _row(lb),
        (states['ret'][0], states['hgrn'][0]) if has_state else None, bb, c)
    stacked_rows = lambda a: a.astype(F32)[:, None, :]
    ffn = (stacked_rows(p['norm_mix_post']), stacked_rows(p['norm_ffn_pre']),
           stacked_rows(p['norm_ffn_post']), p['w_ffn_up'].astype(BF16), p['w_ffn_down'].astype(BF16))
    h = _post_call(x.reshape(n, d), [mix.reshape(n, -1)], p['w_out_even'][0].astype(BF16), *ffn, 0,
                   cfg['rows'])
    d_inner = H_SSD * P_SSD
    conv_dim = d_inner + 2 * G_SSD * N_SSD
    u_w = S5_G * S5_GS
    w1 = p['w_in_odd'][0].astype(BF16)
    st_shape = (nb, G_SSD, d_inner // G_SSD, N_SSD)
    odd_states = None
    if has_state:
        odd_states = (states['conv'][0], jnp.swapaxes(states['ssm'][0], -1, -2).reshape(st_shape))
    y, u, conv_o, sst_o = _odd_call(
        h.reshape(nb, t, d), _row(p['norm_mix_pre'][1]), w1, p['conv_w'][0].astype(F32),
        _row(p['conv_b'][0]), _pad_lanes(p['dt_bias'][0]), _pad_lanes(p['a_log'][0]),
        _row(jnp.repeat(p['d_ssm'][0], P_SSD)), _row(p['ssm_norm_w'][0]), odd_states, bb, c)
    ssm_o = jnp.swapaxes(sst_o.reshape(nb, H_SSD, P_SSD, N_SSD), -1, -2)
    bmat, cmat, lre, lim = _s5_tables(p['s5_lam_re'][0], p['s5_lam_im'][0], p['s5_log_step'][0],
                                      p['s5_b_re'][0], p['s5_b_im'][0], p['s5_c_re'][0],
                                      p['s5_c_im'][0])
    ns = S5_G * S5_P
    if has_state:
        h0re = states['s5_re'][0].reshape(nb, ns)
        h0im = states['s5_im'][0].reshape(nb, ns)
    else:
        h0re = jnp.zeros((nb, ns), F32)
        h0im = jnp.zeros((nb, ns), F32)
    yd, re_o, im_o = _s5_call(u, bmat, cmat, lre, lim, _row(p['s5_d'][0]),
                              p['w_glu'][0].astype(BF16), _row(p['b_glu'][0]), h0re, h0im,
                              cfg['s5_steps'], cfg['s5_rows'])
    yd = yd.reshape(n, u_w)
    h = _post_call(h, [y.reshape(n, d_inner), yd], p['w_out_odd'][0].astype(BF16), *ffn, 1, cfg['rows'])
    return (h.reshape(nb, t, d), ret_o[None], hg_o[None], ssm_o[None], conv_o[None],
            re_o.reshape(1, nb, S5_G, S5_P), im_o.reshape(1, nb, S5_G, S5_P))


PROMPT_CFG = dict(bb=4, c=128, rows=1024, s5_steps=64, s5_rows=8)
SAMPLE_CFG = dict(bb=16, c=8, rows=1024, s5_steps=8, s5_rows=32)


def kernel(x_prompt, x_sample, state_ret, state_hgrn, state_ssm, state_conv, state_s5_re, state_s5_im,
           norm_mix_pre, norm_mix_post, norm_ffn_pre, norm_ffn_post, w_in_even, w_out_even, ret_norm_w,
           hgrn_lower_bounds, hgrn_norm_w, w_in_odd, conv_w, conv_b, dt_bias, a_log, d_ssm, ssm_norm_w,
           s5_lam_re, s5_lam_im, s5_log_step, s5_b_re, s5_b_im, s5_c_re, s5_c_im, s5_d, w_glu, b_glu,
           w_out_odd, w_ffn_up, w_ffn_down):
    p = dict(norm_mix_pre=norm_mix_pre, norm_mix_post=norm_mix_post, norm_ffn_pre=norm_ffn_pre,
             norm_ffn_post=norm_ffn_post, w_in_even=w_in_even, w_out_even=w_out_even, ret_norm_w=ret_norm_w,
             hgrn_lower_bounds=hgrn_lower_bounds, hgrn_norm_w=hgrn_norm_w, w_in_odd=w_in_odd, conv_w=conv_w,
             conv_b=conv_b, dt_bias=dt_bias, a_log=a_log, d_ssm=d_ssm, ssm_norm_w=ssm_norm_w,
             s5_lam_re=s5_lam_re, s5_lam_im=s5_lam_im, s5_log_step=s5_log_step, s5_b_re=s5_b_re,
             s5_b_im=s5_b_im, s5_c_re=s5_c_re, s5_c_im=s5_c_im, s5_d=s5_d, w_glu=w_glu, b_glu=b_glu,
             w_out_odd=w_out_odd, w_ffn_up=w_ffn_up, w_ffn_down=w_ffn_down)
    past_len = 16384
    states = dict(ret=state_ret, hgrn=state_hgrn, ssm=state_ssm, conv=state_conv,
                  s5_re=state_s5_re, s5_im=state_s5_im)
    y_p, ret_p, hg_p, ssm_p, conv_p, re_p, im_p = _forward(x_prompt, 0, None, p, PROMPT_CFG)
    y_s, ret_s, hg_s, ssm_s, conv_s, re_s, im_s = _forward(x_sample, past_len, states, p, SAMPLE_CFG)
    return (y_p, y_s, ret_p, ret_s, hg_p, hg_s, ssm_p, ssm_s, conv_p, conv_s,
            re_p, re_s, im_p, im_s)
```

```python
import functools
import math

import jax
import jax.numpy as jnp
import numpy as np
from jax import lax
from jax.experimental import pallas as pl
from jax.experimental.pallas import tpu as pltpu

F32 = jnp.float32
BF16 = jnp.bfloat16
EPS = 1e-6
ROPE_BASE = 10000.0
LANES = 128
SUBLANES = 8
VMEM_LIMIT = 56 * 1024 * 1024

H_RET = 4
H_HG = 4
HD = 128
H_SSD = 16
P_SSD = 64
N_SSD = 128
G_SSD = 2
S5_G = 32
S5_GS = 16
S5_P = 64
HG_BLK = 16


def _rms(x, w):
    return x * lax.rsqrt(jnp.mean(x * x, axis=-1, keepdims=True) + EPS) * w


def _silu(x):
    return x * jax.nn.sigmoid(x)


def _dot(a, b):
    return jnp.dot(a.astype(BF16), b.astype(BF16), preferred_element_type=F32)


def _dot_nt(a, b):
    return lax.dot_general(a.astype(BF16), b.astype(BF16), (((1,), (1,)), ((), ())),
                           preferred_element_type=F32)


def _dot_tn(a, b):
    return lax.dot_general(a.astype(BF16), b.astype(BF16), (((0,), (0,)), ((), ())),
                           preferred_element_type=F32)


def _split3(x):
    hi = x.astype(BF16)
    r1 = x - hi.astype(F32)
    mid = r1.astype(BF16)
    lo = (r1 - mid.astype(F32)).astype(BF16)
    return hi, mid, lo


def _dot3(a_exact, parts):
    acc = None
    for p in parts:
        d = jnp.dot(a_exact, p, preferred_element_type=F32)
        acc = d if acc is None else acc + d
    return acc


def _dot3_tn(parts, b_exact):
    acc = None
    for p in parts:
        d = lax.dot_general(p, b_exact, (((0,), (0,)), ((), ())), preferred_element_type=F32)
        acc = d if acc is None else acc + d
    return acc


def _ones_where(mask):
    return jnp.where(mask, 1.0, 0.0).astype(BF16)


def _seq_masks(rows, seq_len):
    sh = seq_len.bit_length() - 1
    ti = lax.broadcasted_iota(jnp.int32, (rows, rows), 0)
    si = lax.broadcasted_iota(jnp.int32, (rows, rows), 1)
    same = lax.shift_right_logical(ti, sh) == lax.shift_right_logical(si, sh)
    return ti, si, same, same & (ti >= si)


def _interleave(units, fillers=(), per_round=2):
    live = list(units)
    fillers = list(fillers)
    while live:
        still = []
        for g in live:
            try:
                next(g)
                still.append(g)
            except StopIteration:
                pass
        live = still
        for f in fillers[:per_round]:
            f()
        fillers = fillers[per_round:]
    for f in fillers:
        f()


def _const_spec(shape):
    nd = len(shape)
    return pl.BlockSpec(shape, lambda *_: (0,) * nd, pipeline_mode=pl.Buffered(1))


def _post_kernel(n_mix, ff_chunk, n_parts, h_ref, *refs):
    mix_refs = refs[:n_mix]
    wout_ref, npost_ref, nfpre_ref, nfpost_ref, wup_ref, wdn_ref, o_ref = refs[n_mix:]
    part = h_ref.shape[0] // n_parts
    groups = [slice(i * part, (i + 1) * part) for i in range(n_parts)]
    accs = []
    for rs in groups:
        acc, off = None, 0
        for m_ref in mix_refs:
            w = m_ref.shape[1]
            d = jnp.dot(m_ref[rs, :], wout_ref[off:off + w, :], preferred_element_type=F32)
            acc = d if acc is None else acc + d
            off += w
        accs.append(acc)
    for rs, acc in zip(groups, accs):
        h1 = h_ref[rs, :] + _rms(acc, npost_ref[...])
        hn = _rms(h1, nfpre_ref[...]).astype(BF16)
        ff = None
        for j in range(wup_ref.shape[1] // ff_chunk):
            sl = slice(j * ff_chunk, (j + 1) * ff_chunk)
            up = jnp.dot(hn, wup_ref[:, sl], preferred_element_type=F32)
            act = jnp.square(jnp.maximum(up, 0.0)).astype(BF16)
            d = jnp.dot(act, wdn_ref[sl, :], preferred_element_type=F32)
            ff = d if ff is None else ff + d
        o_ref[rs, :] = h1 + _rms(ff, nfpost_ref[...])


def _layer_spec(shape, layer):
    nd = len(shape) - 1
    return pl.BlockSpec((None,) + tuple(shape[1:]), lambda *_: (layer,) + (0,) * nd,
                        pipeline_mode=pl.Buffered(1))


def _post_call(h, mixes, wout, npost, nfpre, nfpost, wup, wdn, layer, rows):
    n, d = h.shape
    n_mix = len(mixes)
    row_spec = lambda w: pl.BlockSpec((rows, w), lambda i: (i, 0))
    in_specs = ([row_spec(d)] + [row_spec(m.shape[1]) for m in mixes] + [_const_spec(wout.shape)]
                + [_layer_spec(a.shape, layer) for a in (npost, nfpre, nfpost, wup, wdn)])
    return pl.pallas_call(
        functools.partial(_post_kernel, n_mix, 1024, 2),
        grid=(n // rows,),
        in_specs=in_specs,
        out_specs=row_spec(d),
        out_shape=jax.ShapeDtypeStruct((n, d), F32),
        compiler_params=pltpu.CompilerParams(dimension_semantics=("arbitrary",),
                                             vmem_limit_bytes=VMEM_LIMIT),
        name="post_ffn",
    )(h, *mixes, wout, npost, nfpre, nfpost, wup, wdn)


def _ret_decay_tables(bb, seq_len):
    rows = bb * seq_len
    t = np.arange(rows)
    tl = t % seq_len
    valid = (t[:, None] // seq_len == t[None, :] // seq_len) & (t[:, None] >= t[None, :])
    pair, to_row, to_end = [], [], []
    for h in range(H_RET):
        lg = math.log1p(-(2.0 ** (-5.0 - h)))
        pair.append(np.where(valid, np.exp((t[:, None] - t[None, :]) * lg), 0.0))
        to_row.append(np.broadcast_to(np.exp((tl + 1.0) * lg)[:, None], (rows, HD)))
        to_end.append(np.broadcast_to(np.exp((seq_len - 1.0 - tl) * lg)[:, None], (rows, HD)))
    return tuple(jnp.asarray(np.stack(a), F32) for a in (pair, to_row, to_end))


def _even_block(bb, seq_len, proj_s, cos, sin, lb, retw_ref, hgw_ref, dec_refs, oret_ref, ohg_ref,
                mix_ref):
    rows = bb * seq_len
    hw = H_RET * HD
    base = 4 * hw
    sh = seq_len.bit_length() - 1
    ti, si, same, causal = _seq_masks(rows, seq_len)
    pair_ref, to_row_ref, to_end_ref = dec_refs
    seqs = [slice(b * seq_len, (b + 1) * seq_len) for b in range(bb)]
    one_shot = seq_len <= HG_BLK

    f = lb + (1.0 - lb) * jax.nn.sigmoid(proj_s[:, base + hw:base + 2 * hw])
    lf_parts = _split3(jnp.log(f))
    cum = _dot3(_ones_where(causal), lf_parts)
    if bb == 1:
        dtot = _dot3_tn(lf_parts, jnp.ones((rows, HD), BF16))
        cl = cum[rows - 1:rows, :]
    else:
        r_seq = lax.shift_right_logical(lax.broadcasted_iota(jnp.int32, (rows, HD), 0), sh)
        ind = _ones_where(r_seq == lax.broadcasted_iota(jnp.int32, (rows, HD), 1))
        dtot = _dot3_tn(lf_parts, ind)
        cl = _dot3(_ones_where(same), lf_parts)
    if one_shot:
        refc = _dot3(_ones_where(same & ((si & (seq_len - 1)) < seq_len // 2)), lf_parts)

    yield
    r_a, r_qs, r_kv, r_v = [], [], [], []
    for h in range(H_RET):
        col = lambda off: slice(off + h * HD, off + (h + 1) * HD)
        q = proj_s[:, col(0)]
        k = proj_s[:, col(hw)]
        v = proj_s[:, col(2 * hw)]
        q = q * cos + pltpu.roll(q, HD // 2, axis=1) * sin
        k = (k * cos + pltpu.roll(k, HD // 2, axis=1) * sin) * (HD ** -0.5)
        kd = k * to_end_ref[h]
        r_a.append(_dot_nt(q, k))
        r_qs.append([_dot(q[s], oret_ref[b, h]) for b, s in enumerate(seqs)])
        r_kv.append([_dot_tn(kd[s], v[s]) for s in seqs])
        r_v.append(v)

    g_a, g_qs, g_kv, g_v = [], [], [], []
    for h in range(H_HG):
        hs = slice(h * HD, (h + 1) * HD)
        col = lambda off: slice(base + off + h * HD, base + off + (h + 1) * HD)
        q = proj_s[:, col(0)]
        v = proj_s[:, col(2 * hw)]
        k = 1.0 - f[:, hs]
        cum_h = cum[:, hs]
        qe = q * jnp.exp(cum_h)
        ke = k * jnp.exp(cl[:, hs] - cum_h)
        g_qs.append([_dot(qe[s], ohg_ref[b, h]) for b, s in enumerate(seqs)])
        g_kv.append([_dot_tn(ke[s], v[s]) for s in seqs])
        if one_shot:
            ref = refc[:, hs]
            g_a.append([_dot_nt(q * jnp.exp(cum_h - ref), k * jnp.exp(ref - cum_h))])
        else:
            blocks = []
            for i in range(rows // HG_BLK):
                lo, hi = i * HG_BLK, (i + 1) * HG_BLK
                ref = cum_h[lo + HG_BLK // 2 - 1:lo + HG_BLK // 2, :]
                blocks.append(_dot_nt(q[lo:hi] * jnp.exp(cum_h[lo:hi] - ref),
                                      k[:hi] * jnp.exp(ref - cum_h[:hi])))
            g_a.append(blocks)
        g_v.append(v)

    yield
    r_o = []
    for h in range(H_RET):
        lg = math.log1p(-(2.0 ** (-5.0 - h)))
        qs = r_qs[h][0] if bb == 1 else jnp.concatenate(r_qs[h], axis=0)
        r_o.append(_dot(r_a[h] * pair_ref[h], r_v[h]) + qs * to_row_ref[h])
        for b in range(bb):
            oret_ref[b, h] = oret_ref[b, h] * math.exp(seq_len * lg) + r_kv[h][b]

    g_o = []
    for h in range(H_HG):
        if one_shot:
            o = _dot(jnp.where(causal, g_a[h][0], 0.0), g_v[h])
        else:
            outs = []
            for i in range(rows // HG_BLK):
                lo, hi = i * HG_BLK, (i + 1) * HG_BLK
                tb = lax.broadcasted_iota(jnp.int32, (HG_BLK, hi), 0) + lo
                sb = lax.broadcasted_iota(jnp.int32, (HG_BLK, hi), 1)
                outs.append(_dot(jnp.where(tb >= sb, g_a[h][i], 0.0), g_v[h][:hi]))
            o = jnp.concatenate(outs, axis=0)
        qs = g_qs[h][0] if bb == 1 else jnp.concatenate(g_qs[h], axis=0)
        g_o.append(o + qs)
        for b in range(bb):
            dcol = dtot[h * HD:(h + 1) * HD, :]
            if bb > 1:
                dcol = jnp.broadcast_to(dcol[:, b:b + 1], (HD, HD))
            ohg_ref[b, h] = ohg_ref[b, h] * jnp.exp(dcol) + g_kv[h][b]

    yield
    for h in range(H_RET):
        col = lambda off: slice(off + h * HD, off + (h + 1) * HD)
        mu = jnp.mean(r_o[h], axis=-1, keepdims=True)
        oc = r_o[h] - mu
        var = jnp.mean(oc * oc, axis=-1, keepdims=True)
        o = oc * lax.rsqrt(var + EPS) * retw_ref[:, col(0)] * _silu(proj_s[:, col(3 * hw)])
        mix_ref[:, :, col(0)] = o.reshape(bb, seq_len, HD).astype(mix_ref.dtype)
    for h in range(H_HG):
        hs = slice(h * HD, (h + 1) * HD)
        g = proj_s[:, base + 3 * hw + h * HD:base + 3 * hw + (h + 1) * HD]
        o = _rms(g_o[h], hgw_ref[:, hs]) * _silu(g)
        mix_ref[:, :, hw + h * HD:hw + (h + 1) * HD] = o.reshape(bb, seq_len, HD).astype(mix_ref.dtype)


PROJ_COLS = 512


def _project(x_ref, npre_ref, win_ref, proj_s):
    bb, seq_len, d = x_ref.shape
    hn = _rms(x_ref[...].reshape(bb * seq_len, d), npre_ref[...]).astype(BF16)
    n_in = win_ref.shape[1]

    def chunk(lo):
        def run():
            sl = slice(lo, min(lo + PROJ_COLS, n_in))
            proj_s[:, sl] = jnp.dot(hn, win_ref[:, sl], preferred_element_type=F32)
        return run

    return [chunk(lo) for lo in range(0, n_in, PROJ_COLS)]


def _projection_ahead(x_ref, xn_ref, npre_ref, win_ref, proj_s, proj_next_s):
    @pl.when((pl.program_id(0) == 0) & (pl.program_id(1) == 0))
    def _():
        for f in _project(x_ref, npre_ref, win_ref, proj_s):
            f()

    return _project(xn_ref, npre_ref, win_ref, proj_next_s)


def _carry_projection(proj_s, proj_next_s):
    n_in = proj_s.shape[1]
    for lo in range(0, n_in, PROJ_COLS):
        sl = slice(lo, min(lo + PROJ_COLS, n_in))
        proj_s[:, sl] = proj_next_s[:, sl]


def _next_block_map(n_blocks, nt):
    def imap(b, i):
        s = jnp.minimum(b * nt + i + 1, n_blocks * nt - 1)
        return (s // nt, s % nt, 0)
    return imap


def _even_kernel(has_state, x_ref, xn_ref, cos_ref, sin_ref, npre_ref, win_ref, retw_ref, hgw_ref,
                 lb_ref, pair_ref, to_row_ref, to_end_ref, *refs):
    if has_state:
        sret_ref, shg_ref, mix_ref, oret_ref, ohg_ref, proj_s, proj_next_s = refs
    else:
        mix_ref, oret_ref, ohg_ref, proj_s, proj_next_s = refs
    bb, seq_len, _ = x_ref.shape
    dec_refs = (pair_ref, to_row_ref, to_end_ref)

    @pl.when(pl.program_id(1) == 0)
    def _():
        if has_state:
            oret_ref[...] = sret_ref[...]
            ohg_ref[...] = shg_ref[...]
        else:
            oret_ref[...] = jnp.zeros(oret_ref.shape, F32)
            ohg_ref[...] = jnp.zeros(ohg_ref.shape, F32)

    fillers = _projection_ahead(x_ref, xn_ref, npre_ref, win_ref, proj_s, proj_next_s)
    consts = (lb_ref[...], retw_ref, hgw_ref, dec_refs)
    if seq_len > HG_BLK:
        units = []
        for u in range(bb):
            rs, one = pl.ds(u * seq_len, seq_len), pl.ds(u, 1)
            units.append(_even_block(1, seq_len, proj_s.at[rs], cos_ref[...], sin_ref[...], *consts,
                                     oret_ref.at[one], ohg_ref.at[one], mix_ref.at[one]))
    else:
        units = [_even_block(bb, seq_len, proj_s, cos_ref[...], sin_ref[...], *consts,
                             oret_ref, ohg_ref, mix_ref)]
    _interleave(units, fillers)
    _carry_projection(proj_s, proj_next_s)


def _even_call(x, cos, sin, npre, win, retw, hgw, lb, states, bb, c):
    nb_total, t, d = x.shape
    has_state = states is not None
    n_in = win.shape[1]
    rows = bb * c
    grid = (nb_total // bb, t // c)
    tab_rows = c
    if c <= HG_BLK:
        tab_rows = rows
        cos, sin = (jnp.tile(a.reshape(t // c, 1, c, HD), (1, bb, 1, 1)).reshape(-1, HD) for a in (cos, sin))
    st_spec = pl.BlockSpec((bb, H_RET, HD, HD), lambda b, i: (b, 0, 0, 0))
    in_specs = [pl.BlockSpec((bb, c, d), lambda b, i: (b, i, 0)),
                pl.BlockSpec((bb, c, d), _next_block_map(*grid)),
                pl.BlockSpec((tab_rows, HD), lambda b, i: (i, 0)),
                pl.BlockSpec((tab_rows, HD), lambda b, i: (i, 0)),
                _const_spec((1, d)), _const_spec(win.shape),
                _const_spec((1, H_RET * HD)), _const_spec((1, H_HG * HD)), _const_spec((1, H_HG * HD))]
    tables = _ret_decay_tables(bb if c <= HG_BLK else 1, c)
    in_specs += [_const_spec(a.shape) for a in tables]
    args = [x, x, cos, sin, npre, win, retw, hgw, lb, *tables]
    if has_state:
        in_specs += [st_spec, st_spec]
        args += list(states)
    mix_w = (H_RET + H_HG) * HD
    return pl.pallas_call(
        functools.partial(_even_kernel, has_state),
        grid=grid,
        in_specs=in_specs,
        out_specs=[pl.BlockSpec((bb, c, mix_w), lambda b, i: (b, i, 0)), st_spec, st_spec],
        out_shape=[jax.ShapeDtypeStruct((nb_total, t, mix_w), BF16),
                   jax.ShapeDtypeStruct((nb_total, H_RET, HD, HD), F32),
                   jax.ShapeDtypeStruct((nb_total, H_HG, HD, HD), F32)],
        scratch_shapes=[pltpu.VMEM((rows, n_in), F32), pltpu.VMEM((rows, n_in), F32)],
        compiler_params=pltpu.CompilerParams(dimension_semantics=("arbitrary", "arbitrary"),
                                             vmem_limit_bytes=VMEM_LIMIT),
        name="even_mixer",
    )(*args)


def _odd_block(has_state, bb, seq_len, proj_s, tail_s, ost_ref, sst_s, convw_ref, convb_ref, dtb_ref,
               alog_ref, dssm_ref, ssmw_ref, y_ref, u_ref):
    rows = bb * seq_len
    d_inner = H_SSD * P_SSD
    conv_dim = d_inner + 2 * G_SSD * N_SSD
    conv_w = convw_ref.shape[0]
    halo = conv_w - 1
    u_w = u_ref.shape[2]
    gw = d_inner // G_SSD
    heads_per_group = H_SSD // G_SSD
    pairs_per_group = heads_per_group // 2
    seqs = [slice(b * seq_len, (b + 1) * seq_len) for b in range(bb)]

    z = proj_s[:, 0:d_inner]
    xbc = proj_s[:, d_inner:d_inner + conv_dim]
    c0 = d_inner + conv_dim
    dt_raw = proj_s[:, c0:c0 + LANES]
    u_ref[...] = proj_s[:, c0 + H_SSD:c0 + H_SSD + u_w].reshape(bb, seq_len, u_w)

    tail = tail_s[...]
    conv = convb_ref[...] + xbc * convw_ref[halo:halo + 1, :]
    for s in range(1, conv_w):
        rolled = pltpu.roll(xbc, s, axis=0)
        if seq_len == SUBLANES:
            local = lax.broadcasted_iota(jnp.int32, (rows, conv_dim), 0) & (SUBLANES - 1)
            shifted = jnp.where(local < s, pltpu.roll(tail, rows - SUBLANES + s, axis=0), rolled)
        else:
            local = lax.broadcasted_iota(jnp.int32, (SUBLANES, conv_dim), 0)
            head = jnp.where(local < s, pltpu.roll(tail, s, axis=0), rolled[0:SUBLANES])
            shifted = jnp.concatenate([head, rolled[SUBLANES:]], axis=0)
        conv = conv + shifted * convw_ref[halo - s:halo - s + 1, :]
    tail_s[...] = xbc if seq_len == SUBLANES else xbc[rows - SUBLANES:rows]
    act = _silu(conv)
    xs = act[:, :d_inner]

    ti, si, same, causal = _seq_masks(rows, seq_len)
    lo_half = lax.broadcasted_iota(jnp.int32, (rows, LANES), 1) < P_SSD
    dt = jax.nn.softplus(dt_raw + dtb_ref[...])
    la_parts = _split3(dt * (-jnp.exp(alog_ref[...])))
    cum = _dot3(_ones_where(causal), la_parts)
    cum_t = _dot3_tn(la_parts, _ones_where(same & (ti <= si)))
    if bb > 1:
        rest = _dot3(_ones_where(same & (si > ti)), la_parts)
    bms = [act[:, d_inner + g * N_SSD:d_inner + (g + 1) * N_SSD] for g in range(G_SSD)]
    cms = [act[:, d_inner + (G_SSD + g) * N_SSD:d_inner + (G_SSD + g + 1) * N_SSD]
           for g in range(G_SSD)]
    scores = [_dot_nt(cms[g], bms[g]) for g in range(G_SSD)]
    yield

    bc = lambda a, h: jnp.broadcast_to(a[:, h:h + 1], (rows, LANES))
    bc_cum = [bc(cum, h) for h in range(H_SSD)]
    pair = lambda a, p: jnp.where(lo_half, bc(a, 2 * p), bc(a, 2 * p + 1))

    for g in range(G_SSD):
        xdts, xws, cums = [], [], []
        for pp in range(pairs_per_group):
            p = g * pairs_per_group + pp
            xdt = xs[:, p * LANES:(p + 1) * LANES] * pair(dt, p)
            cum_p = jnp.where(lo_half, bc_cum[2 * p], bc_cum[2 * p + 1])
            rest_p = cum_p[rows - 1:rows, :] - cum_p if bb == 1 else pair(rest, p)
            xdts.append(xdt)
            xws.append(xdt * jnp.exp(rest_p))
            cums.append(cum_p)
        xw = jnp.concatenate(xws, axis=1)
        if has_state:
            y_st = [_dot_nt(cms[g][s], ost_ref[b, g]) for b, s in enumerate(seqs)]
            kv = [_dot_tn(xw[s], bms[g][s]) for s in seqs]
        else:
            y_st = [_dot(cms[g][s], sst_s[b, g]) for b, s in enumerate(seqs)]
            kv = [_dot_tn(bms[g][s], xw[s]) for s in seqs]
        y_intra = []
        for pp in range(pairs_per_group):
            p = g * pairs_per_group + pp
            a_pair = []
            for h in (2 * p, 2 * p + 1):
                diff = bc_cum[h][:, :rows] - cum_t[h:h + 1, :]
                a_pair.append(scores[g] * jnp.where(causal, jnp.exp(jnp.where(causal, diff, 0.0)), 0.0))
            both = _dot(jnp.concatenate(a_pair, axis=0), xdts[pp])
            y_intra.append(jnp.where(lo_half, both[:rows], both[rows:]))
        yield
        for b in range(bb):
            last = (b + 1) * seq_len - 1
            if has_state:
                dec = [jnp.broadcast_to(jnp.exp(bc_cum[g * heads_per_group + hh][last:last + 1, :]),
                                        (P_SSD, N_SSD)) for hh in range(heads_per_group)]
                ost_ref[b, g] = ost_ref[b, g] * jnp.concatenate(dec, axis=0) + kv[b]
            else:
                dec = jnp.exp(jnp.concatenate([c_p[last:last + 1, :] for c_p in cums], axis=1))
                sst_s[b, g] = sst_s[b, g] * dec + kv[b]
        y_state = y_st[0] if bb == 1 else jnp.concatenate(y_st, axis=0)
        gs = slice(g * gw, (g + 1) * gw)
        y = (jnp.concatenate(y_intra, axis=1) + y_state * jnp.exp(jnp.concatenate(cums, axis=1))
             + dssm_ref[:, gs] * xs[:, gs])
        y = _rms(y * _silu(z[:, gs]), ssmw_ref[:, gs])
        y_ref[:, :, gs] = y.reshape(bb, seq_len, gw).astype(y_ref.dtype)


def _odd_kernel(has_state, x_ref, xn_ref, npre_ref, win_ref, convw_ref, convb_ref, dtb_ref, alog_ref,
                dssm_ref, ssmw_ref, *refs):
    if has_state:
        sconv_ref, sst_ref, y_ref, u_ref, oconv_ref, ost_ref, proj_s, proj_next_s, tail_s = refs
        sst_s = None
    else:
        y_ref, u_ref, oconv_ref, ost_ref, proj_s, proj_next_s, tail_s, sst_s = refs
    bb, seq_len, _ = x_ref.shape
    halo = convw_ref.shape[0] - 1

    @pl.when(pl.program_id(1) == 0)
    def _():
        tail_s[...] = jnp.zeros(tail_s.shape, F32)
        if has_state:
            for b in range(bb):
                tail_s[(b + 1) * SUBLANES - halo:(b + 1) * SUBLANES, :] = sconv_ref[b]
            ost_ref[...] = sst_ref[...]
        else:
            sst_s[...] = jnp.zeros(sst_s.shape, F32)

    fillers = _projection_ahead(x_ref, xn_ref, npre_ref, win_ref, proj_s, proj_next_s)
    consts = (convw_ref, convb_ref, dtb_ref, alog_ref, dssm_ref, ssmw_ref)
    if seq_len > SUBLANES:
        units = []
        for u in range(bb):
            one = pl.ds(u, 1)
            units.append(_odd_block(
                has_state, 1, seq_len, proj_s.at[pl.ds(u * seq_len, seq_len)],
                tail_s.at[pl.ds(u * SUBLANES, SUBLANES)], ost_ref.at[one],
                None if sst_s is None else sst_s.at[one], *consts, y_ref.at[one], u_ref.at[one]))
    else:
        units = [_odd_block(has_state, bb, seq_len, proj_s, tail_s, ost_ref, sst_s, *consts, y_ref, u_ref)]
    _interleave(units, fillers, per_round=3)
    _carry_projection(proj_s, proj_next_s)

    @pl.when(pl.program_id(1) == pl.num_programs(1) - 1)
    def _():
        for b in range(bb):
            oconv_ref[b] = tail_s[(b + 1) * SUBLANES - halo:(b + 1) * SUBLANES, :]
            if not has_state:
                for g in range(G_SSD):
                    for ps in range(0, ost_ref.shape[2], LANES):
                        ost_ref[b, g, ps:ps + LANES, :] = sst_s[b, g, :, ps:ps + LANES].T


def _odd_call(x, npre, win, convw, convb, dtb, alog, dssm, ssmw, states, bb, c):
    nb_total, t, d = x.shape
    has_state = states is not None
    n_in = win.shape[1]
    d_inner = H_SSD * P_SSD
    conv_dim = d_inner + 2 * G_SSD * N_SSD
    u_w = S5_G * S5_GS
    halo = convw.shape[0] - 1
    rows = bb * c
    grid = (nb_total // bb, t // c)
    st_shape = (G_SSD, d_inner // G_SSD, N_SSD)
    conv_spec = pl.BlockSpec((bb, halo, conv_dim), lambda b, i: (b, 0, 0))
    ssm_spec = pl.BlockSpec((bb,) + st_shape, lambda b, i: (b, 0, 0, 0))
    in_specs = [pl.BlockSpec((bb, c, d), lambda b, i: (b, i, 0)),
                pl.BlockSpec((bb, c, d), _next_block_map(*grid)),
                _const_spec((1, d)), _const_spec(win.shape), _const_spec(convw.shape),
                _const_spec((1, conv_dim)), _const_spec((1, LANES)), _const_spec((1, LANES)),
                _const_spec((1, d_inner)), _const_spec((1, d_inner))]
    args = [x, x, npre, win, convw, convb, dtb, alog, dssm, ssmw]
    scratch = [pltpu.VMEM((rows, n_in), F32), pltpu.VMEM((rows, n_in), F32),
               pltpu.VMEM((bb * SUBLANES, conv_dim), F32)]
    if has_state:
        in_specs += [conv_spec, ssm_spec]
        args += list(states)
    else:
        scratch.append(pltpu.VMEM((bb, G_SSD, N_SSD, d_inner // G_SSD), F32))
    return pl.pallas_call(
        functools.partial(_odd_kernel, has_state),
        grid=grid,
        in_specs=in_specs,
        out_specs=[pl.BlockSpec((bb, c, d_inner), lambda b, i: (b, i, 0)),
                   pl.BlockSpec((bb, c, u_w), lambda b, i: (b, i, 0)),
                   conv_spec, ssm_spec],
        out_shape=[jax.ShapeDtypeStruct((nb_total, t, d_inner), BF16),
                   jax.ShapeDtypeStruct((nb_total, t, u_w), F32),
                   jax.ShapeDtypeStruct((nb_total, halo, conv_dim), F32),
                   jax.ShapeDtypeStruct((nb_total,) + st_shape, F32)],
        scratch_shapes=scratch,
        compiler_params=pltpu.CompilerParams(dimension_semantics=("arbitrary", "arbitrary"),
                                             vmem_limit_bytes=VMEM_LIMIT),
        name="odd_mixer",
    )(*args)


S5_PARTS = 4
S5_PART_IN = S5_G * S5_GS // S5_PARTS
S5_PART_ST = S5_G * S5_P // S5_PARTS
S5_SCAN_VREGS = 4


def _s5_pitch(steps):
    return steps if steps % 16 == 8 else steps + 8


def _s5_kernel(u_ref, bmat_ref, cmat_ref, lre_ref, lim_ref, d_ref, wglu_ref, bglu_ref,
               h0re_ref, h0im_ref, yd_ref, ore_ref, oim_ref, buf_s, slab_s, ut_s):
    r, steps, uw = u_ref.shape
    hs = S5_PART_ST
    scan_w = min(hs, max(LANES, S5_SCAN_VREGS * SUBLANES * LANES // r))
    pitch = _s5_pitch(steps)
    nslab = uw // LANES

    @pl.when(pl.program_id(1) == 0)
    def _():
        ore_ref[...] = h0re_ref[...]
        oim_ref[...] = h0im_ref[...]

    for b in range(r):
        for j in range(nslab):
            slab_s[j, b * pitch:b * pitch + steps, :] = u_ref[b, :, j * LANES:(j + 1) * LANES]
    for t in range(steps):
        for j in range(nslab):
            ut_s[t * r:(t + 1) * r, j * LANES:(j + 1) * LANES] = slab_s[j, pl.ds(t, r, stride=pitch), :]
    u = ut_s[...]
    ub = u.astype(BF16)
    for half in range(S5_PARTS):
        buf_s[:, half * 2 * hs:(half + 1) * 2 * hs] = jnp.dot(
            ub[:, half * S5_PART_IN:(half + 1) * S5_PART_IN], bmat_ref[half],
            preferred_element_type=F32)

    for half in range(S5_PARTS):
        for j in range(hs // scan_w):
            st = slice(half * hs + j * scan_w, half * hs + (j + 1) * scan_w)
            cre = half * 2 * hs + j * scan_w
            cim = cre + hs
            lre = jnp.broadcast_to(lre_ref[:, st], (r, scan_w))
            lim = jnp.broadcast_to(lim_ref[:, st], (r, scan_w))
            hre, him = ore_ref[:, st], oim_ref[:, st]
            for t in range(steps):
                rws = slice(t * r, (t + 1) * r)
                hre, him = (lre * hre - lim * him + buf_s[rws, cre:cre + scan_w],
                            lre * him + lim * hre + buf_s[rws, cim:cim + scan_w])
                buf_s[rws, cre:cre + scan_w] = hre
                buf_s[rws, cim:cim + scan_w] = him
            ore_ref[:, st] = hre
            oim_ref[:, st] = him

    ys = []
    for half in range(S5_PARTS):
        ys.append(jnp.dot(buf_s[:, half * 2 * hs:(half + 1) * 2 * hs].astype(BF16), cmat_ref[half],
                          preferred_element_type=F32))
    y = jnp.concatenate(ys, axis=1) + d_ref[...] * u
    g = jax.nn.gelu(y)
    gate = jax.nn.sigmoid(jnp.dot(g.astype(BF16), wglu_ref[...], preferred_element_type=F32)
                          + bglu_ref[...])
    yd = g * gate
    for t in range(steps):
        for j in range(nslab):
            slab_s[j, pl.ds(t, r, stride=pitch), :] = yd[t * r:(t + 1) * r, j * LANES:(j + 1) * LANES]
    for b in range(r):
        for j in range(nslab):
            yd_ref[b, :, j * LANES:(j + 1) * LANES] = slab_s[j, b * pitch:b * pitch + steps, :].astype(
                yd_ref.dtype)


def _s5_call(u, bmat, cmat, lre, lim, dvec, wglu, bglu, h0re, h0im, steps, r):
    nb_total, t, uw = u.shape
    ns = S5_G * S5_P
    grid = (nb_total // r, t // steps)
    st_spec = pl.BlockSpec((r, ns), lambda b, i: (b, 0))
    return pl.pallas_call(
        _s5_kernel,
        grid=grid,
        in_specs=[pl.BlockSpec((r, steps, uw), lambda b, i: (b, i, 0)),
                  _const_spec(bmat.shape), _const_spec(cmat.shape),
                  _const_spec((1, ns)), _const_spec((1, ns)), _const_spec((1, uw)),
                  _const_spec(wglu.shape), _const_spec((1, uw)), st_spec, st_spec],
        out_specs=[pl.BlockSpec((r, steps, uw), lambda b, i: (b, i, 0)), st_spec, st_spec],
        out_shape=[jax.ShapeDtypeStruct((nb_total, t, uw), BF16),
                   jax.ShapeDtypeStruct((nb_total, ns), F32),
                   jax.ShapeDtypeStruct((nb_total, ns), F32)],
        scratch_shapes=[pltpu.VMEM((steps * r, 2 * ns), F32),
                        pltpu.VMEM((uw // LANES, r * _s5_pitch(steps), LANES), F32),
                        pltpu.VMEM((steps * r, uw), F32)],
        compiler_params=pltpu.CompilerParams(dimension_semantics=("arbitrary", "arbitrary"),
                                             vmem_limit_bytes=VMEM_LIMIT),
        name="s5_mixer",
    )(u, bmat, cmat, lre, lim, dvec, wglu, bglu, h0re, h0im)


def _rope_tables(pos):
    half = HD // 2
    inv_freq = ROPE_BASE ** (-jnp.arange(half, dtype=F32) / half)
    ang = pos[:, None] * inv_freq[None, :]
    cos, sin = jnp.cos(ang), jnp.sin(ang)
    return jnp.concatenate([cos, cos], axis=-1), jnp.concatenate([-sin, sin], axis=-1)


def _pad_lanes(v):
    return jnp.pad(v.astype(F32), (0, LANES - v.shape[0]))[None, :]


def _s5_tables(lam_re, lam_im, log_step, b_re, b_im, c_re, c_im):
    lr, li = lam_re.astype(F32), lam_im.astype(F32)
    dt = jnp.exp(log_step.astype(F32))[:, None]
    mag = jnp.exp(lr * dt)
    bar_re, bar_im = mag * jnp.cos(li * dt), mag * jnp.sin(li * dt)
    den = lr * lr + li * li
    cf_re = ((bar_re - 1.0) * lr + bar_im * li) / den
    cf_im = (bar_im * lr - (bar_re - 1.0) * li) / den
    bb_re = cf_re[..., None] * b_re.astype(F32) - cf_im[..., None] * b_im.astype(F32)
    bb_im = cf_re[..., None] * b_im.astype(F32) + cf_im[..., None] * b_re.astype(F32)
    gh = S5_G // S5_PARTS
    eye = jnp.eye(gh, dtype=BF16)
    b_ri = jnp.stack([bb_re, bb_im]).astype(BF16).reshape(2, S5_PARTS, gh, S5_P, S5_GS)
    bmat = jnp.einsum('rqgpc,gk->qgcrkp', b_ri, eye).reshape(S5_PARTS, gh * S5_GS, 2 * gh * S5_P)
    c_ri = jnp.stack([c_re.astype(F32), -c_im.astype(F32)]).astype(BF16).reshape(
        2, S5_PARTS, gh, S5_GS, S5_P)
    cmat = jnp.einsum('rqgcp,gk->qrgpkc', c_ri, eye).reshape(S5_PARTS, 2 * gh * S5_P, gh * S5_GS)
    return bmat, cmat, bar_re.reshape(1, -1), bar_im.reshape(1, -1)


def _row(v):
    return v.astype(F32).reshape(1, -1)


def _forward(x, pos_offset, states, p, cfg):
    nb, t, d = x.shape
    n = nb * t
    bb, c = cfg['bb'], cfg['c']
    has_state = states is not None
    pos = jnp.arange(pos_offset, pos_offset + t, dtype=F32)
    cos, sin = _rope_tables(pos)
    lb = jnp.cumsum(jax.nn.softmax(p['hgrn_lower_bounds'].astype(F32), axis=0), axis=0)[0]
    mix, ret_o, hg_o = _even_call(
        x, cos, sin, _row(p['norm_mix_pre'][0]), p['w_in_even'][0].astype(BF16),
        _row(p['ret_norm_w'][0]), _row(p['hgrn_norm_w'][0]), _row(lb),
        (states['ret'][0], states['hgrn'][0]) if has_state else None, bb, c)
    stacked_rows = lambda a: a.astype(F32)[:, None, :]
    ffn = (stacked_rows(p['norm_mix_post']), stacked_rows(p['norm_ffn_pre']),
           stacked_rows(p['norm_ffn_post']), p['w_ffn_up'].astype(BF16), p['w_ffn_down'].astype(BF16))
    h = _post_call(x.reshape(n, d), [mix.reshape(n, -1)], p['w_out_even'][0].astype(BF16), *ffn, 0,
                   cfg['rows'])
    d_inner = H_SSD * P_SSD
    conv_dim = d_inner + 2 * G_SSD * N_SSD
    u_w = S5_G * S5_GS
    w1 = p['w_in_odd'][0].astype(BF16)
    st_shape = (nb, G_SSD, d_inner // G_SSD, N_SSD)
    odd_states = None
    if has_state:
        odd_states = (states['conv'][0], jnp.swapaxes(states['ssm'][0], -1, -2).reshape(st_shape))
    y, u, conv_o, sst_o = _odd_call(
        h.reshape(nb, t, d), _row(p['norm_mix_pre'][1]), w1, p['conv_w'][0].astype(F32),
        _row(p['conv_b'][0]), _pad_lanes(p['dt_bias'][0]), _pad_lanes(p['a_log'][0]),
        _row(jnp.repeat(p['d_ssm'][0], P_SSD)), _row(p['ssm_norm_w'][0]), odd_states, bb, c)
    ssm_o = jnp.swapaxes(sst_o.reshape(nb, H_SSD, P_SSD, N_SSD), -1, -2)
    bmat, cmat, lre, lim = _s5_tables(p['s5_lam_re'][0], p['s5_lam_im'][0], p['s5_log_step'][0],
                                      p['s5_b_re'][0], p['s5_b_im'][0], p['s5_c_re'][0],
                                      p['s5_c_im'][0])
    ns = S5_G * S5_P
    if has_state:
        h0re = states['s5_re'][0].reshape(nb, ns)
        h0im = states['s5_im'][0].reshape(nb, ns)
    else:
        h0re = jnp.zeros((nb, ns), F32)
        h0im = jnp.zeros((nb, ns), F32)
    yd, re_o, im_o = _s5_call(u, bmat, cmat, lre, lim, _row(p['s5_d'][0]),
                              p['w_glu'][0].astype(BF16), _row(p['b_glu'][0]), h0re, h0im,
                              cfg['s5_steps'], cfg['s5_rows'])
    yd = yd.reshape(n, u_w)
    h = _post_call(h, [y.reshape(n, d_inner), yd], p['w_out_odd'][0].astype(BF16), *ffn, 1, cfg['rows'])
    return (h.reshape(nb, t, d), ret_o[None], hg_o[None], ssm_o[None], conv_o[None],
            re_o.reshape(1, nb, S5_G, S5_P), im_o.reshape(1, nb, S5_G, S5_P))


PROMPT_CFG = dict(bb=4, c=128, rows=1024, s5_steps=64, s5_rows=8)
SAMPLE_CFG = dict(bb=16, c=8, rows=1024, s5_steps=8, s5_rows=32)


def kernel(x_prompt, x_sample, state_ret, state_hgrn, state_ssm, state_conv, state_s5_re, state_s5_im,
           norm_mix_pre, norm_mix_post, norm_ffn_pre, norm_ffn_post, w_in_even, w_out_even, ret_norm_w,
           hgrn_lower_bounds, hgrn_norm_w, w_in_odd, conv_w, conv_b, dt_bias, a_log, d_ssm, ssm_norm_w,
           s5_lam_re, s5_lam_im, s5_log_step, s5_b_re, s5_b_im, s5_c_re, s5_c_im, s5_d, w_glu, b_glu,
           w_out_odd, w_ffn_up, w_ffn_down):
    p = dict(norm_mix_pre=norm_mix_pre, norm_mix_post=norm_mix_post, norm_ffn_pre=norm_ffn_pre,
             norm_ffn_post=norm_ffn_post, w_in_even=w_in_even, w_out_even=w_out_even, ret_norm_w=ret_norm_w,
             hgrn_lower_bounds=hgrn_lower_bounds, hgrn_norm_w=hgrn_norm_w, w_in_odd=w_in_odd, conv_w=conv_w,
             conv_b=conv_b, dt_bias=dt_bias, a_log=a_log, d_ssm=d_ssm, ssm_norm_w=ssm_norm_w,
             s5_lam_re=s5_lam_re, s5_lam_im=s5_lam_im, s5_log_step=s5_log_step, s5_b_re=s5_b_re,
             s5_b_im=s5_b_im, s5_c_re=s5_c_re, s5_c_im=s5_c_im, s5_d=s5_d, w_glu=w_glu, b_glu=b_glu,
             w_out_odd=w_out_odd, w_ffn_up=w_ffn_up, w_ffn_down=w_ffn_down)
    past_len = 16384
    states = dict(ret=state_ret, hgrn=state_hgrn, ssm=state_ssm, conv=state_conv,
                  s5_re=state_s5_re, s5_im=state_s5_im)
    y_p, ret_p, hg_p, ssm_p, conv_p, re_p, im_p = _forward(x_prompt, 0, None, p, PROMPT_CFG)
    y_s, ret_s, hg_s, ssm_s, conv_s, re_s, im_s = _forward(x_sample, past_len, states, p, SAMPLE_CFG)
    return (y_p, y_s, ret_p, ret_s, hg_p, hg_s, ssm_p, ssm_s, conv_p, conv_s,
            re_p, re_s, im_p, im_s)
```

```python
import functools
import math

import jax
import jax.numpy as jnp
import numpy as np
from jax import lax
from jax.experimental import pallas as pl
from jax.experimental.pallas import tpu as pltpu

F32 = jnp.float32
BF16 = jnp.bfloat16
EPS = 1e-6
ROPE_BASE = 10000.0
LANES = 128
SUBLANES = 8
VMEM_LIMIT = 56 * 1024 * 1024

H_RET = 4
H_HG = 4
HD = 128
H_SSD = 16
P_SSD = 64
N_SSD = 128
G_SSD = 2
S5_G = 32
S5_GS = 16
S5_P = 64
HG_BLK = 16


def _rms(x, w):
    return x * lax.rsqrt(jnp.mean(x * x, axis=-1, keepdims=True) + EPS) * w


def _silu(x):
    return x * jax.nn.sigmoid(x)


def _dot(a, b):
    return jnp.dot(a.astype(BF16), b.astype(BF16), preferred_element_type=F32)


def _dot_nt(a, b):
    return lax.dot_general(a.astype(BF16), b.astype(BF16), (((1,), (1,)), ((), ())),
                           preferred_element_type=F32)


def _dot_tn(a, b):
    return lax.dot_general(a.astype(BF16), b.astype(BF16), (((0,), (0,)), ((), ())),
                           preferred_element_type=F32)


def _split3(x):
    hi = x.astype(BF16)
    r1 = x - hi.astype(F32)
    mid = r1.astype(BF16)
    lo = (r1 - mid.astype(F32)).astype(BF16)
    return hi, mid, lo


def _dot3(a_exact, parts):
    acc = None
    for p in parts:
        d = jnp.dot(a_exact, p, preferred_element_type=F32)
        acc = d if acc is None else acc + d
    return acc


def _dot3_tn(parts, b_exact):
    acc = None
    for p in parts:
        d = lax.dot_general(p, b_exact, (((0,), (0,)), ((), ())), preferred_element_type=F32)
        acc = d if acc is None else acc + d
    return acc


def _ones_where(mask):
    return jnp.where(mask, 1.0, 0.0).astype(BF16)


def _seq_masks(rows, seq_len):
    sh = seq_len.bit_length() - 1
    ti = lax.broadcasted_iota(jnp.int32, (rows, rows), 0)
    si = lax.broadcasted_iota(jnp.int32, (rows, rows), 1)
    same = lax.shift_right_logical(ti, sh) == lax.shift_right_logical(si, sh)
    return ti, si, same, same & (ti >= si)


def _interleave(units, fillers=(), per_round=2):
    live = list(units)
    fillers = list(fillers)
    while live:
        still = []
        for g in live:
            try:
                next(g)
                still.append(g)
            except StopIteration:
                pass
        live = still
        for f in fillers[:per_round]:
            f()
        fillers = fillers[per_round:]
    for f in fillers:
        f()


def _const_spec(shape):
    nd = len(shape)
    return pl.BlockSpec(shape, lambda *_: (0,) * nd, pipeline_mode=pl.Buffered(1))


def _post_kernel(n_mix, ff_chunk, n_parts, h_ref, *refs):
    mix_refs = refs[:n_mix]
    wout_ref, npost_ref, nfpre_ref, nfpost_ref, wup_ref, wdn_ref, o_ref = refs[n_mix:]
    part = h_ref.shape[0] // n_parts
    groups = [slice(i * part, (i + 1) * part) for i in range(n_parts)]
    accs = []
    for rs in groups:
        acc, off = None, 0
        for m_ref in mix_refs:
            w = m_ref.shape[1]
            d = jnp.dot(m_ref[rs, :], wout_ref[off:off + w, :], preferred_element_type=F32)
            acc = d if acc is None else acc + d
            off += w
        accs.append(acc)
    for rs, acc in zip(groups, accs):
        h1 = h_ref[rs, :] + _rms(acc, npost_ref[...])
        hn = _rms(h1, nfpre_ref[...]).astype(BF16)
        ff = None
        for j in range(wup_ref.shape[1] // ff_chunk):
            sl = slice(j * ff_chunk, (j + 1) * ff_chunk)
            up = jnp.dot(hn, wup_ref[:, sl], preferred_element_type=F32)
            act = jnp.square(jnp.maximum(up, 0.0)).astype(BF16)
            d = jnp.dot(act, wdn_ref[sl, :], preferred_element_type=F32)
            ff = d if ff is None else ff + d
        o_ref[rs, :] = h1 + _rms(ff, nfpost_ref[...])


def _layer_spec(shape, layer):
    nd = len(shape) - 1
    return pl.BlockSpec((None,) + tuple(shape[1:]), lambda *_: (layer,) + (0,) * nd,
                        pipeline_mode=pl.Buffered(1))


def _post_call(h, mixes, wout, npost, nfpre, nfpost, wup, wdn, layer, rows):
    n, d = h.shape
    n_mix = len(mixes)
    row_spec = lambda w: pl.BlockSpec((rows, w), lambda i: (i, 0))
    in_specs = ([row_spec(d)] + [row_spec(m.shape[1]) for m in mixes] + [_const_spec(wout.shape)]
                + [_layer_spec(a.shape, layer) for a in (npost, nfpre, nfpost, wup, wdn)])
    return pl.pallas_call(
        functools.partial(_post_kernel, n_mix, 1024, 2),
        grid=(n // rows,),
        in_specs=in_specs,
        out_specs=row_spec(d),
        out_shape=jax.ShapeDtypeStruct((n, d), F32),
        compiler_params=pltpu.CompilerParams(dimension_semantics=("arbitrary",),
                                             vmem_limit_bytes=VMEM_LIMIT),
        name="post_ffn",
    )(h, *mixes, wout, npost, nfpre, nfpost, wup, wdn)


def _ret_decay_tables(bb, seq_len):
    rows = bb * seq_len
    t = np.arange(rows)
    tl = t % seq_len
    valid = (t[:, None] // seq_len == t[None, :] // seq_len) & (t[:, None] >= t[None, :])
    pair, to_row, to_end = [], [], []
    for h in range(H_RET):
        lg = math.log1p(-(2.0 ** (-5.0 - h)))
        pair.append(np.where(valid, np.exp((t[:, None] - t[None, :]) * lg), 0.0))
        to_row.append(np.broadcast_to(np.exp((tl + 1.0) * lg)[:, None], (rows, HD)))
        to_end.append(np.broadcast_to(np.exp((seq_len - 1.0 - tl) * lg)[:, None], (rows, HD)))
    return tuple(jnp.asarray(np.stack(a), F32) for a in (pair, to_row, to_end))


def _even_block(bb, seq_len, proj_s, cos, sin, lb, retw_ref, hgw_ref, dec_refs, oret_ref, ohg_ref,
                mix_ref):
    rows = bb * seq_len
    hw = H_RET * HD
    base = 4 * hw
    sh = seq_len.bit_length() - 1
    ti, si, same, causal = _seq_masks(rows, seq_len)
    pair_ref, to_row_ref, to_end_ref = dec_refs
    seqs = [slice(b * seq_len, (b + 1) * seq_len) for b in range(bb)]
    one_shot = seq_len <= HG_BLK

    f = lb + (1.0 - lb) * jax.nn.sigmoid(proj_s[:, base + hw:base + 2 * hw])
    lf_parts = _split3(jnp.log(f))
    cum = _dot3(_ones_where(causal), lf_parts)
    if bb == 1:
        dtot = _dot3_tn(lf_parts, jnp.ones((rows, HD), BF16))
        cl = cum[rows - 1:rows, :]
    else:
        r_seq = lax.shift_right_logical(lax.broadcasted_iota(jnp.int32, (rows, HD), 0), sh)
        ind = _ones_where(r_seq == lax.broadcasted_iota(jnp.int32, (rows, HD), 1))
        dtot = _dot3_tn(lf_parts, ind)
        cl = _dot3(_ones_where(same), lf_parts)
    if one_shot:
        refc = _dot3(_ones_where(same & ((si & (seq_len - 1)) < seq_len // 2)), lf_parts)

    yield
    r_a, r_qs, r_kv, r_v = [], [], [], []
    for h in range(H_RET):
        col = lambda off: slice(off + h * HD, off + (h + 1) * HD)
        q = proj_s[:, col(0)]
        k = proj_s[:, col(hw)]
        v = proj_s[:, col(2 * hw)]
        q = q * cos + pltpu.roll(q, HD // 2, axis=1) * sin
        k = (k * cos + pltpu.roll(k, HD // 2, axis=1) * sin) * (HD ** -0.5)
        kd = k * to_end_ref[h]
        r_a.append(_dot_nt(q, k))
        r_qs.append([_dot(q[s], oret_ref[b, h]) for b, s in enumerate(seqs)])
        r_kv.append([_dot_tn(kd[s], v[s]) for s in seqs])
        r_v.append(v)

    g_a, g_qs, g_kv, g_v = [], [], [], []
    for h in range(H_HG):
        hs = slice(h * HD, (h + 1) * HD)
        col = lambda off: slice(base + off + h * HD, base + off + (h + 1) * HD)
        q = proj_s[:, col(0)]
        v = proj_s[:, col(2 * hw)]
        k = 1.0 - f[:, hs]
        cum_h = cum[:, hs]
        qe = q * jnp.exp(cum_h)
        ke = k * jnp.exp(cl[:, hs] - cum_h)
        g_qs.append([_dot(qe[s], ohg_ref[b, h]) for b, s in enumerate(seqs)])
        g_kv.append([_dot_tn(ke[s], v[s]) for s in seqs])
        if one_shot:
            ref = refc[:, hs]
            g_a.append([_dot_nt(q * jnp.exp(cum_h - ref), k * jnp.exp(ref - cum_h))])
        else:
            blocks = []
            for i in range(rows // HG_BLK):
                lo, hi = i * HG_BLK, (i + 1) * HG_BLK
                ref = cum_h[lo + HG_BLK // 2 - 1:lo + HG_BLK // 2, :]
                blocks.append(_dot_nt(q[lo:hi] * jnp.exp(cum_h[lo:hi] - ref),
                                      k[:hi] * jnp.exp(ref - cum_h[:hi])))
            g_a.append(blocks)
        g_v.append(v)

    yield
    r_o = []
    for h in range(H_RET):
        lg = math.log1p(-(2.0 ** (-5.0 - h)))
        qs = r_qs[h][0] if bb == 1 else jnp.concatenate(r_qs[h], axis=0)
        r_o.append(_dot(r_a[h] * pair_ref[h], r_v[h]) + qs * to_row_ref[h])
        for b in range(bb):
            oret_ref[b, h] = oret_ref[b, h] * math.exp(seq_len * lg) + r_kv[h][b]

    g_o = []
    for h in range(H_HG):
        if one_shot:
            o = _dot(jnp.where(causal, g_a[h][0], 0.0), g_v[h])
        else:
            wide = [a if a.shape[1] == rows else
                    jnp.concatenate([a, jnp.zeros((HG_BLK, rows - a.shape[1]), F32)], axis=1)
                    for a in g_a[h]]
            o = _dot(jnp.where(causal, jnp.concatenate(wide, axis=0), 0.0), g_v[h])
        qs = g_qs[h][0] if bb == 1 else jnp.concatenate(g_qs[h], axis=0)
        g_o.append(o + qs)
        for b in range(bb):
            dcol = dtot[h * HD:(h + 1) * HD, :]
            if bb > 1:
                dcol = jnp.broadcast_to(dcol[:, b:b + 1], (HD, HD))
            ohg_ref[b, h] = ohg_ref[b, h] * jnp.exp(dcol) + g_kv[h][b]

    yield
    for h in range(H_RET):
        col = lambda off: slice(off + h * HD, off + (h + 1) * HD)
        mu = jnp.mean(r_o[h], axis=-1, keepdims=True)
        oc = r_o[h] - mu
        var = jnp.mean(oc * oc, axis=-1, keepdims=True)
        o = oc * lax.rsqrt(var + EPS) * retw_ref[:, col(0)] * _silu(proj_s[:, col(3 * hw)])
        mix_ref[:, :, col(0)] = o.reshape(bb, seq_len, HD).astype(mix_ref.dtype)
    for h in range(H_HG):
        hs = slice(h * HD, (h + 1) * HD)
        g = proj_s[:, base + 3 * hw + h * HD:base + 3 * hw + (h + 1) * HD]
        o = _rms(g_o[h], hgw_ref[:, hs]) * _silu(g)
        mix_ref[:, :, hw + h * HD:hw + (h + 1) * HD] = o.reshape(bb, seq_len, HD).astype(mix_ref.dtype)


PROJ_COLS = 512


def _project(x_ref, npre_ref, win_ref, proj_s):
    bb, seq_len, d = x_ref.shape
    hn = _rms(x_ref[...].reshape(bb * seq_len, d), npre_ref[...]).astype(BF16)
    n_in = win_ref.shape[1]

    def chunk(lo):
        def run():
            sl = slice(lo, min(lo + PROJ_COLS, n_in))
            proj_s[:, sl] = jnp.dot(hn, win_ref[:, sl], preferred_element_type=F32)
        return run

    return [chunk(lo) for lo in range(0, n_in, PROJ_COLS)]


def _projection_ahead(x_ref, xn_ref, npre_ref, win_ref, proj_s, proj_next_s):
    @pl.when((pl.program_id(0) == 0) & (pl.program_id(1) == 0))
    def _():
        for f in _project(x_ref, npre_ref, win_ref, proj_s):
            f()

    return _project(xn_ref, npre_ref, win_ref, proj_next_s)


def _carry_projection(proj_s, proj_next_s):
    n_in = proj_s.shape[1]
    for lo in range(0, n_in, PROJ_COLS):
        sl = slice(lo, min(lo + PROJ_COLS, n_in))
        proj_s[:, sl] = proj_next_s[:, sl]


def _next_block_map(n_blocks, nt):
    def imap(b, i):
        s = jnp.minimum(b * nt + i + 1, n_blocks * nt - 1)
        return (s // nt, s % nt, 0)
    return imap


def _even_kernel(has_state, x_ref, xn_ref, cos_ref, sin_ref, npre_ref, win_ref, retw_ref, hgw_ref,
                 lb_ref, pair_ref, to_row_ref, to_end_ref, *refs):
    if has_state:
        sret_ref, shg_ref, mix_ref, oret_ref, ohg_ref, proj_s, proj_next_s = refs
    else:
        mix_ref, oret_ref, ohg_ref, proj_s, proj_next_s = refs
    bb, seq_len, _ = x_ref.shape
    dec_refs = (pair_ref, to_row_ref, to_end_ref)

    @pl.when(pl.program_id(1) == 0)
    def _():
        if has_state:
            oret_ref[...] = sret_ref[...]
            ohg_ref[...] = shg_ref[...]
        else:
            oret_ref[...] = jnp.zeros(oret_ref.shape, F32)
            ohg_ref[...] = jnp.zeros(ohg_ref.shape, F32)

    fillers = _projection_ahead(x_ref, xn_ref, npre_ref, win_ref, proj_s, proj_next_s)
    consts = (lb_ref[...], retw_ref, hgw_ref, dec_refs)
    if seq_len > HG_BLK:
        units = []
        for u in range(bb):
            rs, one = pl.ds(u * seq_len, seq_len), pl.ds(u, 1)
            units.append(_even_block(1, seq_len, proj_s.at[rs], cos_ref[...], sin_ref[...], *consts,
                                     oret_ref.at[one], ohg_ref.at[one], mix_ref.at[one]))
    else:
        units = [_even_block(bb, seq_len, proj_s, cos_ref[...], sin_ref[...], *consts,
                             oret_ref, ohg_ref, mix_ref)]
    _interleave(units, fillers)
    _carry_projection(proj_s, proj_next_s)


def _even_call(x, cos, sin, npre, win, retw, hgw, lb, states, bb, c):
    nb_total, t, d = x.shape
    has_state = states is not None
    n_in = win.shape[1]
    rows = bb * c
    grid = (nb_total // bb, t // c)
    tab_rows = c
    if c <= HG_BLK:
        tab_rows = rows
        cos, sin = (jnp.tile(a.reshape(t // c, 1, c, HD), (1, bb, 1, 1)).reshape(-1, HD) for a in (cos, sin))
    st_spec = pl.BlockSpec((bb, H_RET, HD, HD), lambda b, i: (b, 0, 0, 0))
    in_specs = [pl.BlockSpec((bb, c, d), lambda b, i: (b, i, 0)),
                pl.BlockSpec((bb, c, d), _next_block_map(*grid)),
                pl.BlockSpec((tab_rows, HD), lambda b, i: (i, 0)),
                pl.BlockSpec((tab_rows, HD), lambda b, i: (i, 0)),
                _const_spec((1, d)), _const_spec(win.shape),
                _const_spec((1, H_RET * HD)), _const_spec((1, H_HG * HD)), _const_spec((1, H_HG * HD))]
    tables = _ret_decay_tables(bb if c <= HG_BLK else 1, c)
    in_specs += [_const_spec(a.shape) for a in tables]
    args = [x, x, cos, sin, npre, win, retw, hgw, lb, *tables]
    if has_state:
        in_specs += [st_spec, st_spec]
        args += list(states)
    mix_w = (H_RET + H_HG) * HD
    return pl.pallas_call(
        functools.partial(_even_kernel, has_state),
        grid=grid,
        in_specs=in_specs,
        out_specs=[pl.BlockSpec((bb, c, mix_w), lambda b, i: (b, i, 0)), st_spec, st_spec],
        out_shape=[jax.ShapeDtypeStruct((nb_total, t, mix_w), BF16),
                   jax.ShapeDtypeStruct((nb_total, H_RET, HD, HD), F32),
                   jax.ShapeDtypeStruct((nb_total, H_HG, HD, HD), F32)],
        scratch_shapes=[pltpu.VMEM((rows, n_in), F32), pltpu.VMEM((rows, n_in), F32)],
        compiler_params=pltpu.CompilerParams(dimension_semantics=("arbitrary", "arbitrary"),
                                             vmem_limit_bytes=VMEM_LIMIT),
        name="even_mixer",
    )(*args)


def _odd_block(has_state, bb, seq_len, proj_s, tail_s, ost_ref, sst_s, convw_ref, convb_ref, dtb_ref,
               alog_ref, dssm_ref, ssmw_ref, y_ref, u_ref):
    rows = bb * seq_len
    d_inner = H_SSD * P_SSD
    conv_dim = d_inner + 2 * G_SSD * N_SSD
    conv_w = convw_ref.shape[0]
    halo = conv_w - 1
    u_w = u_ref.shape[2]
    gw = d_inner // G_SSD
    heads_per_group = H_SSD // G_SSD
    pairs_per_group = heads_per_group // 2
    seqs = [slice(b * seq_len, (b + 1) * seq_len) for b in range(bb)]

    z = proj_s[:, 0:d_inner]
    xbc = proj_s[:, d_inner:d_inner + conv_dim]
    c0 = d_inner + conv_dim
    dt_raw = proj_s[:, c0:c0 + LANES]
    u_ref[...] = proj_s[:, c0 + H_SSD:c0 + H_SSD + u_w].reshape(bb, seq_len, u_w)

    tail = tail_s[...]
    conv = convb_ref[...] + xbc * convw_ref[halo:halo + 1, :]
    for s in range(1, conv_w):
        rolled = pltpu.roll(xbc, s, axis=0)
        if seq_len == SUBLANES:
            local = lax.broadcasted_iota(jnp.int32, (rows, conv_dim), 0) & (SUBLANES - 1)
            shifted = jnp.where(local < s, pltpu.roll(tail, rows - SUBLANES + s, axis=0), rolled)
        else:
            local = lax.broadcasted_iota(jnp.int32, (SUBLANES, conv_dim), 0)
            head = jnp.where(local < s, pltpu.roll(tail, s, axis=0), rolled[0:SUBLANES])
            shifted = jnp.concatenate([head, rolled[SUBLANES:]], axis=0)
        conv = conv + shifted * convw_ref[halo - s:halo - s + 1, :]
    tail_s[...] = xbc if seq_len == SUBLANES else xbc[rows - SUBLANES:rows]
    act = _silu(conv)
    xs = act[:, :d_inner]

    ti, si, same, causal = _seq_masks(rows, seq_len)
    lo_half = lax.broadcasted_iota(jnp.int32, (rows, LANES), 1) < P_SSD
    dt = jax.nn.softplus(dt_raw + dtb_ref[...])
    la_parts = _split3(dt * (-jnp.exp(alog_ref[...])))
    cum = _dot3(_ones_where(causal), la_parts)
    cum_t = _dot3_tn(la_parts, _ones_where(same & (ti <= si)))
    if bb > 1:
        rest = _dot3(_ones_where(same & (si > ti)), la_parts)
    bms = [act[:, d_inner + g * N_SSD:d_inner + (g + 1) * N_SSD] for g in range(G_SSD)]
    cms = [act[:, d_inner + (G_SSD + g) * N_SSD:d_inner + (G_SSD + g + 1) * N_SSD]
           for g in range(G_SSD)]
    scores = [_dot_nt(cms[g], bms[g]) for g in range(G_SSD)]
    yield

    bc = lambda a, h: jnp.broadcast_to(a[:, h:h + 1], (rows, LANES))
    bc_cum = [bc(cum, h) for h in range(H_SSD)]
    pair = lambda a, p: jnp.where(lo_half, bc(a, 2 * p), bc(a, 2 * p + 1))

    for g in range(G_SSD):
        xdts, xws, cums = [], [], []
        for pp in range(pairs_per_group):
            p = g * pairs_per_group + pp
            xdt = xs[:, p * LANES:(p + 1) * LANES] * pair(dt, p)
            cum_p = jnp.where(lo_half, bc_cum[2 * p], bc_cum[2 * p + 1])
            rest_p = cum_p[rows - 1:rows, :] - cum_p if bb == 1 else pair(rest, p)
            xdts.append(xdt)
            xws.append(xdt * jnp.exp(rest_p))
            cums.append(cum_p)
        xw = jnp.concatenate(xws, axis=1)
        if has_state:
            y_st = [_dot_nt(cms[g][s], ost_ref[b, g]) for b, s in enumerate(seqs)]
            kv = [_dot_tn(xw[s], bms[g][s]) for s in seqs]
        else:
            y_st = [_dot(cms[g][s], sst_s[b, g]) for b, s in enumerate(seqs)]
            kv = [_dot_tn(bms[g][s], xw[s]) for s in seqs]
        y_intra = []
        for pp in range(pairs_per_group):
            p = g * pairs_per_group + pp
            a_pair = []
            for h in (2 * p, 2 * p + 1):
                diff = bc_cum[h][:, :rows] - cum_t[h:h + 1, :]
                a_pair.append(scores[g] * jnp.where(causal, jnp.exp(jnp.where(causal, diff, 0.0)), 0.0))
            both = _dot(jnp.concatenate(a_pair, axis=0), xdts[pp])
            y_intra.append(jnp.where(lo_half, both[:rows], both[rows:]))
        yield
        for b in range(bb):
            last = (b + 1) * seq_len - 1
            if has_state:
                dec = [jnp.broadcast_to(jnp.exp(bc_cum[g * heads_per_group + hh][last:last + 1, :]),
                                        (P_SSD, N_SSD)) for hh in range(heads_per_group)]
                ost_ref[b, g] = ost_ref[b, g] * jnp.concatenate(dec, axis=0) + kv[b]
            else:
                dec = jnp.exp(jnp.concatenate([c_p[last:last + 1, :] for c_p in cums], axis=1))
                sst_s[b, g] = sst_s[b, g] * dec + kv[b]
        y_state = y_st[0] if bb == 1 else jnp.concatenate(y_st, axis=0)
        gs = slice(g * gw, (g + 1) * gw)
        y = (jnp.concatenate(y_intra, axis=1) + y_state * jnp.exp(jnp.concatenate(cums, axis=1))
             + dssm_ref[:, gs] * xs[:, gs])
        y = _rms(y * _silu(z[:, gs]), ssmw_ref[:, gs])
        y_ref[:, :, gs] = y.reshape(bb, seq_len, gw).astype(y_ref.dtype)


def _odd_kernel(has_state, x_ref, xn_ref, npre_ref, win_ref, convw_ref, convb_ref, dtb_ref, alog_ref,
                dssm_ref, ssmw_ref, *refs):
    if has_state:
        sconv_ref, sst_ref, y_ref, u_ref, oconv_ref, ost_ref, proj_s, proj_next_s, tail_s = refs
        sst_s = None
    else:
        y_ref, u_ref, oconv_ref, ost_ref, proj_s, proj_next_s, tail_s, sst_s = refs
    bb, seq_len, _ = x_ref.shape
    halo = convw_ref.shape[0] - 1

    @pl.when(pl.program_id(1) == 0)
    def _():
        tail_s[...] = jnp.zeros(tail_s.shape, F32)
        if has_state:
            for b in range(bb):
                tail_s[(b + 1) * SUBLANES - halo:(b + 1) * SUBLANES, :] = sconv_ref[b]
            ost_ref[...] = sst_ref[...]
        else:
            sst_s[...] = jnp.zeros(sst_s.shape, F32)

    fillers = _projection_ahead(x_ref, xn_ref, npre_ref, win_ref, proj_s, proj_next_s)
    consts = (convw_ref, convb_ref, dtb_ref, alog_ref, dssm_ref, ssmw_ref)
    if seq_len > SUBLANES:
        units = []
        for u in range(bb):
            one = pl.ds(u, 1)
            units.append(_odd_block(
                has_state, 1, seq_len, proj_s.at[pl.ds(u * seq_len, seq_len)],
                tail_s.at[pl.ds(u * SUBLANES, SUBLANES)], ost_ref.at[one],
                None if sst_s is None else sst_s.at[one], *consts, y_ref.at[one], u_ref.at[one]))
    else:
        units = [_odd_block(has_state, bb, seq_len, proj_s, tail_s, ost_ref, sst_s, *consts, y_ref, u_ref)]
    _interleave(units, fillers, per_round=3)
    _carry_projection(proj_s, proj_next_s)

    @pl.when(pl.program_id(1) == pl.num_programs(1) - 1)
    def _():
        for b in range(bb):
            oconv_ref[b] = tail_s[(b + 1) * SUBLANES - halo:(b + 1) * SUBLANES, :]
            if not has_state:
                for g in range(G_SSD):
                    for ps in range(0, ost_ref.shape[2], LANES):
                        ost_ref[b, g, ps:ps + LANES, :] = sst_s[b, g, :, ps:ps + LANES].T


def _odd_call(x, npre, win, convw, convb, dtb, alog, dssm, ssmw, states, bb, c):
    nb_total, t, d = x.shape
    has_state = states is not None
    n_in = win.shape[1]
    d_inner = H_SSD * P_SSD
    conv_dim = d_inner + 2 * G_SSD * N_SSD
    u_w = S5_G * S5_GS
    halo = convw.shape[0] - 1
    rows = bb * c
    grid = (nb_total // bb, t // c)
    st_shape = (G_SSD, d_inner // G_SSD, N_SSD)
    conv_spec = pl.BlockSpec((bb, halo, conv_dim), lambda b, i: (b, 0, 0))
    ssm_spec = pl.BlockSpec((bb,) + st_shape, lambda b, i: (b, 0, 0, 0))
    in_specs = [pl.BlockSpec((bb, c, d), lambda b, i: (b, i, 0)),
                pl.BlockSpec((bb, c, d), _next_block_map(*grid)),
                _const_spec((1, d)), _const_spec(win.shape), _const_spec(convw.shape),
                _const_spec((1, conv_dim)), _const_spec((1, LANES)), _const_spec((1, LANES)),
                _const_spec((1, d_inner)), _const_spec((1, d_inner))]
    args = [x, x, npre, win, convw, convb, dtb, alog, dssm, ssmw]
    scratch = [pltpu.VMEM((rows, n_in), F32), pltpu.VMEM((rows, n_in), F32),
               pltpu.VMEM((bb * SUBLANES, conv_dim), F32)]
    if has_state:
        in_specs += [conv_spec, ssm_spec]
        args += list(states)
    else:
        scratch.append(pltpu.VMEM((bb, G_SSD, N_SSD, d_inner // G_SSD), F32))
    return pl.pallas_call(
        functools.partial(_odd_kernel, has_state),
        grid=grid,
        in_specs=in_specs,
        out_specs=[pl.BlockSpec((bb, c, d_inner), lambda b, i: (b, i, 0)),
                   pl.BlockSpec((bb, c, u_w), lambda b, i: (b, i, 0)),
                   conv_spec, ssm_spec],
        out_shape=[jax.ShapeDtypeStruct((nb_total, t, d_inner), BF16),
                   jax.ShapeDtypeStruct((nb_total, t, u_w), F32),
                   jax.ShapeDtypeStruct((nb_total, halo, conv_dim), F32),
                   jax.ShapeDtypeStruct((nb_total,) + st_shape, F32)],
        scratch_shapes=scratch,
        compiler_params=pltpu.CompilerParams(dimension_semantics=("arbitrary", "arbitrary"),
                                             vmem_limit_bytes=VMEM_LIMIT),
        name="odd_mixer",
    )(*args)


S5_PARTS = 4
S5_PART_IN = S5_G * S5_GS // S5_PARTS
S5_PART_ST = S5_G * S5_P // S5_PARTS
S5_SCAN_VREGS = 4


def _s5_pitch(steps):
    return steps if steps % 16 == 8 else steps + 8


def _s5_kernel(u_ref, bmat_ref, cmat_ref, lre_ref, lim_ref, d_ref, wglu_ref, bglu_ref,
               h0re_ref, h0im_ref, yd_ref, ore_ref, oim_ref, buf_s, slab_s, ut_s):
    r, steps, uw = u_ref.shape
    hs = S5_PART_ST
    scan_w = min(hs, max(LANES, S5_SCAN_VREGS * SUBLANES * LANES // r))
    pitch = _s5_pitch(steps)
    nslab = uw // LANES

    @pl.when(pl.program_id(1) == 0)
    def _():
        ore_ref[...] = h0re_ref[...]
        oim_ref[...] = h0im_ref[...]

    for b in range(r):
        for j in range(nslab):
            slab_s[j, b * pitch:b * pitch + steps, :] = u_ref[b, :, j * LANES:(j + 1) * LANES]
    for t in range(steps):
        for j in range(nslab):
            ut_s[t * r:(t + 1) * r, j * LANES:(j + 1) * LANES] = slab_s[j, pl.ds(t, r, stride=pitch), :]
    u = ut_s[...]
    ub = u.astype(BF16)
    for half in range(S5_PARTS):
        buf_s[:, half * 2 * hs:(half + 1) * 2 * hs] = jnp.dot(
            ub[:, half * S5_PART_IN:(half + 1) * S5_PART_IN], bmat_ref[half],
            preferred_element_type=F32)

    for half in range(S5_PARTS):
        for j in range(hs // scan_w):
            st = slice(half * hs + j * scan_w, half * hs + (j + 1) * scan_w)
            cre = half * 2 * hs + j * scan_w
            cim = cre + hs
            lre = jnp.broadcast_to(lre_ref[:, st], (r, scan_w))
            lim = jnp.broadcast_to(lim_ref[:, st], (r, scan_w))
            hre, him = ore_ref[:, st], oim_ref[:, st]
            for t in range(steps):
                rws = slice(t * r, (t + 1) * r)
                hre, him = (lre * hre - lim * him + buf_s[rws, cre:cre + scan_w],
                            lre * him + lim * hre + buf_s[rws, cim:cim + scan_w])
                buf_s[rws, cre:cre + scan_w] = hre
                buf_s[rws, cim:cim + scan_w] = him
            ore_ref[:, st] = hre
            oim_ref[:, st] = him

    ys = []
    for half in range(S5_PARTS):
        ys.append(jnp.dot(buf_s[:, half * 2 * hs:(half + 1) * 2 * hs].astype(BF16), cmat_ref[half],
                          preferred_element_type=F32))
    y = jnp.concatenate(ys, axis=1) + d_ref[...] * u
    g = jax.nn.gelu(y)
    gate = jax.nn.sigmoid(jnp.dot(g.astype(BF16), wglu_ref[...], preferred_element_type=F32)
                          + bglu_ref[...])
    yd = g * gate
    for t in range(steps):
        for j in range(nslab):
            slab_s[j, pl.ds(t, r, stride=pitch), :] = yd[t * r:(t + 1) * r, j * LANES:(j + 1) * LANES]
    for b in range(r):
        for j in range(nslab):
            yd_ref[b, :, j * LANES:(j + 1) * LANES] = slab_s[j, b * pitch:b * pitch + steps, :].astype(
                yd_ref.dtype)


def _s5_call(u, bmat, cmat, lre, lim, dvec, wglu, bglu, h0re, h0im, steps, r):
    nb_total, t, uw = u.shape
    ns = S5_G * S5_P
    grid = (nb_total // r, t // steps)
    st_spec = pl.BlockSpec((r, ns), lambda b, i: (b, 0))
    return pl.pallas_call(
        _s5_kernel,
        grid=grid,
        in_specs=[pl.BlockSpec((r, steps, uw), lambda b, i: (b, i, 0)),
                  _const_spec(bmat.shape), _const_spec(cmat.shape),
                  _const_spec((1, ns)), _const_spec((1, ns)), _const_spec((1, uw)),
                  _const_spec(wglu.shape), _const_spec((1, uw)), st_spec, st_spec],
        out_specs=[pl.BlockSpec((r, steps, uw), lambda b, i: (b, i, 0)), st_spec, st_spec],
        out_shape=[jax.ShapeDtypeStruct((nb_total, t, uw), BF16),
                   jax.ShapeDtypeStruct((nb_total, ns), F32),
                   jax.ShapeDtypeStruct((nb_total, ns), F32)],
        scratch_shapes=[pltpu.VMEM((steps * r, 2 * ns), F32),
                        pltpu.VMEM((uw // LANES, r * _s5_pitch(steps), LANES), F32),
                        pltpu.VMEM((steps * r, uw), F32)],
        compiler_params=pltpu.CompilerParams(dimension_semantics=("arbitrary", "arbitrary"),
                                             vmem_limit_bytes=VMEM_LIMIT),
        name="s5_mixer",
    )(u, bmat, cmat, lre, lim, dvec, wglu, bglu, h0re, h0im)


def _rope_tables(pos):
    half = HD // 2
    inv_freq = ROPE_BASE ** (-jnp.arange(half, dtype=F32) / half)
    ang = pos[:, None] * inv_freq[None, :]
    cos, sin = jnp.cos(ang), jnp.sin(ang)
    return jnp.concatenate([cos, cos], axis=-1), jnp.concatenate([-sin, sin], axis=-1)


def _pad_lanes(v):
    return jnp.pad(v.astype(F32), (0, LANES - v.shape[0]))[None, :]


def _s5_tables(lam_re, lam_im, log_step, b_re, b_im, c_re, c_im):
    lr, li = lam_re.astype(F32), lam_im.astype(F32)
    dt = jnp.exp(log_step.astype(F32))[:, None]
    mag = jnp.exp(lr * dt)
    bar_re, bar_im = mag * jnp.cos(li * dt), mag * jnp.sin(li * dt)
    den = lr * lr + li * li
    cf_re = ((bar_re - 1.0) * lr + bar_im * li) / den
    cf_im = (bar_im * lr - (bar_re - 1.0) * li) / den
    bb_re = cf_re[..., None] * b_re.astype(F32) - cf_im[..., None] * b_im.astype(F32)
    bb_im = cf_re[..., None] * b_im.astype(F32) + cf_im[..., None] * b_re.astype(F32)
    gh = S5_G // S5_PARTS
    eye = jnp.eye(gh, dtype=BF16)
    b_ri = jnp.stack([bb_re, bb_im]).astype(BF16).reshape(2, S5_PARTS, gh, S5_P, S5_GS)
    bmat = jnp.einsum('rqgpc,gk->qgcrkp', b_ri, eye).reshape(S5_PARTS, gh * S5_GS, 2 * gh * S5_P)
    c_ri = jnp.stack([c_re.astype(F32), -c_im.astype(F32)]).astype(BF16).reshape(
        2, S5_PARTS, gh, S5_GS, S5_P)
    cmat = jnp.einsum('rqgcp,gk->qrgpkc', c_ri, eye).reshape(S5_PARTS, 2 * gh * S5_P, gh * S5_GS)
    return bmat, cmat, bar_re.reshape(1, -1), bar_im.reshape(1, -1)


def _row(v):
    return v.astype(F32).reshape(1, -1)


def _forward(x, pos_offset, states, p, cfg):
    nb, t, d = x.shape
    n = nb * t
    bb, c = cfg['bb'], cfg['c']
    has_state = states is not None
    pos = jnp.arange(pos_offset, pos_offset + t, dtype=F32)
    cos, sin = _rope_tables(pos)
    lb = jnp.cumsum(jax.nn.softmax(p['hgrn_lower_bounds'].astype(F32), axis=0), axis=0)[0]
    mix, ret_o, hg_o = _even_call(
        x, cos, sin, _row(p['norm_mix_pre'][0]), p['w_in_even'][0].astype(BF16),
        _row(p['ret_norm_w'][0]), _row(p['hgrn_norm_w'][0]), _row(lb),
        (states['ret'][0], states['hgrn'][0]) if has_state else None, bb, c)
    stacked_rows = lambda a: a.astype(F32)[:, None, :]
    ffn = (stacked_rows(p['norm_mix_post']), stacked_rows(p['norm_ffn_pre']),
           stacked_rows(p['norm_ffn_post']), p['w_ffn_up'].astype(BF16), p['w_ffn_down'].astype(BF16))
    h = _post_call(x.reshape(n, d), [mix.reshape(n, -1)], p['w_out_even'][0].astype(BF16), *ffn, 0,
                   cfg['rows'])
    d_inner = H_SSD * P_SSD
    conv_dim = d_inner + 2 * G_SSD * N_SSD
    u_w = S5_G * S5_GS
    w1 = p['w_in_odd'][0].astype(BF16)
    st_shape = (nb, G_SSD, d_inner // G_SSD, N_SSD)
    odd_states = None
    if has_state:
        odd_states = (states['conv'][0], jnp.swapaxes(states['ssm'][0], -1, -2).reshape(st_shape))
    y, u, conv_o, sst_o = _odd_call(
        h.reshape(nb, t, d), _row(p['norm_mix_pre'][1]), w1, p['conv_w'][0].astype(F32),
        _row(p['conv_b'][0]), _pad_lanes(p['dt_bias'][0]), _pad_lanes(p['a_log'][0]),
        _row(jnp.repeat(p['d_ssm'][0], P_SSD)), _row(p['ssm_norm_w'][0]), odd_states, bb, c)
    ssm_o = jnp.swapaxes(sst_o.reshape(nb, H_SSD, P_SSD, N_SSD), -1, -2)
    bmat, cmat, lre, lim = _s5_tables(p['s5_lam_re'][0], p['s5_lam_im'][0], p['s5_log_step'][0],
                                      p['s5_b_re'][0], p['s5_b_im'][0], p['s5_c_re'][0],
                                      p['s5_c_im'][0])
    ns = S5_G * S5_P
    if has_state:
        h0re = states['s5_re'][0].reshape(nb, ns)
        h0im = states['s5_im'][0].reshape(nb, ns)
    else:
        h0re = jnp.zeros((nb, ns), F32)
        h0im = jnp.zeros((nb, ns), F32)
    yd, re_o, im_o = _s5_call(u, bmat, cmat, lre, lim, _row(p['s5_d'][0]),
                              p['w_glu'][0].astype(BF16), _row(p['b_glu'][0]), h0re, h0im,
                              cfg['s5_steps'], cfg['s5_rows'])
    yd = yd.reshape(n, u_w)
    h = _post_call(h, [y.reshape(n, d_inner), yd], p['w_out_odd'][0].astype(BF16), *ffn, 1, cfg['rows'])
    return (h.reshape(nb, t, d), ret_o[None], hg_o[None], ssm_o[None], conv_o[None],
            re_o.reshape(1, nb, S5_G, S5_P), im_o.reshape(1, nb, S5_G, S5_P))


PROMPT_CFG = dict(bb=4, c=128, rows=1024, s5_steps=64, s5_rows=8)
SAMPLE_CFG = dict(bb=16, c=8, rows=1024, s5_steps=8, s5_rows=32)


def kernel(x_prompt, x_sample, state_ret, state_hgrn, state_ssm, state_conv, state_s5_re, state_s5_im,
           norm_mix_pre, norm_mix_post, norm_ffn_pre, norm_ffn_post, w_in_even, w_out_even, ret_norm_w,
           hgrn_lower_bounds, hgrn_norm_w, w_in_odd, conv_w, conv_b, dt_bias, a_log, d_ssm, ssm_norm_w,
           s5_lam_re, s5_lam_im, s5_log_step, s5_b_re, s5_b_im, s5_c_re, s5_c_im, s5_d, w_glu, b_glu,
           w_out_odd, w_ffn_up, w_ffn_down):
    p = dict(norm_mix_pre=norm_mix_pre, norm_mix_post=norm_mix_post, norm_ffn_pre=norm_ffn_pre,
             norm_ffn_post=norm_ffn_post, w_in_even=w_in_even, w_out_even=w_out_even, ret_norm_w=ret_norm_w,
             hgrn_lower_bounds=hgrn_lower_bounds, hgrn_norm_w=hgrn_norm_w, w_in_odd=w_in_odd, conv_w=conv_w,
             conv_b=conv_b, dt_bias=dt_bias, a_log=a_log, d_ssm=d_ssm, ssm_norm_w=ssm_norm_w,
             s5_lam_re=s5_lam_re, s5_lam_im=s5_lam_im, s5_log_step=s5_log_step, s5_b_re=s5_b_re,
             s5_b_im=s5_b_im, s5_c_re=s5_c_re, s5_c_im=s5_c_im, s5_d=s5_d, w_glu=w_glu, b_glu=b_glu,
             w_out_odd=w_out_odd, w_ffn_up=w_ffn_up, w_ffn_down=w_ffn_down)
    past_len = 16384
    states = dict(ret=state_ret, hgrn=state_hgrn, ssm=state_ssm, conv=state_conv,
                  s5_re=state_s5_re, s5_im=state_s5_im)
    y_p, ret_p, hg_p, ssm_p, conv_p, re_p, im_p = _forward(x_prompt, 0, None, p, PROMPT_CFG)
    y_s, ret_s, hg_s, ssm_s, conv_s, re_s, im_s = _forward(x_sample, past_len, states, p, SAMPLE_CFG)
    return (y_p, y_s, ret_p, ret_s, hg_p, hg_s, ssm_p, ssm_s, conv_p, conv_s,
            re_p, re_s, im_p, im_s)
```

```python
import functools
import math

import jax
import jax.numpy as jnp
import numpy as np
from jax import lax
from jax.experimental import pallas as pl
from jax.experimental.pallas import tpu as pltpu

F32 = jnp.float32
BF16 = jnp.bfloat16
EPS = 1e-6
ROPE_BASE = 10000.0
LANES = 128
SUBLANES = 8
VMEM_LIMIT = 56 * 1024 * 1024

H_RET = 4
H_HG = 4
HD = 128
H_SSD = 16
P_SSD = 64
N_SSD = 128
G_SSD = 2
S5_G = 32
S5_GS = 16
S5_P = 64
HG_BLK = 16


def _rms(x, w):
    return x * lax.rsqrt(jnp.mean(x * x, axis=-1, keepdims=True) + EPS) * w


def _silu(x):
    return x * jax.nn.sigmoid(x)


def _dot(a, b):
    return jnp.dot(a.astype(BF16), b.astype(BF16), preferred_element_type=F32)


def _dot_nt(a, b):
    return lax.dot_general(a.astype(BF16), b.astype(BF16), (((1,), (1,)), ((), ())),
                           preferred_element_type=F32)


def _dot_tn(a, b):
    return lax.dot_general(a.astype(BF16), b.astype(BF16), (((0,), (0,)), ((), ())),
                           preferred_element_type=F32)


def _split3(x):
    hi = x.astype(BF16)
    r1 = x - hi.astype(F32)
    mid = r1.astype(BF16)
    lo = (r1 - mid.astype(F32)).astype(BF16)
    return hi, mid, lo


def _dot3(a_exact, parts):
    acc = None
    for p in parts:
        d = jnp.dot(a_exact, p, preferred_element_type=F32)
        acc = d if acc is None else acc + d
    return acc


def _dot3_tn(parts, b_exact):
    acc = None
    for p in parts:
        d = lax.dot_general(p, b_exact, (((0,), (0,)), ((), ())), preferred_element_type=F32)
        acc = d if acc is None else acc + d
    return acc


def _ones_where(mask):
    return jnp.where(mask, 1.0, 0.0).astype(BF16)


def _seq_masks(rows, seq_len):
    sh = seq_len.bit_length() - 1
    ti = lax.broadcasted_iota(jnp.int32, (rows, rows), 0)
    si = lax.broadcasted_iota(jnp.int32, (rows, rows), 1)
    same = lax.shift_right_logical(ti, sh) == lax.shift_right_logical(si, sh)
    return ti, si, same, same & (ti >= si)


def _interleave(units, fillers=(), per_round=2):
    live = list(units)
    fillers = list(fillers)
    while live:
        still = []
        for g in live:
            try:
                next(g)
                still.append(g)
            except StopIteration:
                pass
        live = still
        for f in fillers[:per_round]:
            f()
        fillers = fillers[per_round:]
    for f in fillers:
        f()


def _const_spec(shape):
    nd = len(shape)
    return pl.BlockSpec(shape, lambda *_: (0,) * nd, pipeline_mode=pl.Buffered(1))


def _post_kernel(n_mix, ff_chunk, n_parts, h_ref, *refs):
    mix_refs = refs[:n_mix]
    wout_ref, npost_ref, nfpre_ref, nfpost_ref, wup_ref, wdn_ref, o_ref = refs[n_mix:]
    part = h_ref.shape[0] // n_parts
    groups = [slice(i * part, (i + 1) * part) for i in range(n_parts)]
    accs = []
    for rs in groups:
        acc, off = None, 0
        for m_ref in mix_refs:
            w = m_ref.shape[1]
            d = jnp.dot(m_ref[rs, :], wout_ref[off:off + w, :], preferred_element_type=F32)
            acc = d if acc is None else acc + d
            off += w
        accs.append(acc)
    for rs, acc in zip(groups, accs):
        h1 = h_ref[rs, :] + _rms(acc, npost_ref[...])
        hn = _rms(h1, nfpre_ref[...]).astype(BF16)
        ff = None
        for j in range(wup_ref.shape[1] // ff_chunk):
            sl = slice(j * ff_chunk, (j + 1) * ff_chunk)
            up = jnp.dot(hn, wup_ref[:, sl], preferred_element_type=F32)
            act = jnp.square(jnp.maximum(up, 0.0)).astype(BF16)
            d = jnp.dot(act, wdn_ref[sl, :], preferred_element_type=F32)
            ff = d if ff is None else ff + d
        o_ref[rs, :] = h1 + _rms(ff, nfpost_ref[...])


def _layer_spec(shape, layer):
    nd = len(shape) - 1
    return pl.BlockSpec((None,) + tuple(shape[1:]), lambda *_: (layer,) + (0,) * nd,
                        pipeline_mode=pl.Buffered(1))


def _post_call(h, mixes, wout, npost, nfpre, nfpost, wup, wdn, layer, rows):
    n, d = h.shape
    n_mix = len(mixes)
    row_spec = lambda w: pl.BlockSpec((rows, w), lambda i: (i, 0))
    in_specs = ([row_spec(d)] + [row_spec(m.shape[1]) for m in mixes] + [_const_spec(wout.shape)]
                + [_layer_spec(a.shape, layer) for a in (npost, nfpre, nfpost, wup, wdn)])
    return pl.pallas_call(
        functools.partial(_post_kernel, n_mix, 1024, 2),
        grid=(n // rows,),
        in_specs=in_specs,
        out_specs=row_spec(d),
        out_shape=jax.ShapeDtypeStruct((n, d), F32),
        compiler_params=pltpu.CompilerParams(dimension_semantics=("arbitrary",),
                                             vmem_limit_bytes=VMEM_LIMIT),
        name="post_ffn",
    )(h, *mixes, wout, npost, nfpre, nfpost, wup, wdn)


def _ret_decay_tables(bb, seq_len):
    rows = bb * seq_len
    t = np.arange(rows)
    tl = t % seq_len
    valid = (t[:, None] // seq_len == t[None, :] // seq_len) & (t[:, None] >= t[None, :])
    pair, to_row, to_end = [], [], []
    for h in range(H_RET):
        lg = math.log1p(-(2.0 ** (-5.0 - h)))
        pair.append(np.where(valid, np.exp((t[:, None] - t[None, :]) * lg), 0.0))
        to_row.append(np.broadcast_to(np.exp((tl + 1.0) * lg)[:, None], (rows, HD)))
        to_end.append(np.broadcast_to(np.exp((seq_len - 1.0 - tl) * lg)[:, None], (rows, HD)))
    return tuple(jnp.asarray(np.stack(a), F32) for a in (pair, to_row, to_end))


def _even_block(bb, seq_len, proj_s, cos, sin, lb, retw_ref, hgw_ref, dec_refs, oret_ref, ohg_ref,
                mix_ref):
    rows = bb * seq_len
    hw = H_RET * HD
    base = 4 * hw
    ti, si, same, causal = _seq_masks(rows, seq_len)
    pair_ref, to_row_ref, to_end_ref = dec_refs
    seqs = [slice(b * seq_len, (b + 1) * seq_len) for b in range(bb)]
    one_shot = seq_len <= HG_BLK

    f = lb + (1.0 - lb) * jax.nn.sigmoid(proj_s[:, base + hw:base + 2 * hw])
    lf_parts = _split3(jnp.log(f))
    cum = _dot3(_ones_where(causal), lf_parts)
    cl = cum[rows - 1:rows, :] if bb == 1 else _dot3(_ones_where(same), lf_parts)
    if one_shot:
        refc = _dot3(_ones_where(same & ((si & (seq_len - 1)) < seq_len // 2)), lf_parts)

    yield
    r_a, r_qs, r_kv, r_v = [], [], [], []
    for h in range(H_RET):
        col = lambda off: slice(off + h * HD, off + (h + 1) * HD)
        q = proj_s[:, col(0)]
        k = proj_s[:, col(hw)]
        v = proj_s[:, col(2 * hw)]
        q = q * cos + pltpu.roll(q, HD // 2, axis=1) * sin
        k = (k * cos + pltpu.roll(k, HD // 2, axis=1) * sin) * (HD ** -0.5)
        kd = k * to_end_ref[h]
        r_a.append(_dot_nt(q, k))
        r_qs.append([_dot(q[s], oret_ref[b, h]) for b, s in enumerate(seqs)])
        r_kv.append([_dot_tn(kd[s], v[s]) for s in seqs])
        r_v.append(v)

    g_a, g_qs, g_kv, g_v = [], [], [], []
    for h in range(H_HG):
        hs = slice(h * HD, (h + 1) * HD)
        col = lambda off: slice(base + off + h * HD, base + off + (h + 1) * HD)
        q = proj_s[:, col(0)]
        v = proj_s[:, col(2 * hw)]
        k = 1.0 - f[:, hs]
        cum_h = cum[:, hs]
        qe = q * jnp.exp(cum_h)
        ke = k * jnp.exp(cl[:, hs] - cum_h)
        g_qs.append([_dot(qe[s], ohg_ref[b, h]) for b, s in enumerate(seqs)])
        g_kv.append([_dot_tn(ke[s], v[s]) for s in seqs])
        if one_shot:
            ref = refc[:, hs]
            g_a.append([_dot_nt(q * jnp.exp(cum_h - ref), k * jnp.exp(ref - cum_h))])
        else:
            blocks = []
            for i in range(rows // HG_BLK):
                lo, hi = i * HG_BLK, (i + 1) * HG_BLK
                ref = cum_h[lo + HG_BLK // 2 - 1:lo + HG_BLK // 2, :]
                blocks.append(_dot_nt(q[lo:hi] * jnp.exp(cum_h[lo:hi] - ref),
                                      k[:hi] * jnp.exp(ref - cum_h[:hi])))
            g_a.append(blocks)
        g_v.append(v)

    yield
    r_o = []
    for h in range(H_RET):
        lg = math.log1p(-(2.0 ** (-5.0 - h)))
        qs = r_qs[h][0] if bb == 1 else jnp.concatenate(r_qs[h], axis=0)
        r_o.append(_dot(r_a[h] * pair_ref[h], r_v[h]) + qs * to_row_ref[h])
        for b in range(bb):
            oret_ref[b, h] = oret_ref[b, h] * math.exp(seq_len * lg) + r_kv[h][b]

    g_o = []
    for h in range(H_HG):
        if one_shot:
            o = _dot(jnp.where(causal, g_a[h][0], 0.0), g_v[h])
        else:
            wide = [a if a.shape[1] == rows else
                    jnp.concatenate([a, jnp.zeros((HG_BLK, rows - a.shape[1]), F32)], axis=1)
                    for a in g_a[h]]
            o = _dot(jnp.where(causal, jnp.concatenate(wide, axis=0), 0.0), g_v[h])
        qs = g_qs[h][0] if bb == 1 else jnp.concatenate(g_qs[h], axis=0)
        g_o.append(o + qs)
        for b in range(bb):
            r0 = b * seq_len if bb > 1 else 0
            total = cl[r0:r0 + 1, h * HD:(h + 1) * HD]
            ohg_ref[b, h] = ohg_ref[b, h] * jnp.exp(jnp.broadcast_to(total, (HD, HD)).T) + g_kv[h][b]

    yield
    for h in range(H_RET):
        col = lambda off: slice(off + h * HD, off + (h + 1) * HD)
        mu = jnp.mean(r_o[h], axis=-1, keepdims=True)
        oc = r_o[h] - mu
        var = jnp.mean(oc * oc, axis=-1, keepdims=True)
        o = oc * lax.rsqrt(var + EPS) * retw_ref[:, col(0)] * _silu(proj_s[:, col(3 * hw)])
        mix_ref[:, :, col(0)] = o.reshape(bb, seq_len, HD).astype(mix_ref.dtype)
    for h in range(H_HG):
        hs = slice(h * HD, (h + 1) * HD)
        g = proj_s[:, base + 3 * hw + h * HD:base + 3 * hw + (h + 1) * HD]
        o = _rms(g_o[h], hgw_ref[:, hs]) * _silu(g)
        mix_ref[:, :, hw + h * HD:hw + (h + 1) * HD] = o.reshape(bb, seq_len, HD).astype(mix_ref.dtype)


PROJ_COLS = 512


def _project(x_ref, npre_ref, win_ref, proj_s):
    bb, seq_len, d = x_ref.shape
    hn = _rms(x_ref[...].reshape(bb * seq_len, d), npre_ref[...]).astype(BF16)
    n_in = win_ref.shape[1]

    def chunk(lo):
        def run():
            sl = slice(lo, min(lo + PROJ_COLS, n_in))
            proj_s[:, sl] = jnp.dot(hn, win_ref[:, sl], preferred_element_type=F32)
        return run

    return [chunk(lo) for lo in range(0, n_in, PROJ_COLS)]


def _projection_ahead(x_ref, xn_ref, npre_ref, win_ref, proj_s, proj_next_s):
    @pl.when((pl.program_id(0) == 0) & (pl.program_id(1) == 0))
    def _():
        for f in _project(x_ref, npre_ref, win_ref, proj_s):
            f()

    return _project(xn_ref, npre_ref, win_ref, proj_next_s)


def _carry_projection(proj_s, proj_next_s):
    n_in = proj_s.shape[1]
    for lo in range(0, n_in, PROJ_COLS):
        sl = slice(lo, min(lo + PROJ_COLS, n_in))
        proj_s[:, sl] = proj_next_s[:, sl]


def _next_block_map(n_blocks, nt):
    def imap(b, i):
        s = jnp.minimum(b * nt + i + 1, n_blocks * nt - 1)
        return (s // nt, s % nt, 0)
    return imap


def _even_kernel(has_state, x_ref, xn_ref, cos_ref, sin_ref, npre_ref, win_ref, retw_ref, hgw_ref,
                 lb_ref, pair_ref, to_row_ref, to_end_ref, *refs):
    if has_state:
        sret_ref, shg_ref, mix_ref, oret_ref, ohg_ref, proj_s, proj_next_s = refs
    else:
        mix_ref, oret_ref, ohg_ref, proj_s, proj_next_s = refs
    bb, seq_len, _ = x_ref.shape
    dec_refs = (pair_ref, to_row_ref, to_end_ref)

    @pl.when(pl.program_id(1) == 0)
    def _():
        if has_state:
            oret_ref[...] = sret_ref[...]
            ohg_ref[...] = shg_ref[...]
        else:
            oret_ref[...] = jnp.zeros(oret_ref.shape, F32)
            ohg_ref[...] = jnp.zeros(ohg_ref.shape, F32)

    fillers = _projection_ahead(x_ref, xn_ref, npre_ref, win_ref, proj_s, proj_next_s)
    consts = (lb_ref[...], retw_ref, hgw_ref, dec_refs)
    if seq_len > HG_BLK:
        units = []
        for u in range(bb):
            rs, one = pl.ds(u * seq_len, seq_len), pl.ds(u, 1)
            units.append(_even_block(1, seq_len, proj_s.at[rs], cos_ref[...], sin_ref[...], *consts,
                                     oret_ref.at[one], ohg_ref.at[one], mix_ref.at[one]))
    else:
        units = [_even_block(bb, seq_len, proj_s, cos_ref[...], sin_ref[...], *consts,
                             oret_ref, ohg_ref, mix_ref)]
    _interleave(units, fillers)
    _carry_projection(proj_s, proj_next_s)


def _even_call(x, cos, sin, npre, win, retw, hgw, lb, states, bb, c):
    nb_total, t, d = x.shape
    has_state = states is not None
    n_in = win.shape[1]
    rows = bb * c
    grid = (nb_total // bb, t // c)
    tab_rows = c
    if c <= HG_BLK:
        tab_rows = rows
        cos, sin = (jnp.tile(a.reshape(t // c, 1, c, HD), (1, bb, 1, 1)).reshape(-1, HD) for a in (cos, sin))
    st_spec = pl.BlockSpec((bb, H_RET, HD, HD), lambda b, i: (b, 0, 0, 0))
    in_specs = [pl.BlockSpec((bb, c, d), lambda b, i: (b, i, 0)),
                pl.BlockSpec((bb, c, d), _next_block_map(*grid)),
                pl.BlockSpec((tab_rows, HD), lambda b, i: (i, 0)),
                pl.BlockSpec((tab_rows, HD), lambda b, i: (i, 0)),
                _const_spec((1, d)), _const_spec(win.shape),
                _const_spec((1, H_RET * HD)), _const_spec((1, H_HG * HD)), _const_spec((1, H_HG * HD))]
    tables = _ret_decay_tables(bb if c <= HG_BLK else 1, c)
    in_specs += [_const_spec(a.shape) for a in tables]
    args = [x, x, cos, sin, npre, win, retw, hgw, lb, *tables]
    if has_state:
        in_specs += [st_spec, st_spec]
        args += list(states)
    mix_w = (H_RET + H_HG) * HD
    return pl.pallas_call(
        functools.partial(_even_kernel, has_state),
        grid=grid,
        in_specs=in_specs,
        out_specs=[pl.BlockSpec((bb, c, mix_w), lambda b, i: (b, i, 0)), st_spec, st_spec],
        out_shape=[jax.ShapeDtypeStruct((nb_total, t, mix_w), BF16),
                   jax.ShapeDtypeStruct((nb_total, H_RET, HD, HD), F32),
                   jax.ShapeDtypeStruct((nb_total, H_HG, HD, HD), F32)],
        scratch_shapes=[pltpu.VMEM((rows, n_in), F32), pltpu.VMEM((rows, n_in), F32)],
        compiler_params=pltpu.CompilerParams(dimension_semantics=("arbitrary", "arbitrary"),
                                             vmem_limit_bytes=VMEM_LIMIT),
        name="even_mixer",
    )(*args)


def _odd_block(has_state, bb, seq_len, proj_s, tail_s, ost_ref, sst_s, convw_ref, convb_ref, dtb_ref,
               alog_ref, dssm_ref, ssmw_ref, y_ref, u_ref):
    rows = bb * seq_len
    d_inner = H_SSD * P_SSD
    conv_dim = d_inner + 2 * G_SSD * N_SSD
    conv_w = convw_ref.shape[0]
    halo = conv_w - 1
    u_w = u_ref.shape[2]
    gw = d_inner // G_SSD
    heads_per_group = H_SSD // G_SSD
    pairs_per_group = heads_per_group // 2
    seqs = [slice(b * seq_len, (b + 1) * seq_len) for b in range(bb)]

    z = proj_s[:, 0:d_inner]
    xbc = proj_s[:, d_inner:d_inner + conv_dim]
    c0 = d_inner + conv_dim
    dt_raw = proj_s[:, c0:c0 + LANES]
    u_ref[...] = proj_s[:, c0 + H_SSD:c0 + H_SSD + u_w].reshape(bb, seq_len, u_w)

    tail = tail_s[...]
    conv = convb_ref[...] + xbc * convw_ref[halo:halo + 1, :]
    for s in range(1, conv_w):
        rolled = pltpu.roll(xbc, s, axis=0)
        if seq_len == SUBLANES:
            local = lax.broadcasted_iota(jnp.int32, (rows, conv_dim), 0) & (SUBLANES - 1)
            shifted = jnp.where(local < s, pltpu.roll(tail, rows - SUBLANES + s, axis=0), rolled)
        else:
            local = lax.broadcasted_iota(jnp.int32, (SUBLANES, conv_dim), 0)
            head = jnp.where(local < s, pltpu.roll(tail, s, axis=0), rolled[0:SUBLANES])
            shifted = jnp.concatenate([head, rolled[SUBLANES:]], axis=0)
        conv = conv + shifted * convw_ref[halo - s:halo - s + 1, :]
    tail_s[...] = xbc if seq_len == SUBLANES else xbc[rows - SUBLANES:rows]
    act = _silu(conv)
    xs = act[:, :d_inner]

    ti, si, same, causal = _seq_masks(rows, seq_len)
    lo_half = lax.broadcasted_iota(jnp.int32, (rows, LANES), 1) < P_SSD
    dt = jax.nn.softplus(dt_raw + dtb_ref[...])
    la_parts = _split3(dt * (-jnp.exp(alog_ref[...])))
    cum = _dot3(_ones_where(causal), la_parts)
    cum_t = _dot3_tn(la_parts, _ones_where(same & (ti <= si)))
    if bb > 1:
        rest = _dot3(_ones_where(same & (si > ti)), la_parts)
    bms = [act[:, d_inner + g * N_SSD:d_inner + (g + 1) * N_SSD] for g in range(G_SSD)]
    cms = [act[:, d_inner + (G_SSD + g) * N_SSD:d_inner + (G_SSD + g + 1) * N_SSD]
           for g in range(G_SSD)]
    scores = [_dot_nt(cms[g], bms[g]) for g in range(G_SSD)]
    yield

    bc = lambda a, h: jnp.broadcast_to(a[:, h:h + 1], (rows, LANES))
    bc_cum = [bc(cum, h) for h in range(H_SSD)]
    pair = lambda a, p: jnp.where(lo_half, bc(a, 2 * p), bc(a, 2 * p + 1))

    for g in range(G_SSD):
        xdts, xws, cums = [], [], []
        for pp in range(pairs_per_group):
            p = g * pairs_per_group + pp
            xdt = xs[:, p * LANES:(p + 1) * LANES] * pair(dt, p)
            cum_p = jnp.where(lo_half, bc_cum[2 * p], bc_cum[2 * p + 1])
            rest_p = cum_p[rows - 1:rows, :] - cum_p if bb == 1 else pair(rest, p)
            xdts.append(xdt)
            xws.append(xdt * jnp.exp(rest_p))
            cums.append(cum_p)
        xw = jnp.concatenate(xws, axis=1)
        if has_state:
            y_st = [_dot_nt(cms[g][s], ost_ref[b, g]) for b, s in enumerate(seqs)]
            kv = [_dot_tn(xw[s], bms[g][s]) for s in seqs]
        else:
            y_st = [_dot(cms[g][s], sst_s[b, g]) for b, s in enumerate(seqs)]
            kv = [_dot_tn(bms[g][s], xw[s]) for s in seqs]
        y_intra = []
        for pp in range(pairs_per_group):
            p = g * pairs_per_group + pp
            a_pair = []
            for h in (2 * p, 2 * p + 1):
                diff = bc_cum[h][:, :rows] - cum_t[h:h + 1, :]
                a_pair.append(scores[g] * jnp.where(causal, jnp.exp(jnp.where(causal, diff, 0.0)), 0.0))
            both = _dot(jnp.concatenate(a_pair, axis=0), xdts[pp])
            y_intra.append(jnp.where(lo_half, both[:rows], both[rows:]))
        yield
        for b in range(bb):
            last = (b + 1) * seq_len - 1
            if has_state:
                dec = [jnp.broadcast_to(jnp.exp(bc_cum[g * heads_per_group + hh][last:last + 1, :]),
                                        (P_SSD, N_SSD)) for hh in range(heads_per_group)]
                ost_ref[b, g] = ost_ref[b, g] * jnp.concatenate(dec, axis=0) + kv[b]
            else:
                dec = jnp.exp(jnp.concatenate([c_p[last:last + 1, :] for c_p in cums], axis=1))
                sst_s[b, g] = sst_s[b, g] * dec + kv[b]
        y_state = y_st[0] if bb == 1 else jnp.concatenate(y_st, axis=0)
        gs = slice(g * gw, (g + 1) * gw)
        y = (jnp.concatenate(y_intra, axis=1) + y_state * jnp.exp(jnp.concatenate(cums, axis=1))
             + dssm_ref[:, gs] * xs[:, gs])
        y = _rms(y * _silu(z[:, gs]), ssmw_ref[:, gs])
        y_ref[:, :, gs] = y.reshape(bb, seq_len, gw).astype(y_ref.dtype)


def _odd_kernel(has_state, x_ref, xn_ref, npre_ref, win_ref, convw_ref, convb_ref, dtb_ref, alog_ref,
                dssm_ref, ssmw_ref, *refs):
    if has_state:
        sconv_ref, sst_ref, y_ref, u_ref, oconv_ref, ost_ref, proj_s, proj_next_s, tail_s = refs
        sst_s = None
    else:
        y_ref, u_ref, oconv_ref, ost_ref, proj_s, proj_next_s, tail_s, sst_s = refs
    bb, seq_len, _ = x_ref.shape
    halo = convw_ref.shape[0] - 1

    @pl.when(pl.program_id(1) == 0)
    def _():
        tail_s[...] = jnp.zeros(tail_s.shape, F32)
        if has_state:
            for b in range(bb):
                tail_s[(b + 1) * SUBLANES - halo:(b + 1) * SUBLANES, :] = sconv_ref[b]
            ost_ref[...] = sst_ref[...]
        else:
            sst_s[...] = jnp.zeros(sst_s.shape, F32)

    fillers = _projection_ahead(x_ref, xn_ref, npre_ref, win_ref, proj_s, proj_next_s)
    consts = (convw_ref, convb_ref, dtb_ref, alog_ref, dssm_ref, ssmw_ref)
    if seq_len > SUBLANES:
        units = []
        for u in range(bb):
            one = pl.ds(u, 1)
            units.append(_odd_block(
                has_state, 1, seq_len, proj_s.at[pl.ds(u * seq_len, seq_len)],
                tail_s.at[pl.ds(u * SUBLANES, SUBLANES)], ost_ref.at[one],
                None if sst_s is None else sst_s.at[one], *consts, y_ref.at[one], u_ref.at[one]))
    else:
        units = [_odd_block(has_state, bb, seq_len, proj_s, tail_s, ost_ref, sst_s, *consts, y_ref, u_ref)]
    _interleave(units, fillers, per_round=3)
    _carry_projection(proj_s, proj_next_s)

    @pl.when(pl.program_id(1) == pl.num_programs(1) - 1)
    def _():
        for b in range(bb):
            oconv_ref[b] = tail_s[(b + 1) * SUBLANES - halo:(b + 1) * SUBLANES, :]
            if not has_state:
                for g in range(G_SSD):
                    for ps in range(0, ost_ref.shape[2], LANES):
                        ost_ref[b, g, ps:ps + LANES, :] = sst_s[b, g, :, ps:ps + LANES].T


def _odd_call(x, npre, win, convw, convb, dtb, alog, dssm, ssmw, states, bb, c):
    nb_total, t, d = x.shape
    has_state = states is not None
    n_in = win.shape[1]
    d_inner = H_SSD * P_SSD
    conv_dim = d_inner + 2 * G_SSD * N_SSD
    u_w = S5_G * S5_GS
    halo = convw.shape[0] - 1
    rows = bb * c
    grid = (nb_total // bb, t // c)
    st_shape = (G_SSD, d_inner // G_SSD, N_SSD)
    conv_spec = pl.BlockSpec((bb, halo, conv_dim), lambda b, i: (b, 0, 0))
    ssm_spec = pl.BlockSpec((bb,) + st_shape, lambda b, i: (b, 0, 0, 0))
    in_specs = [pl.BlockSpec((bb, c, d), lambda b, i: (b, i, 0)),
                pl.BlockSpec((bb, c, d), _next_block_map(*grid)),
                _const_spec((1, d)), _const_spec(win.shape), _const_spec(convw.shape),
                _const_spec((1, conv_dim)), _const_spec((1, LANES)), _const_spec((1, LANES)),
                _const_spec((1, d_inner)), _const_spec((1, d_inner))]
    args = [x, x, npre, win, convw, convb, dtb, alog, dssm, ssmw]
    scratch = [pltpu.VMEM((rows, n_in), F32), pltpu.VMEM((rows, n_in), F32),
               pltpu.VMEM((bb * SUBLANES, conv_dim), F32)]
    if has_state:
        in_specs += [conv_spec, ssm_spec]
        args += list(states)
    else:
        scratch.append(pltpu.VMEM((bb, G_SSD, N_SSD, d_inner // G_SSD), F32))
    return pl.pallas_call(
        functools.partial(_odd_kernel, has_state),
        grid=grid,
        in_specs=in_specs,
        out_specs=[pl.BlockSpec((bb, c, d_inner), lambda b, i: (b, i, 0)),
                   pl.BlockSpec((bb, c, u_w), lambda b, i: (b, i, 0)),
                   conv_spec, ssm_spec],
        out_shape=[jax.ShapeDtypeStruct((nb_total, t, d_inner), BF16),
                   jax.ShapeDtypeStruct((nb_total, t, u_w), F32),
                   jax.ShapeDtypeStruct((nb_total, halo, conv_dim), F32),
                   jax.ShapeDtypeStruct((nb_total,) + st_shape, F32)],
        scratch_shapes=scratch,
        compiler_params=pltpu.CompilerParams(dimension_semantics=("arbitrary", "arbitrary"),
                                             vmem_limit_bytes=VMEM_LIMIT),
        name="odd_mixer",
    )(*args)


S5_PARTS = 4
S5_PART_IN = S5_G * S5_GS // S5_PARTS
S5_PART_ST = S5_G * S5_P // S5_PARTS
S5_SCAN_VREGS = 4


def _s5_pitch(steps):
    return steps if steps % 16 == 8 else steps + 8


def _s5_kernel(u_ref, bmat_ref, cmat_ref, lre_ref, lim_ref, d_ref, wglu_ref, bglu_ref,
               h0re_ref, h0im_ref, yd_ref, ore_ref, oim_ref, buf_s, slab_s, ut_s):
    r, steps, uw = u_ref.shape
    hs = S5_PART_ST
    scan_w = min(hs, max(LANES, S5_SCAN_VREGS * SUBLANES * LANES // r))
    pitch = _s5_pitch(steps)
    nslab = uw // LANES

    @pl.when(pl.program_id(1) == 0)
    def _():
        ore_ref[...] = h0re_ref[...]
        oim_ref[...] = h0im_ref[...]

    for b in range(r):
        for j in range(nslab):
            slab_s[j, b * pitch:b * pitch + steps, :] = u_ref[b, :, j * LANES:(j + 1) * LANES]
    for t in range(steps):
        for j in range(nslab):
            ut_s[t * r:(t + 1) * r, j * LANES:(j + 1) * LANES] = slab_s[j, pl.ds(t, r, stride=pitch), :]
    u = ut_s[...]
    ub = u.astype(BF16)
    for half in range(S5_PARTS):
        buf_s[:, half * 2 * hs:(half + 1) * 2 * hs] = jnp.dot(
            ub[:, half * S5_PART_IN:(half + 1) * S5_PART_IN], bmat_ref[half],
            preferred_element_type=F32)

    for half in range(S5_PARTS):
        for j in range(hs // scan_w):
            st = slice(half * hs + j * scan_w, half * hs + (j + 1) * scan_w)
            cre = half * 2 * hs + j * scan_w
            cim = cre + hs
            lre = jnp.broadcast_to(lre_ref[:, st], (r, scan_w))
            lim = jnp.broadcast_to(lim_ref[:, st], (r, scan_w))
            hre, him = ore_ref[:, st], oim_ref[:, st]
            for t in range(steps):
                rws = slice(t * r, (t + 1) * r)
                hre, him = (lre * hre - lim * him + buf_s[rws, cre:cre + scan_w],
                            lre * him + lim * hre + buf_s[rws, cim:cim + scan_w])
                buf_s[rws, cre:cre + scan_w] = hre
                buf_s[rws, cim:cim + scan_w] = him
            ore_ref[:, st] = hre
            oim_ref[:, st] = him

    ys = []
    for half in range(S5_PARTS):
        ys.append(jnp.dot(buf_s[:, half * 2 * hs:(half + 1) * 2 * hs].astype(BF16), cmat_ref[half],
                          preferred_element_type=F32))
    y = jnp.concatenate(ys, axis=1) + d_ref[...] * u
    g = jax.nn.gelu(y)
    gate = jax.nn.sigmoid(jnp.dot(g.astype(BF16), wglu_ref[...], preferred_element_type=F32)
                          + bglu_ref[...])
    yd = g * gate
    for t in range(steps):
        for j in range(nslab):
            slab_s[j, pl.ds(t, r, stride=pitch), :] = yd[t * r:(t + 1) * r, j * LANES:(j + 1) * LANES]
    for b in range(r):
        for j in range(nslab):
            yd_ref[b, :, j * LANES:(j + 1) * LANES] = slab_s[j, b * pitch:b * pitch + steps, :].astype(
                yd_ref.dtype)


def _s5_call(u, bmat, cmat, lre, lim, dvec, wglu, bglu, h0re, h0im, steps, r):
    nb_total, t, uw = u.shape
    ns = S5_G * S5_P
    grid = (nb_total // r, t // steps)
    st_spec = pl.BlockSpec((r, ns), lambda b, i: (b, 0))
    return pl.pallas_call(
        _s5_kernel,
        grid=grid,
        in_specs=[pl.BlockSpec((r, steps, uw), lambda b, i: (b, i, 0)),
                  _const_spec(bmat.shape), _const_spec(cmat.shape),
                  _const_spec((1, ns)), _const_spec((1, ns)), _const_spec((1, uw)),
                  _const_spec(wglu.shape), _const_spec((1, uw)), st_spec, st_spec],
        out_specs=[pl.BlockSpec((r, steps, uw), lambda b, i: (b, i, 0)), st_spec, st_spec],
        out_shape=[jax.ShapeDtypeStruct((nb_total, t, uw), BF16),
                   jax.ShapeDtypeStruct((nb_total, ns), F32),
                   jax.ShapeDtypeStruct((nb_total, ns), F32)],
        scratch_shapes=[pltpu.VMEM((steps * r, 2 * ns), F32),
                        pltpu.VMEM((uw // LANES, r * _s5_pitch(steps), LANES), F32),
                        pltpu.VMEM((steps * r, uw), F32)],
        compiler_params=pltpu.CompilerParams(dimension_semantics=("arbitrary", "arbitrary"),
                                             vmem_limit_bytes=VMEM_LIMIT),
        name="s5_mixer",
    )(u, bmat, cmat, lre, lim, dvec, wglu, bglu, h0re, h0im)


def _rope_tables(pos):
    half = HD // 2
    inv_freq = ROPE_BASE ** (-jnp.arange(half, dtype=F32) / half)
    ang = pos[:, None] * inv_freq[None, :]
    cos, sin = jnp.cos(ang), jnp.sin(ang)
    return jnp.concatenate([cos, cos], axis=-1), jnp.concatenate([-sin, sin], axis=-1)


def _pad_lanes(v):
    return jnp.pad(v.astype(F32), (0, LANES - v.shape[0]))[None, :]


def _s5_tables(lam_re, lam_im, log_step, b_re, b_im, c_re, c_im):
    lr, li = lam_re.astype(F32), lam_im.astype(F32)
    dt = jnp.exp(log_step.astype(F32))[:, None]
    mag = jnp.exp(lr * dt)
    bar_re, bar_im = mag * jnp.cos(li * dt), mag * jnp.sin(li * dt)
    den = lr * lr + li * li
    cf_re = ((bar_re - 1.0) * lr + bar_im * li) / den
    cf_im = (bar_im * lr - (bar_re - 1.0) * li) / den
    bb_re = cf_re[..., None] * b_re.astype(F32) - cf_im[..., None] * b_im.astype(F32)
    bb_im = cf_re[..., None] * b_im.astype(F32) + cf_im[..., None] * b_re.astype(F32)
    gh = S5_G // S5_PARTS
    eye = jnp.eye(gh, dtype=BF16)
    b_ri = jnp.stack([bb_re, bb_im]).astype(BF16).reshape(2, S5_PARTS, gh, S5_P, S5_GS)
    bmat = jnp.einsum('rqgpc,gk->qgcrkp', b_ri, eye).reshape(S5_PARTS, gh * S5_GS, 2 * gh * S5_P)
    c_ri = jnp.stack([c_re.astype(F32), -c_im.astype(F32)]).astype(BF16).reshape(
        2, S5_PARTS, gh, S5_GS, S5_P)
    cmat = jnp.einsum('rqgcp,gk->qrgpkc', c_ri, eye).reshape(S5_PARTS, 2 * gh * S5_P, gh * S5_GS)
    return bmat, cmat, bar_re.reshape(1, -1), bar_im.reshape(1, -1)


def _row(v):
    return v.astype(F32).reshape(1, -1)


def _forward(x, pos_offset, states, p, cfg):
    nb, t, d = x.shape
    n = nb * t
    bb, c = cfg['bb'], cfg['c']
    has_state = states is not None
    pos = jnp.arange(pos_offset, pos_offset + t, dtype=F32)
    cos, sin = _rope_tables(pos)
    lb = jnp.cumsum(jax.nn.softmax(p['hgrn_lower_bounds'].astype(F32), axis=0), axis=0)[0]
    mix, ret_o, hg_o = _even_call(
        x, cos, sin, _row(p['norm_mix_pre'][0]), p['w_in_even'][0].astype(BF16),
        _row(p['ret_norm_w'][0]), _row(p['hgrn_norm_w'][0]), _row(lb),
        (states['ret'][0], states['hgrn'][0]) if has_state else None, bb, c)
    stacked_rows = lambda a: a.astype(F32)[:, None, :]
    ffn = (stacked_rows(p['norm_mix_post']), stacked_rows(p['norm_ffn_pre']),
           stacked_rows(p['norm_ffn_post']), p['w_ffn_up'].astype(BF16), p['w_ffn_down'].astype(BF16))
    h = _post_call(x.reshape(n, d), [mix.reshape(n, -1)], p['w_out_even'][0].astype(BF16), *ffn, 0,
                   cfg['rows'])
    d_inner = H_SSD * P_SSD
    conv_dim = d_inner + 2 * G_SSD * N_SSD
    u_w = S5_G * S5_GS
    w1 = p['w_in_odd'][0].astype(BF16)
    st_shape = (nb, G_SSD, d_inner // G_SSD, N_SSD)
    odd_states = None
    if has_state:
        odd_states = (states['conv'][0], jnp.swapaxes(states['ssm'][0], -1, -2).reshape(st_shape))
    y, u, conv_o, sst_o = _odd_call(
        h.reshape(nb, t, d), _row(p['norm_mix_pre'][1]), w1, p['conv_w'][0].astype(F32),
        _row(p['conv_b'][0]), _pad_lanes(p['dt_bias'][0]), _pad_lanes(p['a_log'][0]),
        _row(jnp.repeat(p['d_ssm'][0], P_SSD)), _row(p['ssm_norm_w'][0]), odd_states, bb, c)
    ssm_o = jnp.swapaxes(sst_o.reshape(nb, H_SSD, P_SSD, N_SSD), -1, -2)
    bmat, cmat, lre, lim = _s5_tables(p['s5_lam_re'][0], p['s5_lam_im'][0], p['s5_log_step'][0],
                                      p['s5_b_re'][0], p['s5_b_im'][0], p['s5_c_re'][0],
                                      p['s5_c_im'][0])
    ns = S5_G * S5_P
    if has_state:
        h0re = states['s5_re'][0].reshape(nb, ns)
        h0im = states['s5_im'][0].reshape(nb, ns)
    else:
        h0re = jnp.zeros((nb, ns), F32)
        h0im = jnp.zeros((nb, ns), F32)
    yd, re_o, im_o = _s5_call(u, bmat, cmat, lre, lim, _row(p['s5_d'][0]),
                              p['w_glu'][0].astype(BF16), _row(p['b_glu'][0]), h0re, h0im,
                              cfg['s5_steps'], cfg['s5_rows'])
    yd = yd.reshape(n, u_w)
    h = _post_call(h, [y.reshape(n, d_inner), yd], p['w_out_odd'][0].astype(BF16), *ffn, 1, cfg['rows'])
    return (h.reshape(nb, t, d), ret_o[None], hg_o[None], ssm_o[None], conv_o[None],
            re_o.reshape(1, nb, S5_G, S5_P), im_o.reshape(1, nb, S5_G, S5_P))


PROMPT_CFG = dict(bb=4, c=128, rows=1024, s5_steps=64, s5_rows=8)
SAMPLE_CFG = dict(bb=16, c=8, rows=1024, s5_steps=8, s5_rows=32)


def kernel(x_prompt, x_sample, state_ret, state_hgrn, state_ssm, state_conv, state_s5_re, state_s5_im,
           norm_mix_pre, norm_mix_post, norm_ffn_pre, norm_ffn_post, w_in_even, w_out_even, ret_norm_w,
           hgrn_lower_bounds, hgrn_norm_w, w_in_odd, conv_w, conv_b, dt_bias, a_log, d_ssm, ssm_norm_w,
           s5_lam_re, s5_lam_im, s5_log_step, s5_b_re, s5_b_im, s5_c_re, s5_c_im, s5_d, w_glu, b_glu,
           w_out_odd, w_ffn_up, w_ffn_down):
    p = dict(norm_mix_pre=norm_mix_pre, norm_mix_post=norm_mix_post, norm_ffn_pre=norm_ffn_pre,
             norm_ffn_post=norm_ffn_post, w_in_even=w_in_even, w_out_even=w_out_even, ret_norm_w=ret_norm_w,
             hgrn_lower_bounds=hgrn_lower_bounds, hgrn_norm_w=hgrn_norm_w, w_in_odd=w_in_odd, conv_w=conv_w,
             conv_b=conv_b, dt_bias=dt_bias, a_log=a_log, d_ssm=d_ssm, ssm_norm_w=ssm_norm_w,
             s5_lam_re=s5_lam_re, s5_lam_im=s5_lam_im, s5_log_step=s5_log_step, s5_b_re=s5_b_re,
             s5_b_im=s5_b_im, s5_c_re=s5_c_re, s5_c_im=s5_c_im, s5_d=s5_d, w_glu=w_glu, b_glu=b_glu,
             w_out_odd=w_out_odd, w_ffn_up=w_ffn_up, w_ffn_down=w_ffn_down)
    past_len = 16384
    states = dict(ret=state_ret, hgrn=state_hgrn, ssm=state_ssm, conv=state_conv,
                  s5_re=state_s5_re, s5_im=state_s5_im)
    y_p, ret_p, hg_p, ssm_p, conv_p, re_p, im_p = _forward(x_prompt, 0, None, p, PROMPT_CFG)
    y_s, ret_s, hg_s, ssm_s, conv_s, re_s, im_s = _forward(x_sample, past_len, states, p, SAMPLE_CFG)
    return (y_p, y_s, ret_p, ret_s, hg_p, hg_s, ssm_p, ssm_s, conv_p, conv_s,
            re_p, re_s, im_p, im_s)
```

```python
import functools
import math

import jax
import jax.numpy as jnp
import numpy as np
from jax import lax
from jax.experimental import pallas as pl
from jax.experimental.pallas import tpu as pltpu

F32 = jnp.float32
BF16 = jnp.bfloat16
EPS = 1e-6
ROPE_BASE = 10000.0
LANES = 128
SUBLANES = 8
VMEM_LIMIT = 56 * 1024 * 1024

H_RET = 4
H_HG = 4
HD = 128
H_SSD = 16
P_SSD = 64
N_SSD = 128
G_SSD = 2
S5_G = 32
S5_GS = 16
S5_P = 64
HG_BLK = 16


def _rms(x, w):
    return x * lax.rsqrt(jnp.mean(x * x, axis=-1, keepdims=True) + EPS) * w


def _silu(x):
    return x * jax.nn.sigmoid(x)


def _dot(a, b):
    return jnp.dot(a.astype(BF16), b.astype(BF16), preferred_element_type=F32)


def _dot_nt(a, b):
    return lax.dot_general(a.astype(BF16), b.astype(BF16), (((1,), (1,)), ((), ())),
                           preferred_element_type=F32)


def _dot_tn(a, b):
    return lax.dot_general(a.astype(BF16), b.astype(BF16), (((0,), (0,)), ((), ())),
                           preferred_element_type=F32)


def _split3(x):
    hi = x.astype(BF16)
    r1 = x - hi.astype(F32)
    mid = r1.astype(BF16)
    lo = (r1 - mid.astype(F32)).astype(BF16)
    return hi, mid, lo


def _dot3(a_exact, parts):
    acc = None
    for p in parts:
        d = jnp.dot(a_exact, p, preferred_element_type=F32)
        acc = d if acc is None else acc + d
    return acc


def _dot3_tn(parts, b_exact):
    acc = None
    for p in parts:
        d = lax.dot_general(p, b_exact, (((0,), (0,)), ((), ())), preferred_element_type=F32)
        acc = d if acc is None else acc + d
    return acc


def _ones_where(mask):
    return jnp.where(mask, 1.0, 0.0).astype(BF16)


def _seq_masks(rows, seq_len):
    sh = seq_len.bit_length() - 1
    ti = lax.broadcasted_iota(jnp.int32, (rows, rows), 0)
    si = lax.broadcasted_iota(jnp.int32, (rows, rows), 1)
    same = lax.shift_right_logical(ti, sh) == lax.shift_right_logical(si, sh)
    return ti, si, same, same & (ti >= si)


def _interleave(units, fillers=(), per_round=2):
    live = list(units)
    fillers = list(fillers)
    while live:
        still = []
        for g in live:
            try:
                next(g)
                still.append(g)
            except StopIteration:
                pass
        live = still
        for f in fillers[:per_round]:
            f()
        fillers = fillers[per_round:]
    for f in fillers:
        f()


def _const_spec(shape):
    nd = len(shape)
    return pl.BlockSpec(shape, lambda *_: (0,) * nd, pipeline_mode=pl.Buffered(1))


def _post_kernel(n_mix, ff_chunk, n_parts, h_ref, *refs):
    mix_refs = refs[:n_mix]
    wout_ref, npost_ref, nfpre_ref, nfpost_ref, wup_ref, wdn_ref, o_ref = refs[n_mix:]
    part = h_ref.shape[0] // n_parts
    groups = [slice(i * part, (i + 1) * part) for i in range(n_parts)]
    accs = []
    for rs in groups:
        acc, off = None, 0
        for m_ref in mix_refs:
            w = m_ref.shape[1]
            d = jnp.dot(m_ref[rs, :], wout_ref[off:off + w, :], preferred_element_type=F32)
            acc = d if acc is None else acc + d
            off += w
        accs.append(acc)
    for rs, acc in zip(groups, accs):
        h1 = h_ref[rs, :] + _rms(acc, npost_ref[...])
        hn = _rms(h1, nfpre_ref[...]).astype(BF16)
        ff = None
        for j in range(wup_ref.shape[1] // ff_chunk):
            sl = slice(j * ff_chunk, (j + 1) * ff_chunk)
            up = jnp.dot(hn, wup_ref[:, sl], preferred_element_type=F32)
            act = jnp.square(jnp.maximum(up, 0.0)).astype(BF16)
            d = jnp.dot(act, wdn_ref[sl, :], preferred_element_type=F32)
            ff = d if ff is None else ff + d
        o_ref[rs, :] = h1 + _rms(ff, nfpost_ref[...])


def _layer_spec(shape, layer):
    nd = len(shape) - 1
    return pl.BlockSpec((None,) + tuple(shape[1:]), lambda *_: (layer,) + (0,) * nd,
                        pipeline_mode=pl.Buffered(1))


def _post_call(h, mixes, wout, npost, nfpre, nfpost, wup, wdn, layer, rows):
    n, d = h.shape
    n_mix = len(mixes)
    row_spec = lambda w: pl.BlockSpec((rows, w), lambda i: (i, 0))
    in_specs = ([row_spec(d)] + [row_spec(m.shape[1]) for m in mixes] + [_const_spec(wout.shape)]
                + [_layer_spec(a.shape, layer) for a in (npost, nfpre, nfpost, wup, wdn)])
    return pl.pallas_call(
        functools.partial(_post_kernel, n_mix, 1024, 2),
        grid=(n // rows,),
        in_specs=in_specs,
        out_specs=row_spec(d),
        out_shape=jax.ShapeDtypeStruct((n, d), F32),
        compiler_params=pltpu.CompilerParams(dimension_semantics=("arbitrary",),
                                             vmem_limit_bytes=VMEM_LIMIT),
        name="post_ffn",
    )(h, *mixes, wout, npost, nfpre, nfpost, wup, wdn)


def _ret_decay_tables(bb, seq_len):
    rows = bb * seq_len
    t = np.arange(rows)
    tl = t % seq_len
    valid = (t[:, None] // seq_len == t[None, :] // seq_len) & (t[:, None] >= t[None, :])
    pair, to_row, to_end = [], [], []
    for h in range(H_RET):
        lg = math.log1p(-(2.0 ** (-5.0 - h)))
        pair.append(np.where(valid, np.exp((t[:, None] - t[None, :]) * lg), 0.0))
        to_row.append(np.broadcast_to(np.exp((tl + 1.0) * lg)[:, None], (rows, HD)))
        to_end.append(np.broadcast_to(np.exp((seq_len - 1.0 - tl) * lg)[:, None], (rows, HD)))
    return tuple(jnp.asarray(np.stack(a), F32) for a in (pair, to_row, to_end))


def _even_block(bb, seq_len, proj_s, cos, sin, lb, retw_ref, hgw_ref, dec_refs, oret_ref, ohg_ref,
                mix_ref):
    rows = bb * seq_len
    hw = H_RET * HD
    base = 4 * hw
    ti, si, same, causal = _seq_masks(rows, seq_len)
    pair_ref, to_row_ref, to_end_ref = dec_refs
    seqs = [slice(b * seq_len, (b + 1) * seq_len) for b in range(bb)]
    one_shot = seq_len <= HG_BLK

    f = lb + (1.0 - lb) * jax.nn.sigmoid(proj_s[:, base + hw:base + 2 * hw])
    lf_parts = _split3(jnp.log(f))
    cum = _dot3(_ones_where(causal), lf_parts)
    cl = cum[rows - 1:rows, :] if bb == 1 else _dot3(_ones_where(same), lf_parts)
    if one_shot:
        refc = _dot3(_ones_where(same & ((si & (seq_len - 1)) < seq_len // 2)), lf_parts)

    yield
    r_a, r_qs, r_kv, r_v = [], [], [], []
    for h in range(H_RET):
        col = lambda off: slice(off + h * HD, off + (h + 1) * HD)
        q = proj_s[:, col(0)]
        k = proj_s[:, col(hw)]
        v = proj_s[:, col(2 * hw)]
        q = q * cos + pltpu.roll(q, HD // 2, axis=1) * sin
        k = (k * cos + pltpu.roll(k, HD // 2, axis=1) * sin) * (HD ** -0.5)
        kd = k * to_end_ref[h]
        r_a.append(_dot_nt(q, k))
        r_qs.append([_dot(q[s], oret_ref[b, h]) for b, s in enumerate(seqs)])
        r_kv.append([_dot_tn(kd[s], v[s]) for s in seqs])
        r_v.append(v)

    g_a, g_qs, g_kv, g_v = [], [], [], []
    for h in range(H_HG):
        hs = slice(h * HD, (h + 1) * HD)
        col = lambda off: slice(base + off + h * HD, base + off + (h + 1) * HD)
        q = proj_s[:, col(0)]
        v = proj_s[:, col(2 * hw)]
        k = 1.0 - f[:, hs]
        cum_h = cum[:, hs]
        qe = q * jnp.exp(cum_h)
        ke = k * jnp.exp(cl[:, hs] - cum_h)
        g_qs.append([_dot(qe[s], ohg_ref[b, h]) for b, s in enumerate(seqs)])
        g_kv.append([_dot_tn(ke[s], v[s]) for s in seqs])
        if one_shot:
            ref = refc[:, hs]
            g_a.append([_dot_nt(q * jnp.exp(cum_h - ref), k * jnp.exp(ref - cum_h))])
        else:
            blocks = []
            for s0 in range(0, rows, 2 * HG_BLK):
                if s0:
                    ref = cum_h[s0 - 1:s0, :]
                    earlier = _dot_nt(q[s0:s0 + 2 * HG_BLK] * jnp.exp(cum_h[s0:s0 + 2 * HG_BLK] - ref),
                                      k[:s0] * jnp.exp(ref - cum_h[:s0]))
                for lo in (s0, s0 + HG_BLK):
                    hi = lo + HG_BLK
                    ref = cum_h[lo + HG_BLK // 2 - 1:lo + HG_BLK // 2, :]
                    near = _dot_nt(q[lo:hi] * jnp.exp(cum_h[lo:hi] - ref),
                                   k[s0:hi] * jnp.exp(ref - cum_h[s0:hi]))
                    blocks.append(jnp.concatenate([earlier[lo - s0:hi - s0], near], axis=1) if s0 else near)
            g_a.append(blocks)
        g_v.append(v)

    yield
    r_o = []
    for h in range(H_RET):
        lg = math.log1p(-(2.0 ** (-5.0 - h)))
        qs = r_qs[h][0] if bb == 1 else jnp.concatenate(r_qs[h], axis=0)
        r_o.append(_dot(r_a[h] * pair_ref[h], r_v[h]) + qs * to_row_ref[h])
        for b in range(bb):
            oret_ref[b, h] = oret_ref[b, h] * math.exp(seq_len * lg) + r_kv[h][b]

    g_o = []
    for h in range(H_HG):
        if one_shot:
            o = _dot(jnp.where(causal, g_a[h][0], 0.0), g_v[h])
        else:
            wide = [a if a.shape[1] == rows else
                    jnp.concatenate([a, jnp.zeros((HG_BLK, rows - a.shape[1]), F32)], axis=1)
                    for a in g_a[h]]
            o = _dot(jnp.where(causal, jnp.concatenate(wide, axis=0), 0.0), g_v[h])
        qs = g_qs[h][0] if bb == 1 else jnp.concatenate(g_qs[h], axis=0)
        g_o.append(o + qs)
        for b in range(bb):
            r0 = b * seq_len if bb > 1 else 0
            total = cl[r0:r0 + 1, h * HD:(h + 1) * HD]
            ohg_ref[b, h] = ohg_ref[b, h] * jnp.exp(jnp.broadcast_to(total, (HD, HD)).T) + g_kv[h][b]

    yield
    for h in range(H_RET):
        col = lambda off: slice(off + h * HD, off + (h + 1) * HD)
        mu = jnp.mean(r_o[h], axis=-1, keepdims=True)
        oc = r_o[h] - mu
        var = jnp.mean(oc * oc, axis=-1, keepdims=True)
        o = oc * lax.rsqrt(var + EPS) * retw_ref[:, col(0)] * _silu(proj_s[:, col(3 * hw)])
        mix_ref[:, :, col(0)] = o.reshape(bb, seq_len, HD).astype(mix_ref.dtype)
    for h in range(H_HG):
        hs = slice(h * HD, (h + 1) * HD)
        g = proj_s[:, base + 3 * hw + h * HD:base + 3 * hw + (h + 1) * HD]
        o = _rms(g_o[h], hgw_ref[:, hs]) * _silu(g)
        mix_ref[:, :, hw + h * HD:hw + (h + 1) * HD] = o.reshape(bb, seq_len, HD).astype(mix_ref.dtype)


PROJ_COLS = 512


def _project(x_ref, npre_ref, win_ref, proj_s):
    bb, seq_len, d = x_ref.shape
    hn = _rms(x_ref[...].reshape(bb * seq_len, d), npre_ref[...]).astype(BF16)
    n_in = win_ref.shape[1]

    def chunk(lo):
        def run():
            sl = slice(lo, min(lo + PROJ_COLS, n_in))
            proj_s[:, sl] = jnp.dot(hn, win_ref[:, sl], preferred_element_type=F32)
        return run

    return [chunk(lo) for lo in range(0, n_in, PROJ_COLS)]


def _projection_ahead(x_ref, xn_ref, npre_ref, win_ref, proj_s, proj_next_s):
    @pl.when((pl.program_id(0) == 0) & (pl.program_id(1) == 0))
    def _():
        for f in _project(x_ref, npre_ref, win_ref, proj_s):
            f()

    return _project(xn_ref, npre_ref, win_ref, proj_next_s)


def _carry_projection(proj_s, proj_next_s):
    n_in = proj_s.shape[1]
    for lo in range(0, n_in, PROJ_COLS):
        sl = slice(lo, min(lo + PROJ_COLS, n_in))
        proj_s[:, sl] = proj_next_s[:, sl]


def _next_block_map(n_blocks, nt):
    def imap(b, i):
        s = jnp.minimum(b * nt + i + 1, n_blocks * nt - 1)
        return (s // nt, s % nt, 0)
    return imap


def _even_kernel(has_state, x_ref, xn_ref, cos_ref, sin_ref, npre_ref, win_ref, retw_ref, hgw_ref,
                 lb_ref, pair_ref, to_row_ref, to_end_ref, *refs):
    if has_state:
        sret_ref, shg_ref, mix_ref, oret_ref, ohg_ref, proj_s, proj_next_s = refs
    else:
        mix_ref, oret_ref, ohg_ref, proj_s, proj_next_s = refs
    bb, seq_len, _ = x_ref.shape
    dec_refs = (pair_ref, to_row_ref, to_end_ref)

    @pl.when(pl.program_id(1) == 0)
    def _():
        if has_state:
            oret_ref[...] = sret_ref[...]
            ohg_ref[...] = shg_ref[...]
        else:
            oret_ref[...] = jnp.zeros(oret_ref.shape, F32)
            ohg_ref[...] = jnp.zeros(ohg_ref.shape, F32)

    fillers = _projection_ahead(x_ref, xn_ref, npre_ref, win_ref, proj_s, proj_next_s)
    consts = (lb_ref[...], retw_ref, hgw_ref, dec_refs)
    if seq_len > HG_BLK:
        units = []
        for u in range(bb):
            rs, one = pl.ds(u * seq_len, seq_len), pl.ds(u, 1)
            units.append(_even_block(1, seq_len, proj_s.at[rs], cos_ref[...], sin_ref[...], *consts,
                                     oret_ref.at[one], ohg_ref.at[one], mix_ref.at[one]))
    else:
        units = [_even_block(bb, seq_len, proj_s, cos_ref[...], sin_ref[...], *consts,
                             oret_ref, ohg_ref, mix_ref)]
    _interleave(units, fillers)
    _carry_projection(proj_s, proj_next_s)


def _even_call(x, cos, sin, npre, win, retw, hgw, lb, states, bb, c):
    nb_total, t, d = x.shape
    has_state = states is not None
    n_in = win.shape[1]
    rows = bb * c
    grid = (nb_total // bb, t // c)
    tab_rows = c
    if c <= HG_BLK:
        tab_rows = rows
        cos, sin = (jnp.tile(a.reshape(t // c, 1, c, HD), (1, bb, 1, 1)).reshape(-1, HD) for a in (cos, sin))
    st_spec = pl.BlockSpec((bb, H_RET, HD, HD), lambda b, i: (b, 0, 0, 0))
    in_specs = [pl.BlockSpec((bb, c, d), lambda b, i: (b, i, 0)),
                pl.BlockSpec((bb, c, d), _next_block_map(*grid)),
                pl.BlockSpec((tab_rows, HD), lambda b, i: (i, 0)),
                pl.BlockSpec((tab_rows, HD), lambda b, i: (i, 0)),
                _const_spec((1, d)), _const_spec(win.shape),
                _const_spec((1, H_RET * HD)), _const_spec((1, H_HG * HD)), _const_spec((1, H_HG * HD))]
    tables = _ret_decay_tables(bb if c <= HG_BLK else 1, c)
    in_specs += [_const_spec(a.shape) for a in tables]
    args = [x, x, cos, sin, npre, win, retw, hgw, lb, *tables]
    if has_state:
        in_specs += [st_spec, st_spec]
        args += list(states)
    mix_w = (H_RET + H_HG) * HD
    return pl.pallas_call(
        functools.partial(_even_kernel, has_state),
        grid=grid,
        in_specs=in_specs,
        out_specs=[pl.BlockSpec((bb, c, mix_w), lambda b, i: (b, i, 0)), st_spec, st_spec],
        out_shape=[jax.ShapeDtypeStruct((nb_total, t, mix_w), BF16),
                   jax.ShapeDtypeStruct((nb_total, H_RET, HD, HD), F32),
                   jax.ShapeDtypeStruct((nb_total, H_HG, HD, HD), F32)],
        scratch_shapes=[pltpu.VMEM((rows, n_in), F32), pltpu.VMEM((rows, n_in), F32)],
        compiler_params=pltpu.CompilerParams(dimension_semantics=("arbitrary", "arbitrary"),
                                             vmem_limit_bytes=VMEM_LIMIT),
        name="even_mixer",
    )(*args)


def _odd_block(has_state, bb, seq_len, proj_s, tail_s, ost_ref, sst_s, convw_ref, convb_ref, dtb_ref,
               alog_ref, dssm_ref, ssmw_ref, y_ref, u_ref):
    rows = bb * seq_len
    d_inner = H_SSD * P_SSD
    conv_dim = d_inner + 2 * G_SSD * N_SSD
    conv_w = convw_ref.shape[0]
    halo = conv_w - 1
    u_w = u_ref.shape[2]
    gw = d_inner // G_SSD
    heads_per_group = H_SSD // G_SSD
    pairs_per_group = heads_per_group // 2
    seqs = [slice(b * seq_len, (b + 1) * seq_len) for b in range(bb)]

    z = proj_s[:, 0:d_inner]
    xbc = proj_s[:, d_inner:d_inner + conv_dim]
    c0 = d_inner + conv_dim
    dt_raw = proj_s[:, c0:c0 + LANES]
    u_ref[...] = proj_s[:, c0 + H_SSD:c0 + H_SSD + u_w].reshape(bb, seq_len, u_w)

    tail = tail_s[...]
    conv = convb_ref[...] + xbc * convw_ref[halo:halo + 1, :]
    for s in range(1, conv_w):
        rolled = pltpu.roll(xbc, s, axis=0)
        if seq_len == SUBLANES:
            local = lax.broadcasted_iota(jnp.int32, (rows, conv_dim), 0) & (SUBLANES - 1)
            shifted = jnp.where(local < s, pltpu.roll(tail, rows - SUBLANES + s, axis=0), rolled)
        else:
            local = lax.broadcasted_iota(jnp.int32, (SUBLANES, conv_dim), 0)
            head = jnp.where(local < s, pltpu.roll(tail, s, axis=0), rolled[0:SUBLANES])
            shifted = jnp.concatenate([head, rolled[SUBLANES:]], axis=0)
        conv = conv + shifted * convw_ref[halo - s:halo - s + 1, :]
    tail_s[...] = xbc if seq_len == SUBLANES else xbc[rows - SUBLANES:rows]
    act = _silu(conv)
    xs = act[:, :d_inner]

    ti, si, same, causal = _seq_masks(rows, seq_len)
    lo_half = lax.broadcasted_iota(jnp.int32, (rows, LANES), 1) < P_SSD
    dt = jax.nn.softplus(dt_raw + dtb_ref[...])
    la_parts = _split3(dt * (-jnp.exp(alog_ref[...])))
    cum = _dot3(_ones_where(causal), la_parts)
    cum_t = _dot3_tn(la_parts, _ones_where(same & (ti <= si)))
    if bb > 1:
        rest = _dot3(_ones_where(same & (si > ti)), la_parts)
    bms = [act[:, d_inner + g * N_SSD:d_inner + (g + 1) * N_SSD] for g in range(G_SSD)]
    cms = [act[:, d_inner + (G_SSD + g) * N_SSD:d_inner + (G_SSD + g + 1) * N_SSD]
           for g in range(G_SSD)]
    scores = [_dot_nt(cms[g], bms[g]) for g in range(G_SSD)]
    yield

    bc = lambda a, h: jnp.broadcast_to(a[:, h:h + 1], (rows, LANES))
    bc_cum = [bc(cum, h) for h in range(H_SSD)]
    pair = lambda a, p: jnp.where(lo_half, bc(a, 2 * p), bc(a, 2 * p + 1))

    for g in range(G_SSD):
        xdts, xws, cums = [], [], []
        for pp in range(pairs_per_group):
            p = g * pairs_per_group + pp
            xdt = xs[:, p * LANES:(p + 1) * LANES] * pair(dt, p)
            cum_p = jnp.where(lo_half, bc_cum[2 * p], bc_cum[2 * p + 1])
            rest_p = cum_p[rows - 1:rows, :] - cum_p if bb == 1 else pair(rest, p)
            xdts.append(xdt)
            xws.append(xdt * jnp.exp(rest_p))
            cums.append(cum_p)
        xw = jnp.concatenate(xws, axis=1)
        if has_state:
            y_st = [_dot_nt(cms[g][s], ost_ref[b, g]) for b, s in enumerate(seqs)]
            kv = [_dot_tn(xw[s], bms[g][s]) for s in seqs]
        else:
            y_st = [_dot(cms[g][s], sst_s[b, g]) for b, s in enumerate(seqs)]
            kv = [_dot_tn(bms[g][s], xw[s]) for s in seqs]
        y_intra = []
        for pp in range(pairs_per_group):
            p = g * pairs_per_group + pp
            a_pair = []
            for h in (2 * p, 2 * p + 1):
                diff = bc_cum[h][:, :rows] - cum_t[h:h + 1, :]
                a_pair.append(scores[g] * jnp.where(causal, jnp.exp(jnp.where(causal, diff, 0.0)), 0.0))
            both = _dot(jnp.concatenate(a_pair, axis=0), xdts[pp])
            y_intra.append(jnp.where(lo_half, both[:rows], both[rows:]))
        yield
        for b in range(bb):
            last = (b + 1) * seq_len - 1
            if has_state:
                dec = [jnp.broadcast_to(jnp.exp(bc_cum[g * heads_per_group + hh][last:last + 1, :]),
                                        (P_SSD, N_SSD)) for hh in range(heads_per_group)]
                ost_ref[b, g] = ost_ref[b, g] * jnp.concatenate(dec, axis=0) + kv[b]
            else:
                dec = jnp.exp(jnp.concatenate([c_p[last:last + 1, :] for c_p in cums], axis=1))
                sst_s[b, g] = sst_s[b, g] * dec + kv[b]
        y_state = y_st[0] if bb == 1 else jnp.concatenate(y_st, axis=0)
        gs = slice(g * gw, (g + 1) * gw)
        y = (jnp.concatenate(y_intra, axis=1) + y_state * jnp.exp(jnp.concatenate(cums, axis=1))
             + dssm_ref[:, gs] * xs[:, gs])
        y = _rms(y * _silu(z[:, gs]), ssmw_ref[:, gs])
        y_ref[:, :, gs] = y.reshape(bb, seq_len, gw).astype(y_ref.dtype)


def _odd_kernel(has_state, x_ref, xn_ref, npre_ref, win_ref, convw_ref, convb_ref, dtb_ref, alog_ref,
                dssm_ref, ssmw_ref, *refs):
    if has_state:
        sconv_ref, sst_ref, y_ref, u_ref, oconv_ref, ost_ref, proj_s, proj_next_s, tail_s = refs
        sst_s = None
    else:
        y_ref, u_ref, oconv_ref, ost_ref, proj_s, proj_next_s, tail_s, sst_s = refs
    bb, seq_len, _ = x_ref.shape
    halo = convw_ref.shape[0] - 1

    @pl.when(pl.program_id(1) == 0)
    def _():
        tail_s[...] = jnp.zeros(tail_s.shape, F32)
        if has_state:
            for b in range(bb):
                tail_s[(b + 1) * SUBLANES - halo:(b + 1) * SUBLANES, :] = sconv_ref[b]
            ost_ref[...] = sst_ref[...]
        else:
            sst_s[...] = jnp.zeros(sst_s.shape, F32)

    fillers = _projection_ahead(x_ref, xn_ref, npre_ref, win_ref, proj_s, proj_next_s)
    consts = (convw_ref, convb_ref, dtb_ref, alog_ref, dssm_ref, ssmw_ref)
    if seq_len > SUBLANES:
        units = []
        for u in range(bb):
            one = pl.ds(u, 1)
            units.append(_odd_block(
                has_state, 1, seq_len, proj_s.at[pl.ds(u * seq_len, seq_len)],
                tail_s.at[pl.ds(u * SUBLANES, SUBLANES)], ost_ref.at[one],
                None if sst_s is None else sst_s.at[one], *consts, y_ref.at[one], u_ref.at[one]))
    else:
        units = [_odd_block(has_state, bb, seq_len, proj_s, tail_s, ost_ref, sst_s, *consts, y_ref, u_ref)]
    _interleave(units, fillers, per_round=3)
    _carry_projection(proj_s, proj_next_s)

    @pl.when(pl.program_id(1) == pl.num_programs(1) - 1)
    def _():
        for b in range(bb):
            oconv_ref[b] = tail_s[(b + 1) * SUBLANES - halo:(b + 1) * SUBLANES, :]
            if not has_state:
                for g in range(G_SSD):
                    for ps in range(0, ost_ref.shape[2], LANES):
                        ost_ref[b, g, ps:ps + LANES, :] = sst_s[b, g, :, ps:ps + LANES].T


def _odd_call(x, npre, win, convw, convb, dtb, alog, dssm, ssmw, states, bb, c):
    nb_total, t, d = x.shape
    has_state = states is not None
    n_in = win.shape[1]
    d_inner = H_SSD * P_SSD
    conv_dim = d_inner + 2 * G_SSD * N_SSD
    u_w = S5_G * S5_GS
    halo = convw.shape[0] - 1
    rows = bb * c
    grid = (nb_total // bb, t // c)
    st_shape = (G_SSD, d_inner // G_SSD, N_SSD)
    conv_spec = pl.BlockSpec((bb, halo, conv_dim), lambda b, i: (b, 0, 0))
    ssm_spec = pl.BlockSpec((bb,) + st_shape, lambda b, i: (b, 0, 0, 0))
    in_specs = [pl.BlockSpec((bb, c, d), lambda b, i: (b, i, 0)),
                pl.BlockSpec((bb, c, d), _next_block_map(*grid)),
                _const_spec((1, d)), _const_spec(win.shape), _const_spec(convw.shape),
                _const_spec((1, conv_dim)), _const_spec((1, LANES)), _const_spec((1, LANES)),
                _const_spec((1, d_inner)), _const_spec((1, d_inner))]
    args = [x, x, npre, win, convw, convb, dtb, alog, dssm, ssmw]
    scratch = [pltpu.VMEM((rows, n_in), F32), pltpu.VMEM((rows, n_in), F32),
               pltpu.VMEM((bb * SUBLANES, conv_dim), F32)]
    if has_state:
        in_specs += [conv_spec, ssm_spec]
        args += list(states)
    else:
        scratch.append(pltpu.VMEM((bb, G_SSD, N_SSD, d_inner // G_SSD), F32))
    return pl.pallas_call(
        functools.partial(_odd_kernel, has_state),
        grid=grid,
        in_specs=in_specs,
        out_specs=[pl.BlockSpec((bb, c, d_inner), lambda b, i: (b, i, 0)),
                   pl.BlockSpec((bb, c, u_w), lambda b, i: (b, i, 0)),
                   conv_spec, ssm_spec],
        out_shape=[jax.ShapeDtypeStruct((nb_total, t, d_inner), BF16),
                   jax.ShapeDtypeStruct((nb_total, t, u_w), F32),
                   jax.ShapeDtypeStruct((nb_total, halo, conv_dim), F32),
                   jax.ShapeDtypeStruct((nb_total,) + st_shape, F32)],
        scratch_shapes=scratch,
        compiler_params=pltpu.CompilerParams(dimension_semantics=("arbitrary", "arbitrary"),
                                             vmem_limit_bytes=VMEM_LIMIT),
        name="odd_mixer",
    )(*args)


S5_PARTS = 4
S5_PART_IN = S5_G * S5_GS // S5_PARTS
S5_PART_ST = S5_G * S5_P // S5_PARTS
S5_SCAN_VREGS = 4


def _s5_pitch(steps):
    return steps if steps % 16 == 8 else steps + 8


def _s5_kernel(u_ref, bmat_ref, cmat_ref, lre_ref, lim_ref, d_ref, wglu_ref, bglu_ref,
               h0re_ref, h0im_ref, yd_ref, ore_ref, oim_ref, buf_s, slab_s, ut_s):
    r, steps, uw = u_ref.shape
    hs = S5_PART_ST
    scan_w = min(hs, max(LANES, S5_SCAN_VREGS * SUBLANES * LANES // r))
    pitch = _s5_pitch(steps)
    nslab = uw // LANES

    @pl.when(pl.program_id(1) == 0)
    def _():
        ore_ref[...] = h0re_ref[...]
        oim_ref[...] = h0im_ref[...]

    for b in range(r):
        for j in range(nslab):
            slab_s[j, b * pitch:b * pitch + steps, :] = u_ref[b, :, j * LANES:(j + 1) * LANES]
    for t in range(steps):
        for j in range(nslab):
            ut_s[t * r:(t + 1) * r, j * LANES:(j + 1) * LANES] = slab_s[j, pl.ds(t, r, stride=pitch), :]
    u = ut_s[...]
    ub = u.astype(BF16)
    for half in range(S5_PARTS):
        buf_s[:, half * 2 * hs:(half + 1) * 2 * hs] = jnp.dot(
            ub[:, half * S5_PART_IN:(half + 1) * S5_PART_IN], bmat_ref[half],
            preferred_element_type=F32)

    for half in range(S5_PARTS):
        for j in range(hs // scan_w):
            st = slice(half * hs + j * scan_w, half * hs + (j + 1) * scan_w)
            cre = half * 2 * hs + j * scan_w
            cim = cre + hs
            lre = jnp.broadcast_to(lre_ref[:, st], (r, scan_w))
            lim = jnp.broadcast_to(lim_ref[:, st], (r, scan_w))
            hre, him = ore_ref[:, st], oim_ref[:, st]
            for t in range(steps):
                rws = slice(t * r, (t + 1) * r)
                hre, him = (lre * hre - lim * him + buf_s[rws, cre:cre + scan_w],
                            lre * him + lim * hre + buf_s[rws, cim:cim + scan_w])
                buf_s[rws, cre:cre + scan_w] = hre
                buf_s[rws, cim:cim + scan_w] = him
            ore_ref[:, st] = hre
            oim_ref[:, st] = him

    ys = []
    for half in range(S5_PARTS):
        ys.append(jnp.dot(buf_s[:, half * 2 * hs:(half + 1) * 2 * hs].astype(BF16), cmat_ref[half],
                          preferred_element_type=F32))
    y = jnp.concatenate(ys, axis=1) + d_ref[...] * u
    g = jax.nn.gelu(y)
    gate = jax.nn.sigmoid(jnp.dot(g.astype(BF16), wglu_ref[...], preferred_element_type=F32)
                          + bglu_ref[...])
    yd = g * gate
    for t in range(steps):
        for j in range(nslab):
            slab_s[j, pl.ds(t, r, stride=pitch), :] = yd[t * r:(t + 1) * r, j * LANES:(j + 1) * LANES]
    for b in range(r):
        for j in range(nslab):
            yd_ref[b, :, j * LANES:(j + 1) * LANES] = slab_s[j, b * pitch:b * pitch + steps, :].astype(
                yd_ref.dtype)


def _s5_call(u, bmat, cmat, lre, lim, dvec, wglu, bglu, h0re, h0im, steps, r):
    nb_total, t, uw = u.shape
    ns = S5_G * S5_P
    grid = (nb_total // r, t // steps)
    st_spec = pl.BlockSpec((r, ns), lambda b, i: (b, 0))
    return pl.pallas_call(
        _s5_kernel,
        grid=grid,
        in_specs=[pl.BlockSpec((r, steps, uw), lambda b, i: (b, i, 0)),
                  _const_spec(bmat.shape), _const_spec(cmat.shape),
                  _const_spec((1, ns)), _const_spec((1, ns)), _const_spec((1, uw)),
                  _const_spec(wglu.shape), _const_spec((1, uw)), st_spec, st_spec],
        out_specs=[pl.BlockSpec((r, steps, uw), lambda b, i: (b, i, 0)), st_spec, st_spec],
        out_shape=[jax.ShapeDtypeStruct((nb_total, t, uw), BF16),
                   jax.ShapeDtypeStruct((nb_total, ns), F32),
                   jax.ShapeDtypeStruct((nb_total, ns), F32)],
        scratch_shapes=[pltpu.VMEM((steps * r, 2 * ns), F32),
                        pltpu.VMEM((uw // LANES, r * _s5_pitch(steps), LANES), F32),
                        pltpu.VMEM((steps * r, uw), F32)],
        compiler_params=pltpu.CompilerParams(dimension_semantics=("arbitrary", "arbitrary"),
                                             vmem_limit_bytes=VMEM_LIMIT),
        name="s5_mixer",
    )(u, bmat, cmat, lre, lim, dvec, wglu, bglu, h0re, h0im)


def _rope_tables(pos):
    half = HD // 2
    inv_freq = ROPE_BASE ** (-jnp.arange(half, dtype=F32) / half)
    ang = pos[:, None] * inv_freq[None, :]
    cos, sin = jnp.cos(ang), jnp.sin(ang)
    return jnp.concatenate([cos, cos], axis=-1), jnp.concatenate([-sin, sin], axis=-1)


def _pad_lanes(v):
    return jnp.pad(v.astype(F32), (0, LANES - v.shape[0]))[None, :]


def _s5_tables(lam_re, lam_im, log_step, b_re, b_im, c_re, c_im):
    lr, li = lam_re.astype(F32), lam_im.astype(F32)
    dt = jnp.exp(log_step.astype(F32))[:, None]
    mag = jnp.exp(lr * dt)
    bar_re, bar_im = mag * jnp.cos(li * dt), mag * jnp.sin(li * dt)
    den = lr * lr + li * li
    cf_re = ((bar_re - 1.0) * lr + bar_im * li) / den
    cf_im = (bar_im * lr - (bar_re - 1.0) * li) / den
    bb_re = cf_re[..., None] * b_re.astype(F32) - cf_im[..., None] * b_im.astype(F32)
    bb_im = cf_re[..., None] * b_im.astype(F32) + cf_im[..., None] * b_re.astype(F32)
    gh = S5_G // S5_PARTS
    eye = jnp.eye(gh, dtype=BF16)
    b_ri = jnp.stack([bb_re, bb_im]).astype(BF16).reshape(2, S5_PARTS, gh, S5_P, S5_GS)
    bmat = jnp.einsum('rqgpc,gk->qgcrkp', b_ri, eye).reshape(S5_PARTS, gh * S5_GS, 2 * gh * S5_P)
    c_ri = jnp.stack([c_re.astype(F32), -c_im.astype(F32)]).astype(BF16).reshape(
        2, S5_PARTS, gh, S5_GS, S5_P)
    cmat = jnp.einsum('rqgcp,gk->qrgpkc', c_ri, eye).reshape(S5_PARTS, 2 * gh * S5_P, gh * S5_GS)
    return bmat, cmat, bar_re.reshape(1, -1), bar_im.reshape(1, -1)


def _row(v):
    return v.astype(F32).reshape(1, -1)


def _forward(x, pos_offset, states, p, cfg):
    nb, t, d = x.shape
    n = nb * t
    bb, c = cfg['bb'], cfg['c']
    has_state = states is not None
    pos = jnp.arange(pos_offset, pos_offset + t, dtype=F32)
    cos, sin = _rope_tables(pos)
    lb = jnp.cumsum(jax.nn.softmax(p['hgrn_lower_bounds'].astype(F32), axis=0), axis=0)[0]
    mix, ret_o, hg_o = _even_call(
        x, cos, sin, _row(p['norm_mix_pre'][0]), p['w_in_even'][0].astype(BF16),
        _row(p['ret_norm_w'][0]), _row(p['hgrn_norm_w'][0]), _row(lb),
        (states['ret'][0], states['hgrn'][0]) if has_state else None, bb, c)
    stacked_rows = lambda a: a.astype(F32)[:, None, :]
    ffn = (stacked_rows(p['norm_mix_post']), stacked_rows(p['norm_ffn_pre']),
           stacked_rows(p['norm_ffn_post']), p['w_ffn_up'].astype(BF16), p['w_ffn_down'].astype(BF16))
    h = _post_call(x.reshape(n, d), [mix.reshape(n, -1)], p['w_out_even'][0].astype(BF16), *ffn, 0,
                   cfg['rows'])
    d_inner = H_SSD * P_SSD
    conv_dim = d_inner + 2 * G_SSD * N_SSD
    u_w = S5_G * S5_GS
    w1 = p['w_in_odd'][0].astype(BF16)
    st_shape = (nb, G_SSD, d_inner // G_SSD, N_SSD)
    odd_states = None
    if has_state:
        odd_states = (states['conv'][0], jnp.swapaxes(states['ssm'][0], -1, -2).reshape(st_shape))
    y, u, conv_o, sst_o = _odd_call(
        h.reshape(nb, t, d), _row(p['norm_mix_pre'][1]), w1, p['conv_w'][0].astype(F32),
        _row(p['conv_b'][0]), _pad_lanes(p['dt_bias'][0]), _pad_lanes(p['a_log'][0]),
        _row(jnp.repeat(p['d_ssm'][0], P_SSD)), _row(p['ssm_norm_w'][0]), odd_states, bb, c)
    ssm_o = jnp.swapaxes(sst_o.reshape(nb, H_SSD, P_SSD, N_SSD), -1, -2)
    bmat, cmat, lre, lim = _s5_tables(p['s5_lam_re'][0], p['s5_lam_im'][0], p['s5_log_step'][0],
                                      p['s5_b_re'][0], p['s5_b_im'][0], p['s5_c_re'][0],
                                      p['s5_c_im'][0])
    ns = S5_G * S5_P
    if has_state:
        h0re = states['s5_re'][0].reshape(nb, ns)
        h0im = states['s5_im'][0].reshape(nb, ns)
    else:
        h0re = jnp.zeros((nb, ns), F32)
        h0im = jnp.zeros((nb, ns), F32)
    yd, re_o, im_o = _s5_call(u, bmat, cmat, lre, lim, _row(p['s5_d'][0]),
                              p['w_glu'][0].astype(BF16), _row(p['b_glu'][0]), h0re, h0im,
                              cfg['s5_steps'], cfg['s5_rows'])
    yd = yd.reshape(n, u_w)
    h = _post_call(h, [y.reshape(n, d_inner), yd], p['w_out_odd'][0].astype(BF16), *ffn, 1, cfg['rows'])
    return (h.reshape(nb, t, d), ret_o[None], hg_o[None], ssm_o[None], conv_o[None],
            re_o.reshape(1, nb, S5_G, S5_P), im_o.reshape(1, nb, S5_G, S5_P))


PROMPT_CFG = dict(bb=4, c=128, rows=1024, s5_steps=64, s5_rows=8)
SAMPLE_CFG = dict(bb=16, c=8, rows=1024, s5_steps=8, s5_rows=32)


def kernel(x_prompt, x_sample, state_ret, state_hgrn, state_ssm, state_conv, state_s5_re, state_s5_im,
           norm_mix_pre, norm_mix_post, norm_ffn_pre, norm_ffn_post, w_in_even, w_out_even, ret_norm_w,
           hgrn_lower_bounds, hgrn_norm_w, w_in_odd, conv_w, conv_b, dt_bias, a_log, d_ssm, ssm_norm_w,
           s5_lam_re, s5_lam_im, s5_log_step, s5_b_re, s5_b_im, s5_c_re, s5_c_im, s5_d, w_glu, b_glu,
           w_out_odd, w_ffn_up, w_ffn_down):
    p = dict(norm_mix_pre=norm_mix_pre, norm_mix_post=norm_mix_post, norm_ffn_pre=norm_ffn_pre,
             norm_ffn_post=norm_ffn_post, w_in_even=w_in_even, w_out_even=w_out_even, ret_norm_w=ret_norm_w,
             hgrn_lower_bounds=hgrn_lower_bounds, hgrn_norm_w=hgrn_norm_w, w_in_odd=w_in_odd, conv_w=conv_w,
             conv_b=conv_b, dt_bias=dt_bias, a_log=a_log, d_ssm=d_ssm, ssm_norm_w=ssm_norm_w,
             s5_lam_re=s5_lam_re, s5_lam_im=s5_lam_im, s5_log_step=s5_log_step, s5_b_re=s5_b_re,
             s5_b_im=s5_b_im, s5_c_re=s5_c_re, s5_c_im=s5_c_im, s5_d=s5_d, w_glu=w_glu, b_glu=b_glu,
             w_out_odd=w_out_odd, w_ffn_up=w_ffn_up, w_ffn_down=w_ffn_down)
    past_len = 16384
    states = dict(ret=state_ret, hgrn=state_hgrn, ssm=state_ssm, conv=state_conv,
                  s5_re=state_s5_re, s5_im=state_s5_im)
    y_p, ret_p, hg_p, ssm_p, conv_p, re_p, im_p = _forward(x_prompt, 0, None, p, PROMPT_CFG)
    y_s, ret_s, hg_s, ssm_s, conv_s, re_s, im_s = _forward(x_sample, past_len, states, p, SAMPLE_CFG)
    return (y_p, y_s, ret_p, ret_s, hg_p, hg_s, ssm_p, ssm_s, conv_p, conv_s,
            re_p, re_s, im_p, im_s)
```
